```python
import jax, jax.numpy as jnp
from jax import lax
import numpy as np

D_MODEL = 1024
BATCH = 8
SEQ = 2048
DEPTH = 1
DEC_BATCH = 128
DEC_SEQ = 1
PAST_LEN = 16384
PAGE_SIZE = 128

MIX_WIDTH = D_MODEL
SSM_WIDTH = MIX_WIDTH // 2
SSM_GROUP = 16
SSM_GROUPS = SSM_WIDTH // SSM_GROUP
SSM_STATE = 64
GM_WIDTH = MIX_WIDTH - SSM_WIDTH
GM_HEADS = 4
GM_HEAD_DIM = GM_WIDTH // GM_HEADS
CHUNK = 128
IN_WIDTH = SSM_WIDTH + 2 * GM_WIDTH
N_EXPERTS = 64
TOP_K = 8
N_ROUTE_GROUPS = 8
TOPK_ROUTE_GROUPS = 4
EXPERT_HIDDEN = 256
SHARED_HIDDEN = 256
ROUTED_SCALE = 2.5
MOE_BLOCK = 128
EPS = 1e-6
DT_MIN = 0.001
DT_MAX = 0.1

kernel_name = "hybrid_s5_gmlp_moe_adaln_step"


def rmsnorm(x, g):
    xf = x.astype(jnp.float32)
    y = xf * lax.rsqrt(jnp.mean(xf * xf, axis=-1, keepdims=True) + EPS)
    return (y * g.astype(jnp.float32)).astype(x.dtype)


def modulate(h, shift, scale):
    return h * (1 + scale[:, None, :]) + shift[:, None, :]


def _cmul(ar, ai, br, bi):
    return ar * br - ai * bi, ar * bi + ai * br


def _scan_op(e1, e2):
    a1r, a1i, b1r, b1i = e1
    a2r, a2i, b2r, b2i = e2
    ar, ai = _cmul(a2r, a2i, a1r, a1i)
    br, bi = _cmul(a2r, a2i, b1r, b1i)
    return ar, ai, br + b2r, bi + b2i


def s5_mixer(u, h0_re, h0_im, lam_re, lam_im, log_dt, b_re, b_im, c_re, c_im, d_skip, w_glu, b_glu):
    f32 = jnp.float32
    Bn, L, _ = u.shape
    uf = u.astype(f32).reshape(Bn, L, SSM_GROUPS, SSM_GROUP)
    lr = jnp.minimum(lam_re.astype(f32), -1e-4)
    li = lam_im.astype(f32)
    dt = jnp.exp(log_dt.astype(f32))[:, None]
    mag = jnp.exp(lr * dt)
    abar_r, abar_i = mag * jnp.cos(li * dt), mag * jnp.sin(li * dt)
    den = lr * lr + li * li
    nr, ni = abar_r - 1.0, abar_i
    f_r = (nr * lr + ni * li) / den
    f_i = (ni * lr - nr * li) / den
    br_, bi_ = b_re.astype(f32), b_im.astype(f32)
    bbar_r = f_r[..., None] * br_ - f_i[..., None] * bi_
    bbar_i = f_r[..., None] * bi_ + f_i[..., None] * br_
    bu_r = jnp.einsum('blgh,gnh->blgn', uf, bbar_r)
    bu_i = jnp.einsum('blgh,gnh->blgn', uf, bbar_i)
    ir, ii = _cmul(abar_r, abar_i, h0_re.astype(f32), h0_im.astype(f32))
    bu_r = bu_r.at[:, 0].add(ir)
    bu_i = bu_i.at[:, 0].add(ii)
    ar = jnp.broadcast_to(abar_r, bu_r.shape)
    ai = jnp.broadcast_to(abar_i, bu_i.shape)
    _, _, hr, hi = lax.associative_scan(_scan_op, (ar, ai, bu_r, bu_i), axis=1)
    y = (jnp.einsum('blgn,ghn->blgh', hr, c_re.astype(f32))
         - jnp.einsum('blgn,ghn->blgh', hi, c_im.astype(f32))
         + d_skip.astype(f32) * uf).reshape(Bn, L, SSM_WIDTH)
    y = jax.nn.gelu(y)
    y = y * jax.nn.sigmoid(y @ w_glu.astype(f32) + b_glu.astype(f32))
    return y.astype(u.dtype), hr[:, -1], hi[:, -1]


def spatial_gate(v, w_s, b_s):
    Bn, L = v.shape[:2]
    n_chunks = -(-L // CHUNK)
    Lp = n_chunks * CHUNK
    vp = jnp.pad(v, ((0, 0), (0, Lp - L), (0, 0), (0, 0))).reshape(Bn, n_chunks, CHUNK, GM_HEADS, GM_HEAD_DIM)
    w = w_s * jnp.tril(jnp.ones((CHUNK, CHUNK), w_s.dtype))[None]
    s = jnp.einsum('hij,bcjhd->bcihd', w, vp) + b_s.T[None, None, :, :, None]
    return s.reshape(Bn, Lp, GM_HEADS, GM_HEAD_DIM)[:, :L]


def gmlp_mixer(z, ln_g, ln_b, w_s, b_s):
    Bn, L, _ = z.shape
    u, v = jnp.split(z, 2, axis=-1)
    vf = v.reshape(Bn, L, GM_HEADS, GM_HEAD_DIM).astype(jnp.float32)
    mu = jnp.mean(vf, axis=-1, keepdims=True)
    var = jnp.mean(jnp.square(vf - mu), axis=-1, keepdims=True)
    vf = ((vf - mu) * lax.rsqrt(var + EPS) * ln_g.astype(jnp.float32).reshape(GM_HEADS, GM_HEAD_DIM)
          + ln_b.astype(jnp.float32).reshape(GM_HEADS, GM_HEAD_DIM))
    vn = vf.astype(z.dtype)
    s = spatial_gate(vn, w_s, b_s).reshape(Bn, L, GM_WIDTH)
    return u * s, vn.reshape(Bn, L, GM_WIDTH)


def route(h, router_w, router_bias):
    T = h.shape[0]
    scores = jax.nn.sigmoid((h @ router_w).astype(jnp.float32))
    biased = scores + router_bias.astype(jnp.float32)
    gs = lax.top_k(biased.reshape(T, N_ROUTE_GROUPS, -1), 2)[0].sum(-1)
    _, gidx = lax.top_k(gs, TOPK_ROUTE_GROUPS)
    gmask = (gidx[..., None] == jnp.arange(N_ROUTE_GROUPS)).any(axis=1)
    emask = jnp.repeat(gmask, N_EXPERTS // N_ROUTE_GROUPS, axis=1)
    _, idx = lax.top_k(jnp.where(emask, biased, -jnp.inf), TOP_K)
    w = jnp.take_along_axis(scores, idx, axis=1)
    w = w / jnp.sum(w, axis=-1, keepdims=True) * ROUTED_SCALE
    return w, idx


def routed_experts(h, w, idx, w_gate, w_up, w_down):
    T, D = h.shape
    A = T * TOP_K
    flat_e = idx.reshape(-1)
    flat_t = jnp.repeat(jnp.arange(T, dtype=jnp.int32), TOP_K)
    flat_w = w.reshape(-1)
    order = jnp.argsort(flat_e)
    se, st, sw = flat_e[order], flat_t[order], flat_w[order]
    counts = jnp.zeros((N_EXPERTS,), jnp.int32).at[flat_e].add(1)
    padded = (counts + MOE_BLOCK - 1) // MOE_BLOCK * MOE_BLOCK
    pad_end = jnp.cumsum(padded)
    pad_start = pad_end - padded
    start = jnp.cumsum(counts) - counts
    dest = pad_start[se] + jnp.arange(A, dtype=jnp.int32) - start[se]
    n_blocks = (A + N_EXPERTS * (MOE_BLOCK - 1)) // MOE_BLOCK
    slot_tok = jnp.full((n_blocks * MOE_BLOCK,), T, jnp.int32).at[dest].set(st)
    slot_w = jnp.zeros((n_blocks * MOE_BLOCK,), jnp.float32).at[dest].set(sw)
    block_e = jnp.minimum(jnp.searchsorted(pad_end, jnp.arange(n_blocks, dtype=jnp.int32) * MOE_BLOCK, side='right'),
                          N_EXPERTS - 1)
    h_pad = jnp.concatenate([h, jnp.zeros((1, D), h.dtype)], axis=0)

    def one_block(args):
        tok, e = args
        xb = h_pad[tok]
        hid = jax.nn.silu(xb @ w_gate[e]) * (xb @ w_up[e])
        return hid @ w_down[e]

    ys = lax.map(one_block, (slot_tok.reshape(n_blocks, MOE_BLOCK), block_e)).reshape(-1, D)
    ys = ys * slot_w[:, None].astype(ys.dtype)
    out = jnp.zeros((T + 1, D), ys.dtype).at[slot_tok].add(ys)
    return out[:T]


def swiglu(h, wg, wu, wd):
    return (jax.nn.silu(h @ wg) * (h @ wu)) @ wd


def block_forward(x, c, h0_re, h0_im, g_norm1, g_norm2, w_ada, b_ada, w_in, w_out,
                  lam_re, lam_im, log_dt, b_re, b_im, c_re, c_im, d_skip, w_glu, b_glu,
                  ln_g, ln_b, w_s, b_s, router_w, router_bias, moe_wg, moe_wu, moe_wd,
                  sh_wg, sh_wu, sh_wd):
    Bn, L, D = x.shape
    mod = jax.nn.silu(c) @ w_ada + b_ada
    sh1, sc1, gt1, sh2, sc2, gt2 = jnp.split(mod, 6, axis=-1)
    h = modulate(rmsnorm(x, g_norm1), sh1, sc1)
    proj = h @ w_in
    y_a, hr, hi = s5_mixer(proj[..., :SSM_WIDTH], h0_re, h0_im, lam_re, lam_im, log_dt,
                           b_re, b_im, c_re, c_im, d_skip, w_glu, b_glu)
    y_b, v_rows = gmlp_mixer(jax.nn.gelu(proj[..., SSM_WIDTH:]), ln_g, ln_b, w_s, b_s)
    x = x + gt1[:, None, :] * (jnp.concatenate([y_a, y_b], axis=-1) @ w_out)
    h = modulate(rmsnorm(x, g_norm2), sh2, sc2).reshape(-1, D)
    w, idx = route(h, router_w, router_bias)
    ffn = swiglu(h, sh_wg, sh_wu, sh_wd) + routed_experts(h, w, idx, moe_wg, moe_wu, moe_wd)
    x = x + gt2[:, None, :] * ffn.reshape(Bn, L, D)
    return x, hr, hi, v_rows


def setup_inputs(seed: int = 0) -> dict:
    key = jax.random.key(seed)
    ks = iter(jax.random.split(key, 48))
    f32 = jnp.float32
    D = D_MODEL

    def nrm(shape, scale):
        return jax.random.normal(next(ks), shape, f32) * scale

    inp = {}
    inp["x_prompt"] = nrm((BATCH, SEQ, D), 1.0)
    inp["x_sample"] = nrm((DEC_BATCH, DEC_SEQ, D), 1.0)
    inp["c_prompt"] = nrm((BATCH, D), 1.0)
    inp["c_sample"] = nrm((DEC_BATCH, D), 1.0)
    inp["state_ssm_re"] = nrm((DEPTH, DEC_BATCH, SSM_GROUPS, SSM_STATE), 0.3)
    inp["state_ssm_im"] = nrm((DEPTH, DEC_BATCH, SSM_GROUPS, SSM_STATE), 0.3)
    inp["g_norm1"] = 1.0 + nrm((DEPTH, D), 0.02)
    inp["g_norm2"] = 1.0 + nrm((DEPTH, D), 0.02)
    inp["w_ada"] = nrm((DEPTH, D, 6 * D), 0.5 * D ** -0.5)
    inp["b_ada"] = nrm((DEPTH, 6 * D), 0.01)
    inp["w_in"] = nrm((DEPTH, D, IN_WIDTH), D ** -0.5)
    inp["w_out"] = nrm((DEPTH, MIX_WIDTH, D), MIX_WIDTH ** -0.5)
    inp["ssm_lambda_re"] = -0.5 + nrm((DEPTH, SSM_GROUPS, SSM_STATE), 0.01)
    inp["ssm_lambda_im"] = jnp.pi * jnp.arange(SSM_STATE, dtype=f32) + nrm((DEPTH, SSM_GROUPS, SSM_STATE), 0.01)
    inp["ssm_log_dt"] = jax.random.uniform(next(ks), (DEPTH, SSM_GROUPS), f32,
                                           minval=float(np.log(DT_MIN)), maxval=float(np.log(DT_MAX)))
    inp["ssm_b_re"] = nrm((DEPTH, SSM_GROUPS, SSM_STATE, SSM_GROUP), (2.0 * SSM_GROUP) ** -0.5)
    inp["ssm_b_im"] = nrm((DEPTH, SSM_GROUPS, SSM_STATE, SSM_GROUP), (2.0 * SSM_GROUP) ** -0.5)
    inp["ssm_c_re"] = nrm((DEPTH, SSM_GROUPS, SSM_GROUP, SSM_STATE), 0.5 ** 0.5)
    inp["ssm_c_im"] = nrm((DEPTH, SSM_GROUPS, SSM_GROUP, SSM_STATE), 0.5 ** 0.5)
    inp["ssm_d"] = nrm((DEPTH, SSM_GROUPS, SSM_GROUP), 1.0)
    inp["ssm_w_glu"] = nrm((DEPTH, SSM_WIDTH, SSM_WIDTH), SSM_WIDTH ** -0.5)
    inp["ssm_b_glu"] = nrm((DEPTH, SSM_WIDTH), 0.01)
    inp["gm_ln_g"] = 1.0 + nrm((DEPTH, GM_WIDTH), 0.02)
    inp["gm_ln_b"] = nrm((DEPTH, GM_WIDTH), 0.02)
    inp["gm_w_s"] = nrm((DEPTH, GM_HEADS, CHUNK, CHUNK), 0.5 * CHUNK ** -0.5)
    inp["gm_b_s"] = 1.0 + nrm((DEPTH, GM_HEADS, CHUNK), 0.02)
    inp["router_w"] = nrm((DEPTH, D, N_EXPERTS), D ** -0.5)
    inp["router_bias"] = nrm((DEPTH, N_EXPERTS), 0.01)
    inp["moe_w_gate"] = nrm((DEPTH, N_EXPERTS, D, EXPERT_HIDDEN), D ** -0.5)
    inp["moe_w_up"] = nrm((DEPTH, N_EXPERTS, D, EXPERT_HIDDEN), D ** -0.5)
    inp["moe_w_down"] = nrm((DEPTH, N_EXPERTS, EXPERT_HIDDEN, D), EXPERT_HIDDEN ** -0.5)
    inp["shared_w_gate"] = nrm((DEPTH, D, SHARED_HIDDEN), D ** -0.5)
    inp["shared_w_up"] = nrm((DEPTH, D, SHARED_HIDDEN), D ** -0.5)
    inp["shared_w_down"] = nrm((DEPTH, SHARED_HIDDEN, D), SHARED_HIDDEN ** -0.5)
    inp["g_final"] = 1.0 + nrm((D,), 0.02)
    return inp


def reference(x_prompt, x_sample, c_prompt, c_sample, state_ssm_re, state_ssm_im,
              g_norm1, g_norm2, w_ada, b_ada, w_in, w_out,
              ssm_lambda_re, ssm_lambda_im, ssm_log_dt, ssm_b_re, ssm_b_im, ssm_c_re, ssm_c_im,
              ssm_d, ssm_w_glu, ssm_b_glu,
              gm_ln_g, gm_ln_b, gm_w_s, gm_b_s,
              router_w, router_bias, moe_w_gate, moe_w_up, moe_w_down,
              shared_w_gate, shared_w_up, shared_w_down, g_final):
    xp, xs = x_prompt, x_sample
    zeros_state = jnp.zeros((BATCH, SSM_GROUPS, SSM_STATE), jnp.float32)
    p_re, p_im, s_re, s_im, s_v = [], [], [], [], []
    for l in range(DEPTH):
        lp = (g_norm1[l], g_norm2[l], w_ada[l], b_ada[l], w_in[l], w_out[l],
              ssm_lambda_re[l], ssm_lambda_im[l], ssm_log_dt[l], ssm_b_re[l], ssm_b_im[l],
              ssm_c_re[l], ssm_c_im[l], ssm_d[l], ssm_w_glu[l], ssm_b_glu[l],
              gm_ln_g[l], gm_ln_b[l], gm_w_s[l], gm_b_s[l],
              router_w[l], router_bias[l], moe_w_gate[l], moe_w_up[l], moe_w_down[l],
              shared_w_gate[l], shared_w_up[l], shared_w_down[l])
        xp, hpr, hpi, _ = block_forward(xp, c_prompt, zeros_state, zeros_state, *lp)
        xs, hsr, hsi, vs = block_forward(xs, c_sample, state_ssm_re[l], state_ssm_im[l], *lp)
        p_re.append(hpr)
        p_im.append(hpi)
        s_re.append(hsr)
        s_im.append(hsi)
        s_v.append(vs)
    y_prompt = rmsnorm(xp, g_final)
    y_sample = rmsnorm(xs, g_final)
    return (y_prompt, y_sample, jnp.stack(p_re), jnp.stack(p_im), jnp.stack(s_re), jnp.stack(s_im), jnp.stack(s_v))
```

```python
import functools

import jax
import jax.numpy as jnp
from jax import lax
from jax.experimental import pallas as pl
from jax.experimental.pallas import tpu as pltpu

F32 = jnp.float32
BF16 = jnp.bfloat16

EPS = 1e-6
SSM_GROUP = 16
SSM_STATE = 64
GM_HEADS = 4
CHUNK = 128
N_EXPERTS = 64
TOP_K = 8
N_ROUTE_GROUPS = 8
TOPK_ROUTE_GROUPS = 4
ROUTED_SCALE = 2.5

LANES = 128
SUBLANES = 8
VMEM_LIMIT = 48 * 1024 * 1024
ROW_TILE = 512
SCAN_TILE = 128
MOE_TILE = 1024


def _params(n_axes):
    return pltpu.CompilerParams(dimension_semantics=("arbitrary",) * n_axes,
                                vmem_limit_bytes=VMEM_LIMIT)


def _dot(a, b):
    return jnp.dot(a.astype(BF16), b.astype(BF16), preferred_element_type=F32)


def _rmsnorm(x, g):
    ms = jnp.mean(x * x, axis=-1, keepdims=True)
    return x * lax.rsqrt(ms + EPS) * g


def _ada_kernel(c_ref, w_ref, b_ref, o_ref):
    c = c_ref[...]
    o_ref[...] = _dot(jax.nn.silu(c), w_ref[...]) + b_ref[...]


def _ada(c, w, b, tn=512):
    m, d = c.shape
    n = w.shape[1]
    return pl.pallas_call(
        _ada_kernel,
        grid=(n // tn,),
        in_specs=[pl.BlockSpec((m, d), lambda j: (0, 0)),
                  pl.BlockSpec((d, tn), lambda j: (0, j)),
                  pl.BlockSpec((1, tn), lambda j: (0, j))],
        out_specs=pl.BlockSpec((m, tn), lambda j: (0, j)),
        out_shape=jax.ShapeDtypeStruct((m, n), F32),
        compiler_params=_params(1),
        name="ada_mod",
    )(c, w, b)


def _premix_common(x_ref, sh_ref, sc_ref, g_ref, win_ref, lng_ref, lnb_ref, sw):
    x = x_ref[...]
    h = _rmsnorm(x, g_ref[...]) * (1.0 + sc_ref[...]) + sh_ref[...]
    proj = _dot(h, win_ref[...])
    u_a = proj[:, :sw]
    z = jax.nn.gelu(proj[:, sw:])
    gw = z.shape[1] // 2
    u_b = z[:, :gw]
    v_b = z[:, gw:]
    hd = gw // GM_HEADS
    parts = []
    for k in range(GM_HEADS):
        vh = v_b[:, k * hd:(k + 1) * hd]
        mu = jnp.mean(vh, axis=-1, keepdims=True)
        dv = vh - mu
        var = jnp.mean(dv * dv, axis=-1, keepdims=True)
        parts.append(dv * lax.rsqrt(var + EPS))
    vn = jnp.concatenate(parts, axis=-1) * lng_ref[...] + lnb_ref[...]
    return u_a, u_b, vn


def _premix_seq_kernel(x_ref, sh_ref, sc_ref, g_ref, win_ref, lng_ref, lnb_ref,
                       ws_ref, bs_ref, ua_ref, yb_ref, *, sw):
    u_a, u_b, vn = _premix_common(x_ref, sh_ref, sc_ref, g_ref, win_ref, lng_ref, lnb_ref, sw)
    ua_ref[...] = u_a
    tm, gw = u_b.shape
    hd = gw // GM_HEADS
    vnb = vn.astype(BF16)
    for c in range(tm // CHUNK):
        rows = slice(c * CHUNK, (c + 1) * CHUNK)
        for k in range(GM_HEADS):
            cols = slice(k * hd, (k + 1) * hd)
            s = jnp.dot(ws_ref[k], vnb[rows, cols], preferred_element_type=F32)
            yb_ref[rows, cols] = u_b[rows, cols] * (s + bs_ref[:, cols])


def _premix_step_kernel(x_ref, sh_ref, sc_ref, g_ref, win_ref, lng_ref, lnb_ref,
                        w0_ref, b0_ref, ua_ref, yb_ref, vn_ref, *, sw):
    u_a, u_b, vn = _premix_common(x_ref, sh_ref, sc_ref, g_ref, win_ref, lng_ref, lnb_ref, sw)
    ua_ref[...] = u_a
    vn_ref[...] = vn
    yb_ref[...] = u_b * (vn * w0_ref[...] + b0_ref[...])


def _row_spec(tm, d):
    return pl.BlockSpec((tm, d), lambda i: (i, 0))


def _full_spec(shape):
    nd = len(shape)
    return pl.BlockSpec(shape, lambda i: (0,) * nd)


def _mod_spec(per_row, tm, d, tiles_per_batch):
    if per_row:
        return pl.BlockSpec((tm, d), lambda i: (i, 0))
    return pl.BlockSpec((None, 1, d), lambda i: (i // tiles_per_batch, 0, 0))


def _premix_seq(x, sh, sc, g, w_in, ln_g, ln_b, ws_tril, bs_full, seq_len, tm):
    t, d = x.shape
    sw = ln_g.shape[1]
    tpb = seq_len // tm
    mod = _mod_spec(False, tm, d, tpb)
    return pl.pallas_call(
        functools.partial(_premix_seq_kernel, sw=sw),
        grid=(t // tm,),
        in_specs=[_row_spec(tm, d), mod, mod, _full_spec(g.shape), _full_spec(w_in.shape),
                  _full_spec(ln_g.shape), _full_spec(ln_b.shape), _full_spec(ws_tril.shape),
                  _full_spec(bs_full.shape)],
        out_specs=[_row_spec(tm, sw), _row_spec(tm, sw)],
        out_shape=[jax.ShapeDtypeStruct((t, sw), F32), jax.ShapeDtypeStruct((t, sw), F32)],
        compiler_params=_params(1),
        name="premix_seq",
    )(x, sh, sc, g, w_in, ln_g, ln_b, ws_tril, bs_full)


def _premix_step(x, sh, sc, g, w_in, ln_g, ln_b, w0, b0):
    t, d = x.shape
    sw = ln_g.shape[1]
    mod = _mod_spec(True, t, d, 1)
    return pl.pallas_call(
        functools.partial(_premix_step_kernel, sw=sw),
        grid=(1,),
        in_specs=[_row_spec(t, d), mod, mod, _full_spec(g.shape), _full_spec(w_in.shape),
                  _full_spec(ln_g.shape), _full_spec(ln_b.shape), _full_spec(w0.shape),
                  _full_spec(b0.shape)],
        out_specs=[_row_spec(t, sw)] * 3,
        out_shape=[jax.ShapeDtypeStruct((t, sw), F32)] * 3,
        compiler_params=_params(1),
        name="premix_step",
    )(x, sh, sc, g, w_in, ln_g, ln_b, w0, b0)


def _s5_output(xs, u, cj_ref, d_ref, wglu_ref, bglu_ref, n_chunks):
    cw = xs.shape[1] // n_chunks
    xb = xs.astype(BF16)
    ys = [jnp.dot(xb[:, j * cw:(j + 1) * cw], cj_ref[j], preferred_element_type=F32)
          for j in range(n_chunks)]
    y = jnp.concatenate(ys, axis=-1) + d_ref[...] * u
    y = jax.nn.gelu(y)
    gate = jax.nn.sigmoid(_dot(y, wglu_ref[...]) + bglu_ref[...])
    return y * gate


def _s5_seq_kernel(u_ref, bj_ref, cj_ref, are_ref, aim_ref, d_ref, wglu_ref, bglu_ref,
                   y_ref, sre_ref, sim_ref, bu_ref, st_ref, *, nb, tl, n_chunks, unroll):
    i = pl.program_id(0)
    n_tiles = bu_ref.shape[0]
    tpc = n_tiles // n_chunks
    hpc = tpc // 2
    hw = hpc * LANES
    kw = u_ref.shape[2] // n_chunks

    @pl.when(i == 0)
    def _():
        st_ref[...] = jnp.zeros_like(st_ref)

    for b in range(nb):
        ub = u_ref[b].astype(BF16)
        for j in range(n_chunks):
            bu = jnp.dot(ub[:, j * kw:(j + 1) * kw], bj_ref[j], preferred_element_type=F32)
            for c in range(tpc):
                bu_ref[j * tpc + c, pl.ds(b, tl, stride=nb), :] = bu[:, c * LANES:(c + 1) * LANES]

    for j in range(n_chunks):
        re = slice(j * 2 * hw, j * 2 * hw + hw)
        im = slice(j * 2 * hw + hw, (j + 1) * 2 * hw)
        ar = jnp.broadcast_to(are_ref[:, j * hw:(j + 1) * hw], (nb, hw))
        ai = jnp.broadcast_to(aim_ref[:, j * hw:(j + 1) * hw], (nb, hw))
        re_tiles = [j * tpc + c for c in range(hpc)]
        im_tiles = [j * tpc + hpc + c for c in range(hpc)]

        def body(s, carry, ar=ar, ai=ai, re_tiles=re_tiles, im_tiles=im_tiles):
            xr, xi = carry
            for k in range(unroll):
                r0 = pl.multiple_of((s * unroll + k) * nb, nb)
                br = jnp.concatenate([bu_ref[c, pl.ds(r0, nb), :] for c in re_tiles], axis=-1)
                bi = jnp.concatenate([bu_ref[c, pl.ds(r0, nb), :] for c in im_tiles], axis=-1)
                nr = ar * xr - ai * xi + br
                ni = ar * xi + ai * xr + bi
                for q, c in enumerate(re_tiles):
                    bu_ref[c, pl.ds(r0, nb), :] = nr[:, q * LANES:(q + 1) * LANES]
                for q, c in enumerate(im_tiles):
                    bu_ref[c, pl.ds(r0, nb), :] = ni[:, q * LANES:(q + 1) * LANES]
                xr, xi = nr, ni
            return xr, xi

        xr, xi = lax.fori_loop(0, tl // unroll, body, (st_ref[:, re], st_ref[:, im]))
        st_ref[:, re] = xr
        st_ref[:, im] = xi
        sre_ref[:, j * hw:(j + 1) * hw] = xr
        sim_ref[:, j * hw:(j + 1) * hw] = xi

    for b in range(nb):
        xs = jnp.concatenate([bu_ref[c, pl.ds(b, tl, stride=nb), :] for c in range(n_tiles)],
                             axis=-1)
        y_ref[b] = _s5_output(xs, u_ref[b], cj_ref, d_ref, wglu_ref, bglu_ref, n_chunks)


def _s5_seq(u, bj, cj, a_re, a_im, d_vec, w_glu, b_glu, tl, unroll=4):
    nb, seq_len, sw = u.shape
    n_chunks, _, cw = bj.shape
    ns = a_re.shape[1]
    kern = functools.partial(_s5_seq_kernel, nb=nb, tl=tl, n_chunks=n_chunks, unroll=unroll)
    return pl.pallas_call(
        kern,
        grid=(seq_len // tl,),
        in_specs=[pl.BlockSpec((nb, tl, sw), lambda i: (0, i, 0)),
                  _full_spec(bj.shape), _full_spec(cj.shape), _full_spec(a_re.shape),
                  _full_spec(a_im.shape), _full_spec(d_vec.shape), _full_spec(w_glu.shape),
                  _full_spec(b_glu.shape)],
        out_specs=[pl.BlockSpec((nb, tl, sw), lambda i: (0, i, 0)),
                   _full_spec((nb, ns)), _full_spec((nb, ns))],
        out_shape=[jax.ShapeDtypeStruct((nb, seq_len, sw), F32),
                   jax.ShapeDtypeStruct((nb, ns), F32), jax.ShapeDtypeStruct((nb, ns), F32)],
        scratch_shapes=[pltpu.VMEM((n_chunks * cw // LANES, tl * nb, LANES), F32),
                        pltpu.VMEM((nb, n_chunks * cw), F32)],
        compiler_params=_params(1),
        name="s5_seq",
    )(u, bj, cj, a_re, a_im, d_vec, w_glu, b_glu)


def _s5_step_kernel(u_ref, hre_ref, him_ref, bj_ref, cj_ref, are_ref, aim_ref, d_ref,
                    wglu_ref, bglu_ref, y_ref, sre_ref, sim_ref, *, n_chunks):
    u = u_ref[...]
    ub = u.astype(BF16)
    kw = u.shape[1] // n_chunks
    hw = bj_ref.shape[2] // 2
    cols = []
    for j in range(n_chunks):
        st = slice(j * hw, (j + 1) * hw)
        bu = jnp.dot(ub[:, j * kw:(j + 1) * kw], bj_ref[j], preferred_element_type=F32)
        ar, ai = are_ref[:, st], aim_ref[:, st]
        hr, hi = hre_ref[:, st], him_ref[:, st]
        xr = ar * hr - ai * hi + bu[:, :hw]
        xi = ar * hi + ai * hr + bu[:, hw:]
        sre_ref[:, st] = xr
        sim_ref[:, st] = xi
        cols += [xr, xi]
    xs = jnp.concatenate(cols, axis=-1)
    y_ref[...] = _s5_output(xs, u, cj_ref, d_ref, wglu_ref, bglu_ref, n_chunks)


def _s5_step(u, h_re, h_im, bj, cj, a_re, a_im, d_vec, w_glu, b_glu):
    t, sw = u.shape
    ns = a_re.shape[1]
    args = (u, h_re, h_im, bj, cj, a_re, a_im, d_vec, w_glu, b_glu)
    return pl.pallas_call(
        functools.partial(_s5_step_kernel, n_chunks=bj.shape[0]),
        grid=(1,),
        in_specs=[_full_spec(a.shape) for a in args],
        out_specs=[_full_spec((t, sw)), _full_spec((t, ns)), _full_spec((t, ns))],
        out_shape=[jax.ShapeDtypeStruct((t, sw), F32), jax.ShapeDtypeStruct((t, ns), F32),
                   jax.ShapeDtypeStruct((t, ns), F32)],
        compiler_params=_params(1),
        name="s5_step",
    )(*args)


def _group_reduce(x, lane, width, op):
    n = x.shape[-1]
    s = 1
    while s < width:
        up = pltpu.roll(x, s, 1)
        dn = pltpu.roll(x, n - s, 1)
        x = op(x, jnp.where((lane & s) != 0, up, dn))
        s *= 2
    return x


def _first_max(x, lane, big):
    m = jnp.max(x, axis=-1, keepdims=True)
    return jnp.min(jnp.where(x == m, lane, big), axis=-1, keepdims=True)


def _route(logits, bias):
    rows, n = logits.shape
    lane = lax.broadcasted_iota(jnp.int32, (rows, n), 1)
    lane_f = lane.astype(F32)
    neg = jnp.float32(-jnp.inf)
    big = jnp.float32(n)
    gsz = N_EXPERTS // N_ROUTE_GROUPS
    group_start = (lane - (lane & (gsz - 1))).astype(F32)
    scores = jax.nn.sigmoid(logits)
    biased = scores + bias
    m1 = _group_reduce(biased, lane, gsz, jnp.maximum)
    first = _group_reduce(jnp.where(biased == m1, lane_f, big), lane, gsz, jnp.minimum)
    m2 = _group_reduce(jnp.where(lane_f == first, neg, biased), lane, gsz, jnp.maximum)
    gs = m1 + m2
    gsel = jnp.zeros((rows, n), F32)
    work = gs
    for _ in range(TOPK_ROUTE_GROUPS):
        hit = group_start == _first_max(work, lane_f, big)
        gsel = jnp.where(hit, 1.0, gsel)
        work = jnp.where(hit, neg, work)
    work = jnp.where(gsel > 0.0, biased, neg)
    w = jnp.zeros((rows, n), F32)
    for _ in range(TOP_K):
        hit = lane_f == _first_max(work, lane_f, big)
        w = jnp.where(hit, scores, w)
        work = jnp.where(hit, neg, work)
    return w / jnp.sum(w, axis=-1, keepdims=True) * ROUTED_SCALE


def _split_bf16(x):
    hi = x.astype(BF16)
    lo = (x - hi.astype(F32)).astype(BF16)
    return hi, lo


def _postmix_kernel(x_ref, ya_ref, yb_ref, gt1_ref, sh2_ref, sc2_ref, gt2_ref, g2_ref,
                    wout_ref, rwh_ref, rwl_ref, rb_ref, swg_ref, swu_ref, swd_ref,
                    xacc_ref, h2_ref, wd_ref):
    y = jnp.concatenate([ya_ref[...], yb_ref[...]], axis=-1)
    x1 = x_ref[...] + gt1_ref[...] * _dot(y, wout_ref[...])
    h2 = _rmsnorm(x1, g2_ref[...]) * (1.0 + sc2_ref[...]) + sh2_ref[...]
    hb = h2.astype(BF16)
    h2_ref[...] = hb
    h_hi, h_lo = _split_bf16(h2)
    logits = (jnp.dot(h_hi, rwh_ref[...], preferred_element_type=F32)
              + jnp.dot(h_hi, rwl_ref[...], preferred_element_type=F32)
              + jnp.dot(h_lo, rwh_ref[...], preferred_element_type=F32))
    wd_ref[...] = _route(logits, rb_ref[...])
    hid = jax.nn.silu(jnp.dot(hb, swg_ref[...], preferred_element_type=F32)) * \
        jnp.dot(hb, swu_ref[...], preferred_element_type=F32)
    shared = _dot(hid, swd_ref[...])
    xacc_ref[...] = x1 + gt2_ref[...] * shared


def _postmix(x, ya, yb, gt1, sh2, sc2, gt2, g2, w_out, rw_hi, rw_lo, rb, swg, swu, swd,
             per_row, seq_len, tm):
    t, d = x.shape
    sw = ya.shape[1]
    ne = rw_hi.shape[1]
    mod = _mod_spec(per_row, tm, d, max(seq_len // tm, 1))
    weights = (g2, w_out, rw_hi, rw_lo, rb, swg, swu, swd)
    return pl.pallas_call(
        _postmix_kernel,
        grid=(t // tm,),
        in_specs=[_row_spec(tm, d), _row_spec(tm, sw), _row_spec(tm, sw), mod, mod, mod, mod]
        + [_full_spec(w.shape) for w in weights],
        out_specs=[_row_spec(tm, d), _row_spec(tm, d), _row_spec(tm, ne)],
        out_shape=[jax.ShapeDtypeStruct((t, d), F32), jax.ShapeDtypeStruct((t, d), BF16),
                   jax.ShapeDtypeStruct((t, ne), F32)],
        compiler_params=_params(1),
        name="postmix",
    )(x, ya, yb, gt1, sh2, sc2, gt2, *weights)


def _moe_dense_kernel(h_ref, wd_ref, xacc_ref, gt2_ref, gf_ref, wg_ref, wu_ref, wdn_ref,
                      o_ref, acc_ref, *, final_norm):
    e = pl.program_id(1)

    @pl.when(e == 0)
    def _():
        acc_ref[...] = jnp.zeros_like(acc_ref)

    hb = h_ref[...]
    wd = wd_ref[...]
    lane = lax.broadcasted_iota(jnp.int32, wd.shape, 1)
    col = jnp.sum(jnp.where(lane == e, wd, 0.0), axis=-1, keepdims=True)
    hid = jax.nn.silu(jnp.dot(hb, wg_ref[...], preferred_element_type=F32)) * \
        jnp.dot(hb, wu_ref[...], preferred_element_type=F32)
    acc_ref[...] += col * _dot(hid, wdn_ref[...])

    @pl.when(e == pl.num_programs(1) - 1)
    def _():
        xo = xacc_ref[...] + gt2_ref[...] * acc_ref[...]
        o_ref[...] = _rmsnorm(xo, gf_ref[...]) if final_norm else xo


def _moe_dense(h2, wd, xacc, gt2, g_final, wg, wu, wdn, per_row, seq_len, tm, final_norm):
    t, d = xacc.shape
    ne, _, eh = wg.shape
    nl = wd.shape[1]
    tpb = max(seq_len // tm, 1)
    if per_row:
        mod = pl.BlockSpec((tm, d), lambda i, e: (i, 0))
    else:
        mod = pl.BlockSpec((None, 1, d), lambda i, e: (i // tpb, 0, 0))
    row = lambda w: pl.BlockSpec((tm, w), lambda i, e: (i, 0))
    return pl.pallas_call(
        functools.partial(_moe_dense_kernel, final_norm=final_norm),
        grid=(t // tm, ne),
        in_specs=[row(d), row(nl), row(d), mod,
                  pl.BlockSpec(g_final.shape, lambda i, e: (0, 0)),
                  pl.BlockSpec((None, d, eh), lambda i, e: (e, 0, 0)),
                  pl.BlockSpec((None, d, eh), lambda i, e: (e, 0, 0)),
                  pl.BlockSpec((None, eh, d), lambda i, e: (e, 0, 0))],
        out_specs=row(d),
        out_shape=jax.ShapeDtypeStruct((t, d), F32),
        scratch_shapes=[pltpu.VMEM((tm, d), F32)],
        compiler_params=_params(2),
        name="moe_dense",
    )(h2, wd, xacc, gt2, g_final, wg, wu, wdn)


def _s5_discretize(lam_re, lam_im, log_dt, b_re, b_im, c_re, c_im):
    g, n = lam_re.shape
    hg = b_re.shape[2]
    lr = jnp.minimum(lam_re.astype(F32), -1e-4)
    li = lam_im.astype(F32)
    dt = jnp.exp(log_dt.astype(F32))[:, None]
    mag = jnp.exp(lr * dt)
    abar_r, abar_i = mag * jnp.cos(li * dt), mag * jnp.sin(li * dt)
    den = lr * lr + li * li
    nr, ni = abar_r - 1.0, abar_i
    f_r = (nr * lr + ni * li) / den
    f_i = (ni * lr - nr * li) / den
    br_, bi_ = b_re.astype(F32), b_im.astype(F32)
    bbar_r = f_r[..., None] * br_ - f_i[..., None] * bi_
    bbar_i = f_r[..., None] * bi_ + f_i[..., None] * br_
    gpc = LANES // hg
    n_chunks = g // gpc
    eye = jnp.eye(gpc, dtype=F32)

    def b_chunks(bb):
        bb = bb.reshape(n_chunks, gpc, n, hg)
        m = jnp.einsum('jgnh,gk->jghkn', bb, eye)
        return m.reshape(n_chunks, gpc * hg, gpc * n)

    def c_chunks(cc):
        cc = cc.astype(F32).reshape(n_chunks, gpc, hg, n)
        m = jnp.einsum('jghn,gk->jgnkh', cc, eye)
        return m.reshape(n_chunks, gpc * n, gpc * hg)

    bj = jnp.concatenate([b_chunks(bbar_r), b_chunks(bbar_i)], axis=2).astype(BF16)
    cj = jnp.concatenate([c_chunks(c_re), -c_chunks(c_im)], axis=1).astype(BF16)
    return abar_r.reshape(1, g * n), abar_i.reshape(1, g * n), bj, cj


def kernel(x_prompt, x_sample, c_prompt, c_sample, state_ssm_re, state_ssm_im, g_norm1, g_norm2, w_ada, b_ada, w_in, w_out, ssm_lambda_re, ssm_lambda_im, ssm_log_dt, ssm_b_re, ssm_b_im, ssm_c_re, ssm_c_im, ssm_d, ssm_w_glu, ssm_b_glu, gm_ln_g, gm_ln_b, gm_w_s, gm_b_s, router_w, router_bias, moe_w_gate, moe_w_up, moe_w_down, shared_w_gate, shared_w_up, shared_w_down, g_final):
    bp, seq_len, d = x_prompt.shape
    bs, dec_len, _ = x_sample.shape
    depth = w_ada.shape[0]
    assert dec_len == 1 and seq_len % CHUNK == 0 and bp % SUBLANES == 0
    groups, n_state = ssm_lambda_re.shape[1:]
    sw = groups * SSM_GROUP
    gw = gm_ln_g.shape[1]
    hd = gw // GM_HEADS

    xp = x_prompt.reshape(bp * seq_len, d)
    xs = x_sample.reshape(bs, d)
    c_all = jnp.concatenate([c_prompt, c_sample], axis=0)
    gf = g_final.reshape(1, d)

    p_re, p_im, s_re, s_im, s_v = [], [], [], [], []
    for l in range(depth):
        mod = _ada(c_all, w_ada[l].astype(BF16), b_ada[l].reshape(1, -1))
        mod_p = [m.reshape(bp, 1, d) for m in jnp.split(mod[:bp], 6, axis=-1)]
        mod_s = jnp.split(mod[bp:], 6, axis=-1)

        g1 = g_norm1[l].reshape(1, d)
        g2 = g_norm2[l].reshape(1, d)
        w_in_b = w_in[l].astype(BF16)
        w_out_b = w_out[l].astype(BF16)
        ln_g = gm_ln_g[l].reshape(1, gw)
        ln_b = gm_ln_b[l].reshape(1, gw)
        ws_tril = (gm_w_s[l] * jnp.tril(jnp.ones((CHUNK, CHUNK), F32))[None]).astype(BF16)
        bs_full = jnp.repeat(gm_b_s[l].T, hd, axis=1)
        w0 = jnp.repeat(gm_w_s[l][:, 0, 0], hd).reshape(1, gw)
        b0 = jnp.repeat(gm_b_s[l][:, 0], hd).reshape(1, gw)
        a_re, a_im, bj, cj = _s5_discretize(ssm_lambda_re[l], ssm_lambda_im[l], ssm_log_dt[l],
                                            ssm_b_re[l], ssm_b_im[l], ssm_c_re[l], ssm_c_im[l])
        d_vec = ssm_d[l].reshape(1, sw)
        w_glu = ssm_w_glu[l].astype(BF16)
        b_glu = ssm_b_glu[l].reshape(1, sw)
        pad = LANES - N_EXPERTS
        rw = jnp.pad(router_w[l], ((0, 0), (0, pad)))
        rw_hi = rw.astype(BF16)
        rw_lo = (rw - rw_hi.astype(F32)).astype(BF16)
        rb = jnp.pad(router_bias[l], (0, pad), constant_values=-jnp.inf).reshape(1, LANES)
        swg, swu, swd = (shared_w_gate[l].astype(BF16), shared_w_up[l].astype(BF16),
                         shared_w_down[l].astype(BF16))
        ewg, ewu, ewd = (moe_w_gate[l].astype(BF16), moe_w_up[l].astype(BF16),
                         moe_w_down[l].astype(BF16))

        tm_row = min(ROW_TILE, seq_len)
        ua, yb = _premix_seq(xp, mod_p[0], mod_p[1], g1, w_in_b, ln_g, ln_b, ws_tril, bs_full,
                             seq_len, tm=tm_row)
        ya, hpr, hpi = _s5_seq(ua.reshape(bp, seq_len, sw), bj, cj, a_re, a_im, d_vec,
                               w_glu, b_glu, tl=min(SCAN_TILE, seq_len))
        xacc, h2, wd = _postmix(xp, ya.reshape(bp * seq_len, sw), yb, mod_p[2], mod_p[3],
                                mod_p[4], mod_p[5], g2, w_out_b, rw_hi, rw_lo, rb, swg, swu, swd,
                                per_row=False, seq_len=seq_len, tm=tm_row)
        last = l == depth - 1
        xp = _moe_dense(h2, wd, xacc, mod_p[5], gf, ewg, ewu, ewd, per_row=False,
                        seq_len=seq_len, tm=min(MOE_TILE, seq_len), final_norm=last)
        p_re.append(hpr.reshape(bp, groups, n_state))
        p_im.append(hpi.reshape(bp, groups, n_state))

        ua, yb, vn = _premix_step(xs, mod_s[0], mod_s[1], g1, w_in_b, ln_g, ln_b, w0, b0)
        ya, hsr, hsi = _s5_step(ua, state_ssm_re[l].reshape(bs, groups * n_state),
                                state_ssm_im[l].reshape(bs, groups * n_state),
                                bj, cj, a_re, a_im, d_vec, w_glu, b_glu)
        xacc, h2, wd = _postmix(xs, ya, yb, mod_s[2], mod_s[3], mod_s[4], mod_s[5], g2, w_out_b,
                                rw_hi, rw_lo, rb, swg, swu, swd, per_row=True, seq_len=1, tm=bs)
        xs = _moe_dense(h2, wd, xacc, mod_s[5], gf, ewg, ewu, ewd, per_row=True, seq_len=1,
                        tm=bs, final_norm=last)
        s_re.append(hsr.reshape(bs, groups, n_state))
        s_im.append(hsi.reshape(bs, groups, n_state))
        s_v.append(vn.reshape(bs, 1, gw))

    y_prompt = xp.reshape(bp, seq_len, d)
    y_sample = xs.reshape(bs, 1, d)
    return (y_prompt, y_sample, jnp.stack(p_re), jnp.stack(p_im), jnp.stack(s_re),
            jnp.stack(s_im), jnp.stack(s_v))
```

```python
import functools

import jax
import jax.numpy as jnp
from jax import lax
from jax.experimental import pallas as pl
from jax.experimental.pallas import tpu as pltpu

F32 = jnp.float32
BF16 = jnp.bfloat16

EPS = 1e-6
SSM_GROUP = 16
SSM_STATE = 64
GM_HEADS = 4
CHUNK = 128
N_EXPERTS = 64
TOP_K = 8
N_ROUTE_GROUPS = 8
TOPK_ROUTE_GROUPS = 4
ROUTED_SCALE = 2.5

LANES = 128
SUBLANES = 8
VMEM_LIMIT = 48 * 1024 * 1024
ROW_TILE = 512
SCAN_TILE = 128
MOE_GROUP = 2048


def _params(n_axes):
    return pltpu.CompilerParams(dimension_semantics=("arbitrary",) * n_axes,
                                vmem_limit_bytes=VMEM_LIMIT)


def _dot(a, b):
    return jnp.dot(a.astype(BF16), b.astype(BF16), preferred_element_type=F32)


def _rmsnorm(x, g):
    ms = jnp.mean(x * x, axis=-1, keepdims=True)
    return x * lax.rsqrt(ms + EPS) * g


def _ada_kernel(c_ref, w_ref, b_ref, o_ref):
    c = c_ref[...]
    o_ref[...] = _dot(jax.nn.silu(c), w_ref[...]) + b_ref[...]


def _ada(c, w, b, tn=512):
    m, d = c.shape
    n = w.shape[1]
    return pl.pallas_call(
        _ada_kernel,
        grid=(n // tn,),
        in_specs=[pl.BlockSpec((m, d), lambda j: (0, 0)),
                  pl.BlockSpec((d, tn), lambda j: (0, j)),
                  pl.BlockSpec((1, tn), lambda j: (0, j))],
        out_specs=pl.BlockSpec((m, tn), lambda j: (0, j)),
        out_shape=jax.ShapeDtypeStruct((m, n), F32),
        compiler_params=_params(1),
        name="ada_mod",
    )(c, w, b)


def _premix_common(x_ref, sh_ref, sc_ref, g_ref, win_ref, lng_ref, lnb_ref, sw):
    x = x_ref[...]
    h = _rmsnorm(x, g_ref[...]) * (1.0 + sc_ref[...]) + sh_ref[...]
    proj = _dot(h, win_ref[...])
    u_a = proj[:, :sw]
    z = jax.nn.gelu(proj[:, sw:])
    gw = z.shape[1] // 2
    u_b = z[:, :gw]
    v_b = z[:, gw:]
    hd = gw // GM_HEADS
    parts = []
    for k in range(GM_HEADS):
        vh = v_b[:, k * hd:(k + 1) * hd]
        mu = jnp.mean(vh, axis=-1, keepdims=True)
        dv = vh - mu
        var = jnp.mean(dv * dv, axis=-1, keepdims=True)
        parts.append(dv * lax.rsqrt(var + EPS))
    vn = jnp.concatenate(parts, axis=-1) * lng_ref[...] + lnb_ref[...]
    return u_a, u_b, vn


def _premix_seq_kernel(x_ref, sh_ref, sc_ref, g_ref, win_ref, lng_ref, lnb_ref,
                       ws_ref, bs_ref, ua_ref, yb_ref, *, sw):
    u_a, u_b, vn = _premix_common(x_ref, sh_ref, sc_ref, g_ref, win_ref, lng_ref, lnb_ref, sw)
    ua_ref[...] = u_a
    tm, gw = u_b.shape
    hd = gw // GM_HEADS
    vnb = vn.astype(BF16)
    for c in range(tm // CHUNK):
        rows = slice(c * CHUNK, (c + 1) * CHUNK)
        for k in range(GM_HEADS):
            cols = slice(k * hd, (k + 1) * hd)
            s = jnp.dot(ws_ref[k], vnb[rows, cols], preferred_element_type=F32)
            yb_ref[rows, cols] = u_b[rows, cols] * (s + bs_ref[:, cols])


def _premix_step_kernel(x_ref, sh_ref, sc_ref, g_ref, win_ref, lng_ref, lnb_ref,
                        w0_ref, b0_ref, ua_ref, yb_ref, vn_ref, *, sw):
    u_a, u_b, vn = _premix_common(x_ref, sh_ref, sc_ref, g_ref, win_ref, lng_ref, lnb_ref, sw)
    ua_ref[...] = u_a
    vn_ref[...] = vn
    yb_ref[...] = u_b * (vn * w0_ref[...] + b0_ref[...])


def _row_spec(tm, d):
    return pl.BlockSpec((tm, d), lambda i: (i, 0))


def _full_spec(shape):
    nd = len(shape)
    return pl.BlockSpec(shape, lambda i: (0,) * nd)


def _mod_spec(per_row, tm, d, tiles_per_batch):
    if per_row:
        return pl.BlockSpec((tm, d), lambda i: (i, 0))
    return pl.BlockSpec((None, 1, d), lambda i: (i // tiles_per_batch, 0, 0))


def _premix_seq(x, sh, sc, g, w_in, ln_g, ln_b, ws_tril, bs_full, seq_len, tm):
    t, d = x.shape
    sw = ln_g.shape[1]
    tpb = seq_len // tm
    mod = _mod_spec(False, tm, d, tpb)
    return pl.pallas_call(
        functools.partial(_premix_seq_kernel, sw=sw),
        grid=(t // tm,),
        in_specs=[_row_spec(tm, d), mod, mod, _full_spec(g.shape), _full_spec(w_in.shape),
                  _full_spec(ln_g.shape), _full_spec(ln_b.shape), _full_spec(ws_tril.shape),
                  _full_spec(bs_full.shape)],
        out_specs=[_row_spec(tm, sw), _row_spec(tm, sw)],
        out_shape=[jax.ShapeDtypeStruct((t, sw), F32), jax.ShapeDtypeStruct((t, sw), F32)],
        compiler_params=_params(1),
        name="premix_seq",
    )(x, sh, sc, g, w_in, ln_g, ln_b, ws_tril, bs_full)


def _premix_step(x, sh, sc, g, w_in, ln_g, ln_b, w0, b0):
    t, d = x.shape
    sw = ln_g.shape[1]
    mod = _mod_spec(True, t, d, 1)
    return pl.pallas_call(
        functools.partial(_premix_step_kernel, sw=sw),
        grid=(1,),
        in_specs=[_row_spec(t, d), mod, mod, _full_spec(g.shape), _full_spec(w_in.shape),
                  _full_spec(ln_g.shape), _full_spec(ln_b.shape), _full_spec(w0.shape),
                  _full_spec(b0.shape)],
        out_specs=[_row_spec(t, sw)] * 3,
        out_shape=[jax.ShapeDtypeStruct((t, sw), F32)] * 3,
        compiler_params=_params(1),
        name="premix_step",
    )(x, sh, sc, g, w_in, ln_g, ln_b, w0, b0)


def _s5_output(xs, u, cj_ref, d_ref, wglu_ref, bglu_ref, n_chunks):
    cw = xs.shape[1] // n_chunks
    xb = xs.astype(BF16)
    ys = [jnp.dot(xb[:, j * cw:(j + 1) * cw], cj_ref[j], preferred_element_type=F32)
          for j in range(n_chunks)]
    y = jnp.concatenate(ys, axis=-1) + d_ref[...] * u
    y = jax.nn.gelu(y)
    gate = jax.nn.sigmoid(_dot(y, wglu_ref[...]) + bglu_ref[...])
    return y * gate


def _s5_seq_kernel(u_ref, bj_ref, cj_ref, are_ref, aim_ref, d_ref, wglu_ref, bglu_ref,
                   y_ref, sre_ref, sim_ref, bu_ref, st_ref, *, nb, tl, n_chunks, unroll):
    i = pl.program_id(0)
    n_tiles = bu_ref.shape[0]
    tpc = n_tiles // n_chunks
    hpc = tpc // 2
    hw = hpc * LANES
    kw = u_ref.shape[2] // n_chunks

    @pl.when(i == 0)
    def _():
        st_ref[...] = jnp.zeros_like(st_ref)

    for b in range(nb):
        ub = u_ref[b].astype(BF16)
        for j in range(n_chunks):
            bu = jnp.dot(ub[:, j * kw:(j + 1) * kw], bj_ref[j], preferred_element_type=F32)
            for c in range(tpc):
                bu_ref[j * tpc + c, pl.ds(b, tl, stride=nb), :] = bu[:, c * LANES:(c + 1) * LANES]

    for j in range(n_chunks):
        re = slice(j * 2 * hw, j * 2 * hw + hw)
        im = slice(j * 2 * hw + hw, (j + 1) * 2 * hw)
        ar = jnp.broadcast_to(are_ref[:, j * hw:(j + 1) * hw], (nb, hw))
        ai = jnp.broadcast_to(aim_ref[:, j * hw:(j + 1) * hw], (nb, hw))
        re_tiles = [j * tpc + c for c in range(hpc)]
        im_tiles = [j * tpc + hpc + c for c in range(hpc)]

        def body(s, carry, ar=ar, ai=ai, re_tiles=re_tiles, im_tiles=im_tiles):
            xr, xi = carry
            for k in range(unroll):
                r0 = pl.multiple_of((s * unroll + k) * nb, nb)
                br = jnp.concatenate([bu_ref[c, pl.ds(r0, nb), :] for c in re_tiles], axis=-1)
                bi = jnp.concatenate([bu_ref[c, pl.ds(r0, nb), :] for c in im_tiles], axis=-1)
                nr = ar * xr - ai * xi + br
                ni = ar * xi + ai * xr + bi
                for q, c in enumerate(re_tiles):
                    bu_ref[c, pl.ds(r0, nb), :] = nr[:, q * LANES:(q + 1) * LANES]
                for q, c in enumerate(im_tiles):
                    bu_ref[c, pl.ds(r0, nb), :] = ni[:, q * LANES:(q + 1) * LANES]
                xr, xi = nr, ni
            return xr, xi

        xr, xi = lax.fori_loop(0, tl // unroll, body, (st_ref[:, re], st_ref[:, im]))
        st_ref[:, re] = xr
        st_ref[:, im] = xi
        sre_ref[:, j * hw:(j + 1) * hw] = xr
        sim_ref[:, j * hw:(j + 1) * hw] = xi

    for b in range(nb):
        xs = jnp.concatenate([bu_ref[c, pl.ds(b, tl, stride=nb), :] for c in range(n_tiles)],
                             axis=-1)
        y_ref[b] = _s5_output(xs, u_ref[b], cj_ref, d_ref, wglu_ref, bglu_ref, n_chunks)


def _s5_seq(u, bj, cj, a_re, a_im, d_vec, w_glu, b_glu, tl, unroll=4):
    nb, seq_len, sw = u.shape
    n_chunks, _, cw = bj.shape
    ns = a_re.shape[1]
    kern = functools.partial(_s5_seq_kernel, nb=nb, tl=tl, n_chunks=n_chunks, unroll=unroll)
    return pl.pallas_call(
        kern,
        grid=(seq_len // tl,),
        in_specs=[pl.BlockSpec((nb, tl, sw), lambda i: (0, i, 0)),
                  _full_spec(bj.shape), _full_spec(cj.shape), _full_spec(a_re.shape),
                  _full_spec(a_im.shape), _full_spec(d_vec.shape), _full_spec(w_glu.shape),
                  _full_spec(b_glu.shape)],
        out_specs=[pl.BlockSpec((nb, tl, sw), lambda i: (0, i, 0)),
                   _full_spec((nb, ns)), _full_spec((nb, ns))],
        out_shape=[jax.ShapeDtypeStruct((nb, seq_len, sw), F32),
                   jax.ShapeDtypeStruct((nb, ns), F32), jax.ShapeDtypeStruct((nb, ns), F32)],
        scratch_shapes=[pltpu.VMEM((n_chunks * cw // LANES, tl * nb, LANES), F32),
                        pltpu.VMEM((nb, n_chunks * cw), F32)],
        compiler_params=_params(1),
        name="s5_seq",
    )(u, bj, cj, a_re, a_im, d_vec, w_glu, b_glu)


def _s5_step_kernel(u_ref, hre_ref, him_ref, bj_ref, cj_ref, are_ref, aim_ref, d_ref,
                    wglu_ref, bglu_ref, y_ref, sre_ref, sim_ref, *, n_chunks):
    u = u_ref[...]
    ub = u.astype(BF16)
    kw = u.shape[1] // n_chunks
    hw = bj_ref.shape[2] // 2
    cols = []
    for j in range(n_chunks):
        st = slice(j * hw, (j + 1) * hw)
        bu = jnp.dot(ub[:, j * kw:(j + 1) * kw], bj_ref[j], preferred_element_type=F32)
        ar, ai = are_ref[:, st], aim_ref[:, st]
        hr, hi = hre_ref[:, st], him_ref[:, st]
        xr = ar * hr - ai * hi + bu[:, :hw]
        xi = ar * hi + ai * hr + bu[:, hw:]
        sre_ref[:, st] = xr
        sim_ref[:, st] = xi
        cols += [xr, xi]
    xs = jnp.concatenate(cols, axis=-1)
    y_ref[...] = _s5_output(xs, u, cj_ref, d_ref, wglu_ref, bglu_ref, n_chunks)


def _s5_step(u, h_re, h_im, bj, cj, a_re, a_im, d_vec, w_glu, b_glu):
    t, sw = u.shape
    ns = a_re.shape[1]
    args = (u, h_re, h_im, bj, cj, a_re, a_im, d_vec, w_glu, b_glu)
    return pl.pallas_call(
        functools.partial(_s5_step_kernel, n_chunks=bj.shape[0]),
        grid=(1,),
        in_specs=[_full_spec(a.shape) for a in args],
        out_specs=[_full_spec((t, sw)), _full_spec((t, ns)), _full_spec((t, ns))],
        out_shape=[jax.ShapeDtypeStruct((t, sw), F32), jax.ShapeDtypeStruct((t, ns), F32),
                   jax.ShapeDtypeStruct((t, ns), F32)],
        compiler_params=_params(1),
        name="s5_step",
    )(*args)


def _group_reduce(x, lane, width, op):
    n = x.shape[-1]
    s = 1
    while s < width:
        up = pltpu.roll(x, s, 1)
        dn = pltpu.roll(x, n - s, 1)
        x = op(x, jnp.where((lane & s) != 0, up, dn))
        s *= 2
    return x


def _first_max(x, lane, big):
    m = jnp.max(x, axis=-1, keepdims=True)
    return jnp.min(jnp.where(x == m, lane, big), axis=-1, keepdims=True)


def _route(logits, bias):
    rows, n = logits.shape
    lane = lax.broadcasted_iota(jnp.int32, (rows, n), 1)
    lane_f = lane.astype(F32)
    neg = jnp.float32(-jnp.inf)
    big = jnp.float32(n)
    gsz = N_EXPERTS // N_ROUTE_GROUPS
    group_start = (lane - (lane & (gsz - 1))).astype(F32)
    scores = jax.nn.sigmoid(logits)
    biased = scores + bias
    m1 = _group_reduce(biased, lane, gsz, jnp.maximum)
    first = _group_reduce(jnp.where(biased == m1, lane_f, big), lane, gsz, jnp.minimum)
    m2 = _group_reduce(jnp.where(lane_f == first, neg, biased), lane, gsz, jnp.maximum)
    gs = m1 + m2
    gsel = jnp.zeros((rows, n), F32)
    work = gs
    for _ in range(TOPK_ROUTE_GROUPS):
        hit = group_start == _first_max(work, lane_f, big)
        gsel = jnp.where(hit, 1.0, gsel)
        work = jnp.where(hit, neg, work)
    work = jnp.where(gsel > 0.0, biased, neg)
    w = jnp.zeros((rows, n), F32)
    for _ in range(TOP_K):
        hit = lane_f == _first_max(work, lane_f, big)
        w = jnp.where(hit, scores, w)
        work = jnp.where(hit, neg, work)
    return w / jnp.sum(w, axis=-1, keepdims=True) * ROUTED_SCALE


def _split_bf16(x):
    hi = x.astype(BF16)
    lo = (x - hi.astype(F32)).astype(BF16)
    return hi, lo


def _postmix_kernel(x_ref, ya_ref, yb_ref, gt1_ref, sh2_ref, sc2_ref, gt2_ref, g2_ref,
                    wout_ref, rwh_ref, rwl_ref, rb_ref, swg_ref, swu_ref, swd_ref,
                    xacc_ref, h2_ref, wd_ref, *, token_tiles):
    y = jnp.concatenate([ya_ref[...], yb_ref[...]], axis=-1)
    x1 = x_ref[...] + gt1_ref[...] * _dot(y, wout_ref[...])
    h2 = _rmsnorm(x1, g2_ref[...]) * (1.0 + sc2_ref[...]) + sh2_ref[...]
    hb = h2.astype(BF16)
    if token_tiles:
        tm, d = h2.shape
        nj = d // LANES
        for j in range(nj):
            h2_ref[pl.ds(j, tm, stride=nj), :] = h2[:, j * LANES:(j + 1) * LANES]
    else:
        h2_ref[...] = hb
    h_hi, h_lo = _split_bf16(h2)
    logits = (jnp.dot(h_hi, rwh_ref[...], preferred_element_type=F32)
              + jnp.dot(h_hi, rwl_ref[...], preferred_element_type=F32)
              + jnp.dot(h_lo, rwh_ref[...], preferred_element_type=F32))
    wd_ref[...] = _route(logits, rb_ref[...])
    hid = jax.nn.silu(jnp.dot(hb, swg_ref[...], preferred_element_type=F32)) * \
        jnp.dot(hb, swu_ref[...], preferred_element_type=F32)
    shared = _dot(hid, swd_ref[...])
    xacc_ref[...] = x1 + gt2_ref[...] * shared


def _postmix(x, ya, yb, gt1, sh2, sc2, gt2, g2, w_out, rw_hi, rw_lo, rb, swg, swu, swd,
             per_row, seq_len, tm, token_tiles):
    t, d = x.shape
    sw = ya.shape[1]
    ne = rw_hi.shape[1]
    nj = d // LANES
    mod = _mod_spec(per_row, tm, d, max(seq_len // tm, 1))
    weights = (g2, w_out, rw_hi, rw_lo, rb, swg, swu, swd)
    if token_tiles:
        h2_spec, h2_shape = _row_spec(tm * nj, LANES), jax.ShapeDtypeStruct((t * nj, LANES), F32)
    else:
        h2_spec, h2_shape = _row_spec(tm, d), jax.ShapeDtypeStruct((t, d), BF16)
    return pl.pallas_call(
        functools.partial(_postmix_kernel, token_tiles=token_tiles),
        grid=(t // tm,),
        in_specs=[_row_spec(tm, d), _row_spec(tm, sw), _row_spec(tm, sw), mod, mod, mod, mod]
        + [_full_spec(w.shape) for w in weights],
        out_specs=[_row_spec(tm, d), h2_spec, _row_spec(tm, ne)],
        out_shape=[jax.ShapeDtypeStruct((t, d), F32), h2_shape,
                   jax.ShapeDtypeStruct((t, ne), F32)],
        compiler_params=_params(1),
        name="postmix",
    )(x, ya, yb, gt1, sh2, sc2, gt2, *weights)


def _moe_dense_kernel(h_ref, wd_ref, xacc_ref, gt2_ref, gf_ref, wg_ref, wu_ref, wdn_ref,
                      o_ref, acc_ref, *, final_norm):
    e = pl.program_id(1)

    @pl.when(e == 0)
    def _():
        acc_ref[...] = jnp.zeros_like(acc_ref)

    hb = h_ref[...]
    wd = wd_ref[...]
    lane = lax.broadcasted_iota(jnp.int32, wd.shape, 1)
    col = jnp.sum(jnp.where(lane == e, wd, 0.0), axis=-1, keepdims=True)
    hid = jax.nn.silu(jnp.dot(hb, wg_ref[...], preferred_element_type=F32)) * \
        jnp.dot(hb, wu_ref[...], preferred_element_type=F32)
    acc_ref[...] += col * _dot(hid, wdn_ref[...])

    @pl.when(e == pl.num_programs(1) - 1)
    def _():
        xo = xacc_ref[...] + gt2_ref[...] * acc_ref[...]
        o_ref[...] = _rmsnorm(xo, gf_ref[...]) if final_norm else xo


def _moe_dense(h2, wd, xacc, gt2, g_final, wg, wu, wdn, per_row, seq_len, tm, final_norm):
    t, d = xacc.shape
    ne, _, eh = wg.shape
    nl = wd.shape[1]
    tpb = max(seq_len // tm, 1)
    if per_row:
        mod = pl.BlockSpec((tm, d), lambda i, e: (i, 0))
    else:
        mod = pl.BlockSpec((None, 1, d), lambda i, e: (i // tpb, 0, 0))
    row = lambda w: pl.BlockSpec((tm, w), lambda i, e: (i, 0))
    return pl.pallas_call(
        functools.partial(_moe_dense_kernel, final_norm=final_norm),
        grid=(t // tm, ne),
        in_specs=[row(d), row(nl), row(d), mod,
                  pl.BlockSpec(g_final.shape, lambda i, e: (0, 0)),
                  pl.BlockSpec((None, d, eh), lambda i, e: (e, 0, 0)),
                  pl.BlockSpec((None, d, eh), lambda i, e: (e, 0, 0)),
                  pl.BlockSpec((None, eh, d), lambda i, e: (e, 0, 0))],
        out_specs=row(d),
        out_shape=jax.ShapeDtypeStruct((t, d), F32),
        scratch_shapes=[pltpu.VMEM((tm, d), F32)],
        compiler_params=_params(2),
        name="moe_dense",
    )(h2, wd, xacc, gt2, g_final, wg, wu, wdn)


TOKEN_BITS = 12


def _route_plan_kernel(wd_ref, list_ref, wl_ref, cnt_ref, *, tg):
    wt = wd_ref[...].T[:N_EXPERTS]
    sel = wt > 0.0
    lane = lax.broadcasted_iota(jnp.int32, wt.shape, 1)
    c = sel.astype(jnp.int32)
    s = 1
    while s < tg:
        c = c + jnp.where(lane >= s, pltpu.roll(c, s, 1), 0)
        s *= 2
    cnt_ref[...] = jnp.broadcast_to(c[:, tg - 1:tg], cnt_ref.shape)
    dist = lane - (c - 1)
    key = jnp.where(sel, (dist << TOKEN_BITS) | lane, -1)
    w = jnp.where(sel, wt, 0.0)
    s = 1
    while s < tg:
        leaving = (key >= 0) & (((key >> TOKEN_BITS) & s) != 0)
        key_in = pltpu.roll(key, tg - s, 1)
        w_in = pltpu.roll(w, tg - s, 1)
        arriving = (key_in >= 0) & (((key_in >> TOKEN_BITS) & s) != 0)
        key = jnp.where(arriving, key_in, jnp.where(leaving, -1, key))
        w = jnp.where(arriving, w_in, jnp.where(leaving, 0.0, w))
        s *= 2
    list_ref[...] = jnp.where(key >= 0, key & ((1 << TOKEN_BITS) - 1), tg)
    wl_ref[...] = w


def _route_plan(wd, tg):
    t, nl = wd.shape
    assert tg <= (1 << TOKEN_BITS) and t % tg == 0
    g = t // tg
    grp = lambda w: pl.BlockSpec((None, N_EXPERTS, w), lambda i: (i, 0, 0))
    return pl.pallas_call(
        functools.partial(_route_plan_kernel, tg=tg),
        grid=(g,),
        in_specs=[pl.BlockSpec((tg, nl), lambda i: (i, 0))],
        out_specs=[grp(tg), grp(tg), grp(LANES)],
        out_shape=[jax.ShapeDtypeStruct((g, N_EXPERTS, tg), jnp.int32),
                   jax.ShapeDtypeStruct((g, N_EXPERTS, tg), F32),
                   jax.ShapeDtypeStruct((g, N_EXPERTS, LANES), jnp.int32)],
        compiler_params=_params(1),
        name="route_plan",
    )(wd)


def _moe_sparse_kernel(cnt_ref, h_ref, list_ref, wl_ref, wg_ref, wu_ref, wdn_ref, o_ref,
                       xt_ref, yt_ref, *, tg, bm, nj, batch):
    g = pl.program_id(0)
    e = pl.program_id(1)
    pitch = bm + 1

    @pl.when(e == 0)
    def _():
        o_ref[...] = jnp.zeros_like(o_ref)

    n = cnt_ref[g * pl.num_programs(1) + e]

    def block(b, carry):
        base = b * bm
        for m in range(bm):
            tok = jnp.minimum(list_ref[0, base + m], tg - 1)
            r0 = pl.multiple_of(tok * nj, nj)
            xt_ref[pl.ds(m, nj, stride=pitch), :] = h_ref[pl.ds(r0, nj), :]
        x = jnp.concatenate([xt_ref[j * pitch:j * pitch + bm, :] for j in range(nj)], axis=-1)
        xb = x.astype(BF16)
        hid = jax.nn.silu(jnp.dot(xb, wg_ref[...], preferred_element_type=F32)) * \
            jnp.dot(xb, wu_ref[...], preferred_element_type=F32)
        y = _dot(hid, wdn_ref[...])
        for j in range(nj):
            yt_ref[j * pitch:j * pitch + bm, :] = y[:, j * LANES:(j + 1) * LANES]
        for m0 in range(0, bm, batch):
            rows, vals = [], []
            for m in range(m0, m0 + batch):
                r0 = pl.multiple_of(list_ref[0, base + m] * nj, nj)
                w = wl_ref[0, base + m]
                rows.append(r0)
                vals.append(o_ref[pl.ds(r0, nj), :] + w * yt_ref[pl.ds(m, nj, stride=pitch), :])
            for r0, v in zip(rows, vals):
                o_ref[pl.ds(r0, nj), :] = v
        return carry

    lax.fori_loop(0, (n + bm - 1) // bm, block, 0)


def _moe_sparse(cnt, h2t, lst, wl, wg, wu, wdn, tg, bm=128, batch=8):
    ne, d, eh = wg.shape
    nj = d // LANES
    assert nj == SUBLANES
    g = h2t.shape[0] // (tg * nj)
    kern = functools.partial(_moe_sparse_kernel, tg=tg, bm=bm, nj=nj, batch=batch)
    slot_row = pl.BlockSpec((None, 1, tg), lambda gi, ei, c: (gi * ne + ei, 0, 0),
                            memory_space=pltpu.SMEM)
    grid_spec = pltpu.PrefetchScalarGridSpec(
        num_scalar_prefetch=1,
        grid=(g, ne),
        in_specs=[pl.BlockSpec((tg * nj, LANES), lambda gi, ei, c: (gi, 0)),
                  slot_row, slot_row,
                  pl.BlockSpec((None, d, eh), lambda gi, ei, c: (ei, 0, 0)),
                  pl.BlockSpec((None, d, eh), lambda gi, ei, c: (ei, 0, 0)),
                  pl.BlockSpec((None, eh, d), lambda gi, ei, c: (ei, 0, 0))],
        out_specs=pl.BlockSpec((None, (tg + 1) * nj, LANES), lambda gi, ei, c: (gi, 0, 0)),
        scratch_shapes=[pltpu.VMEM((nj * (bm + 1), LANES), F32),
                        pltpu.VMEM((nj * (bm + 1), LANES), F32)],
    )
    return pl.pallas_call(
        kern,
        grid_spec=grid_spec,
        out_shape=jax.ShapeDtypeStruct((g, (tg + 1) * nj, LANES), F32),
        compiler_params=_params(2),
        name="moe_sparse",
    )(cnt, h2t, lst, wl, wg, wu, wdn)


def _finalize_kernel(r_ref, xacc_ref, gt2_ref, gf_ref, o_ref, *, final_norm):
    tm, d = xacc_ref.shape
    nj = d // LANES
    routed = jnp.concatenate([r_ref[pl.ds(j, tm, stride=nj), :] for j in range(nj)], axis=-1)
    xo = xacc_ref[...] + gt2_ref[...] * routed
    o_ref[...] = _rmsnorm(xo, gf_ref[...]) if final_norm else xo


def _finalize(routed, xacc, gt2, g_final, tg, seq_len, tm, final_norm):
    t, d = xacc.shape
    nj = d // LANES
    tpg = tg // tm
    return pl.pallas_call(
        functools.partial(_finalize_kernel, final_norm=final_norm),
        grid=(t // tm,),
        in_specs=[pl.BlockSpec((None, tm * nj, LANES), lambda i: (i // tpg, i % tpg, 0)),
                  _row_spec(tm, d), _mod_spec(False, tm, d, seq_len // tm),
                  _full_spec(g_final.shape)],
        out_specs=_row_spec(tm, d),
        out_shape=jax.ShapeDtypeStruct((t, d), F32),
        compiler_params=_params(1),
        name="finalize",
    )(routed, xacc, gt2, g_final)


def _s5_discretize(lam_re, lam_im, log_dt, b_re, b_im, c_re, c_im):
    g, n = lam_re.shape
    hg = b_re.shape[2]
    lr = jnp.minimum(lam_re.astype(F32), -1e-4)
    li = lam_im.astype(F32)
    dt = jnp.exp(log_dt.astype(F32))[:, None]
    mag = jnp.exp(lr * dt)
    abar_r, abar_i = mag * jnp.cos(li * dt), mag * jnp.sin(li * dt)
    den = lr * lr + li * li
    nr, ni = abar_r - 1.0, abar_i
    f_r = (nr * lr + ni * li) / den
    f_i = (ni * lr - nr * li) / den
    br_, bi_ = b_re.astype(F32), b_im.astype(F32)
    bbar_r = f_r[..., None] * br_ - f_i[..., None] * bi_
    bbar_i = f_r[..., None] * bi_ + f_i[..., None] * br_
    gpc = LANES // hg
    n_chunks = g // gpc
    eye = jnp.eye(gpc, dtype=F32)

    def b_chunks(bb):
        bb = bb.reshape(n_chunks, gpc, n, hg)
        m = jnp.einsum('jgnh,gk->jghkn', bb, eye)
        return m.reshape(n_chunks, gpc * hg, gpc * n)

    def c_chunks(cc):
        cc = cc.astype(F32).reshape(n_chunks, gpc, hg, n)
        m = jnp.einsum('jghn,gk->jgnkh', cc, eye)
        return m.reshape(n_chunks, gpc * n, gpc * hg)

    bj = jnp.concatenate([b_chunks(bbar_r), b_chunks(bbar_i)], axis=2).astype(BF16)
    cj = jnp.concatenate([c_chunks(c_re), -c_chunks(c_im)], axis=1).astype(BF16)
    return abar_r.reshape(1, g * n), abar_i.reshape(1, g * n), bj, cj


def kernel(x_prompt, x_sample, c_prompt, c_sample, state_ssm_re, state_ssm_im, g_norm1, g_norm2, w_ada, b_ada, w_in, w_out, ssm_lambda_re, ssm_lambda_im, ssm_log_dt, ssm_b_re, ssm_b_im, ssm_c_re, ssm_c_im, ssm_d, ssm_w_glu, ssm_b_glu, gm_ln_g, gm_ln_b, gm_w_s, gm_b_s, router_w, router_bias, moe_w_gate, moe_w_up, moe_w_down, shared_w_gate, shared_w_up, shared_w_down, g_final):
    bp, seq_len, d = x_prompt.shape
    bs, dec_len, _ = x_sample.shape
    depth = w_ada.shape[0]
    assert dec_len == 1 and seq_len % CHUNK == 0 and bp % SUBLANES == 0
    groups, n_state = ssm_lambda_re.shape[1:]
    sw = groups * SSM_GROUP
    gw = gm_ln_g.shape[1]
    hd = gw // GM_HEADS

    xp = x_prompt.reshape(bp * seq_len, d)
    xs = x_sample.reshape(bs, d)
    c_all = jnp.concatenate([c_prompt, c_sample], axis=0)
    gf = g_final.reshape(1, d)

    p_re, p_im, s_re, s_im, s_v = [], [], [], [], []
    for l in range(depth):
        mod = _ada(c_all, w_ada[l].astype(BF16), b_ada[l].reshape(1, -1))
        mod_p = [m.reshape(bp, 1, d) for m in jnp.split(mod[:bp], 6, axis=-1)]
        mod_s = jnp.split(mod[bp:], 6, axis=-1)

        g1 = g_norm1[l].reshape(1, d)
        g2 = g_norm2[l].reshape(1, d)
        w_in_b = w_in[l].astype(BF16)
        w_out_b = w_out[l].astype(BF16)
        ln_g = gm_ln_g[l].reshape(1, gw)
        ln_b = gm_ln_b[l].reshape(1, gw)
        ws_tril = (gm_w_s[l] * jnp.tril(jnp.ones((CHUNK, CHUNK), F32))[None]).astype(BF16)
        bs_full = jnp.repeat(gm_b_s[l].T, hd, axis=1)
        w0 = jnp.repeat(gm_w_s[l][:, 0, 0], hd).reshape(1, gw)
        b0 = jnp.repeat(gm_b_s[l][:, 0], hd).reshape(1, gw)
        a_re, a_im, bj, cj = _s5_discretize(ssm_lambda_re[l], ssm_lambda_im[l], ssm_log_dt[l],
                                            ssm_b_re[l], ssm_b_im[l], ssm_c_re[l], ssm_c_im[l])
        d_vec = ssm_d[l].reshape(1, sw)
        w_glu = ssm_w_glu[l].astype(BF16)
        b_glu = ssm_b_glu[l].reshape(1, sw)
        pad = LANES - N_EXPERTS
        rw = jnp.pad(router_w[l], ((0, 0), (0, pad)))
        rw_hi = rw.astype(BF16)
        rw_lo = (rw - rw_hi.astype(F32)).astype(BF16)
        rb = jnp.pad(router_bias[l], (0, pad), constant_values=-jnp.inf).reshape(1, LANES)
        swg, swu, swd = (shared_w_gate[l].astype(BF16), shared_w_up[l].astype(BF16),
                         shared_w_down[l].astype(BF16))
        ewg, ewu, ewd = (moe_w_gate[l].astype(BF16), moe_w_up[l].astype(BF16),
                         moe_w_down[l].astype(BF16))

        tm_row = min(ROW_TILE, seq_len)
        ua, yb = _premix_seq(xp, mod_p[0], mod_p[1], g1, w_in_b, ln_g, ln_b, ws_tril, bs_full,
                             seq_len, tm=tm_row)
        ya, hpr, hpi = _s5_seq(ua.reshape(bp, seq_len, sw), bj, cj, a_re, a_im, d_vec,
                               w_glu, b_glu, tl=min(SCAN_TILE, seq_len))
        xacc, h2, wd = _postmix(xp, ya.reshape(bp * seq_len, sw), yb, mod_p[2], mod_p[3],
                                mod_p[4], mod_p[5], g2, w_out_b, rw_hi, rw_lo, rb, swg, swu, swd,
                                per_row=False, seq_len=seq_len, tm=tm_row, token_tiles=True)
        last = l == depth - 1
        tg = min(MOE_GROUP, seq_len)
        slots, slot_w, counts = _route_plan(wd, tg)
        n_slots = slots.shape[0] * N_EXPERTS
        routed = _moe_sparse(counts[:, :, 0].reshape(-1), h2, slots.reshape(n_slots, 1, tg),
                             slot_w.reshape(n_slots, 1, tg), ewg, ewu, ewd, tg)
        xp = _finalize(routed, xacc, mod_p[5], gf, tg, seq_len, tm_row, final_norm=last)
        p_re.append(hpr.reshape(bp, groups, n_state))
        p_im.append(hpi.reshape(bp, groups, n_state))

        ua, yb, vn = _premix_step(xs, mod_s[0], mod_s[1], g1, w_in_b, ln_g, ln_b, w0, b0)
        ya, hsr, hsi = _s5_step(ua, state_ssm_re[l].reshape(bs, groups * n_state),
                                state_ssm_im[l].reshape(bs, groups * n_state),
                                bj, cj, a_re, a_im, d_vec, w_glu, b_glu)
        xacc, h2, wd = _postmix(xs, ya, yb, mod_s[2], mod_s[3], mod_s[4], mod_s[5], g2, w_out_b,
                                rw_hi, rw_lo, rb, swg, swu, swd, per_row=True, seq_len=1, tm=bs,
                                token_tiles=False)
        xs = _moe_dense(h2, wd, xacc, mod_s[5], gf, ewg, ewu, ewd, per_row=True, seq_len=1,
                        tm=bs, final_norm=last)
        s_re.append(hsr.reshape(bs, groups, n_state))
        s_im.append(hsi.reshape(bs, groups, n_state))
        s_v.append(vn.reshape(bs, 1, gw))

    y_prompt = xp.reshape(bp, seq_len, d)
    y_sample = xs.reshape(bs, 1, d)
    return (y_prompt, y_sample, jnp.stack(p_re), jnp.stack(p_im), jnp.stack(s_re),
            jnp.stack(s_im), jnp.stack(s_v))
```

```python
import functools

import jax
import jax.numpy as jnp
from jax import lax
from jax.experimental import pallas as pl
from jax.experimental.pallas import tpu as pltpu

F32 = jnp.float32
BF16 = jnp.bfloat16

EPS = 1e-6
SSM_GROUP = 16
SSM_STATE = 64
GM_HEADS = 4
CHUNK = 128
N_EXPERTS = 64
TOP_K = 8
N_ROUTE_GROUPS = 8
TOPK_ROUTE_GROUPS = 4
ROUTED_SCALE = 2.5

LANES = 128
SUBLANES = 8
VMEM_LIMIT = 48 * 1024 * 1024
ROW_TILE = 512
SCAN_TILE = 128
MOE_GROUP = 2048
MOE_BLOCK = 128


def _params(n_axes):
    return pltpu.CompilerParams(dimension_semantics=("arbitrary",) * n_axes,
                                vmem_limit_bytes=VMEM_LIMIT)


def _dot(a, b):
    return jnp.dot(a.astype(BF16), b.astype(BF16), preferred_element_type=F32)


def _rmsnorm(x, g):
    ms = jnp.mean(x * x, axis=-1, keepdims=True)
    return x * lax.rsqrt(ms + EPS) * g


def _ada_kernel(c_ref, w_ref, b_ref, o_ref):
    c = c_ref[...]
    o_ref[...] = _dot(jax.nn.silu(c), w_ref[...]) + b_ref[...]


def _ada(c, w, b, tn=512):
    m, d = c.shape
    n = w.shape[1]
    return pl.pallas_call(
        _ada_kernel,
        grid=(n // tn,),
        in_specs=[pl.BlockSpec((m, d), lambda j: (0, 0)),
                  pl.BlockSpec((d, tn), lambda j: (0, j)),
                  pl.BlockSpec((1, tn), lambda j: (0, j))],
        out_specs=pl.BlockSpec((m, tn), lambda j: (0, j)),
        out_shape=jax.ShapeDtypeStruct((m, n), F32),
        compiler_params=_params(1),
        name="ada_mod",
    )(c, w, b)


def _premix_common(x_ref, sh_ref, sc_ref, g_ref, win_ref, lng_ref, lnb_ref, sw):
    x = x_ref[...]
    h = _rmsnorm(x, g_ref[...]) * (1.0 + sc_ref[...]) + sh_ref[...]
    proj = _dot(h, win_ref[...])
    u_a = proj[:, :sw]
    z = jax.nn.gelu(proj[:, sw:])
    gw = z.shape[1] // 2
    u_b = z[:, :gw]
    v_b = z[:, gw:]
    hd = gw // GM_HEADS
    parts = []
    for k in range(GM_HEADS):
        vh = v_b[:, k * hd:(k + 1) * hd]
        mu = jnp.mean(vh, axis=-1, keepdims=True)
        dv = vh - mu
        var = jnp.mean(dv * dv, axis=-1, keepdims=True)
        parts.append(dv * lax.rsqrt(var + EPS))
    vn = jnp.concatenate(parts, axis=-1) * lng_ref[...] + lnb_ref[...]
    return u_a, u_b, vn


def _premix_seq_kernel(x_ref, sh_ref, sc_ref, g_ref, win_ref, lng_ref, lnb_ref,
                       ws_ref, bs_ref, ua_ref, yb_ref, *, sw):
    u_a, u_b, vn = _premix_common(x_ref, sh_ref, sc_ref, g_ref, win_ref, lng_ref, lnb_ref, sw)
    ua_ref[...] = u_a
    tm, gw = u_b.shape
    hd = gw // GM_HEADS
    vnb = vn.astype(BF16)
    for c in range(tm // CHUNK):
        rows = slice(c * CHUNK, (c + 1) * CHUNK)
        for k in range(GM_HEADS):
            cols = slice(k * hd, (k + 1) * hd)
            s = jnp.dot(ws_ref[k], vnb[rows, cols], preferred_element_type=F32)
            yb_ref[rows, cols] = u_b[rows, cols] * (s + bs_ref[:, cols])


def _premix_step_kernel(x_ref, sh_ref, sc_ref, g_ref, win_ref, lng_ref, lnb_ref,
                        w0_ref, b0_ref, ua_ref, yb_ref, vn_ref, *, sw):
    u_a, u_b, vn = _premix_common(x_ref, sh_ref, sc_ref, g_ref, win_ref, lng_ref, lnb_ref, sw)
    ua_ref[...] = u_a
    vn_ref[...] = vn
    yb_ref[...] = u_b * (vn * w0_ref[...] + b0_ref[...])


def _row_spec(tm, d):
    return pl.BlockSpec((tm, d), lambda i: (i, 0))


def _full_spec(shape):
    nd = len(shape)
    return pl.BlockSpec(shape, lambda i: (0,) * nd)


def _mod_spec(per_row, tm, d, tiles_per_batch):
    if per_row:
        return pl.BlockSpec((tm, d), lambda i: (i, 0))
    return pl.BlockSpec((None, 1, d), lambda i: (i // tiles_per_batch, 0, 0))


def _premix_seq(x, sh, sc, g, w_in, ln_g, ln_b, ws_tril, bs_full, seq_len, tm):
    t, d = x.shape
    sw = ln_g.shape[1]
    tpb = seq_len // tm
    mod = _mod_spec(False, tm, d, tpb)
    return pl.pallas_call(
        functools.partial(_premix_seq_kernel, sw=sw),
        grid=(t // tm,),
        in_specs=[_row_spec(tm, d), mod, mod, _full_spec(g.shape), _full_spec(w_in.shape),
                  _full_spec(ln_g.shape), _full_spec(ln_b.shape), _full_spec(ws_tril.shape),
                  _full_spec(bs_full.shape)],
        out_specs=[_row_spec(tm, sw), _row_spec(tm, sw)],
        out_shape=[jax.ShapeDtypeStruct((t, sw), F32), jax.ShapeDtypeStruct((t, sw), F32)],
        compiler_params=_params(1),
        name="premix_seq",
    )(x, sh, sc, g, w_in, ln_g, ln_b, ws_tril, bs_full)


def _premix_step(x, sh, sc, g, w_in, ln_g, ln_b, w0, b0):
    t, d = x.shape
    sw = ln_g.shape[1]
    mod = _mod_spec(True, t, d, 1)
    return pl.pallas_call(
        functools.partial(_premix_step_kernel, sw=sw),
        grid=(1,),
        in_specs=[_row_spec(t, d), mod, mod, _full_spec(g.shape), _full_spec(w_in.shape),
                  _full_spec(ln_g.shape), _full_spec(ln_b.shape), _full_spec(w0.shape),
                  _full_spec(b0.shape)],
        out_specs=[_row_spec(t, sw)] * 3,
        out_shape=[jax.ShapeDtypeStruct((t, sw), F32)] * 3,
        compiler_params=_params(1),
        name="premix_step",
    )(x, sh, sc, g, w_in, ln_g, ln_b, w0, b0)


def _s5_output(xs, u, cj_ref, d_ref, wglu_ref, bglu_ref, n_chunks):
    cw = xs.shape[1] // n_chunks
    xb = xs.astype(BF16)
    ys = [jnp.dot(xb[:, j * cw:(j + 1) * cw], cj_ref[j], preferred_element_type=F32)
          for j in range(n_chunks)]
    y = jnp.concatenate(ys, axis=-1) + d_ref[...] * u
    y = jax.nn.gelu(y)
    gate = jax.nn.sigmoid(_dot(y, wglu_ref[...]) + bglu_ref[...])
    return y * gate


def _s5_seq_kernel(u_ref, bj_ref, cj_ref, are_ref, aim_ref, d_ref, wglu_ref, bglu_ref,
                   y_ref, sre_ref, sim_ref, bu_ref, st_ref, *, nb, tl, n_chunks, unroll):
    i = pl.program_id(0)
    n_tiles = bu_ref.shape[0]
    tpc = n_tiles // n_chunks
    hpc = tpc // 2
    hw = hpc * LANES
    kw = u_ref.shape[2] // n_chunks

    @pl.when(i == 0)
    def _():
        st_ref[...] = jnp.zeros_like(st_ref)

    for b in range(nb):
        ub = u_ref[b].astype(BF16)
        for j in range(n_chunks):
            bu = jnp.dot(ub[:, j * kw:(j + 1) * kw], bj_ref[j], preferred_element_type=F32)
            for c in range(tpc):
                bu_ref[j * tpc + c, pl.ds(b, tl, stride=nb), :] = bu[:, c * LANES:(c + 1) * LANES]

    for j in range(n_chunks):
        re = slice(j * 2 * hw, j * 2 * hw + hw)
        im = slice(j * 2 * hw + hw, (j + 1) * 2 * hw)
        ar = jnp.broadcast_to(are_ref[:, j * hw:(j + 1) * hw], (nb, hw))
        ai = jnp.broadcast_to(aim_ref[:, j * hw:(j + 1) * hw], (nb, hw))
        re_tiles = [j * tpc + c for c in range(hpc)]
        im_tiles = [j * tpc + hpc + c for c in range(hpc)]

        def body(s, carry, ar=ar, ai=ai, re_tiles=re_tiles, im_tiles=im_tiles):
            xr, xi = carry
            for k in range(unroll):
                r0 = pl.multiple_of((s * unroll + k) * nb, nb)
                br = jnp.concatenate([bu_ref[c, pl.ds(r0, nb), :] for c in re_tiles], axis=-1)
                bi = jnp.concatenate([bu_ref[c, pl.ds(r0, nb), :] for c in im_tiles], axis=-1)
                nr = ar * xr - ai * xi + br
                ni = ar * xi + ai * xr + bi
                for q, c in enumerate(re_tiles):
                    bu_ref[c, pl.ds(r0, nb), :] = nr[:, q * LANES:(q + 1) * LANES]
                for q, c in enumerate(im_tiles):
                    bu_ref[c, pl.ds(r0, nb), :] = ni[:, q * LANES:(q + 1) * LANES]
                xr, xi = nr, ni
            return xr, xi

        xr, xi = lax.fori_loop(0, tl // unroll, body, (st_ref[:, re], st_ref[:, im]))
        st_ref[:, re] = xr
        st_ref[:, im] = xi
        sre_ref[:, j * hw:(j + 1) * hw] = xr
        sim_ref[:, j * hw:(j + 1) * hw] = xi

    for b in range(nb):
        xs = jnp.concatenate([bu_ref[c, pl.ds(b, tl, stride=nb), :] for c in range(n_tiles)],
                             axis=-1)
        y_ref[b] = _s5_output(xs, u_ref[b], cj_ref, d_ref, wglu_ref, bglu_ref, n_chunks)


def _s5_seq(u, bj, cj, a_re, a_im, d_vec, w_glu, b_glu, tl, unroll=4):
    nb, seq_len, sw = u.shape
    n_chunks, _, cw = bj.shape
    ns = a_re.shape[1]
    kern = functools.partial(_s5_seq_kernel, nb=nb, tl=tl, n_chunks=n_chunks, unroll=unroll)
    return pl.pallas_call(
        kern,
        grid=(seq_len // tl,),
        in_specs=[pl.BlockSpec((nb, tl, sw), lambda i: (0, i, 0)),
                  _full_spec(bj.shape), _full_spec(cj.shape), _full_spec(a_re.shape),
                  _full_spec(a_im.shape), _full_spec(d_vec.shape), _full_spec(w_glu.shape),
                  _full_spec(b_glu.shape)],
        out_specs=[pl.BlockSpec((nb, tl, sw), lambda i: (0, i, 0)),
                   _full_spec((nb, ns)), _full_spec((nb, ns))],
        out_shape=[jax.ShapeDtypeStruct((nb, seq_len, sw), F32),
                   jax.ShapeDtypeStruct((nb, ns), F32), jax.ShapeDtypeStruct((nb, ns), F32)],
        scratch_shapes=[pltpu.VMEM((n_chunks * cw // LANES, tl * nb, LANES), F32),
                        pltpu.VMEM((nb, n_chunks * cw), F32)],
        compiler_params=_params(1),
        name="s5_seq",
    )(u, bj, cj, a_re, a_im, d_vec, w_glu, b_glu)


def _s5_step_kernel(u_ref, hre_ref, him_ref, bj_ref, cj_ref, are_ref, aim_ref, d_ref,
                    wglu_ref, bglu_ref, y_ref, sre_ref, sim_ref, *, n_chunks):
    u = u_ref[...]
    ub = u.astype(BF16)
    kw = u.shape[1] // n_chunks
    hw = bj_ref.shape[2] // 2
    cols = []
    for j in range(n_chunks):
        st = slice(j * hw, (j + 1) * hw)
        bu = jnp.dot(ub[:, j * kw:(j + 1) * kw], bj_ref[j], preferred_element_type=F32)
        ar, ai = are_ref[:, st], aim_ref[:, st]
        hr, hi = hre_ref[:, st], him_ref[:, st]
        xr = ar * hr - ai * hi + bu[:, :hw]
        xi = ar * hi + ai * hr + bu[:, hw:]
        sre_ref[:, st] = xr
        sim_ref[:, st] = xi
        cols += [xr, xi]
    xs = jnp.concatenate(cols, axis=-1)
    y_ref[...] = _s5_output(xs, u, cj_ref, d_ref, wglu_ref, bglu_ref, n_chunks)


def _s5_step(u, h_re, h_im, bj, cj, a_re, a_im, d_vec, w_glu, b_glu):
    t, sw = u.shape
    ns = a_re.shape[1]
    args = (u, h_re, h_im, bj, cj, a_re, a_im, d_vec, w_glu, b_glu)
    return pl.pallas_call(
        functools.partial(_s5_step_kernel, n_chunks=bj.shape[0]),
        grid=(1,),
        in_specs=[_full_spec(a.shape) for a in args],
        out_specs=[_full_spec((t, sw)), _full_spec((t, ns)), _full_spec((t, ns))],
        out_shape=[jax.ShapeDtypeStruct((t, sw), F32), jax.ShapeDtypeStruct((t, ns), F32),
                   jax.ShapeDtypeStruct((t, ns), F32)],
        compiler_params=_params(1),
        name="s5_step",
    )(*args)


def _group_reduce(x, lane, width, op):
    n = x.shape[-1]
    s = 1
    while s < width:
        up = pltpu.roll(x, s, 1)
        dn = pltpu.roll(x, n - s, 1)
        x = op(x, jnp.where((lane & s) != 0, up, dn))
        s *= 2
    return x


def _first_max(x, lane, big):
    m = jnp.max(x, axis=-1, keepdims=True)
    return jnp.min(jnp.where(x == m, lane, big), axis=-1, keepdims=True)


def _route(logits, bias):
    rows, n = logits.shape
    lane = lax.broadcasted_iota(jnp.int32, (rows, n), 1)
    lane_f = lane.astype(F32)
    neg = jnp.float32(-jnp.inf)
    big = jnp.float32(n)
    gsz = N_EXPERTS // N_ROUTE_GROUPS
    group_start = (lane - (lane & (gsz - 1))).astype(F32)
    scores = jax.nn.sigmoid(logits)
    biased = scores + bias
    m1 = _group_reduce(biased, lane, gsz, jnp.maximum)
    first = _group_reduce(jnp.where(biased == m1, lane_f, big), lane, gsz, jnp.minimum)
    m2 = _group_reduce(jnp.where(lane_f == first, neg, biased), lane, gsz, jnp.maximum)
    gs = m1 + m2
    gsel = jnp.zeros((rows, n), F32)
    work = gs
    for _ in range(TOPK_ROUTE_GROUPS):
        hit = group_start == _first_max(work, lane_f, big)
        gsel = jnp.where(hit, 1.0, gsel)
        work = jnp.where(hit, neg, work)
    work = jnp.where(gsel > 0.0, biased, neg)
    w = jnp.zeros((rows, n), F32)
    for _ in range(TOP_K):
        hit = lane_f == _first_max(work, lane_f, big)
        w = jnp.where(hit, scores, w)
        work = jnp.where(hit, neg, work)
    return w / jnp.sum(w, axis=-1, keepdims=True) * ROUTED_SCALE


def _split_bf16(x):
    hi = x.astype(BF16)
    lo = (x - hi.astype(F32)).astype(BF16)
    return hi, lo


def _postmix_kernel(x_ref, ya_ref, yb_ref, gt1_ref, sh2_ref, sc2_ref, gt2_ref, g2_ref,
                    wout_ref, rwh_ref, rwl_ref, rb_ref, swg_ref, swu_ref, swd_ref,
                    xacc_ref, h2_ref, wd_ref, *, token_tiles):
    y = jnp.concatenate([ya_ref[...], yb_ref[...]], axis=-1)
    x1 = x_ref[...] + gt1_ref[...] * _dot(y, wout_ref[...])
    h2 = _rmsnorm(x1, g2_ref[...]) * (1.0 + sc2_ref[...]) + sh2_ref[...]
    hb = h2.astype(BF16)
    if token_tiles:
        tm, d = h2.shape
        nj = d // LANES
        for j in range(nj):
            h2_ref[pl.ds(j, tm, stride=nj), :] = h2[:, j * LANES:(j + 1) * LANES]
    else:
        h2_ref[...] = hb
    h_hi, h_lo = _split_bf16(h2)
    logits = (jnp.dot(h_hi, rwh_ref[...], preferred_element_type=F32)
              + jnp.dot(h_hi, rwl_ref[...], preferred_element_type=F32)
              + jnp.dot(h_lo, rwh_ref[...], preferred_element_type=F32))
    wd_ref[...] = _route(logits, rb_ref[...])
    hid = jax.nn.silu(jnp.dot(hb, swg_ref[...], preferred_element_type=F32)) * \
        jnp.dot(hb, swu_ref[...], preferred_element_type=F32)
    shared = _dot(hid, swd_ref[...])
    xacc_ref[...] = x1 + gt2_ref[...] * shared


def _postmix(x, ya, yb, gt1, sh2, sc2, gt2, g2, w_out, rw_hi, rw_lo, rb, swg, swu, swd,
             per_row, seq_len, tm, token_tiles):
    t, d = x.shape
    sw = ya.shape[1]
    ne = rw_hi.shape[1]
    nj = d // LANES
    mod = _mod_spec(per_row, tm, d, max(seq_len // tm, 1))
    weights = (g2, w_out, rw_hi, rw_lo, rb, swg, swu, swd)
    if token_tiles:
        h2_spec, h2_shape = _row_spec(tm * nj, LANES), jax.ShapeDtypeStruct((t * nj, LANES), F32)
    else:
        h2_spec, h2_shape = _row_spec(tm, d), jax.ShapeDtypeStruct((t, d), BF16)
    return pl.pallas_call(
        functools.partial(_postmix_kernel, token_tiles=token_tiles),
        grid=(t // tm,),
        in_specs=[_row_spec(tm, d), _row_spec(tm, sw), _row_spec(tm, sw), mod, mod, mod, mod]
        + [_full_spec(w.shape) for w in weights],
        out_specs=[_row_spec(tm, d), h2_spec, _row_spec(tm, ne)],
        out_shape=[jax.ShapeDtypeStruct((t, d), F32), h2_shape,
                   jax.ShapeDtypeStruct((t, ne), F32)],
        compiler_params=_params(1),
        name="postmix",
    )(x, ya, yb, gt1, sh2, sc2, gt2, *weights)


def _moe_dense_kernel(h_ref, wd_ref, xacc_ref, gt2_ref, gf_ref, wg_ref, wu_ref, wdn_ref,
                      o_ref, acc_ref, *, final_norm):
    e = pl.program_id(1)

    @pl.when(e == 0)
    def _():
        acc_ref[...] = jnp.zeros_like(acc_ref)

    hb = h_ref[...]
    wd = wd_ref[...]
    lane = lax.broadcasted_iota(jnp.int32, wd.shape, 1)
    col = jnp.sum(jnp.where(lane == e, wd, 0.0), axis=-1, keepdims=True)
    hid = jax.nn.silu(jnp.dot(hb, wg_ref[...], preferred_element_type=F32)) * \
        jnp.dot(hb, wu_ref[...], preferred_element_type=F32)
    acc_ref[...] += col * _dot(hid, wdn_ref[...])

    @pl.when(e == pl.num_programs(1) - 1)
    def _():
        xo = xacc_ref[...] + gt2_ref[...] * acc_ref[...]
        o_ref[...] = _rmsnorm(xo, gf_ref[...]) if final_norm else xo


def _moe_dense(h2, wd, xacc, gt2, g_final, wg, wu, wdn, per_row, seq_len, tm, final_norm):
    t, d = xacc.shape
    ne, _, eh = wg.shape
    nl = wd.shape[1]
    tpb = max(seq_len // tm, 1)
    if per_row:
        mod = pl.BlockSpec((tm, d), lambda i, e: (i, 0))
    else:
        mod = pl.BlockSpec((None, 1, d), lambda i, e: (i // tpb, 0, 0))
    row = lambda w: pl.BlockSpec((tm, w), lambda i, e: (i, 0))
    return pl.pallas_call(
        functools.partial(_moe_dense_kernel, final_norm=final_norm),
        grid=(t // tm, ne),
        in_specs=[row(d), row(nl), row(d), mod,
                  pl.BlockSpec(g_final.shape, lambda i, e: (0, 0)),
                  pl.BlockSpec((None, d, eh), lambda i, e: (e, 0, 0)),
                  pl.BlockSpec((None, d, eh), lambda i, e: (e, 0, 0)),
                  pl.BlockSpec((None, eh, d), lambda i, e: (e, 0, 0))],
        out_specs=row(d),
        out_shape=jax.ShapeDtypeStruct((t, d), F32),
        scratch_shapes=[pltpu.VMEM((tm, d), F32)],
        compiler_params=_params(2),
        name="moe_dense",
    )(h2, wd, xacc, gt2, g_final, wg, wu, wdn)


TOKEN_BITS = 12


def _route_plan_kernel(wd_ref, src_ref, dst_ref, wl_ref, cnt_ref, *, tg, nj):
    wt = wd_ref[...].T[:N_EXPERTS]
    sel = wt > 0.0
    lane = lax.broadcasted_iota(jnp.int32, wt.shape, 1)
    c = sel.astype(jnp.int32)
    s = 1
    while s < tg:
        c = c + jnp.where(lane >= s, pltpu.roll(c, s, 1), 0)
        s *= 2
    cnt_ref[...] = jnp.broadcast_to(c[:, tg - 1:tg], cnt_ref.shape)
    dist = lane - (c - 1)
    key = jnp.where(sel, (dist << TOKEN_BITS) | lane, -1)
    w = jnp.where(sel, wt, 0.0)
    s = 1
    while s < tg:
        leaving = (key >= 0) & (((key >> TOKEN_BITS) & s) != 0)
        key_in = pltpu.roll(key, tg - s, 1)
        w_in = pltpu.roll(w, tg - s, 1)
        arriving = (key_in >= 0) & (((key_in >> TOKEN_BITS) & s) != 0)
        key = jnp.where(arriving, key_in, jnp.where(leaving, -1, key))
        w = jnp.where(arriving, w_in, jnp.where(leaving, 0.0, w))
        s *= 2
    tok = key & ((1 << TOKEN_BITS) - 1)
    src_ref[...] = jnp.where(key >= 0, tok, tg - 1) * nj
    dst_ref[...] = jnp.where(key >= 0, tok, tg) * nj
    wl_ref[...] = w


def _route_plan(wd, tg, nj):
    t, nl = wd.shape
    assert tg <= (1 << TOKEN_BITS) and t % tg == 0
    g = t // tg
    grp = lambda w: pl.BlockSpec((None, N_EXPERTS, w), lambda i: (i, 0, 0))
    slots = jax.ShapeDtypeStruct((g, N_EXPERTS, tg), jnp.int32)
    return pl.pallas_call(
        functools.partial(_route_plan_kernel, tg=tg, nj=nj),
        grid=(g,),
        in_specs=[pl.BlockSpec((tg, nl), lambda i: (i, 0))],
        out_specs=[grp(tg), grp(tg), grp(tg), grp(LANES)],
        out_shape=[slots, slots, jax.ShapeDtypeStruct((g, N_EXPERTS, tg), F32),
                   jax.ShapeDtypeStruct((g, N_EXPERTS, LANES), jnp.int32)],
        compiler_params=_params(1),
        name="route_plan",
    )(wd)


def _block_table(counts, tg, bm, nb_max):
    g, ne = counts.shape
    i32 = jnp.int32
    nblk = (counts + bm - 1) // bm
    bend = jnp.cumsum(nblk, axis=1)
    bstart = bend - nblk
    nb = bend[:, -1]
    k = jnp.arange(nb_max, dtype=i32)[None, :]
    kc = jnp.minimum(k, jnp.maximum(nb[:, None] - 1, 0))
    e_k = jnp.minimum(jnp.sum(bend[:, None, :] <= kc[:, :, None], axis=2), ne - 1).astype(i32)
    b_k = kc - jnp.take_along_axis(bstart, e_k, axis=1)
    first = (b_k == 0) & (k < nb[:, None])
    active = nblk > 0
    parity = jnp.take_along_axis(jnp.cumsum(active.astype(i32), axis=1) - 1, e_k, axis=1) % 2
    ids = jnp.where(active, jnp.arange(ne, dtype=i32)[None, :], ne)
    first_active_from = jnp.flip(lax.cummin(jnp.flip(ids, 1), axis=1), 1)
    nxt = jnp.concatenate([first_active_from[:, 1:], jnp.full((g, 1), ne, i32)], axis=1)
    next_k = jnp.take_along_axis(jnp.where(nxt >= ne, -1, nxt), e_k, axis=1)
    row = (jnp.arange(g, dtype=i32)[:, None] * ne + e_k) * (tg // bm) + b_k
    flat = lambda a: a.reshape(-1).astype(i32)
    return (nb.astype(i32), flat(e_k), flat(first), flat(parity), flat(next_k)), flat(row)


def _moe_group_kernel(nb_ref, exp_ref, first_ref, par_ref, next_ref,
                      h_ref, src_ref, dst_ref, wl_ref, wg_hbm, wu_hbm, wdn_hbm, o_ref,
                      wg_buf, wu_buf, wdn_buf, wsem, xt_a, xt_b, yt_a, yt_b,
                      *, bm, nj, batch, nb_max):
    g = pl.program_id(0)
    nb = nb_ref[g]
    tb = g * nb_max
    pitch = bm + 1
    o_ref[...] = jnp.zeros_like(o_ref)

    def weight_copies(e, slot):
        return (pltpu.make_async_copy(wg_hbm.at[e], wg_buf.at[slot], wsem.at[slot]),
                pltpu.make_async_copy(wu_hbm.at[e], wu_buf.at[slot], wsem.at[slot]),
                pltpu.make_async_copy(wdn_hbm.at[e], wdn_buf.at[slot], wsem.at[slot]))

    def handoff(k):
        @pl.when(first_ref[tb + k] == 1)
        def _():
            slot = par_ref[tb + k]
            for c in weight_copies(exp_ref[tb + k], slot):
                c.wait()
            nxt = next_ref[tb + k]

            @pl.when(nxt >= 0)
            def _():
                for c in weight_copies(nxt, 1 - slot):
                    c.start()

    def gather(k, xt):
        src = src_ref.at[pl.ds(pl.multiple_of(k * bm, bm), bm)]
        for m in range(bm):
            r0 = pl.multiple_of(src[m], nj)
            xt[pl.ds(m, nj, stride=pitch), :] = h_ref[pl.ds(r0, nj), :]

    def experts(k, xt, yt):
        slot = par_ref[tb + k]
        x = jnp.concatenate([xt[j * pitch:j * pitch + bm, :] for j in range(nj)], axis=-1)
        xb = x.astype(BF16)
        hid = jax.nn.silu(jnp.dot(xb, wg_buf[slot], preferred_element_type=F32)) * \
            jnp.dot(xb, wu_buf[slot], preferred_element_type=F32)
        y = _dot(hid, wdn_buf[slot])
        for j in range(nj):
            yt[j * pitch:j * pitch + bm, :] = y[:, j * LANES:(j + 1) * LANES]

    def scatter(k, yt):
        base = pl.multiple_of(k * bm, bm)
        dst = dst_ref.at[pl.ds(base, bm)]
        wts = wl_ref.at[pl.ds(base, bm)]
        for m0 in range(0, bm, batch):
            rows, vals = [], []
            for m in range(m0, m0 + batch):
                r0 = pl.multiple_of(dst[m], nj)
                rows.append(r0)
                vals.append(o_ref[pl.ds(r0, nj), :] + wts[m] * yt[pl.ds(m, nj, stride=pitch), :])
            for r0, v in zip(rows, vals):
                o_ref[pl.ds(r0, nj), :] = v

    @pl.when(nb > 0)
    def _():
        for c in weight_copies(exp_ref[tb], 0):
            c.start()
        gather(0, xt_a)
        handoff(0)
        gather(1, xt_b)
        experts(0, xt_a, yt_a)

        def trip(i, carry):
            k = 2 * i + 1
            handoff(k)
            gather(k + 1, xt_a)
            experts(k, xt_b, yt_b)
            scatter(k - 1, yt_a)
            handoff(k + 1)
            gather(k + 2, xt_b)
            experts(k + 1, xt_a, yt_a)
            scatter(k, yt_b)
            return carry

        lax.fori_loop(0, nb // 2, trip, 0)

        @pl.when(nb % 2 == 1)
        def _():
            scatter(nb - 1, yt_a)


def _moe_group(table, h2t, src, dst, wl, wg, wu, wdn, tg, bm, nb_max, batch=8):
    ne, d, eh = wg.shape
    nj = d // LANES
    assert nj == SUBLANES
    g = h2t.shape[0] // (tg * nj)
    kern = functools.partial(_moe_group_kernel, bm=bm, nj=nj, batch=batch, nb_max=nb_max)
    slots = pl.BlockSpec((nb_max * bm,), lambda gi, *_: (gi,), memory_space=pltpu.SMEM,
                         pipeline_mode=pl.Buffered(1))
    hbm = pl.BlockSpec(memory_space=pl.ANY)
    pitch_rows = nj * (bm + 1)
    grid_spec = pltpu.PrefetchScalarGridSpec(
        num_scalar_prefetch=len(table),
        grid=(g,),
        in_specs=[pl.BlockSpec((tg * nj, LANES), lambda gi, *_: (gi, 0)),
                  slots, slots, slots, hbm, hbm, hbm],
        out_specs=pl.BlockSpec((None, (tg + 1) * nj, LANES), lambda gi, *_: (gi, 0, 0)),
        scratch_shapes=[pltpu.VMEM((2, d, eh), BF16), pltpu.VMEM((2, d, eh), BF16),
                        pltpu.VMEM((2, eh, d), BF16), pltpu.SemaphoreType.DMA((2,))]
        + [pltpu.VMEM((pitch_rows, LANES), F32)] * 4,
    )
    return pl.pallas_call(
        kern,
        grid_spec=grid_spec,
        out_shape=jax.ShapeDtypeStruct((g, (tg + 1) * nj, LANES), F32),
        compiler_params=_params(1),
        name="moe_group",
    )(*table, h2t, src, dst, wl, wg, wu, wdn)


def _finalize_kernel(r_ref, xacc_ref, gt2_ref, gf_ref, o_ref, *, final_norm):
    tm, d = xacc_ref.shape
    nj = d // LANES
    routed = jnp.concatenate([r_ref[pl.ds(j, tm, stride=nj), :] for j in range(nj)], axis=-1)
    xo = xacc_ref[...] + gt2_ref[...] * routed
    o_ref[...] = _rmsnorm(xo, gf_ref[...]) if final_norm else xo


def _finalize(routed, xacc, gt2, g_final, tg, seq_len, tm, final_norm):
    t, d = xacc.shape
    nj = d // LANES
    tpg = tg // tm
    return pl.pallas_call(
        functools.partial(_finalize_kernel, final_norm=final_norm),
        grid=(t // tm,),
        in_specs=[pl.BlockSpec((None, tm * nj, LANES), lambda i: (i // tpg, i % tpg, 0)),
                  _row_spec(tm, d), _mod_spec(False, tm, d, seq_len // tm),
                  _full_spec(g_final.shape)],
        out_specs=_row_spec(tm, d),
        out_shape=jax.ShapeDtypeStruct((t, d), F32),
        compiler_params=_params(1),
        name="finalize",
    )(routed, xacc, gt2, g_final)


def _s5_discretize(lam_re, lam_im, log_dt, b_re, b_im, c_re, c_im):
    g, n = lam_re.shape
    hg = b_re.shape[2]
    lr = jnp.minimum(lam_re.astype(F32), -1e-4)
    li = lam_im.astype(F32)
    dt = jnp.exp(log_dt.astype(F32))[:, None]
    mag = jnp.exp(lr * dt)
    abar_r, abar_i = mag * jnp.cos(li * dt), mag * jnp.sin(li * dt)
    den = lr * lr + li * li
    nr, ni = abar_r - 1.0, abar_i
    f_r = (nr * lr + ni * li) / den
    f_i = (ni * lr - nr * li) / den
    br_, bi_ = b_re.astype(F32), b_im.astype(F32)
    bbar_r = f_r[..., None] * br_ - f_i[..., None] * bi_
    bbar_i = f_r[..., None] * bi_ + f_i[..., None] * br_
    gpc = LANES // hg
    n_chunks = g // gpc
    eye = jnp.eye(gpc, dtype=F32)

    def b_chunks(bb):
        bb = bb.reshape(n_chunks, gpc, n, hg)
        m = jnp.einsum('jgnh,gk->jghkn', bb, eye)
        return m.reshape(n_chunks, gpc * hg, gpc * n)

    def c_chunks(cc):
        cc = cc.astype(F32).reshape(n_chunks, gpc, hg, n)
        m = jnp.einsum('jghn,gk->jgnkh', cc, eye)
        return m.reshape(n_chunks, gpc * n, gpc * hg)

    bj = jnp.concatenate([b_chunks(bbar_r), b_chunks(bbar_i)], axis=2).astype(BF16)
    cj = jnp.concatenate([c_chunks(c_re), -c_chunks(c_im)], axis=1).astype(BF16)
    return abar_r.reshape(1, g * n), abar_i.reshape(1, g * n), bj, cj


def kernel(x_prompt, x_sample, c_prompt, c_sample, state_ssm_re, state_ssm_im, g_norm1, g_norm2, w_ada, b_ada, w_in, w_out, ssm_lambda_re, ssm_lambda_im, ssm_log_dt, ssm_b_re, ssm_b_im, ssm_c_re, ssm_c_im, ssm_d, ssm_w_glu, ssm_b_glu, gm_ln_g, gm_ln_b, gm_w_s, gm_b_s, router_w, router_bias, moe_w_gate, moe_w_up, moe_w_down, shared_w_gate, shared_w_up, shared_w_down, g_final):
    bp, seq_len, d = x_prompt.shape
    bs, dec_len, _ = x_sample.shape
    depth = w_ada.shape[0]
    assert dec_len == 1 and seq_len % CHUNK == 0 and bp % SUBLANES == 0
    groups, n_state = ssm_lambda_re.shape[1:]
    sw = groups * SSM_GROUP
    gw = gm_ln_g.shape[1]
    hd = gw // GM_HEADS

    xp = x_prompt.reshape(bp * seq_len, d)
    xs = x_sample.reshape(bs, d)
    c_all = jnp.concatenate([c_prompt, c_sample], axis=0)
    gf = g_final.reshape(1, d)

    p_re, p_im, s_re, s_im, s_v = [], [], [], [], []
    for l in range(depth):
        mod = _ada(c_all, w_ada[l].astype(BF16), b_ada[l].reshape(1, -1))
        mod_p = [m.reshape(bp, 1, d) for m in jnp.split(mod[:bp], 6, axis=-1)]
        mod_s = jnp.split(mod[bp:], 6, axis=-1)

        g1 = g_norm1[l].reshape(1, d)
        g2 = g_norm2[l].reshape(1, d)
        w_in_b = w_in[l].astype(BF16)
        w_out_b = w_out[l].astype(BF16)
        ln_g = gm_ln_g[l].reshape(1, gw)
        ln_b = gm_ln_b[l].reshape(1, gw)
        ws_tril = (gm_w_s[l] * jnp.tril(jnp.ones((CHUNK, CHUNK), F32))[None]).astype(BF16)
        bs_full = jnp.repeat(gm_b_s[l].T, hd, axis=1)
        w0 = jnp.repeat(gm_w_s[l][:, 0, 0], hd).reshape(1, gw)
        b0 = jnp.repeat(gm_b_s[l][:, 0], hd).reshape(1, gw)
        a_re, a_im, bj, cj = _s5_discretize(ssm_lambda_re[l], ssm_lambda_im[l], ssm_log_dt[l],
                                            ssm_b_re[l], ssm_b_im[l], ssm_c_re[l], ssm_c_im[l])
        d_vec = ssm_d[l].reshape(1, sw)
        w_glu = ssm_w_glu[l].astype(BF16)
        b_glu = ssm_b_glu[l].reshape(1, sw)
        pad = LANES - N_EXPERTS
        rw = jnp.pad(router_w[l], ((0, 0), (0, pad)))
        rw_hi = rw.astype(BF16)
        rw_lo = (rw - rw_hi.astype(F32)).astype(BF16)
        rb = jnp.pad(router_bias[l], (0, pad), constant_values=-jnp.inf).reshape(1, LANES)
        swg, swu, swd = (shared_w_gate[l].astype(BF16), shared_w_up[l].astype(BF16),
                         shared_w_down[l].astype(BF16))
        ewg, ewu, ewd = (moe_w_gate[l].astype(BF16), moe_w_up[l].astype(BF16),
                         moe_w_down[l].astype(BF16))

        tm_row = min(ROW_TILE, seq_len)
        ua, yb = _premix_seq(xp, mod_p[0], mod_p[1], g1, w_in_b, ln_g, ln_b, ws_tril, bs_full,
                             seq_len, tm=tm_row)
        ya, hpr, hpi = _s5_seq(ua.reshape(bp, seq_len, sw), bj, cj, a_re, a_im, d_vec,
                               w_glu, b_glu, tl=min(SCAN_TILE, seq_len))
        xacc, h2, wd = _postmix(xp, ya.reshape(bp * seq_len, sw), yb, mod_p[2], mod_p[3],
                                mod_p[4], mod_p[5], g2, w_out_b, rw_hi, rw_lo, rb, swg, swu, swd,
                                per_row=False, seq_len=seq_len, tm=tm_row, token_tiles=True)
        last = l == depth - 1
        tg = min(MOE_GROUP, seq_len)
        bm = min(MOE_BLOCK, tg)
        nb_max = -(-(tg * TOP_K // bm + N_EXPERTS + 2) // SUBLANES) * SUBLANES
        src, dst, slot_w, counts = _route_plan(wd, tg, d // LANES)
        table, block_row = _block_table(counts[:, :, 0], tg, bm, nb_max)
        by_block = lambda a: jnp.take(a.reshape(-1, bm), block_row, axis=0).reshape(-1)
        routed = _moe_group(table, h2, by_block(src), by_block(dst), by_block(slot_w),
                            ewg, ewu, ewd, tg, bm, nb_max)
        xp = _finalize(routed, xacc, mod_p[5], gf, tg, seq_len, tm_row, final_norm=last)
        p_re.append(hpr.reshape(bp, groups, n_state))
        p_im.append(hpi.reshape(bp, groups, n_state))

        ua, yb, vn = _premix_step(xs, mod_s[0], mod_s[1], g1, w_in_b, ln_g, ln_b, w0, b0)
        ya, hsr, hsi = _s5_step(ua, state_ssm_re[l].reshape(bs, groups * n_state),
                                state_ssm_im[l].reshape(bs, groups * n_state),
                                bj, cj, a_re, a_im, d_vec, w_glu, b_glu)
        xacc, h2, wd = _postmix(xs, ya, yb, mod_s[2], mod_s[3], mod_s[4], mod_s[5], g2, w_out_b,
                                rw_hi, rw_lo, rb, swg, swu, swd, per_row=True, seq_len=1, tm=bs,
                                token_tiles=False)
        xs = _moe_dense(h2, wd, xacc, mod_s[5], gf, ewg, ewu, ewd, per_row=True, seq_len=1,
                        tm=bs, final_norm=last)
        s_re.append(hsr.reshape(bs, groups, n_state))
        s_im.append(hsi.reshape(bs, groups, n_state))
        s_v.append(vn.reshape(bs, 1, gw))

    y_prompt = xp.reshape(bp, seq_len, d)
    y_sample = xs.reshape(bs, 1, d)
    return (y_prompt, y_sample, jnp.stack(p_re), jnp.stack(p_im), jnp.stack(s_re),
            jnp.stack(s_im), jnp.stack(s_v))
```

```python
import functools

import jax
import jax.numpy as jnp
from jax import lax
from jax.experimental import pallas as pl
from jax.experimental.pallas import tpu as pltpu

F32 = jnp.float32
BF16 = jnp.bfloat16

EPS = 1e-6
SSM_GROUP = 16
SSM_STATE = 64
GM_HEADS = 4
CHUNK = 128
N_EXPERTS = 64
TOP_K = 8
N_ROUTE_GROUPS = 8
TOPK_ROUTE_GROUPS = 4
ROUTED_SCALE = 2.5

LANES = 128
SUBLANES = 8
VMEM_LIMIT = 48 * 1024 * 1024
ROW_TILE = 512
SCAN_TILE = 128
MOE_GROUP = 2048
MOE_BLOCK = 128
MOE_WEIGHT_BUFFERS = 3


def _params(n_axes):
    return pltpu.CompilerParams(dimension_semantics=("arbitrary",) * n_axes,
                                vmem_limit_bytes=VMEM_LIMIT)


def _dot(a, b):
    return jnp.dot(a.astype(BF16), b.astype(BF16), preferred_element_type=F32)


def _rmsnorm(x, g):
    ms = jnp.mean(x * x, axis=-1, keepdims=True)
    return x * lax.rsqrt(ms + EPS) * g


def _ada_kernel(c_ref, w_ref, b_ref, o_ref):
    c = c_ref[...]
    o_ref[...] = _dot(jax.nn.silu(c), w_ref[...]) + b_ref[...]


def _ada(c, w, b, tn=512):
    m, d = c.shape
    n = w.shape[1]
    return pl.pallas_call(
        _ada_kernel,
        grid=(n // tn,),
        in_specs=[pl.BlockSpec((m, d), lambda j: (0, 0)),
                  pl.BlockSpec((d, tn), lambda j: (0, j)),
                  pl.BlockSpec((1, tn), lambda j: (0, j))],
        out_specs=pl.BlockSpec((m, tn), lambda j: (0, j)),
        out_shape=jax.ShapeDtypeStruct((m, n), F32),
        compiler_params=_params(1),
        name="ada_mod",
    )(c, w, b)


def _premix_common(x_ref, sh_ref, sc_ref, g_ref, win_ref, lng_ref, lnb_ref, sw):
    x = x_ref[...]
    h = _rmsnorm(x, g_ref[...]) * (1.0 + sc_ref[...]) + sh_ref[...]
    proj = _dot(h, win_ref[...])
    u_a = proj[:, :sw]
    z = jax.nn.gelu(proj[:, sw:])
    gw = z.shape[1] // 2
    u_b = z[:, :gw]
    v_b = z[:, gw:]
    hd = gw // GM_HEADS
    parts = []
    for k in range(GM_HEADS):
        vh = v_b[:, k * hd:(k + 1) * hd]
        mu = jnp.mean(vh, axis=-1, keepdims=True)
        dv = vh - mu
        var = jnp.mean(dv * dv, axis=-1, keepdims=True)
        parts.append(dv * lax.rsqrt(var + EPS))
    vn = jnp.concatenate(parts, axis=-1) * lng_ref[...] + lnb_ref[...]
    return u_a, u_b, vn


def _premix_seq_kernel(x_ref, sh_ref, sc_ref, g_ref, win_ref, lng_ref, lnb_ref,
                       ws_ref, bs_ref, ua_ref, yb_ref, *, sw):
    u_a, u_b, vn = _premix_common(x_ref, sh_ref, sc_ref, g_ref, win_ref, lng_ref, lnb_ref, sw)
    ua_ref[...] = u_a
    tm, gw = u_b.shape
    hd = gw // GM_HEADS
    vnb = vn.astype(BF16)
    for c in range(tm // CHUNK):
        rows = slice(c * CHUNK, (c + 1) * CHUNK)
        for k in range(GM_HEADS):
            cols = slice(k * hd, (k + 1) * hd)
            s = jnp.dot(ws_ref[k], vnb[rows, cols], preferred_element_type=F32)
            yb_ref[rows, cols] = u_b[rows, cols] * (s + bs_ref[:, cols])


def _premix_step_kernel(x_ref, sh_ref, sc_ref, g_ref, win_ref, lng_ref, lnb_ref,
                        w0_ref, b0_ref, ua_ref, yb_ref, vn_ref, *, sw):
    u_a, u_b, vn = _premix_common(x_ref, sh_ref, sc_ref, g_ref, win_ref, lng_ref, lnb_ref, sw)
    ua_ref[...] = u_a
    vn_ref[...] = vn
    yb_ref[...] = u_b * (vn * w0_ref[...] + b0_ref[...])


def _row_spec(tm, d):
    return pl.BlockSpec((tm, d), lambda i: (i, 0))


def _full_spec(shape):
    nd = len(shape)
    return pl.BlockSpec(shape, lambda i: (0,) * nd)


def _mod_spec(per_row, tm, d, tiles_per_batch):
    if per_row:
        return pl.BlockSpec((tm, d), lambda i: (i, 0))
    return pl.BlockSpec((None, 1, d), lambda i: (i // tiles_per_batch, 0, 0))


def _premix_seq(x, sh, sc, g, w_in, ln_g, ln_b, ws_tril, bs_full, seq_len, tm):
    t, d = x.shape
    sw = ln_g.shape[1]
    tpb = seq_len // tm
    mod = _mod_spec(False, tm, d, tpb)
    return pl.pallas_call(
        functools.partial(_premix_seq_kernel, sw=sw),
        grid=(t // tm,),
        in_specs=[_row_spec(tm, d), mod, mod, _full_spec(g.shape), _full_spec(w_in.shape),
                  _full_spec(ln_g.shape), _full_spec(ln_b.shape), _full_spec(ws_tril.shape),
                  _full_spec(bs_full.shape)],
        out_specs=[_row_spec(tm, sw), _row_spec(tm, sw)],
        out_shape=[jax.ShapeDtypeStruct((t, sw), F32), jax.ShapeDtypeStruct((t, sw), F32)],
        compiler_params=_params(1),
        name="premix_seq",
    )(x, sh, sc, g, w_in, ln_g, ln_b, ws_tril, bs_full)


def _premix_step(x, sh, sc, g, w_in, ln_g, ln_b, w0, b0):
    t, d = x.shape
    sw = ln_g.shape[1]
    mod = _mod_spec(True, t, d, 1)
    return pl.pallas_call(
        functools.partial(_premix_step_kernel, sw=sw),
        grid=(1,),
        in_specs=[_row_spec(t, d), mod, mod, _full_spec(g.shape), _full_spec(w_in.shape),
                  _full_spec(ln_g.shape), _full_spec(ln_b.shape), _full_spec(w0.shape),
                  _full_spec(b0.shape)],
        out_specs=[_row_spec(t, sw)] * 3,
        out_shape=[jax.ShapeDtypeStruct((t, sw), F32)] * 3,
        compiler_params=_params(1),
        name="premix_step",
    )(x, sh, sc, g, w_in, ln_g, ln_b, w0, b0)


def _s5_output(xs, u, cj_ref, d_ref, wglu_ref, bglu_ref, n_chunks):
    cw = xs.shape[1] // n_chunks
    xb = xs.astype(BF16)
    ys = [jnp.dot(xb[:, j * cw:(j + 1) * cw], cj_ref[j], preferred_element_type=F32)
          for j in range(n_chunks)]
    y = jnp.concatenate(ys, axis=-1) + d_ref[...] * u
    y = jax.nn.gelu(y)
    gate = jax.nn.sigmoid(_dot(y, wglu_ref[...]) + bglu_ref[...])
    return y * gate


def _s5_seq_kernel(u_ref, bj_ref, cj_ref, are_ref, aim_ref, d_ref, wglu_ref, bglu_ref,
                   y_ref, sre_ref, sim_ref, bu_ref, st_ref, *, nb, tl, n_chunks, unroll):
    i = pl.program_id(0)
    n_tiles = bu_ref.shape[0]
    tpc = n_tiles // n_chunks
    hpc = tpc // 2
    hw = hpc * LANES
    kw = u_ref.shape[2] // n_chunks

    @pl.when(i == 0)
    def _():
        st_ref[...] = jnp.zeros_like(st_ref)

    for b in range(nb):
        ub = u_ref[b].astype(BF16)
        for j in range(n_chunks):
            bu = jnp.dot(ub[:, j * kw:(j + 1) * kw], bj_ref[j], preferred_element_type=F32)
            for c in range(tpc):
                bu_ref[j * tpc + c, pl.ds(b, tl, stride=nb), :] = bu[:, c * LANES:(c + 1) * LANES]

    for j in range(n_chunks):
        re = slice(j * 2 * hw, j * 2 * hw + hw)
        im = slice(j * 2 * hw + hw, (j + 1) * 2 * hw)
        ar = jnp.broadcast_to(are_ref[:, j * hw:(j + 1) * hw], (nb, hw))
        ai = jnp.broadcast_to(aim_ref[:, j * hw:(j + 1) * hw], (nb, hw))
        re_tiles = [j * tpc + c for c in range(hpc)]
        im_tiles = [j * tpc + hpc + c for c in range(hpc)]

        def body(s, carry, ar=ar, ai=ai, re_tiles=re_tiles, im_tiles=im_tiles):
            xr, xi = carry
            for k in range(unroll):
                r0 = pl.multiple_of((s * unroll + k) * nb, nb)
                br = jnp.concatenate([bu_ref[c, pl.ds(r0, nb), :] for c in re_tiles], axis=-1)
                bi = jnp.concatenate([bu_ref[c, pl.ds(r0, nb), :] for c in im_tiles], axis=-1)
                nr = ar * xr - ai * xi + br
                ni = ar * xi + ai * xr + bi
                for q, c in enumerate(re_tiles):
                    bu_ref[c, pl.ds(r0, nb), :] = nr[:, q * LANES:(q + 1) * LANES]
                for q, c in enumerate(im_tiles):
                    bu_ref[c, pl.ds(r0, nb), :] = ni[:, q * LANES:(q + 1) * LANES]
                xr, xi = nr, ni
            return xr, xi

        xr, xi = lax.fori_loop(0, tl // unroll, body, (st_ref[:, re], st_ref[:, im]))
        st_ref[:, re] = xr
        st_ref[:, im] = xi
        sre_ref[:, j * hw:(j + 1) * hw] = xr
        sim_ref[:, j * hw:(j + 1) * hw] = xi

    for b in range(nb):
        xs = jnp.concatenate([bu_ref[c, pl.ds(b, tl, stride=nb), :] for c in range(n_tiles)],
                             axis=-1)
        y_ref[b] = _s5_output(xs, u_ref[b], cj_ref, d_ref, wglu_ref, bglu_ref, n_chunks)


def _s5_seq(u, bj, cj, a_re, a_im, d_vec, w_glu, b_glu, tl, unroll=4):
    nb, seq_len, sw = u.shape
    n_chunks, _, cw = bj.shape
    ns = a_re.shape[1]
    kern = functools.partial(_s5_seq_kernel, nb=nb, tl=tl, n_chunks=n_chunks, unroll=unroll)
    return pl.pallas_call(
        kern,
        grid=(seq_len // tl,),
        in_specs=[pl.BlockSpec((nb, tl, sw), lambda i: (0, i, 0)),
                  _full_spec(bj.shape), _full_spec(cj.shape), _full_spec(a_re.shape),
                  _full_spec(a_im.shape), _full_spec(d_vec.shape), _full_spec(w_glu.shape),
                  _full_spec(b_glu.shape)],
        out_specs=[pl.BlockSpec((nb, tl, sw), lambda i: (0, i, 0)),
                   _full_spec((nb, ns)), _full_spec((nb, ns))],
        out_shape=[jax.ShapeDtypeStruct((nb, seq_len, sw), F32),
                   jax.ShapeDtypeStruct((nb, ns), F32), jax.ShapeDtypeStruct((nb, ns), F32)],
        scratch_shapes=[pltpu.VMEM((n_chunks * cw // LANES, tl * nb, LANES), F32),
                        pltpu.VMEM((nb, n_chunks * cw), F32)],
        compiler_params=_params(1),
        name="s5_seq",
    )(u, bj, cj, a_re, a_im, d_vec, w_glu, b_glu)


def _s5_step_kernel(u_ref, hre_ref, him_ref, bj_ref, cj_ref, are_ref, aim_ref, d_ref,
                    wglu_ref, bglu_ref, y_ref, sre_ref, sim_ref, *, n_chunks):
    u = u_ref[...]
    ub = u.astype(BF16)
    kw = u.shape[1] // n_chunks
    hw = bj_ref.shape[2] // 2
    cols = []
    for j in range(n_chunks):
        st = slice(j * hw, (j + 1) * hw)
        bu = jnp.dot(ub[:, j * kw:(j + 1) * kw], bj_ref[j], preferred_element_type=F32)
        ar, ai = are_ref[:, st], aim_ref[:, st]
        hr, hi = hre_ref[:, st], him_ref[:, st]
        xr = ar * hr - ai * hi + bu[:, :hw]
        xi = ar * hi + ai * hr + bu[:, hw:]
        sre_ref[:, st] = xr
        sim_ref[:, st] = xi
        cols += [xr, xi]
    xs = jnp.concatenate(cols, axis=-1)
    y_ref[...] = _s5_output(xs, u, cj_ref, d_ref, wglu_ref, bglu_ref, n_chunks)


def _s5_step(u, h_re, h_im, bj, cj, a_re, a_im, d_vec, w_glu, b_glu):
    t, sw = u.shape
    ns = a_re.shape[1]
    args = (u, h_re, h_im, bj, cj, a_re, a_im, d_vec, w_glu, b_glu)
    return pl.pallas_call(
        functools.partial(_s5_step_kernel, n_chunks=bj.shape[0]),
        grid=(1,),
        in_specs=[_full_spec(a.shape) for a in args],
        out_specs=[_full_spec((t, sw)), _full_spec((t, ns)), _full_spec((t, ns))],
        out_shape=[jax.ShapeDtypeStruct((t, sw), F32), jax.ShapeDtypeStruct((t, ns), F32),
                   jax.ShapeDtypeStruct((t, ns), F32)],
        compiler_params=_params(1),
        name="s5_step",
    )(*args)


def _group_reduce(x, lane, width, op):
    n = x.shape[-1]
    s = 1
    while s < width:
        up = pltpu.roll(x, s, 1)
        dn = pltpu.roll(x, n - s, 1)
        x = op(x, jnp.where((lane & s) != 0, up, dn))
        s *= 2
    return x


def _first_max(x, lane, big):
    m = jnp.max(x, axis=-1, keepdims=True)
    return jnp.min(jnp.where(x == m, lane, big), axis=-1, keepdims=True)


def _route(logits, bias):
    rows, n = logits.shape
    lane = lax.broadcasted_iota(jnp.int32, (rows, n), 1)
    lane_f = lane.astype(F32)
    neg = jnp.float32(-jnp.inf)
    big = jnp.float32(n)
    gsz = N_EXPERTS // N_ROUTE_GROUPS
    group_start = (lane - (lane & (gsz - 1))).astype(F32)
    scores = jax.nn.sigmoid(logits)
    biased = scores + bias
    m1 = _group_reduce(biased, lane, gsz, jnp.maximum)
    first = _group_reduce(jnp.where(biased == m1, lane_f, big), lane, gsz, jnp.minimum)
    m2 = _group_reduce(jnp.where(lane_f == first, neg, biased), lane, gsz, jnp.maximum)
    gs = m1 + m2
    gsel = jnp.zeros((rows, n), F32)
    work = gs
    for _ in range(TOPK_ROUTE_GROUPS):
        hit = group_start == _first_max(work, lane_f, big)
        gsel = jnp.where(hit, 1.0, gsel)
        work = jnp.where(hit, neg, work)
    work = jnp.where(gsel > 0.0, biased, neg)
    w = jnp.zeros((rows, n), F32)
    for _ in range(TOP_K):
        hit = lane_f == _first_max(work, lane_f, big)
        w = jnp.where(hit, scores, w)
        work = jnp.where(hit, neg, work)
    return w / jnp.sum(w, axis=-1, keepdims=True) * ROUTED_SCALE


def _split_bf16(x):
    hi = x.astype(BF16)
    lo = (x - hi.astype(F32)).astype(BF16)
    return hi, lo


def _postmix_kernel(x_ref, ya_ref, yb_ref, gt1_ref, sh2_ref, sc2_ref, gt2_ref, g2_ref,
                    wout_ref, rwh_ref, rwl_ref, rb_ref, swg_ref, swu_ref, swd_ref,
                    xacc_ref, h2_ref, wd_ref, *, token_tiles):
    y = jnp.concatenate([ya_ref[...], yb_ref[...]], axis=-1)
    x1 = x_ref[...] + gt1_ref[...] * _dot(y, wout_ref[...])
    h2 = _rmsnorm(x1, g2_ref[...]) * (1.0 + sc2_ref[...]) + sh2_ref[...]
    hb = h2.astype(BF16)
    if token_tiles:
        tm, d = h2.shape
        nj = d // LANES
        for j in range(nj):
            h2_ref[pl.ds(j, tm, stride=nj), :] = h2[:, j * LANES:(j + 1) * LANES]
    else:
        h2_ref[...] = hb
    h_hi, h_lo = _split_bf16(h2)
    logits = (jnp.dot(h_hi, rwh_ref[...], preferred_element_type=F32)
              + jnp.dot(h_hi, rwl_ref[...], preferred_element_type=F32)
              + jnp.dot(h_lo, rwh_ref[...], preferred_element_type=F32))
    wd_ref[...] = _route(logits, rb_ref[...])
    hid = jax.nn.silu(jnp.dot(hb, swg_ref[...], preferred_element_type=F32)) * \
        jnp.dot(hb, swu_ref[...], preferred_element_type=F32)
    shared = _dot(hid, swd_ref[...])
    xacc_ref[...] = x1 + gt2_ref[...] * shared


def _postmix(x, ya, yb, gt1, sh2, sc2, gt2, g2, w_out, rw_hi, rw_lo, rb, swg, swu, swd,
             per_row, seq_len, tm, token_tiles):
    t, d = x.shape
    sw = ya.shape[1]
    ne = rw_hi.shape[1]
    nj = d // LANES
    mod = _mod_spec(per_row, tm, d, max(seq_len // tm, 1))
    weights = (g2, w_out, rw_hi, rw_lo, rb, swg, swu, swd)
    if token_tiles:
        h2_spec, h2_shape = _row_spec(tm * nj, LANES), jax.ShapeDtypeStruct((t * nj, LANES), F32)
    else:
        h2_spec, h2_shape = _row_spec(tm, d), jax.ShapeDtypeStruct((t, d), BF16)
    return pl.pallas_call(
        functools.partial(_postmix_kernel, token_tiles=token_tiles),
        grid=(t // tm,),
        in_specs=[_row_spec(tm, d), _row_spec(tm, sw), _row_spec(tm, sw), mod, mod, mod, mod]
        + [_full_spec(w.shape) for w in weights],
        out_specs=[_row_spec(tm, d), h2_spec, _row_spec(tm, ne)],
        out_shape=[jax.ShapeDtypeStruct((t, d), F32), h2_shape,
                   jax.ShapeDtypeStruct((t, ne), F32)],
        compiler_params=_params(1),
        name="postmix",
    )(x, ya, yb, gt1, sh2, sc2, gt2, *weights)


def _moe_dense_kernel(h_ref, wd_ref, xacc_ref, gt2_ref, gf_ref, wg_ref, wu_ref, wdn_ref,
                      o_ref, acc_ref, *, final_norm):
    e = pl.program_id(1)

    @pl.when(e == 0)
    def _():
        acc_ref[...] = jnp.zeros_like(acc_ref)

    hb = h_ref[...]
    wd = wd_ref[...]
    lane = lax.broadcasted_iota(jnp.int32, wd.shape, 1)
    col = jnp.sum(jnp.where(lane == e, wd, 0.0), axis=-1, keepdims=True)
    hid = jax.nn.silu(jnp.dot(hb, wg_ref[...], preferred_element_type=F32)) * \
        jnp.dot(hb, wu_ref[...], preferred_element_type=F32)
    acc_ref[...] += col * _dot(hid, wdn_ref[...])

    @pl.when(e == pl.num_programs(1) - 1)
    def _():
        xo = xacc_ref[...] + gt2_ref[...] * acc_ref[...]
        o_ref[...] = _rmsnorm(xo, gf_ref[...]) if final_norm else xo


def _moe_dense(h2, wd, xacc, gt2, g_final, wg, wu, wdn, per_row, seq_len, tm, final_norm):
    t, d = xacc.shape
    ne, _, eh = wg.shape
    nl = wd.shape[1]
    tpb = max(seq_len // tm, 1)
    if per_row:
        mod = pl.BlockSpec((tm, d), lambda i, e: (i, 0))
    else:
        mod = pl.BlockSpec((None, 1, d), lambda i, e: (i // tpb, 0, 0))
    row = lambda w: pl.BlockSpec((tm, w), lambda i, e: (i, 0))
    return pl.pallas_call(
        functools.partial(_moe_dense_kernel, final_norm=final_norm),
        grid=(t // tm, ne),
        in_specs=[row(d), row(nl), row(d), mod,
                  pl.BlockSpec(g_final.shape, lambda i, e: (0, 0)),
                  pl.BlockSpec((None, d, eh), lambda i, e: (e, 0, 0)),
                  pl.BlockSpec((None, d, eh), lambda i, e: (e, 0, 0)),
                  pl.BlockSpec((None, eh, d), lambda i, e: (e, 0, 0))],
        out_specs=row(d),
        out_shape=jax.ShapeDtypeStruct((t, d), F32),
        scratch_shapes=[pltpu.VMEM((tm, d), F32)],
        compiler_params=_params(2),
        name="moe_dense",
    )(h2, wd, xacc, gt2, g_final, wg, wu, wdn)


TOKEN_BITS = 12


def _route_plan_kernel(wd_ref, src_ref, dst_ref, wl_ref, cnt_ref, *, tg, nj):
    wt = wd_ref[...].T[:N_EXPERTS]
    sel = wt > 0.0
    lane = lax.broadcasted_iota(jnp.int32, wt.shape, 1)
    c = sel.astype(jnp.int32)
    s = 1
    while s < tg:
        c = c + jnp.where(lane >= s, pltpu.roll(c, s, 1), 0)
        s *= 2
    cnt_ref[...] = jnp.broadcast_to(c[:, tg - 1:tg], cnt_ref.shape)
    dist = lane - (c - 1)
    key = jnp.where(sel, (dist << TOKEN_BITS) | lane, -1)
    w = jnp.where(sel, wt, 0.0)
    s = 1
    while s < tg:
        leaving = (key >= 0) & (((key >> TOKEN_BITS) & s) != 0)
        key_in = pltpu.roll(key, tg - s, 1)
        w_in = pltpu.roll(w, tg - s, 1)
        arriving = (key_in >= 0) & (((key_in >> TOKEN_BITS) & s) != 0)
        key = jnp.where(arriving, key_in, jnp.where(leaving, -1, key))
        w = jnp.where(arriving, w_in, jnp.where(leaving, 0.0, w))
        s *= 2
    tok = key & ((1 << TOKEN_BITS) - 1)
    src_ref[...] = jnp.where(key >= 0, tok, tg - 1) * nj
    dst_ref[...] = jnp.where(key >= 0, tok, tg) * nj
    wl_ref[...] = w


def _route_plan(wd, tg, nj):
    t, nl = wd.shape
    assert tg <= (1 << TOKEN_BITS) and t % tg == 0
    g = t // tg
    grp = lambda w: pl.BlockSpec((None, N_EXPERTS, w), lambda i: (i, 0, 0))
    slots = jax.ShapeDtypeStruct((g, N_EXPERTS, tg), jnp.int32)
    return pl.pallas_call(
        functools.partial(_route_plan_kernel, tg=tg, nj=nj),
        grid=(g,),
        in_specs=[pl.BlockSpec((tg, nl), lambda i: (i, 0))],
        out_specs=[grp(tg), grp(tg), grp(tg), grp(LANES)],
        out_shape=[slots, slots, jax.ShapeDtypeStruct((g, N_EXPERTS, tg), F32),
                   jax.ShapeDtypeStruct((g, N_EXPERTS, LANES), jnp.int32)],
        compiler_params=_params(1),
        name="route_plan",
    )(wd)


def _block_table(counts, tg, bm, nb_max):
    g, ne = counts.shape
    i32 = jnp.int32
    nblk = (counts + bm - 1) // bm
    bend = jnp.cumsum(nblk, axis=1)
    bstart = bend - nblk
    nb = bend[:, -1]
    k = jnp.arange(nb_max, dtype=i32)[None, :]
    kc = jnp.minimum(k, jnp.maximum(nb[:, None] - 1, 0))
    e_k = jnp.minimum(jnp.sum(bend[:, None, :] <= kc[:, :, None], axis=2), ne - 1).astype(i32)
    b_k = kc - jnp.take_along_axis(bstart, e_k, axis=1)
    first = (b_k == 0) & (k < nb[:, None])
    active = nblk > 0
    parity = (jnp.take_along_axis(jnp.cumsum(active.astype(i32), axis=1) - 1, e_k, axis=1)
              % MOE_WEIGHT_BUFFERS)
    ids = jnp.where(active, jnp.arange(ne, dtype=i32)[None, :], ne)
    first_active_from = jnp.flip(lax.cummin(jnp.flip(ids, 1), axis=1), 1)
    nxt = jnp.concatenate([first_active_from[:, 1:], jnp.full((g, 1), ne, i32)], axis=1)
    next_k = jnp.take_along_axis(jnp.where(nxt >= ne, -1, nxt), e_k, axis=1)
    row = (jnp.arange(g, dtype=i32)[:, None] * ne + e_k) * (tg // bm) + b_k
    flat = lambda a: a.reshape(-1).astype(i32)
    return (nb.astype(i32), flat(e_k), flat(first), flat(parity), flat(next_k)), flat(row)


def _moe_group_kernel(nb_ref, exp_ref, first_ref, par_ref, next_ref,
                      h_ref, src_ref, dst_ref, wl_ref, wg_hbm, wu_hbm, wdn_hbm, o_ref,
                      wg_buf, wu_buf, wdn_buf, wsem, xt_a, xt_b, hid_a, hid_b, yt_a, yt_b,
                      *, bm, nj, batch, nb_max):
    g = pl.program_id(0)
    nb = nb_ref[g]
    tb = g * nb_max
    pitch = bm + 1
    o_ref[...] = jnp.zeros_like(o_ref)

    def weight_copies(e, slot):
        return (pltpu.make_async_copy(wg_hbm.at[e], wg_buf.at[slot], wsem.at[slot]),
                pltpu.make_async_copy(wu_hbm.at[e], wu_buf.at[slot], wsem.at[slot]),
                pltpu.make_async_copy(wdn_hbm.at[e], wdn_buf.at[slot], wsem.at[slot]))

    n_wbuf = wg_buf.shape[0]

    def handoff(k):
        @pl.when(first_ref[tb + k] == 1)
        def _():
            slot = par_ref[tb + k]
            for c in weight_copies(exp_ref[tb + k], slot):
                c.wait()
            nxt = next_ref[tb + k]

            @pl.when(nxt >= 0)
            def _():
                for c in weight_copies(nxt, (slot + 1) % n_wbuf):
                    c.start()

    def gather_rows(k, xt):
        src = src_ref.at[pl.ds(pl.multiple_of(k * bm, bm), bm)]

        def emit(lo, hi):
            for m in range(lo, hi):
                r0 = pl.multiple_of(src[m], nj)
                xt[pl.ds(m, nj, stride=pitch), :] = h_ref[pl.ds(r0, nj), :]
        return emit

    def scatter_rows(k, yt):
        base = pl.multiple_of(k * bm, bm)
        dst = dst_ref.at[pl.ds(base, bm)]
        wts = wl_ref.at[pl.ds(base, bm)]

        def emit(lo, hi):
            for m0 in range(lo, hi, batch):
                rows, vals = [], []
                for m in range(m0, min(m0 + batch, hi)):
                    r0 = pl.multiple_of(dst[m], nj)
                    rows.append(r0)
                    vals.append(o_ref[pl.ds(r0, nj), :]
                                + wts[m] * yt[pl.ds(m, nj, stride=pitch), :])
                for r0, v in zip(rows, vals):
                    o_ref[pl.ds(r0, nj), :] = v
        return emit

    n_chunks = nj // 2
    kc = 2 * LANES
    n_pieces = 2 * n_chunks + 1
    cuts = [bm * i // n_pieces for i in range(n_pieces + 1)]

    def step(up_blk, down_blk, side_work):
        piece = 0

        def side():
            nonlocal piece
            for emit in side_work:
                emit(cuts[piece], cuts[piece + 1])
            piece += 1

        gate = up = None
        for c in range(n_chunks):
            if up_blk is not None:
                k, xt, _ = up_blk
                slot = par_ref[tb + k]
                xc = jnp.concatenate([xt[j * pitch:j * pitch + bm, :] for j in (2 * c, 2 * c + 1)],
                                     axis=-1).astype(BF16)
                pg = jnp.dot(xc, wg_buf[slot, c * kc:(c + 1) * kc, :], preferred_element_type=F32)
                pu = jnp.dot(xc, wu_buf[slot, c * kc:(c + 1) * kc, :], preferred_element_type=F32)
                gate = pg if gate is None else gate + pg
                up = pu if up is None else up + pu
            side()
            if down_blk is not None:
                k, hid, yt = down_blk
                slot = par_ref[tb + k]
                yc = jnp.dot(hid[...], wdn_buf[slot, :, c * kc:(c + 1) * kc],
                             preferred_element_type=F32)
                for q in range(2):
                    j = 2 * c + q
                    yt[j * pitch:j * pitch + bm, :] = yc[:, q * LANES:(q + 1) * LANES]
            side()
        if up_blk is not None:
            up_blk[2][...] = (jax.nn.silu(gate) * up).astype(BF16)
        side()

    @pl.when(nb > 0)
    def _():
        for c in weight_copies(exp_ref[tb], 0):
            c.start()
        gather_rows(0, xt_a)(0, bm)
        handoff(0)
        step((0, xt_a, hid_a), None, [gather_rows(1, xt_b)])
        handoff(1)
        step((1, xt_b, hid_b), (0, hid_a, yt_a), [gather_rows(2, xt_a)])

        def trip(i, carry):
            t = 2 * i + 1
            handoff(t + 1)
            step((t + 1, xt_a, hid_a), (t, hid_b, yt_b),
                 [gather_rows(t + 2, xt_b), scatter_rows(t - 1, yt_a)])
            handoff(t + 2)
            step((t + 2, xt_b, hid_b), (t + 1, hid_a, yt_a),
                 [gather_rows(t + 3, xt_a), scatter_rows(t, yt_b)])
            return carry

        lax.fori_loop(0, nb // 2, trip, 0)

        @pl.when(nb % 2 == 1)
        def _():
            scatter_rows(nb - 1, yt_a)(0, bm)


def _moe_group(table, h2t, src, dst, wl, wg, wu, wdn, tg, bm, nb_max, batch=8):
    ne, d, eh = wg.shape
    nj = d // LANES
    assert nj == SUBLANES
    g = h2t.shape[0] // (tg * nj)
    kern = functools.partial(_moe_group_kernel, bm=bm, nj=nj, batch=batch, nb_max=nb_max)
    slots = pl.BlockSpec((nb_max * bm,), lambda gi, *_: (gi,), memory_space=pltpu.SMEM,
                         pipeline_mode=pl.Buffered(1))
    hbm = pl.BlockSpec(memory_space=pl.ANY)
    pitch_rows = nj * (bm + 1)
    grid_spec = pltpu.PrefetchScalarGridSpec(
        num_scalar_prefetch=len(table),
        grid=(g,),
        in_specs=[pl.BlockSpec((tg * nj, LANES), lambda gi, *_: (gi, 0)),
                  slots, slots, slots, hbm, hbm, hbm],
        out_specs=pl.BlockSpec((None, (tg + 1) * nj, LANES), lambda gi, *_: (gi, 0, 0)),
        scratch_shapes=[pltpu.VMEM((MOE_WEIGHT_BUFFERS, d, eh), BF16),
                        pltpu.VMEM((MOE_WEIGHT_BUFFERS, d, eh), BF16),
                        pltpu.VMEM((MOE_WEIGHT_BUFFERS, eh, d), BF16),
                        pltpu.SemaphoreType.DMA((MOE_WEIGHT_BUFFERS,)),
                        pltpu.VMEM((pitch_rows, LANES), F32), pltpu.VMEM((pitch_rows, LANES), F32),
                        pltpu.VMEM((bm, eh), BF16), pltpu.VMEM((bm, eh), BF16),
                        pltpu.VMEM((pitch_rows, LANES), F32), pltpu.VMEM((pitch_rows, LANES), F32)],
    )
    return pl.pallas_call(
        kern,
        grid_spec=grid_spec,
        out_shape=jax.ShapeDtypeStruct((g, (tg + 1) * nj, LANES), F32),
        compiler_params=_params(1),
        name="moe_group",
    )(*table, h2t, src, dst, wl, wg, wu, wdn)


def _finalize_kernel(r_ref, xacc_ref, gt2_ref, gf_ref, o_ref, *, final_norm):
    tm, d = xacc_ref.shape
    nj = d // LANES
    routed = jnp.concatenate([r_ref[pl.ds(j, tm, stride=nj), :] for j in range(nj)], axis=-1)
    xo = xacc_ref[...] + gt2_ref[...] * routed
    o_ref[...] = _rmsnorm(xo, gf_ref[...]) if final_norm else xo


def _finalize(routed, xacc, gt2, g_final, tg, seq_len, tm, final_norm):
    t, d = xacc.shape
    nj = d // LANES
    tpg = tg // tm
    return pl.pallas_call(
        functools.partial(_finalize_kernel, final_norm=final_norm),
        grid=(t // tm,),
        in_specs=[pl.BlockSpec((None, tm * nj, LANES), lambda i: (i // tpg, i % tpg, 0)),
                  _row_spec(tm, d), _mod_spec(False, tm, d, seq_len // tm),
                  _full_spec(g_final.shape)],
        out_specs=_row_spec(tm, d),
        out_shape=jax.ShapeDtypeStruct((t, d), F32),
        compiler_params=_params(1),
        name="finalize",
    )(routed, xacc, gt2, g_final)


def _s5_discretize(lam_re, lam_im, log_dt, b_re, b_im, c_re, c_im):
    g, n = lam_re.shape
    hg = b_re.shape[2]
    lr = jnp.minimum(lam_re.astype(F32), -1e-4)
    li = lam_im.astype(F32)
    dt = jnp.exp(log_dt.astype(F32))[:, None]
    mag = jnp.exp(lr * dt)
    abar_r, abar_i = mag * jnp.cos(li * dt), mag * jnp.sin(li * dt)
    den = lr * lr + li * li
    nr, ni = abar_r - 1.0, abar_i
    f_r = (nr * lr + ni * li) / den
    f_i = (ni * lr - nr * li) / den
    br_, bi_ = b_re.astype(F32), b_im.astype(F32)
    bbar_r = f_r[..., None] * br_ - f_i[..., None] * bi_
    bbar_i = f_r[..., None] * bi_ + f_i[..., None] * br_
    gpc = LANES // hg
    n_chunks = g // gpc
    eye = jnp.eye(gpc, dtype=F32)

    def b_chunks(bb):
        bb = bb.reshape(n_chunks, gpc, n, hg)
        m = jnp.einsum('jgnh,gk->jghkn', bb, eye)
        return m.reshape(n_chunks, gpc * hg, gpc * n)

    def c_chunks(cc):
        cc = cc.astype(F32).reshape(n_chunks, gpc, hg, n)
        m = jnp.einsum('jghn,gk->jgnkh', cc, eye)
        return m.reshape(n_chunks, gpc * n, gpc * hg)

    bj = jnp.concatenate([b_chunks(bbar_r), b_chunks(bbar_i)], axis=2).astype(BF16)
    cj = jnp.concatenate([c_chunks(c_re), -c_chunks(c_im)], axis=1).astype(BF16)
    return abar_r.reshape(1, g * n), abar_i.reshape(1, g * n), bj, cj


def kernel(x_prompt, x_sample, c_prompt, c_sample, state_ssm_re, state_ssm_im, g_norm1, g_norm2, w_ada, b_ada, w_in, w_out, ssm_lambda_re, ssm_lambda_im, ssm_log_dt, ssm_b_re, ssm_b_im, ssm_c_re, ssm_c_im, ssm_d, ssm_w_glu, ssm_b_glu, gm_ln_g, gm_ln_b, gm_w_s, gm_b_s, router_w, router_bias, moe_w_gate, moe_w_up, moe_w_down, shared_w_gate, shared_w_up, shared_w_down, g_final):
    bp, seq_len, d = x_prompt.shape
    bs, dec_len, _ = x_sample.shape
    depth = w_ada.shape[0]
    assert dec_len == 1 and seq_len % CHUNK == 0 and bp % SUBLANES == 0
    groups, n_state = ssm_lambda_re.shape[1:]
    sw = groups * SSM_GROUP
    gw = gm_ln_g.shape[1]
    hd = gw // GM_HEADS

    xp = x_prompt.reshape(bp * seq_len, d)
    xs = x_sample.reshape(bs, d)
    c_all = jnp.concatenate([c_prompt, c_sample], axis=0)
    gf = g_final.reshape(1, d)

    p_re, p_im, s_re, s_im, s_v = [], [], [], [], []
    for l in range(depth):
        mod = _ada(c_all, w_ada[l].astype(BF16), b_ada[l].reshape(1, -1))
        mod_p = [m.reshape(bp, 1, d) for m in jnp.split(mod[:bp], 6, axis=-1)]
        mod_s = jnp.split(mod[bp:], 6, axis=-1)

        g1 = g_norm1[l].reshape(1, d)
        g2 = g_norm2[l].reshape(1, d)
        w_in_b = w_in[l].astype(BF16)
        w_out_b = w_out[l].astype(BF16)
        ln_g = gm_ln_g[l].reshape(1, gw)
        ln_b = gm_ln_b[l].reshape(1, gw)
        ws_tril = (gm_w_s[l] * jnp.tril(jnp.ones((CHUNK, CHUNK), F32))[None]).astype(BF16)
        bs_full = jnp.repeat(gm_b_s[l].T, hd, axis=1)
        w0 = jnp.repeat(gm_w_s[l][:, 0, 0], hd).reshape(1, gw)
        b0 = jnp.repeat(gm_b_s[l][:, 0], hd).reshape(1, gw)
        a_re, a_im, bj, cj = _s5_discretize(ssm_lambda_re[l], ssm_lambda_im[l], ssm_log_dt[l],
                                            ssm_b_re[l], ssm_b_im[l], ssm_c_re[l], ssm_c_im[l])
        d_vec = ssm_d[l].reshape(1, sw)
        w_glu = ssm_w_glu[l].astype(BF16)
        b_glu = ssm_b_glu[l].reshape(1, sw)
        pad = LANES - N_EXPERTS
        rw = jnp.pad(router_w[l], ((0, 0), (0, pad)))
        rw_hi = rw.astype(BF16)
        rw_lo = (rw - rw_hi.astype(F32)).astype(BF16)
        rb = jnp.pad(router_bias[l], (0, pad), constant_values=-jnp.inf).reshape(1, LANES)
        swg, swu, swd = (shared_w_gate[l].astype(BF16), shared_w_up[l].astype(BF16),
                         shared_w_down[l].astype(BF16))
        ewg, ewu, ewd = (moe_w_gate[l].astype(BF16), moe_w_up[l].astype(BF16),
                         moe_w_down[l].astype(BF16))

        tm_row = min(ROW_TILE, seq_len)
        ua, yb = _premix_seq(xp, mod_p[0], mod_p[1], g1, w_in_b, ln_g, ln_b, ws_tril, bs_full,
                             seq_len, tm=tm_row)
        ya, hpr, hpi = _s5_seq(ua.reshape(bp, seq_len, sw), bj, cj, a_re, a_im, d_vec,
                               w_glu, b_glu, tl=min(SCAN_TILE, seq_len))
        xacc, h2, wd = _postmix(xp, ya.reshape(bp * seq_len, sw), yb, mod_p[2], mod_p[3],
                                mod_p[4], mod_p[5], g2, w_out_b, rw_hi, rw_lo, rb, swg, swu, swd,
                                per_row=False, seq_len=seq_len, tm=tm_row, token_tiles=True)
        last = l == depth - 1
        tg = min(MOE_GROUP, seq_len)
        bm = min(MOE_BLOCK, tg)
        nb_max = -(-(tg * TOP_K // bm + N_EXPERTS + 3) // SUBLANES) * SUBLANES
        src, dst, slot_w, counts = _route_plan(wd, tg, d // LANES)
        table, block_row = _block_table(counts[:, :, 0], tg, bm, nb_max)
        by_block = lambda a: jnp.take(a.reshape(-1, bm), block_row, axis=0).reshape(-1)
        routed = _moe_group(table, h2, by_block(src), by_block(dst), by_block(slot_w),
                            ewg, ewu, ewd, tg, bm, nb_max)
        xp = _finalize(routed, xacc, mod_p[5], gf, tg, seq_len, tm_row, final_norm=last)
        p_re.append(hpr.reshape(bp, groups, n_state))
        p_im.append(hpi.reshape(bp, groups, n_state))

        ua, yb, vn = _premix_step(xs, mod_s[0], mod_s[1], g1, w_in_b, ln_g, ln_b, w0, b0)
        ya, hsr, hsi = _s5_step(ua, state_ssm_re[l].reshape(bs, groups * n_state),
                                state_ssm_im[l].reshape(bs, groups * n_state),
                                bj, cj, a_re, a_im, d_vec, w_glu, b_glu)
        xacc, h2, wd = _postmix(xs, ya, yb, mod_s[2], mod_s[3], mod_s[4], mod_s[5], g2, w_out_b,
                                rw_hi, rw_lo, rb, swg, swu, swd, per_row=True, seq_len=1, tm=bs,
                                token_tiles=False)
        xs = _moe_dense(h2, wd, xacc, mod_s[5], gf, ewg, ewu, ewd, per_row=True, seq_len=1,
                        tm=bs, final_norm=last)
        s_re.append(hsr.reshape(bs, groups, n_state))
        s_im.append(hsi.reshape(bs, groups, n_state))
        s_v.append(vn.reshape(bs, 1, gw))

    y_prompt = xp.reshape(bp, seq_len, d)
    y_sample = xs.reshape(bs, 1, d)
    return (y_prompt, y_sample, jnp.stack(p_re), jnp.stack(p_im), jnp.stack(s_re),
            jnp.stack(s_im), jnp.stack(s_v))
```

```python
import functools

import jax
import jax.numpy as jnp
from jax import lax
from jax.experimental import pallas as pl
from jax.experimental.pallas import tpu as pltpu

F32 = jnp.float32
BF16 = jnp.bfloat16

EPS = 1e-6
SSM_GROUP = 16
SSM_STATE = 64
GM_HEADS = 4
CHUNK = 128
N_EXPERTS = 64
TOP_K = 8
N_ROUTE_GROUPS = 8
TOPK_ROUTE_GROUPS = 4
ROUTED_SCALE = 2.5

LANES = 128
SUBLANES = 8
VMEM_LIMIT = 48 * 1024 * 1024
ROW_TILE = 512
SCAN_TILE = 128
MOE_GROUP = 2048
MOE_BLOCK = 128
MOE_WEIGHT_BUFFERS = 4


def _params(n_axes):
    return pltpu.CompilerParams(dimension_semantics=("arbitrary",) * n_axes,
                                vmem_limit_bytes=VMEM_LIMIT)


def _dot(a, b):
    return jnp.dot(a.astype(BF16), b.astype(BF16), preferred_element_type=F32)


def _rmsnorm(x, g):
    ms = jnp.mean(x * x, axis=-1, keepdims=True)
    return x * lax.rsqrt(ms + EPS) * g


def _ada_kernel(c_ref, w_ref, b_ref, o_ref):
    c = c_ref[...]
    o_ref[...] = _dot(jax.nn.silu(c), w_ref[...]) + b_ref[...]


def _ada(c, w, b, tn=512):
    m, d = c.shape
    n = w.shape[1]
    return pl.pallas_call(
        _ada_kernel,
        grid=(n // tn,),
        in_specs=[pl.BlockSpec((m, d), lambda j: (0, 0)),
                  pl.BlockSpec((d, tn), lambda j: (0, j)),
                  pl.BlockSpec((1, tn), lambda j: (0, j))],
        out_specs=pl.BlockSpec((m, tn), lambda j: (0, j)),
        out_shape=jax.ShapeDtypeStruct((m, n), F32),
        compiler_params=_params(1),
        name="ada_mod",
    )(c, w, b)


def _premix_common(x_ref, sh_ref, sc_ref, g_ref, win_ref, lng_ref, lnb_ref, sw):
    x = x_ref[...]
    h = _rmsnorm(x, g_ref[...]) * (1.0 + sc_ref[...]) + sh_ref[...]
    proj = _dot(h, win_ref[...])
    u_a = proj[:, :sw]
    z = jax.nn.gelu(proj[:, sw:])
    gw = z.shape[1] // 2
    u_b = z[:, :gw]
    v_b = z[:, gw:]
    hd = gw // GM_HEADS
    parts = []
    for k in range(GM_HEADS):
        vh = v_b[:, k * hd:(k + 1) * hd]
        mu = jnp.mean(vh, axis=-1, keepdims=True)
        dv = vh - mu
        var = jnp.mean(dv * dv, axis=-1, keepdims=True)
        parts.append(dv * lax.rsqrt(var + EPS))
    vn = jnp.concatenate(parts, axis=-1) * lng_ref[...] + lnb_ref[...]
    return u_a, u_b, vn


def _premix_seq_kernel(x_ref, sh_ref, sc_ref, g_ref, win_ref, lng_ref, lnb_ref,
                       ws_ref, bs_ref, ua_ref, yb_ref, *, sw):
    u_a, u_b, vn = _premix_common(x_ref, sh_ref, sc_ref, g_ref, win_ref, lng_ref, lnb_ref, sw)
    ua_ref[...] = u_a
    tm, gw = u_b.shape
    hd = gw // GM_HEADS
    vnb = vn.astype(BF16)
    for c in range(tm // CHUNK):
        rows = slice(c * CHUNK, (c + 1) * CHUNK)
        for k in range(GM_HEADS):
            cols = slice(k * hd, (k + 1) * hd)
            s = jnp.dot(ws_ref[k], vnb[rows, cols], preferred_element_type=F32)
            yb_ref[rows, cols] = u_b[rows, cols] * (s + bs_ref[:, cols])


def _premix_step_kernel(x_ref, sh_ref, sc_ref, g_ref, win_ref, lng_ref, lnb_ref,
                        w0_ref, b0_ref, ua_ref, yb_ref, vn_ref, *, sw):
    u_a, u_b, vn = _premix_common(x_ref, sh_ref, sc_ref, g_ref, win_ref, lng_ref, lnb_ref, sw)
    ua_ref[...] = u_a
    vn_ref[...] = vn
    yb_ref[...] = u_b * (vn * w0_ref[...] + b0_ref[...])


def _row_spec(tm, d):
    return pl.BlockSpec((tm, d), lambda i: (i, 0))


def _full_spec(shape):
    nd = len(shape)
    return pl.BlockSpec(shape, lambda i: (0,) * nd)


def _mod_spec(per_row, tm, d, tiles_per_batch):
    if per_row:
        return pl.BlockSpec((tm, d), lambda i: (i, 0))
    return pl.BlockSpec((None, 1, d), lambda i: (i // tiles_per_batch, 0, 0))


def _premix_seq(x, sh, sc, g, w_in, ln_g, ln_b, ws_tril, bs_full, seq_len, tm):
    t, d = x.shape
    sw = ln_g.shape[1]
    tpb = seq_len // tm
    mod = _mod_spec(False, tm, d, tpb)
    return pl.pallas_call(
        functools.partial(_premix_seq_kernel, sw=sw),
        grid=(t // tm,),
        in_specs=[_row_spec(tm, d), mod, mod, _full_spec(g.shape), _full_spec(w_in.shape),
                  _full_spec(ln_g.shape), _full_spec(ln_b.shape), _full_spec(ws_tril.shape),
                  _full_spec(bs_full.shape)],
        out_specs=[_row_spec(tm, sw), _row_spec(tm, sw)],
        out_shape=[jax.ShapeDtypeStruct((t, sw), F32), jax.ShapeDtypeStruct((t, sw), F32)],
        compiler_params=_params(1),
        name="premix_seq",
    )(x, sh, sc, g, w_in, ln_g, ln_b, ws_tril, bs_full)


def _premix_step(x, sh, sc, g, w_in, ln_g, ln_b, w0, b0):
    t, d = x.shape
    sw = ln_g.shape[1]
    mod = _mod_spec(True, t, d, 1)
    return pl.pallas_call(
        functools.partial(_premix_step_kernel, sw=sw),
        grid=(1,),
        in_specs=[_row_spec(t, d), mod, mod, _full_spec(g.shape), _full_spec(w_in.shape),
                  _full_spec(ln_g.shape), _full_spec(ln_b.shape), _full_spec(w0.shape),
                  _full_spec(b0.shape)],
        out_specs=[_row_spec(t, sw)] * 3,
        out_shape=[jax.ShapeDtypeStruct((t, sw), F32)] * 3,
        compiler_params=_params(1),
        name="premix_step",
    )(x, sh, sc, g, w_in, ln_g, ln_b, w0, b0)


def _s5_output(xs, u, cj_ref, d_ref, wglu_ref, bglu_ref, n_chunks):
    cw = xs.shape[1] // n_chunks
    xb = xs.astype(BF16)
    ys = [jnp.dot(xb[:, j * cw:(j + 1) * cw], cj_ref[j], preferred_element_type=F32)
          for j in range(n_chunks)]
    y = jnp.concatenate(ys, axis=-1) + d_ref[...] * u
    y = jax.nn.gelu(y)
    gate = jax.nn.sigmoid(_dot(y, wglu_ref[...]) + bglu_ref[...])
    return y * gate


def _s5_seq_kernel(u_ref, bj_ref, cj_ref, are_ref, aim_ref, d_ref, wglu_ref, bglu_ref,
                   y_ref, sre_ref, sim_ref, bu_ref, st_ref, *, nb, tl, n_chunks, unroll):
    i = pl.program_id(0)
    n_tiles = bu_ref.shape[0]
    tpc = n_tiles // n_chunks
    hpc = tpc // 2
    hw = hpc * LANES
    kw = u_ref.shape[2] // n_chunks

    @pl.when(i == 0)
    def _():
        st_ref[...] = jnp.zeros_like(st_ref)

    for b in range(nb):
        ub = u_ref[b].astype(BF16)
        for j in range(n_chunks):
            bu = jnp.dot(ub[:, j * kw:(j + 1) * kw], bj_ref[j], preferred_element_type=F32)
            for c in range(tpc):
                bu_ref[j * tpc + c, pl.ds(b, tl, stride=nb), :] = bu[:, c * LANES:(c + 1) * LANES]

    for j in range(n_chunks):
        re = slice(j * 2 * hw, j * 2 * hw + hw)
        im = slice(j * 2 * hw + hw, (j + 1) * 2 * hw)
        ar = jnp.broadcast_to(are_ref[:, j * hw:(j + 1) * hw], (nb, hw))
        ai = jnp.broadcast_to(aim_ref[:, j * hw:(j + 1) * hw], (nb, hw))
        re_tiles = [j * tpc + c for c in range(hpc)]
        im_tiles = [j * tpc + hpc + c for c in range(hpc)]

        def body(s, carry, ar=ar, ai=ai, re_tiles=re_tiles, im_tiles=im_tiles):
            xr, xi = carry
            for k in range(unroll):
                r0 = pl.multiple_of((s * unroll + k) * nb, nb)
                br = jnp.concatenate([bu_ref[c, pl.ds(r0, nb), :] for c in re_tiles], axis=-1)
                bi = jnp.concatenate([bu_ref[c, pl.ds(r0, nb), :] for c in im_tiles], axis=-1)
                nr = ar * xr - ai * xi + br
                ni = ar * xi + ai * xr + bi
                for q, c in enumerate(re_tiles):
                    bu_ref[c, pl.ds(r0, nb), :] = nr[:, q * LANES:(q + 1) * LANES]
                for q, c in enumerate(im_tiles):
                    bu_ref[c, pl.ds(r0, nb), :] = ni[:, q * LANES:(q + 1) * LANES]
                xr, xi = nr, ni
            return xr, xi

        xr, xi = lax.fori_loop(0, tl // unroll, body, (st_ref[:, re], st_ref[:, im]))
        st_ref[:, re] = xr
        st_ref[:, im] = xi
        sre_ref[:, j * hw:(j + 1) * hw] = xr
        sim_ref[:, j * hw:(j + 1) * hw] = xi

    for b in range(nb):
        xs = jnp.concatenate([bu_ref[c, pl.ds(b, tl, stride=nb), :] for c in range(n_tiles)],
                             axis=-1)
        y_ref[b] = _s5_output(xs, u_ref[b], cj_ref, d_ref, wglu_ref, bglu_ref, n_chunks)


def _s5_seq(u, bj, cj, a_re, a_im, d_vec, w_glu, b_glu, tl, unroll=4):
    nb, seq_len, sw = u.shape
    n_chunks, _, cw = bj.shape
    ns = a_re.shape[1]
    kern = functools.partial(_s5_seq_kernel, nb=nb, tl=tl, n_chunks=n_chunks, unroll=unroll)
    return pl.pallas_call(
        kern,
        grid=(seq_len // tl,),
        in_specs=[pl.BlockSpec((nb, tl, sw), lambda i: (0, i, 0)),
                  _full_spec(bj.shape), _full_spec(cj.shape), _full_spec(a_re.shape),
                  _full_spec(a_im.shape), _full_spec(d_vec.shape), _full_spec(w_glu.shape),
                  _full_spec(b_glu.shape)],
        out_specs=[pl.BlockSpec((nb, tl, sw), lambda i: (0, i, 0)),
                   _full_spec((nb, ns)), _full_spec((nb, ns))],
        out_shape=[jax.ShapeDtypeStruct((nb, seq_len, sw), F32),
                   jax.ShapeDtypeStruct((nb, ns), F32), jax.ShapeDtypeStruct((nb, ns), F32)],
        scratch_shapes=[pltpu.VMEM((n_chunks * cw // LANES, tl * nb, LANES), F32),
                        pltpu.VMEM((nb, n_chunks * cw), F32)],
        compiler_params=_params(1),
        name="s5_seq",
    )(u, bj, cj, a_re, a_im, d_vec, w_glu, b_glu)


def _s5_step_kernel(u_ref, hre_ref, him_ref, bj_ref, cj_ref, are_ref, aim_ref, d_ref,
                    wglu_ref, bglu_ref, y_ref, sre_ref, sim_ref, *, n_chunks):
    u = u_ref[...]
    ub = u.astype(BF16)
    kw = u.shape[1] // n_chunks
    hw = bj_ref.shape[2] // 2
    cols = []
    for j in range(n_chunks):
        st = slice(j * hw, (j + 1) * hw)
        bu = jnp.dot(ub[:, j * kw:(j + 1) * kw], bj_ref[j], preferred_element_type=F32)
        ar, ai = are_ref[:, st], aim_ref[:, st]
        hr, hi = hre_ref[:, st], him_ref[:, st]
        xr = ar * hr - ai * hi + bu[:, :hw]
        xi = ar * hi + ai * hr + bu[:, hw:]
        sre_ref[:, st] = xr
        sim_ref[:, st] = xi
        cols += [xr, xi]
    xs = jnp.concatenate(cols, axis=-1)
    y_ref[...] = _s5_output(xs, u, cj_ref, d_ref, wglu_ref, bglu_ref, n_chunks)


def _s5_step(u, h_re, h_im, bj, cj, a_re, a_im, d_vec, w_glu, b_glu):
    t, sw = u.shape
    ns = a_re.shape[1]
    args = (u, h_re, h_im, bj, cj, a_re, a_im, d_vec, w_glu, b_glu)
    return pl.pallas_call(
        functools.partial(_s5_step_kernel, n_chunks=bj.shape[0]),
        grid=(1,),
        in_specs=[_full_spec(a.shape) for a in args],
        out_specs=[_full_spec((t, sw)), _full_spec((t, ns)), _full_spec((t, ns))],
        out_shape=[jax.ShapeDtypeStruct((t, sw), F32), jax.ShapeDtypeStruct((t, ns), F32),
                   jax.ShapeDtypeStruct((t, ns), F32)],
        compiler_params=_params(1),
        name="s5_step",
    )(*args)


def _group_reduce(x, lane, width, op):
    n = x.shape[-1]
    s = 1
    while s < width:
        up = pltpu.roll(x, s, 1)
        dn = pltpu.roll(x, n - s, 1)
        x = op(x, jnp.where((lane & s) != 0, up, dn))
        s *= 2
    return x


def _first_max(x, lane, big):
    m = jnp.max(x, axis=-1, keepdims=True)
    return jnp.min(jnp.where(x == m, lane, big), axis=-1, keepdims=True)


def _route(logits, bias):
    rows, n = logits.shape
    lane = lax.broadcasted_iota(jnp.int32, (rows, n), 1)
    lane_f = lane.astype(F32)
    neg = jnp.float32(-jnp.inf)
    big = jnp.float32(n)
    gsz = N_EXPERTS // N_ROUTE_GROUPS
    group_start = (lane - (lane & (gsz - 1))).astype(F32)
    scores = jax.nn.sigmoid(logits)
    biased = scores + bias
    m1 = _group_reduce(biased, lane, gsz, jnp.maximum)
    first = _group_reduce(jnp.where(biased == m1, lane_f, big), lane, gsz, jnp.minimum)
    m2 = _group_reduce(jnp.where(lane_f == first, neg, biased), lane, gsz, jnp.maximum)
    gs = m1 + m2
    gsel = jnp.zeros((rows, n), F32)
    work = gs
    for _ in range(TOPK_ROUTE_GROUPS):
        hit = group_start == _first_max(work, lane_f, big)
        gsel = jnp.where(hit, 1.0, gsel)
        work = jnp.where(hit, neg, work)
    work = jnp.where(gsel > 0.0, biased, neg)
    w = jnp.zeros((rows, n), F32)
    for _ in range(TOP_K):
        hit = lane_f == _first_max(work, lane_f, big)
        w = jnp.where(hit, scores, w)
        work = jnp.where(hit, neg, work)
    return w / jnp.sum(w, axis=-1, keepdims=True) * ROUTED_SCALE


def _split_bf16(x):
    hi = x.astype(BF16)
    lo = (x - hi.astype(F32)).astype(BF16)
    return hi, lo


def _postmix_kernel(x_ref, ya_ref, yb_ref, gt1_ref, sh2_ref, sc2_ref, gt2_ref, g2_ref,
                    wout_ref, rwh_ref, rwl_ref, rb_ref, swg_ref, swu_ref, swd_ref,
                    xacc_ref, h2_ref, wd_ref, *, token_tiles):
    y = jnp.concatenate([ya_ref[...], yb_ref[...]], axis=-1)
    x1 = x_ref[...] + gt1_ref[...] * _dot(y, wout_ref[...])
    h2 = _rmsnorm(x1, g2_ref[...]) * (1.0 + sc2_ref[...]) + sh2_ref[...]
    hb = h2.astype(BF16)
    if token_tiles:
        tm, d = h2.shape
        nj = d // LANES
        for j in range(nj):
            h2_ref[pl.ds(j, tm, stride=nj), :] = h2[:, j * LANES:(j + 1) * LANES]
    else:
        h2_ref[...] = hb
    h_hi, h_lo = _split_bf16(h2)
    logits = (jnp.dot(h_hi, rwh_ref[...], preferred_element_type=F32)
              + jnp.dot(h_hi, rwl_ref[...], preferred_element_type=F32)
              + jnp.dot(h_lo, rwh_ref[...], preferred_element_type=F32))
    wd_ref[...] = _route(logits, rb_ref[...])
    hid = jax.nn.silu(jnp.dot(hb, swg_ref[...], preferred_element_type=F32)) * \
        jnp.dot(hb, swu_ref[...], preferred_element_type=F32)
    shared = _dot(hid, swd_ref[...])
    xacc_ref[...] = x1 + gt2_ref[...] * shared


def _postmix(x, ya, yb, gt1, sh2, sc2, gt2, g2, w_out, rw_hi, rw_lo, rb, swg, swu, swd,
             per_row, seq_len, tm, token_tiles):
    t, d = x.shape
    sw = ya.shape[1]
    ne = rw_hi.shape[1]
    nj = d // LANES
    mod = _mod_spec(per_row, tm, d, max(seq_len // tm, 1))
    weights = (g2, w_out, rw_hi, rw_lo, rb, swg, swu, swd)
    if token_tiles:
        h2_spec, h2_shape = _row_spec(tm * nj, LANES), jax.ShapeDtypeStruct((t * nj, LANES), F32)
    else:
        h2_spec, h2_shape = _row_spec(tm, d), jax.ShapeDtypeStruct((t, d), BF16)
    return pl.pallas_call(
        functools.partial(_postmix_kernel, token_tiles=token_tiles),
        grid=(t // tm,),
        in_specs=[_row_spec(tm, d), _row_spec(tm, sw), _row_spec(tm, sw), mod, mod, mod, mod]
        + [_full_spec(w.shape) for w in weights],
        out_specs=[_row_spec(tm, d), h2_spec, _row_spec(tm, ne)],
        out_shape=[jax.ShapeDtypeStruct((t, d), F32), h2_shape,
                   jax.ShapeDtypeStruct((t, ne), F32)],
        compiler_params=_params(1),
        name="postmix",
    )(x, ya, yb, gt1, sh2, sc2, gt2, *weights)


def _moe_dense_kernel(h_ref, wd_ref, xacc_ref, gt2_ref, gf_ref, wg_ref, wu_ref, wdn_ref,
                      o_ref, acc_ref, *, final_norm):
    e = pl.program_id(1)

    @pl.when(e == 0)
    def _():
        acc_ref[...] = jnp.zeros_like(acc_ref)

    hb = h_ref[...]
    wd = wd_ref[...]
    lane = lax.broadcasted_iota(jnp.int32, wd.shape, 1)
    col = jnp.sum(jnp.where(lane == e, wd, 0.0), axis=-1, keepdims=True)
    hid = jax.nn.silu(jnp.dot(hb, wg_ref[...], preferred_element_type=F32)) * \
        jnp.dot(hb, wu_ref[...], preferred_element_type=F32)
    acc_ref[...] += col * _dot(hid, wdn_ref[...])

    @pl.when(e == pl.num_programs(1) - 1)
    def _():
        xo = xacc_ref[...] + gt2_ref[...] * acc_ref[...]
        o_ref[...] = _rmsnorm(xo, gf_ref[...]) if final_norm else xo


def _moe_dense(h2, wd, xacc, gt2, g_final, wg, wu, wdn, per_row, seq_len, tm, final_norm):
    t, d = xacc.shape
    ne, _, eh = wg.shape
    nl = wd.shape[1]
    tpb = max(seq_len // tm, 1)
    if per_row:
        mod = pl.BlockSpec((tm, d), lambda i, e: (i, 0))
    else:
        mod = pl.BlockSpec((None, 1, d), lambda i, e: (i // tpb, 0, 0))
    row = lambda w: pl.BlockSpec((tm, w), lambda i, e: (i, 0))
    return pl.pallas_call(
        functools.partial(_moe_dense_kernel, final_norm=final_norm),
        grid=(t // tm, ne),
        in_specs=[row(d), row(nl), row(d), mod,
                  pl.BlockSpec(g_final.shape, lambda i, e: (0, 0)),
                  pl.BlockSpec((None, d, eh), lambda i, e: (e, 0, 0)),
                  pl.BlockSpec((None, d, eh), lambda i, e: (e, 0, 0)),
                  pl.BlockSpec((None, eh, d), lambda i, e: (e, 0, 0))],
        out_specs=row(d),
        out_shape=jax.ShapeDtypeStruct((t, d), F32),
        scratch_shapes=[pltpu.VMEM((tm, d), F32)],
        compiler_params=_params(2),
        name="moe_dense",
    )(h2, wd, xacc, gt2, g_final, wg, wu, wdn)


TOKEN_BITS = 12


def _route_plan_kernel(wd_ref, rows_ref, wl_ref, cnt_ref, *, tg, nj):
    wt = wd_ref[...].T[:N_EXPERTS]
    sel = wt > 0.0
    lane = lax.broadcasted_iota(jnp.int32, wt.shape, 1)
    c = sel.astype(jnp.int32)
    s = 1
    while s < tg:
        c = c + jnp.where(lane >= s, pltpu.roll(c, s, 1), 0)
        s *= 2
    cnt_ref[...] = jnp.broadcast_to(c[:, tg - 1:tg], cnt_ref.shape)
    dist = lane - (c - 1)
    key = jnp.where(sel, (dist << TOKEN_BITS) | lane, -1)
    w = jnp.where(sel, wt, 0.0)
    s = 1
    while s < tg:
        leaving = (key >= 0) & (((key >> TOKEN_BITS) & s) != 0)
        key_in = pltpu.roll(key, tg - s, 1)
        w_in = pltpu.roll(w, tg - s, 1)
        arriving = (key_in >= 0) & (((key_in >> TOKEN_BITS) & s) != 0)
        key = jnp.where(arriving, key_in, jnp.where(leaving, -1, key))
        w = jnp.where(arriving, w_in, jnp.where(leaving, 0.0, w))
        s *= 2
    rows_ref[...] = jnp.where(key >= 0, key & ((1 << TOKEN_BITS) - 1), 0) * nj
    wl_ref[...] = w


def _route_plan(wd, tg, nj):
    t, nl = wd.shape
    assert tg <= (1 << TOKEN_BITS) and t % tg == 0
    g = t // tg
    grp = lambda w: pl.BlockSpec((None, N_EXPERTS, w), lambda i: (i, 0, 0))
    return pl.pallas_call(
        functools.partial(_route_plan_kernel, tg=tg, nj=nj),
        grid=(g,),
        in_specs=[pl.BlockSpec((tg, nl), lambda i: (i, 0))],
        out_specs=[grp(tg), grp(tg), grp(LANES)],
        out_shape=[jax.ShapeDtypeStruct((g, N_EXPERTS, tg), jnp.int32),
                   jax.ShapeDtypeStruct((g, N_EXPERTS, tg), F32),
                   jax.ShapeDtypeStruct((g, N_EXPERTS, LANES), jnp.int32)],
        compiler_params=_params(1),
        name="route_plan",
    )(wd)


def _block_table(counts, tg, bm, nb_max):
    g, ne = counts.shape
    i32 = jnp.int32
    nblk = (counts + bm - 1) // bm
    bend = jnp.cumsum(nblk, axis=1)
    bstart = bend - nblk
    nb = bend[:, -1]
    k = jnp.arange(nb_max, dtype=i32)[None, :]
    kc = jnp.minimum(k, jnp.maximum(nb[:, None] - 1, 0))
    e_k = jnp.minimum(jnp.sum(bend[:, None, :] <= kc[:, :, None], axis=2), ne - 1).astype(i32)
    b_k = kc - jnp.take_along_axis(bstart, e_k, axis=1)
    first = (b_k == 0) & (k < nb[:, None])
    active = nblk > 0
    parity = (jnp.take_along_axis(jnp.cumsum(active.astype(i32), axis=1) - 1, e_k, axis=1)
              % MOE_WEIGHT_BUFFERS)
    ids = jnp.where(active, jnp.arange(ne, dtype=i32)[None, :], ne)
    first_active_from = jnp.flip(lax.cummin(jnp.flip(ids, 1), axis=1), 1)
    nxt = jnp.concatenate([first_active_from[:, 1:], jnp.full((g, 1), ne, i32)], axis=1)
    nxt = jnp.where(nxt >= ne, -1, nxt)
    nxt2 = jnp.where(nxt < 0, -1, jnp.take_along_axis(nxt, jnp.maximum(nxt, 0), axis=1))
    next_k = jnp.take_along_axis(nxt, e_k, axis=1)
    next2_k = jnp.take_along_axis(nxt2, e_k, axis=1)
    row = (jnp.arange(g, dtype=i32)[:, None] * ne + e_k) * (tg // bm) + b_k
    flat = lambda a: a.reshape(-1).astype(i32)
    table = (nb.astype(i32), flat(e_k), flat(first), flat(parity), flat(next_k), flat(next2_k))
    return table, flat(row)


def _moe_group_kernel(nb_ref, exp_ref, first_ref, par_ref, next_ref, next2_ref,
                      h_ref, rows_ref, wl_ref, wg_hbm, wu_hbm, wdn_hbm, o_ref,
                      wg_buf, wu_buf, wdn_buf, wsem, xt_a, xt_b, hid_a, hid_b, yt_a, yt_b,
                      *, bm, nj, batch, nb_max):
    g = pl.program_id(0)
    nb = nb_ref[g]
    tb = g * nb_max
    pitch = bm + 1
    o_ref[...] = jnp.zeros_like(o_ref)

    def weight_copies(e, slot):
        return (pltpu.make_async_copy(wg_hbm.at[e], wg_buf.at[slot], wsem.at[slot]),
                pltpu.make_async_copy(wu_hbm.at[e], wu_buf.at[slot], wsem.at[slot]),
                pltpu.make_async_copy(wdn_hbm.at[e], wdn_buf.at[slot], wsem.at[slot]))

    n_wbuf = wg_buf.shape[0]

    def handoff(k):
        @pl.when(first_ref[tb + k] == 1)
        def _():
            slot = par_ref[tb + k]
            for c in weight_copies(exp_ref[tb + k], slot):
                c.wait()
            ahead = next2_ref[tb + k]

            @pl.when(ahead >= 0)
            def _():
                for c in weight_copies(ahead, (slot + 2) % n_wbuf):
                    c.start()

    def gather_rows(k, xt):
        rows = rows_ref.at[pl.ds(pl.multiple_of(k * bm, bm), bm)]

        def emit(lo, hi):
            for m in range(lo, hi):
                r0 = pl.multiple_of(rows[m], nj)
                xt[pl.ds(m, nj, stride=pitch), :] = h_ref[pl.ds(r0, nj), :]
        return emit

    def scatter_rows(k, yt):
        base = pl.multiple_of(k * bm, bm)
        rows = rows_ref.at[pl.ds(base, bm)]
        wts = wl_ref.at[pl.ds(base, bm)]

        def emit(lo, hi):
            for m0 in range(lo, hi, batch):
                updates = []
                for m in range(m0, min(m0 + batch, hi)):
                    r0 = pl.multiple_of(rows[m], nj)
                    updates.append((r0, o_ref[pl.ds(r0, nj), :]
                                    + wts[m] * yt[pl.ds(m, nj, stride=pitch), :]))
                for r0, v in reversed(updates):
                    o_ref[pl.ds(r0, nj), :] = v
        return emit

    n_chunks = nj // 2
    kc = 2 * LANES
    n_pieces = 2 * n_chunks + 1
    cuts = [bm * i // n_pieces for i in range(n_pieces + 1)]

    def step(up_blk, down_blk, side_work):
        piece = 0

        def side():
            nonlocal piece
            for emit in side_work:
                emit(cuts[piece], cuts[piece + 1])
            piece += 1

        gate = up = None
        for c in range(n_chunks):
            if up_blk is not None:
                k, xt, _ = up_blk
                slot = par_ref[tb + k]
                xc = jnp.concatenate([xt[j * pitch:j * pitch + bm, :] for j in (2 * c, 2 * c + 1)],
                                     axis=-1).astype(BF16)
                pg = jnp.dot(xc, wg_buf[slot, c * kc:(c + 1) * kc, :], preferred_element_type=F32)
                pu = jnp.dot(xc, wu_buf[slot, c * kc:(c + 1) * kc, :], preferred_element_type=F32)
                gate = pg if gate is None else gate + pg
                up = pu if up is None else up + pu
            side()
            if down_blk is not None:
                k, hid, yt = down_blk
                slot = par_ref[tb + k]
                yc = jnp.dot(hid[...], wdn_buf[slot, :, c * kc:(c + 1) * kc],
                             preferred_element_type=F32)
                for q in range(2):
                    j = 2 * c + q
                    yt[j * pitch:j * pitch + bm, :] = yc[:, q * LANES:(q + 1) * LANES]
            side()
        if up_blk is not None:
            up_blk[2][...] = (jax.nn.silu(gate) * up).astype(BF16)
        side()

    @pl.when(nb > 0)
    def _():
        for c in weight_copies(exp_ref[tb], 0):
            c.start()
        second = next_ref[tb]

        @pl.when(second >= 0)
        def _():
            for c in weight_copies(second, 1):
                c.start()
        gather_rows(0, xt_a)(0, bm)
        handoff(0)
        step((0, xt_a, hid_a), None, [gather_rows(1, xt_b)])
        handoff(1)
        step((1, xt_b, hid_b), (0, hid_a, yt_a), [gather_rows(2, xt_a)])

        def trip(i, carry):
            t = 2 * i + 1
            handoff(t + 1)
            step((t + 1, xt_a, hid_a), (t, hid_b, yt_b),
                 [gather_rows(t + 2, xt_b), scatter_rows(t - 1, yt_a)])
            handoff(t + 2)
            step((t + 2, xt_b, hid_b), (t + 1, hid_a, yt_a),
                 [gather_rows(t + 3, xt_a), scatter_rows(t, yt_b)])
            return carry

        lax.fori_loop(0, nb // 2, trip, 0)

        @pl.when(nb % 2 == 1)
        def _():
            scatter_rows(nb - 1, yt_a)(0, bm)


def _moe_group(table, h2t, rows, wl, wg, wu, wdn, tg, bm, nb_max, batch=8):
    ne, d, eh = wg.shape
    nj = d // LANES
    assert nj == SUBLANES
    g = h2t.shape[0] // (tg * nj)
    kern = functools.partial(_moe_group_kernel, bm=bm, nj=nj, batch=batch, nb_max=nb_max)
    slots = pl.BlockSpec((nb_max * bm,), lambda gi, *_: (gi,), memory_space=pltpu.SMEM)
    hbm = pl.BlockSpec(memory_space=pl.ANY)
    pitch_rows = nj * (bm + 1)
    grid_spec = pltpu.PrefetchScalarGridSpec(
        num_scalar_prefetch=len(table),
        grid=(g,),
        in_specs=[pl.BlockSpec((tg * nj, LANES), lambda gi, *_: (gi, 0)),
                  slots, slots, hbm, hbm, hbm],
        out_specs=pl.BlockSpec((tg * nj, LANES), lambda gi, *_: (gi, 0)),
        scratch_shapes=[pltpu.VMEM((MOE_WEIGHT_BUFFERS, d, eh), BF16),
                        pltpu.VMEM((MOE_WEIGHT_BUFFERS, d, eh), BF16),
                        pltpu.VMEM((MOE_WEIGHT_BUFFERS, eh, d), BF16),
                        pltpu.SemaphoreType.DMA((MOE_WEIGHT_BUFFERS,)),
                        pltpu.VMEM((pitch_rows, LANES), F32), pltpu.VMEM((pitch_rows, LANES), F32),
                        pltpu.VMEM((bm, eh), BF16), pltpu.VMEM((bm, eh), BF16),
                        pltpu.VMEM((pitch_rows, LANES), F32), pltpu.VMEM((pitch_rows, LANES), F32)],
    )
    return pl.pallas_call(
        kern,
        grid_spec=grid_spec,
        out_shape=jax.ShapeDtypeStruct((g * tg * nj, LANES), F32),
        compiler_params=_params(1),
        name="moe_group",
    )(*table, h2t, rows, wl, wg, wu, wdn)


def _finalize_kernel(r_ref, xacc_ref, gt2_ref, gf_ref, o_ref, *, final_norm):
    tm, d = xacc_ref.shape
    nj = d // LANES
    routed = jnp.concatenate([r_ref[pl.ds(j, tm, stride=nj), :] for j in range(nj)], axis=-1)
    xo = xacc_ref[...] + gt2_ref[...] * routed
    o_ref[...] = _rmsnorm(xo, gf_ref[...]) if final_norm else xo


def _finalize(routed, xacc, gt2, g_final, seq_len, tm, final_norm):
    t, d = xacc.shape
    nj = d // LANES
    return pl.pallas_call(
        functools.partial(_finalize_kernel, final_norm=final_norm),
        grid=(t // tm,),
        in_specs=[_row_spec(tm * nj, LANES),
                  _row_spec(tm, d), _mod_spec(False, tm, d, seq_len // tm),
                  _full_spec(g_final.shape)],
        out_specs=_row_spec(tm, d),
        out_shape=jax.ShapeDtypeStruct((t, d), F32),
        compiler_params=_params(1),
        name="finalize",
    )(routed, xacc, gt2, g_final)


def _s5_discretize(lam_re, lam_im, log_dt, b_re, b_im, c_re, c_im):
    g, n = lam_re.shape
    hg = b_re.shape[2]
    lr = jnp.minimum(lam_re.astype(F32), -1e-4)
    li = lam_im.astype(F32)
    dt = jnp.exp(log_dt.astype(F32))[:, None]
    mag = jnp.exp(lr * dt)
    abar_r, abar_i = mag * jnp.cos(li * dt), mag * jnp.sin(li * dt)
    den = lr * lr + li * li
    nr, ni = abar_r - 1.0, abar_i
    f_r = (nr * lr + ni * li) / den
    f_i = (ni * lr - nr * li) / den
    br_, bi_ = b_re.astype(F32), b_im.astype(F32)
    bbar_r = f_r[..., None] * br_ - f_i[..., None] * bi_
    bbar_i = f_r[..., None] * bi_ + f_i[..., None] * br_
    gpc = LANES // hg
    n_chunks = g // gpc
    eye = jnp.eye(gpc, dtype=F32)

    def b_chunks(bb):
        bb = bb.reshape(n_chunks, gpc, n, hg)
        m = jnp.einsum('jgnh,gk->jghkn', bb, eye)
        return m.reshape(n_chunks, gpc * hg, gpc * n)

    def c_chunks(cc):
        cc = cc.astype(F32).reshape(n_chunks, gpc, hg, n)
        m = jnp.einsum('jghn,gk->jgnkh', cc, eye)
        return m.reshape(n_chunks, gpc * n, gpc * hg)

    bj = jnp.concatenate([b_chunks(bbar_r), b_chunks(bbar_i)], axis=2).astype(BF16)
    cj = jnp.concatenate([c_chunks(c_re), -c_chunks(c_im)], axis=1).astype(BF16)
    return abar_r.reshape(1, g * n), abar_i.reshape(1, g * n), bj, cj


def kernel(x_prompt, x_sample, c_prompt, c_sample, state_ssm_re, state_ssm_im, g_norm1, g_norm2, w_ada, b_ada, w_in, w_out, ssm_lambda_re, ssm_lambda_im, ssm_log_dt, ssm_b_re, ssm_b_im, ssm_c_re, ssm_c_im, ssm_d, ssm_w_glu, ssm_b_glu, gm_ln_g, gm_ln_b, gm_w_s, gm_b_s, router_w, router_bias, moe_w_gate, moe_w_up, moe_w_down, shared_w_gate, shared_w_up, shared_w_down, g_final):
    bp, seq_len, d = x_prompt.shape
    bs, dec_len, _ = x_sample.shape
    depth = w_ada.shape[0]
    assert dec_len == 1 and seq_len % CHUNK == 0 and bp % SUBLANES == 0
    groups, n_state = ssm_lambda_re.shape[1:]
    sw = groups * SSM_GROUP
    gw = gm_ln_g.shape[1]
    hd = gw // GM_HEADS

    xp = x_prompt.reshape(bp * seq_len, d)
    xs = x_sample.reshape(bs, d)
    c_all = jnp.concatenate([c_prompt, c_sample], axis=0)
    gf = g_final.reshape(1, d)

    p_re, p_im, s_re, s_im, s_v = [], [], [], [], []
    for l in range(depth):
        mod = _ada(c_all, w_ada[l].astype(BF16), b_ada[l].reshape(1, -1))
        mod_p = [m.reshape(bp, 1, d) for m in jnp.split(mod[:bp], 6, axis=-1)]
        mod_s = jnp.split(mod[bp:], 6, axis=-1)

        g1 = g_norm1[l].reshape(1, d)
        g2 = g_norm2[l].reshape(1, d)
        w_in_b = w_in[l].astype(BF16)
        w_out_b = w_out[l].astype(BF16)
        ln_g = gm_ln_g[l].reshape(1, gw)
        ln_b = gm_ln_b[l].reshape(1, gw)
        ws_tril = (gm_w_s[l] * jnp.tril(jnp.ones((CHUNK, CHUNK), F32))[None]).astype(BF16)
        bs_full = jnp.repeat(gm_b_s[l].T, hd, axis=1)
        w0 = jnp.repeat(gm_w_s[l][:, 0, 0], hd).reshape(1, gw)
        b0 = jnp.repeat(gm_b_s[l][:, 0], hd).reshape(1, gw)
        a_re, a_im, bj, cj = _s5_discretize(ssm_lambda_re[l], ssm_lambda_im[l], ssm_log_dt[l],
                                            ssm_b_re[l], ssm_b_im[l], ssm_c_re[l], ssm_c_im[l])
        d_vec = ssm_d[l].reshape(1, sw)
        w_glu = ssm_w_glu[l].astype(BF16)
        b_glu = ssm_b_glu[l].reshape(1, sw)
        pad = LANES - N_EXPERTS
        rw = jnp.pad(router_w[l], ((0, 0), (0, pad)))
        rw_hi = rw.astype(BF16)
        rw_lo = (rw - rw_hi.astype(F32)).astype(BF16)
        rb = jnp.pad(router_bias[l], (0, pad), constant_values=-jnp.inf).reshape(1, LANES)
        swg, swu, swd = (shared_w_gate[l].astype(BF16), shared_w_up[l].astype(BF16),
                         shared_w_down[l].astype(BF16))
        ewg, ewu, ewd = (moe_w_gate[l].astype(BF16), moe_w_up[l].astype(BF16),
                         moe_w_down[l].astype(BF16))

        tm_row = min(ROW_TILE, seq_len)
        ua, yb = _premix_seq(xp, mod_p[0], mod_p[1], g1, w_in_b, ln_g, ln_b, ws_tril, bs_full,
                             seq_len, tm=tm_row)
        ya, hpr, hpi = _s5_seq(ua.reshape(bp, seq_len, sw), bj, cj, a_re, a_im, d_vec,
                               w_glu, b_glu, tl=min(SCAN_TILE, seq_len))
        xacc, h2, wd = _postmix(xp, ya.reshape(bp * seq_len, sw), yb, mod_p[2], mod_p[3],
                                mod_p[4], mod_p[5], g2, w_out_b, rw_hi, rw_lo, rb, swg, swu, swd,
                                per_row=False, seq_len=seq_len, tm=tm_row, token_tiles=True)
        last = l == depth - 1
        tg = min(MOE_GROUP, seq_len)
        bm = min(MOE_BLOCK, tg)
        nb_max = -(-(tg * TOP_K // bm + N_EXPERTS + 3) // SUBLANES) * SUBLANES
        slot_rows, slot_w, counts = _route_plan(wd, tg, d // LANES)
        table, block_row = _block_table(counts[:, :, 0], tg, bm, nb_max)
        by_block = lambda a: jnp.take(a.reshape(-1, bm), block_row, axis=0).reshape(-1)
        routed = _moe_group(table, h2, by_block(slot_rows), by_block(slot_w),
                            ewg, ewu, ewd, tg, bm, nb_max)
        xp = _finalize(routed, xacc, mod_p[5], gf, seq_len, tm_row, final_norm=last)
        p_re.append(hpr.reshape(bp, groups, n_state))
        p_im.append(hpi.reshape(bp, groups, n_state))

        ua, yb, vn = _premix_step(xs, mod_s[0], mod_s[1], g1, w_in_b, ln_g, ln_b, w0, b0)
        ya, hsr, hsi = _s5_step(ua, state_ssm_re[l].reshape(bs, groups * n_state),
                                state_ssm_im[l].reshape(bs, groups * n_state),
                                bj, cj, a_re, a_im, d_vec, w_glu, b_glu)
        xacc, h2, wd = _postmix(xs, ya, yb, mod_s[2], mod_s[3], mod_s[4], mod_s[5], g2, w_out_b,
                                rw_hi, rw_lo, rb, swg, swu, swd, per_row=True, seq_len=1, tm=bs,
                                token_tiles=False)
        xs = _moe_dense(h2, wd, xacc, mod_s[5], gf, ewg, ewu, ewd, per_row=True, seq_len=1,
                        tm=bs, final_norm=last)
        s_re.append(hsr.reshape(bs, groups, n_state))
        s_im.append(hsi.reshape(bs, groups, n_state))
        s_v.append(vn.reshape(bs, 1, gw))

    y_prompt = xp.reshape(bp, seq_len, d)
    y_sample = xs.reshape(bs, 1, d)
    return (y_prompt, y_sample, jnp.stack(p_re), jnp.stack(p_im), jnp.stack(s_re),
            jnp.stack(s_im), jnp.stack(s_v))
```

```python
import functools

import jax
import jax.numpy as jnp
from jax import lax
from jax.experimental import pallas as pl
from jax.experimental.pallas import tpu as pltpu

F32 = jnp.float32
BF16 = jnp.bfloat16

EPS = 1e-6
SSM_GROUP = 16
SSM_STATE = 64
GM_HEADS = 4
CHUNK = 128
N_EXPERTS = 64
TOP_K = 8
N_ROUTE_GROUPS = 8
TOPK_ROUTE_GROUPS = 4
ROUTED_SCALE = 2.5

LANES = 128
SUBLANES = 8
VMEM_LIMIT = 48 * 1024 * 1024
ROW_TILE = 512
SCAN_TILE = 128
MOE_GROUP = 2048
MOE_BLOCK = 256
MOE_WEIGHT_BUFFERS = 4


def _params(n_axes):
    return pltpu.CompilerParams(dimension_semantics=("arbitrary",) * n_axes,
                                vmem_limit_bytes=VMEM_LIMIT)


def _dot(a, b):
    return jnp.dot(a.astype(BF16), b.astype(BF16), preferred_element_type=F32)


def _rmsnorm(x, g):
    ms = jnp.mean(x * x, axis=-1, keepdims=True)
    return x * lax.rsqrt(ms + EPS) * g


def _ada_kernel(c_ref, w_ref, b_ref, o_ref):
    c = c_ref[...]
    o_ref[...] = _dot(jax.nn.silu(c), w_ref[...]) + b_ref[...]


def _ada(c, w, b, tn=512):
    m, d = c.shape
    n = w.shape[1]
    return pl.pallas_call(
        _ada_kernel,
        grid=(n // tn,),
        in_specs=[pl.BlockSpec((m, d), lambda j: (0, 0)),
                  pl.BlockSpec((d, tn), lambda j: (0, j)),
                  pl.BlockSpec((1, tn), lambda j: (0, j))],
        out_specs=pl.BlockSpec((m, tn), lambda j: (0, j)),
        out_shape=jax.ShapeDtypeStruct((m, n), F32),
        compiler_params=_params(1),
        name="ada_mod",
    )(c, w, b)


def _premix_common(x_ref, sh_ref, sc_ref, g_ref, win_ref, lng_ref, lnb_ref, sw):
    x = x_ref[...]
    h = _rmsnorm(x, g_ref[...]) * (1.0 + sc_ref[...]) + sh_ref[...]
    proj = _dot(h, win_ref[...])
    u_a = proj[:, :sw]
    z = jax.nn.gelu(proj[:, sw:])
    gw = z.shape[1] // 2
    u_b = z[:, :gw]
    v_b = z[:, gw:]
    hd = gw // GM_HEADS
    parts = []
    for k in range(GM_HEADS):
        vh = v_b[:, k * hd:(k + 1) * hd]
        mu = jnp.mean(vh, axis=-1, keepdims=True)
        dv = vh - mu
        var = jnp.mean(dv * dv, axis=-1, keepdims=True)
        parts.append(dv * lax.rsqrt(var + EPS))
    vn = jnp.concatenate(parts, axis=-1) * lng_ref[...] + lnb_ref[...]
    return u_a, u_b, vn


def _premix_seq_kernel(x_ref, sh_ref, sc_ref, g_ref, win_ref, lng_ref, lnb_ref,
                       ws_ref, bs_ref, ua_ref, yb_ref, *, sw):
    u_a, u_b, vn = _premix_common(x_ref, sh_ref, sc_ref, g_ref, win_ref, lng_ref, lnb_ref, sw)
    ua_ref[...] = u_a
    tm, gw = u_b.shape
    hd = gw // GM_HEADS
    vnb = vn.astype(BF16)
    for c in range(tm // CHUNK):
        rows = slice(c * CHUNK, (c + 1) * CHUNK)
        for k in range(GM_HEADS):
            cols = slice(k * hd, (k + 1) * hd)
            s = jnp.dot(ws_ref[k], vnb[rows, cols], preferred_element_type=F32)
            yb_ref[rows, cols] = u_b[rows, cols] * (s + bs_ref[:, cols])


def _premix_step_kernel(x_ref, sh_ref, sc_ref, g_ref, win_ref, lng_ref, lnb_ref,
                        w0_ref, b0_ref, ua_ref, yb_ref, vn_ref, *, sw):
    u_a, u_b, vn = _premix_common(x_ref, sh_ref, sc_ref, g_ref, win_ref, lng_ref, lnb_ref, sw)
    ua_ref[...] = u_a
    vn_ref[...] = vn
    yb_ref[...] = u_b * (vn * w0_ref[...] + b0_ref[...])


def _row_spec(tm, d):
    return pl.BlockSpec((tm, d), lambda i: (i, 0))


def _full_spec(shape):
    nd = len(shape)
    return pl.BlockSpec(shape, lambda i: (0,) * nd)


def _mod_spec(per_row, tm, d, tiles_per_batch):
    if per_row:
        return pl.BlockSpec((tm, d), lambda i: (i, 0))
    return pl.BlockSpec((None, 1, d), lambda i: (i // tiles_per_batch, 0, 0))


def _premix_seq(x, sh, sc, g, w_in, ln_g, ln_b, ws_tril, bs_full, seq_len, tm):
    t, d = x.shape
    sw = ln_g.shape[1]
    tpb = seq_len // tm
    mod = _mod_spec(False, tm, d, tpb)
    return pl.pallas_call(
        functools.partial(_premix_seq_kernel, sw=sw),
        grid=(t // tm,),
        in_specs=[_row_spec(tm, d), mod, mod, _full_spec(g.shape), _full_spec(w_in.shape),
                  _full_spec(ln_g.shape), _full_spec(ln_b.shape), _full_spec(ws_tril.shape),
                  _full_spec(bs_full.shape)],
        out_specs=[_row_spec(tm, sw), _row_spec(tm, sw)],
        out_shape=[jax.ShapeDtypeStruct((t, sw), F32), jax.ShapeDtypeStruct((t, sw), F32)],
        compiler_params=_params(1),
        name="premix_seq",
    )(x, sh, sc, g, w_in, ln_g, ln_b, ws_tril, bs_full)


def _premix_step(x, sh, sc, g, w_in, ln_g, ln_b, w0, b0):
    t, d = x.shape
    sw = ln_g.shape[1]
    mod = _mod_spec(True, t, d, 1)
    return pl.pallas_call(
        functools.partial(_premix_step_kernel, sw=sw),
        grid=(1,),
        in_specs=[_row_spec(t, d), mod, mod, _full_spec(g.shape), _full_spec(w_in.shape),
                  _full_spec(ln_g.shape), _full_spec(ln_b.shape), _full_spec(w0.shape),
                  _full_spec(b0.shape)],
        out_specs=[_row_spec(t, sw)] * 3,
        out_shape=[jax.ShapeDtypeStruct((t, sw), F32)] * 3,
        compiler_params=_params(1),
        name="premix_step",
    )(x, sh, sc, g, w_in, ln_g, ln_b, w0, b0)


def _s5_output(xs, u, cj_ref, d_ref, wglu_ref, bglu_ref, n_chunks):
    cw = xs.shape[1] // n_chunks
    xb = xs.astype(BF16)
    ys = [jnp.dot(xb[:, j * cw:(j + 1) * cw], cj_ref[j], preferred_element_type=F32)
          for j in range(n_chunks)]
    y = jnp.concatenate(ys, axis=-1) + d_ref[...] * u
    y = jax.nn.gelu(y)
    gate = jax.nn.sigmoid(_dot(y, wglu_ref[...]) + bglu_ref[...])
    return y * gate


def _s5_seq_kernel(u_ref, bj_ref, cj_ref, are_ref, aim_ref, d_ref, wglu_ref, bglu_ref,
                   y_ref, sre_ref, sim_ref, bu_ref, st_ref, *, nb, tl, n_chunks, unroll):
    i = pl.program_id(0)
    n_tiles = bu_ref.shape[0]
    tpc = n_tiles // n_chunks
    hpc = tpc // 2
    hw = hpc * LANES
    kw = u_ref.shape[2] // n_chunks

    @pl.when(i == 0)
    def _():
        st_ref[...] = jnp.zeros_like(st_ref)

    for b in range(nb):
        ub = u_ref[b].astype(BF16)
        for j in range(n_chunks):
            bu = jnp.dot(ub[:, j * kw:(j + 1) * kw], bj_ref[j], preferred_element_type=F32)
            for c in range(tpc):
                bu_ref[j * tpc + c, pl.ds(b, tl, stride=nb), :] = bu[:, c * LANES:(c + 1) * LANES]

    for j in range(n_chunks):
        re = slice(j * 2 * hw, j * 2 * hw + hw)
        im = slice(j * 2 * hw + hw, (j + 1) * 2 * hw)
        ar = jnp.broadcast_to(are_ref[:, j * hw:(j + 1) * hw], (nb, hw))
        ai = jnp.broadcast_to(aim_ref[:, j * hw:(j + 1) * hw], (nb, hw))
        re_tiles = [j * tpc + c for c in range(hpc)]
        im_tiles = [j * tpc + hpc + c for c in range(hpc)]

        def body(s, carry, ar=ar, ai=ai, re_tiles=re_tiles, im_tiles=im_tiles):
            xr, xi = carry
            for k in range(unroll):
                r0 = pl.multiple_of((s * unroll + k) * nb, nb)
                br = jnp.concatenate([bu_ref[c, pl.ds(r0, nb), :] for c in re_tiles], axis=-1)
                bi = jnp.concatenate([bu_ref[c, pl.ds(r0, nb), :] for c in im_tiles], axis=-1)
                nr = ar * xr - ai * xi + br
                ni = ar * xi + ai * xr + bi
                for q, c in enumerate(re_tiles):
                    bu_ref[c, pl.ds(r0, nb), :] = nr[:, q * LANES:(q + 1) * LANES]
                for q, c in enumerate(im_tiles):
                    bu_ref[c, pl.ds(r0, nb), :] = ni[:, q * LANES:(q + 1) * LANES]
                xr, xi = nr, ni
            return xr, xi

        xr, xi = lax.fori_loop(0, tl // unroll, body, (st_ref[:, re], st_ref[:, im]))
        st_ref[:, re] = xr
        st_ref[:, im] = xi
        sre_ref[:, j * hw:(j + 1) * hw] = xr
        sim_ref[:, j * hw:(j + 1) * hw] = xi

    for b in range(nb):
        xs = jnp.concatenate([bu_ref[c, pl.ds(b, tl, stride=nb), :] for c in range(n_tiles)],
                             axis=-1)
        y_ref[b] = _s5_output(xs, u_ref[b], cj_ref, d_ref, wglu_ref, bglu_ref, n_chunks)


def _s5_seq(u, bj, cj, a_re, a_im, d_vec, w_glu, b_glu, tl, unroll=4):
    nb, seq_len, sw = u.shape
    n_chunks, _, cw = bj.shape
    ns = a_re.shape[1]
    kern = functools.partial(_s5_seq_kernel, nb=nb, tl=tl, n_chunks=n_chunks, unroll=unroll)
    return pl.pallas_call(
        kern,
        grid=(seq_len // tl,),
        in_specs=[pl.BlockSpec((nb, tl, sw), lambda i: (0, i, 0)),
                  _full_spec(bj.shape), _full_spec(cj.shape), _full_spec(a_re.shape),
                  _full_spec(a_im.shape), _full_spec(d_vec.shape), _full_spec(w_glu.shape),
                  _full_spec(b_glu.shape)],
        out_specs=[pl.BlockSpec((nb, tl, sw), lambda i: (0, i, 0)),
                   _full_spec((nb, ns)), _full_spec((nb, ns))],
        out_shape=[jax.ShapeDtypeStruct((nb, seq_len, sw), F32),
                   jax.ShapeDtypeStruct((nb, ns), F32), jax.ShapeDtypeStruct((nb, ns), F32)],
        scratch_shapes=[pltpu.VMEM((n_chunks * cw // LANES, tl * nb, LANES), F32),
                        pltpu.VMEM((nb, n_chunks * cw), F32)],
        compiler_params=_params(1),
        name="s5_seq",
    )(u, bj, cj, a_re, a_im, d_vec, w_glu, b_glu)


def _s5_step_kernel(u_ref, hre_ref, him_ref, bj_ref, cj_ref, are_ref, aim_ref, d_ref,
                    wglu_ref, bglu_ref, y_ref, sre_ref, sim_ref, *, n_chunks):
    u = u_ref[...]
    ub = u.astype(BF16)
    kw = u.shape[1] // n_chunks
    hw = bj_ref.shape[2] // 2
    cols = []
    for j in range(n_chunks):
        st = slice(j * hw, (j + 1) * hw)
        bu = jnp.dot(ub[:, j * kw:(j + 1) * kw], bj_ref[j], preferred_element_type=F32)
        ar, ai = are_ref[:, st], aim_ref[:, st]
        hr, hi = hre_ref[:, st], him_ref[:, st]
        xr = ar * hr - ai * hi + bu[:, :hw]
        xi = ar * hi + ai * hr + bu[:, hw:]
        sre_ref[:, st] = xr
        sim_ref[:, st] = xi
        cols += [xr, xi]
    xs = jnp.concatenate(cols, axis=-1)
    y_ref[...] = _s5_output(xs, u, cj_ref, d_ref, wglu_ref, bglu_ref, n_chunks)


def _s5_step(u, h_re, h_im, bj, cj, a_re, a_im, d_vec, w_glu, b_glu):
    t, sw = u.shape
    ns = a_re.shape[1]
    args = (u, h_re, h_im, bj, cj, a_re, a_im, d_vec, w_glu, b_glu)
    return pl.pallas_call(
        functools.partial(_s5_step_kernel, n_chunks=bj.shape[0]),
        grid=(1,),
        in_specs=[_full_spec(a.shape) for a in args],
        out_specs=[_full_spec((t, sw)), _full_spec((t, ns)), _full_spec((t, ns))],
        out_shape=[jax.ShapeDtypeStruct((t, sw), F32), jax.ShapeDtypeStruct((t, ns), F32),
                   jax.ShapeDtypeStruct((t, ns), F32)],
        compiler_params=_params(1),
        name="s5_step",
    )(*args)


def _group_reduce(x, lane, width, op):
    n = x.shape[-1]
    s = 1
    while s < width:
        up = pltpu.roll(x, s, 1)
        dn = pltpu.roll(x, n - s, 1)
        x = op(x, jnp.where((lane & s) != 0, up, dn))
        s *= 2
    return x


def _first_max(x, lane, big):
    m = jnp.max(x, axis=-1, keepdims=True)
    return jnp.min(jnp.where(x == m, lane, big), axis=-1, keepdims=True)


def _route(logits, bias):
    rows, n = logits.shape
    lane = lax.broadcasted_iota(jnp.int32, (rows, n), 1)
    lane_f = lane.astype(F32)
    neg = jnp.float32(-jnp.inf)
    big = jnp.float32(n)
    gsz = N_EXPERTS // N_ROUTE_GROUPS
    group_start = (lane - (lane & (gsz - 1))).astype(F32)
    scores = jax.nn.sigmoid(logits)
    biased = scores + bias
    m1 = _group_reduce(biased, lane, gsz, jnp.maximum)
    first = _group_reduce(jnp.where(biased == m1, lane_f, big), lane, gsz, jnp.minimum)
    m2 = _group_reduce(jnp.where(lane_f == first, neg, biased), lane, gsz, jnp.maximum)
    gs = m1 + m2
    gsel = jnp.zeros((rows, n), F32)
    work = gs
    for _ in range(TOPK_ROUTE_GROUPS):
        hit = group_start == _first_max(work, lane_f, big)
        gsel = jnp.where(hit, 1.0, gsel)
        work = jnp.where(hit, neg, work)
    work = jnp.where(gsel > 0.0, biased, neg)
    w = jnp.zeros((rows, n), F32)
    for _ in range(TOP_K):
        hit = lane_f == _first_max(work, lane_f, big)
        w = jnp.where(hit, scores, w)
        work = jnp.where(hit, neg, work)
    return w / jnp.sum(w, axis=-1, keepdims=True) * ROUTED_SCALE


def _split_bf16(x):
    hi = x.astype(BF16)
    lo = (x - hi.astype(F32)).astype(BF16)
    return hi, lo


def _postmix_kernel(x_ref, ya_ref, yb_ref, gt1_ref, sh2_ref, sc2_ref, gt2_ref, g2_ref,
                    wout_ref, rwh_ref, rwl_ref, rb_ref, swg_ref, swu_ref, swd_ref,
                    xacc_ref, h2_ref, wd_ref, *, token_tiles):
    y = jnp.concatenate([ya_ref[...], yb_ref[...]], axis=-1)
    x1 = x_ref[...] + gt1_ref[...] * _dot(y, wout_ref[...])
    h2 = _rmsnorm(x1, g2_ref[...]) * (1.0 + sc2_ref[...]) + sh2_ref[...]
    hb = h2.astype(BF16)
    if token_tiles:
        tm, d = h2.shape
        nj = d // LANES
        for j in range(nj):
            h2_ref[pl.ds(j, tm, stride=nj), :] = h2[:, j * LANES:(j + 1) * LANES]
    else:
        h2_ref[...] = hb
    h_hi, h_lo = _split_bf16(h2)
    logits = (jnp.dot(h_hi, rwh_ref[...], preferred_element_type=F32)
              + jnp.dot(h_hi, rwl_ref[...], preferred_element_type=F32)
              + jnp.dot(h_lo, rwh_ref[...], preferred_element_type=F32))
    wd_ref[...] = _route(logits, rb_ref[...])
    hid = jax.nn.silu(jnp.dot(hb, swg_ref[...], preferred_element_type=F32)) * \
        jnp.dot(hb, swu_ref[...], preferred_element_type=F32)
    shared = _dot(hid, swd_ref[...])
    xacc_ref[...] = x1 + gt2_ref[...] * shared


def _postmix(x, ya, yb, gt1, sh2, sc2, gt2, g2, w_out, rw_hi, rw_lo, rb, swg, swu, swd,
             per_row, seq_len, tm, token_tiles):
    t, d = x.shape
    sw = ya.shape[1]
    ne = rw_hi.shape[1]
    nj = d // LANES
    mod = _mod_spec(per_row, tm, d, max(seq_len // tm, 1))
    weights = (g2, w_out, rw_hi, rw_lo, rb, swg, swu, swd)
    if token_tiles:
        h2_spec, h2_shape = _row_spec(tm * nj, LANES), jax.ShapeDtypeStruct((t * nj, LANES), F32)
    else:
        h2_spec, h2_shape = _row_spec(tm, d), jax.ShapeDtypeStruct((t, d), BF16)
    return pl.pallas_call(
        functools.partial(_postmix_kernel, token_tiles=token_tiles),
        grid=(t // tm,),
        in_specs=[_row_spec(tm, d), _row_spec(tm, sw), _row_spec(tm, sw), mod, mod, mod, mod]
        + [_full_spec(w.shape) for w in weights],
        out_specs=[_row_spec(tm, d), h2_spec, _row_spec(tm, ne)],
        out_shape=[jax.ShapeDtypeStruct((t, d), F32), h2_shape,
                   jax.ShapeDtypeStruct((t, ne), F32)],
        compiler_params=_params(1),
        name="postmix",
    )(x, ya, yb, gt1, sh2, sc2, gt2, *weights)


def _moe_dense_kernel(h_ref, wd_ref, xacc_ref, gt2_ref, gf_ref, wg_ref, wu_ref, wdn_ref,
                      o_ref, acc_ref, *, final_norm):
    e = pl.program_id(1)

    @pl.when(e == 0)
    def _():
        acc_ref[...] = jnp.zeros_like(acc_ref)

    hb = h_ref[...]
    wd = wd_ref[...]
    lane = lax.broadcasted_iota(jnp.int32, wd.shape, 1)
    col = jnp.sum(jnp.where(lane == e, wd, 0.0), axis=-1, keepdims=True)
    hid = jax.nn.silu(jnp.dot(hb, wg_ref[...], preferred_element_type=F32)) * \
        jnp.dot(hb, wu_ref[...], preferred_element_type=F32)
    acc_ref[...] += col * _dot(hid, wdn_ref[...])

    @pl.when(e == pl.num_programs(1) - 1)
    def _():
        xo = xacc_ref[...] + gt2_ref[...] * acc_ref[...]
        o_ref[...] = _rmsnorm(xo, gf_ref[...]) if final_norm else xo


def _moe_dense(h2, wd, xacc, gt2, g_final, wg, wu, wdn, per_row, seq_len, tm, final_norm):
    t, d = xacc.shape
    ne, _, eh = wg.shape
    nl = wd.shape[1]
    tpb = max(seq_len // tm, 1)
    if per_row:
        mod = pl.BlockSpec((tm, d), lambda i, e: (i, 0))
    else:
        mod = pl.BlockSpec((None, 1, d), lambda i, e: (i // tpb, 0, 0))
    row = lambda w: pl.BlockSpec((tm, w), lambda i, e: (i, 0))
    return pl.pallas_call(
        functools.partial(_moe_dense_kernel, final_norm=final_norm),
        grid=(t // tm, ne),
        in_specs=[row(d), row(nl), row(d), mod,
                  pl.BlockSpec(g_final.shape, lambda i, e: (0, 0)),
                  pl.BlockSpec((None, d, eh), lambda i, e: (e, 0, 0)),
                  pl.BlockSpec((None, d, eh), lambda i, e: (e, 0, 0)),
                  pl.BlockSpec((None, eh, d), lambda i, e: (e, 0, 0))],
        out_specs=row(d),
        out_shape=jax.ShapeDtypeStruct((t, d), F32),
        scratch_shapes=[pltpu.VMEM((tm, d), F32)],
        compiler_params=_params(2),
        name="moe_dense",
    )(h2, wd, xacc, gt2, g_final, wg, wu, wdn)


TOKEN_BITS = 12


def _route_plan_kernel(wd_ref, rows_ref, wl_ref, cnt_ref, *, tg, nj):
    wt = wd_ref[...].T[:N_EXPERTS]
    sel = wt > 0.0
    lane = lax.broadcasted_iota(jnp.int32, wt.shape, 1)
    c = sel.astype(jnp.int32)
    s = 1
    while s < tg:
        c = c + jnp.where(lane >= s, pltpu.roll(c, s, 1), 0)
        s *= 2
    cnt_ref[...] = jnp.broadcast_to(c[:, tg - 1:tg], cnt_ref.shape)
    dist = lane - (c - 1)
    key = jnp.where(sel, (dist << TOKEN_BITS) | lane, -1)
    w = jnp.where(sel, wt, 0.0)
    s = 1
    while s < tg:
        leaving = (key >= 0) & (((key >> TOKEN_BITS) & s) != 0)
        key_in = pltpu.roll(key, tg - s, 1)
        w_in = pltpu.roll(w, tg - s, 1)
        arriving = (key_in >= 0) & (((key_in >> TOKEN_BITS) & s) != 0)
        key = jnp.where(arriving, key_in, jnp.where(leaving, -1, key))
        w = jnp.where(arriving, w_in, jnp.where(leaving, 0.0, w))
        s *= 2
    rows_ref[...] = jnp.where(key >= 0, key & ((1 << TOKEN_BITS) - 1), 0) * nj
    wl_ref[...] = w


def _route_plan(wd, tg, nj):
    t, nl = wd.shape
    assert tg <= (1 << TOKEN_BITS) and t % tg == 0
    g = t // tg
    grp = lambda w: pl.BlockSpec((None, N_EXPERTS, w), lambda i: (i, 0, 0))
    return pl.pallas_call(
        functools.partial(_route_plan_kernel, tg=tg, nj=nj),
        grid=(g,),
        in_specs=[pl.BlockSpec((tg, nl), lambda i: (i, 0))],
        out_specs=[grp(tg), grp(tg), grp(LANES)],
        out_shape=[jax.ShapeDtypeStruct((g, N_EXPERTS, tg), jnp.int32),
                   jax.ShapeDtypeStruct((g, N_EXPERTS, tg), F32),
                   jax.ShapeDtypeStruct((g, N_EXPERTS, LANES), jnp.int32)],
        compiler_params=_params(1),
        name="route_plan",
    )(wd)


def _block_table(counts, tg, bm, nb_max):
    g, ne = counts.shape
    i32 = jnp.int32
    nblk = (counts + bm - 1) // bm
    bend = jnp.cumsum(nblk, axis=1)
    bstart = bend - nblk
    nb = bend[:, -1]
    k = jnp.arange(nb_max, dtype=i32)[None, :]
    kc = jnp.minimum(k, jnp.maximum(nb[:, None] - 1, 0))
    e_k = jnp.minimum(jnp.sum(bend[:, None, :] <= kc[:, :, None], axis=2), ne - 1).astype(i32)
    b_k = kc - jnp.take_along_axis(bstart, e_k, axis=1)
    first = (b_k == 0) & (k < nb[:, None])
    active = nblk > 0
    parity = (jnp.take_along_axis(jnp.cumsum(active.astype(i32), axis=1) - 1, e_k, axis=1)
              % MOE_WEIGHT_BUFFERS)
    ids = jnp.where(active, jnp.arange(ne, dtype=i32)[None, :], ne)
    first_active_from = jnp.flip(lax.cummin(jnp.flip(ids, 1), axis=1), 1)
    nxt = jnp.concatenate([first_active_from[:, 1:], jnp.full((g, 1), ne, i32)], axis=1)
    nxt = jnp.where(nxt >= ne, -1, nxt)
    nxt2 = jnp.where(nxt < 0, -1, jnp.take_along_axis(nxt, jnp.maximum(nxt, 0), axis=1))
    next_k = jnp.take_along_axis(nxt, e_k, axis=1)
    next2_k = jnp.take_along_axis(nxt2, e_k, axis=1)
    row = (jnp.arange(g, dtype=i32)[:, None] * ne + e_k) * (tg // bm) + b_k
    flat = lambda a: a.reshape(-1).astype(i32)
    table = (nb.astype(i32), flat(e_k), flat(first), flat(parity), flat(next_k), flat(next2_k))
    return table, flat(row)


def _moe_group_kernel(nb_ref, exp_ref, first_ref, par_ref, next_ref, next2_ref,
                      h_ref, rows_ref, wl_ref, wg_hbm, wu_hbm, wdn_hbm, o_ref,
                      wg_buf, wu_buf, wdn_buf, wsem, xt_a, xt_b, hid_a, hid_b, yt_a, yt_b,
                      *, bm, nj, batch, nb_max):
    g = pl.program_id(0)
    nb = nb_ref[g]
    tb = g * nb_max
    pitch = bm + 1
    o_ref[...] = jnp.zeros_like(o_ref)

    def weight_copies(e, slot):
        return (pltpu.make_async_copy(wg_hbm.at[e], wg_buf.at[slot], wsem.at[slot]),
                pltpu.make_async_copy(wu_hbm.at[e], wu_buf.at[slot], wsem.at[slot]),
                pltpu.make_async_copy(wdn_hbm.at[e], wdn_buf.at[slot], wsem.at[slot]))

    n_wbuf = wg_buf.shape[0]

    def handoff(k):
        @pl.when(first_ref[tb + k] == 1)
        def _():
            slot = par_ref[tb + k]
            for c in weight_copies(exp_ref[tb + k], slot):
                c.wait()
            ahead = next2_ref[tb + k]

            @pl.when(ahead >= 0)
            def _():
                for c in weight_copies(ahead, (slot + 2) % n_wbuf):
                    c.start()

    def gather_rows(k, xt):
        rows = rows_ref.at[pl.ds(pl.multiple_of(k * bm, bm), bm)]

        def emit(lo, hi):
            for m in range(lo, hi):
                r0 = pl.multiple_of(rows[m], nj)
                xt[pl.ds(m, nj, stride=pitch), :] = h_ref[pl.ds(r0, nj), :]
        return emit

    def scatter_rows(k, yt):
        base = pl.multiple_of(k * bm, bm)
        rows = rows_ref.at[pl.ds(base, bm)]
        wts = wl_ref.at[pl.ds(base, bm)]

        def emit(lo, hi):
            for m0 in range(lo, hi, batch):
                updates = []
                for m in range(m0, min(m0 + batch, hi)):
                    r0 = pl.multiple_of(rows[m], nj)
                    updates.append((r0, o_ref[pl.ds(r0, nj), :]
                                    + wts[m] * yt[pl.ds(m, nj, stride=pitch), :]))
                for r0, v in reversed(updates):
                    o_ref[pl.ds(r0, nj), :] = v
        return emit

    n_chunks = nj // 2
    kc = 2 * LANES
    n_pieces = 2 * n_chunks + 1
    cuts = [bm * i // n_pieces for i in range(n_pieces + 1)]

    def step(up_blk, down_blk, side_work):
        piece = 0

        def side():
            nonlocal piece
            for emit in side_work:
                emit(cuts[piece], cuts[piece + 1])
            piece += 1

        gate = up = None
        for c in range(n_chunks):
            if up_blk is not None:
                k, xt, _ = up_blk
                slot = par_ref[tb + k]
                xc = jnp.concatenate([xt[j * pitch:j * pitch + bm, :] for j in (2 * c, 2 * c + 1)],
                                     axis=-1).astype(BF16)
                pg = jnp.dot(xc, wg_buf[slot, c * kc:(c + 1) * kc, :], preferred_element_type=F32)
                pu = jnp.dot(xc, wu_buf[slot, c * kc:(c + 1) * kc, :], preferred_element_type=F32)
                gate = pg if gate is None else gate + pg
                up = pu if up is None else up + pu
            side()
            if down_blk is not None:
                k, hid, yt = down_blk
                slot = par_ref[tb + k]
                yc = jnp.dot(hid[...], wdn_buf[slot, :, c * kc:(c + 1) * kc],
                             preferred_element_type=F32)
                for q in range(2):
                    j = 2 * c + q
                    yt[j * pitch:j * pitch + bm, :] = yc[:, q * LANES:(q + 1) * LANES]
            side()
        if up_blk is not None:
            up_blk[2][...] = (jax.nn.silu(gate) * up).astype(BF16)
        side()

    @pl.when(nb > 0)
    def _():
        for c in weight_copies(exp_ref[tb], 0):
            c.start()
        second = next_ref[tb]

        @pl.when(second >= 0)
        def _():
            for c in weight_copies(second, 1):
                c.start()
        gather_rows(0, xt_a)(0, bm)
        handoff(0)
        step((0, xt_a, hid_a), None, [gather_rows(1, xt_b)])
        handoff(1)
        step((1, xt_b, hid_b), (0, hid_a, yt_a), [gather_rows(2, xt_a)])

        def trip(i, carry):
            t = 2 * i + 1
            handoff(t + 1)
            step((t + 1, xt_a, hid_a), (t, hid_b, yt_b),
                 [gather_rows(t + 2, xt_b), scatter_rows(t - 1, yt_a)])
            handoff(t + 2)
            step((t + 2, xt_b, hid_b), (t + 1, hid_a, yt_a),
                 [gather_rows(t + 3, xt_a), scatter_rows(t, yt_b)])
            return carry

        lax.fori_loop(0, nb // 2, trip, 0)

        @pl.when(nb % 2 == 1)
        def _():
            scatter_rows(nb - 1, yt_a)(0, bm)


def _moe_group(table, h2t, rows, wl, wg, wu, wdn, tg, bm, nb_max, batch=8):
    ne, d, eh = wg.shape
    nj = d // LANES
    assert nj == SUBLANES
    g = h2t.shape[0] // (tg * nj)
    kern = functools.partial(_moe_group_kernel, bm=bm, nj=nj, batch=batch, nb_max=nb_max)
    slots = pl.BlockSpec((nb_max * bm,), lambda gi, *_: (gi,), memory_space=pltpu.SMEM)
    hbm = pl.BlockSpec(memory_space=pl.ANY)
    pitch_rows = nj * (bm + 1)
    grid_spec = pltpu.PrefetchScalarGridSpec(
        num_scalar_prefetch=len(table),
        grid=(g,),
        in_specs=[pl.BlockSpec((tg * nj, LANES), lambda gi, *_: (gi, 0)),
                  slots, slots, hbm, hbm, hbm],
        out_specs=pl.BlockSpec((tg * nj, LANES), lambda gi, *_: (gi, 0)),
        scratch_shapes=[pltpu.VMEM((MOE_WEIGHT_BUFFERS, d, eh), BF16),
                        pltpu.VMEM((MOE_WEIGHT_BUFFERS, d, eh), BF16),
                        pltpu.VMEM((MOE_WEIGHT_BUFFERS, eh, d), BF16),
                        pltpu.SemaphoreType.DMA((MOE_WEIGHT_BUFFERS,)),
                        pltpu.VMEM((pitch_rows, LANES), F32), pltpu.VMEM((pitch_rows, LANES), F32),
                        pltpu.VMEM((bm, eh), BF16), pltpu.VMEM((bm, eh), BF16),
                        pltpu.VMEM((pitch_rows, LANES), F32), pltpu.VMEM((pitch_rows, LANES), F32)],
    )
    return pl.pallas_call(
        kern,
        grid_spec=grid_spec,
        out_shape=jax.ShapeDtypeStruct((g * tg * nj, LANES), F32),
        compiler_params=_params(1),
        name="moe_group",
    )(*table, h2t, rows, wl, wg, wu, wdn)


def _finalize_kernel(r_ref, xacc_ref, gt2_ref, gf_ref, o_ref, *, final_norm):
    tm, d = xacc_ref.shape
    nj = d // LANES
    routed = jnp.concatenate([r_ref[pl.ds(j, tm, stride=nj), :] for j in range(nj)], axis=-1)
    xo = xacc_ref[...] + gt2_ref[...] * routed
    o_ref[...] = _rmsnorm(xo, gf_ref[...]) if final_norm else xo


def _finalize(routed, xacc, gt2, g_final, seq_len, tm, final_norm):
    t, d = xacc.shape
    nj = d // LANES
    return pl.pallas_call(
        functools.partial(_finalize_kernel, final_norm=final_norm),
        grid=(t // tm,),
        in_specs=[_row_spec(tm * nj, LANES),
                  _row_spec(tm, d), _mod_spec(False, tm, d, seq_len // tm),
                  _full_spec(g_final.shape)],
        out_specs=_row_spec(tm, d),
        out_shape=jax.ShapeDtypeStruct((t, d), F32),
        compiler_params=_params(1),
        name="finalize",
    )(routed, xacc, gt2, g_final)


def _s5_discretize(lam_re, lam_im, log_dt, b_re, b_im, c_re, c_im):
    g, n = lam_re.shape
    hg = b_re.shape[2]
    lr = jnp.minimum(lam_re.astype(F32), -1e-4)
    li = lam_im.astype(F32)
    dt = jnp.exp(log_dt.astype(F32))[:, None]
    mag = jnp.exp(lr * dt)
    abar_r, abar_i = mag * jnp.cos(li * dt), mag * jnp.sin(li * dt)
    den = lr * lr + li * li
    nr, ni = abar_r - 1.0, abar_i
    f_r = (nr * lr + ni * li) / den
    f_i = (ni * lr - nr * li) / den
    br_, bi_ = b_re.astype(F32), b_im.astype(F32)
    bbar_r = f_r[..., None] * br_ - f_i[..., None] * bi_
    bbar_i = f_r[..., None] * bi_ + f_i[..., None] * br_
    gpc = LANES // hg
    n_chunks = g // gpc
    eye = jnp.eye(gpc, dtype=F32)

    def b_chunks(bb):
        bb = bb.reshape(n_chunks, gpc, n, hg)
        m = jnp.einsum('jgnh,gk->jghkn', bb, eye)
        return m.reshape(n_chunks, gpc * hg, gpc * n)

    def c_chunks(cc):
        cc = cc.astype(F32).reshape(n_chunks, gpc, hg, n)
        m = jnp.einsum('jghn,gk->jgnkh', cc, eye)
        return m.reshape(n_chunks, gpc * n, gpc * hg)

    bj = jnp.concatenate([b_chunks(bbar_r), b_chunks(bbar_i)], axis=2).astype(BF16)
    cj = jnp.concatenate([c_chunks(c_re), -c_chunks(c_im)], axis=1).astype(BF16)
    return abar_r.reshape(1, g * n), abar_i.reshape(1, g * n), bj, cj


def kernel(x_prompt, x_sample, c_prompt, c_sample, state_ssm_re, state_ssm_im, g_norm1, g_norm2, w_ada, b_ada, w_in, w_out, ssm_lambda_re, ssm_lambda_im, ssm_log_dt, ssm_b_re, ssm_b_im, ssm_c_re, ssm_c_im, ssm_d, ssm_w_glu, ssm_b_glu, gm_ln_g, gm_ln_b, gm_w_s, gm_b_s, router_w, router_bias, moe_w_gate, moe_w_up, moe_w_down, shared_w_gate, shared_w_up, shared_w_down, g_final):
    bp, seq_len, d = x_prompt.shape
    bs, dec_len, _ = x_sample.shape
    depth = w_ada.shape[0]
    assert dec_len == 1 and seq_len % CHUNK == 0 and bp % SUBLANES == 0
    groups, n_state = ssm_lambda_re.shape[1:]
    sw = groups * SSM_GROUP
    gw = gm_ln_g.shape[1]
    hd = gw // GM_HEADS

    xp = x_prompt.reshape(bp * seq_len, d)
    xs = x_sample.reshape(bs, d)
    c_all = jnp.concatenate([c_prompt, c_sample], axis=0)
    gf = g_final.reshape(1, d)

    p_re, p_im, s_re, s_im, s_v = [], [], [], [], []
    for l in range(depth):
        mod = _ada(c_all, w_ada[l].astype(BF16), b_ada[l].reshape(1, -1))
        mod_p = [m.reshape(bp, 1, d) for m in jnp.split(mod[:bp], 6, axis=-1)]
        mod_s = jnp.split(mod[bp:], 6, axis=-1)

        g1 = g_norm1[l].reshape(1, d)
        g2 = g_norm2[l].reshape(1, d)
        w_in_b = w_in[l].astype(BF16)
        w_out_b = w_out[l].astype(BF16)
        ln_g = gm_ln_g[l].reshape(1, gw)
        ln_b = gm_ln_b[l].reshape(1, gw)
        ws_tril = (gm_w_s[l] * jnp.tril(jnp.ones((CHUNK, CHUNK), F32))[None]).astype(BF16)
        bs_full = jnp.repeat(gm_b_s[l].T, hd, axis=1)
        w0 = jnp.repeat(gm_w_s[l][:, 0, 0], hd).reshape(1, gw)
        b0 = jnp.repeat(gm_b_s[l][:, 0], hd).reshape(1, gw)
        a_re, a_im, bj, cj = _s5_discretize(ssm_lambda_re[l], ssm_lambda_im[l], ssm_log_dt[l],
                                            ssm_b_re[l], ssm_b_im[l], ssm_c_re[l], ssm_c_im[l])
        d_vec = ssm_d[l].reshape(1, sw)
        w_glu = ssm_w_glu[l].astype(BF16)
        b_glu = ssm_b_glu[l].reshape(1, sw)
        pad = LANES - N_EXPERTS
        rw = jnp.pad(router_w[l], ((0, 0), (0, pad)))
        rw_hi = rw.astype(BF16)
        rw_lo = (rw - rw_hi.astype(F32)).astype(BF16)
        rb = jnp.pad(router_bias[l], (0, pad), constant_values=-jnp.inf).reshape(1, LANES)
        swg, swu, swd = (shared_w_gate[l].astype(BF16), shared_w_up[l].astype(BF16),
                         shared_w_down[l].astype(BF16))
        ewg, ewu, ewd = (moe_w_gate[l].astype(BF16), moe_w_up[l].astype(BF16),
                         moe_w_down[l].astype(BF16))

        tm_row = min(ROW_TILE, seq_len)
        ua, yb = _premix_seq(xp, mod_p[0], mod_p[1], g1, w_in_b, ln_g, ln_b, ws_tril, bs_full,
                             seq_len, tm=tm_row)
        ya, hpr, hpi = _s5_seq(ua.reshape(bp, seq_len, sw), bj, cj, a_re, a_im, d_vec,
                               w_glu, b_glu, tl=min(SCAN_TILE, seq_len))
        xacc, h2, wd = _postmix(xp, ya.reshape(bp * seq_len, sw), yb, mod_p[2], mod_p[3],
                                mod_p[4], mod_p[5], g2, w_out_b, rw_hi, rw_lo, rb, swg, swu, swd,
                                per_row=False, seq_len=seq_len, tm=tm_row, token_tiles=True)
        last = l == depth - 1
        tg = min(MOE_GROUP, seq_len)
        bm = min(MOE_BLOCK, tg)
        nb_max = -(-(tg * TOP_K // bm + N_EXPERTS + 3) // SUBLANES) * SUBLANES
        slot_rows, slot_w, counts = _route_plan(wd, tg, d // LANES)
        table, block_row = _block_table(counts[:, :, 0], tg, bm, nb_max)
        by_block = lambda a: jnp.take(a.reshape(-1, bm), block_row, axis=0).reshape(-1)
        routed = _moe_group(table, h2, by_block(slot_rows), by_block(slot_w),
                            ewg, ewu, ewd, tg, bm, nb_max)
        xp = _finalize(routed, xacc, mod_p[5], gf, seq_len, tm_row, final_norm=last)
        p_re.append(hpr.reshape(bp, groups, n_state))
        p_im.append(hpi.reshape(bp, groups, n_state))

        ua, yb, vn = _premix_step(xs, mod_s[0], mod_s[1], g1, w_in_b, ln_g, ln_b, w0, b0)
        ya, hsr, hsi = _s5_step(ua, state_ssm_re[l].reshape(bs, groups * n_state),
                                state_ssm_im[l].reshape(bs, groups * n_state),
                                bj, cj, a_re, a_im, d_vec, w_glu, b_glu)
        xacc, h2, wd = _postmix(xs, ya, yb, mod_s[2], mod_s[3], mod_s[4], mod_s[5], g2, w_out_b,
                                rw_hi, rw_lo, rb, swg, swu, swd, per_row=True, seq_len=1, tm=bs,
                                token_tiles=False)
        xs = _moe_dense(h2, wd, xacc, mod_s[5], gf, ewg, ewu, ewd, per_row=True, seq_len=1,
                        tm=bs, final_norm=last)
        s_re.append(hsr.reshape(bs, groups, n_state))
        s_im.append(hsi.reshape(bs, groups, n_state))
        s_v.append(vn.reshape(bs, 1, gw))

    y_prompt = xp.reshape(bp, seq_len, d)
    y_sample = xs.reshape(bs, 1, d)
    return (y_prompt, y_sample, jnp.stack(p_re), jnp.stack(p_im), jnp.stack(s_re),
            jnp.stack(s_im), jnp.stack(s_v))
```

```python
import functools

import jax
import jax.numpy as jnp
from jax import lax
from jax.experimental import pallas as pl
from jax.experimental.pallas import tpu as pltpu

F32 = jnp.float32
BF16 = jnp.bfloat16

EPS = 1e-6
SSM_GROUP = 16
SSM_STATE = 64
GM_HEADS = 4
CHUNK = 128
N_EXPERTS = 64
TOP_K = 8
N_ROUTE_GROUPS = 8
TOPK_ROUTE_GROUPS = 4
ROUTED_SCALE = 2.5

LANES = 128
SUBLANES = 8
VMEM_LIMIT = 48 * 1024 * 1024
ROW_TILE = 512
SCAN_TILE = 128
MOE_GROUP = 2048
MOE_BLOCK = 128
MOE_WEIGHT_BUFFERS = 4


def _params(n_axes):
    return pltpu.CompilerParams(dimension_semantics=("arbitrary",) * n_axes,
                                vmem_limit_bytes=VMEM_LIMIT)


def _dot(a, b):
    return jnp.dot(a.astype(BF16), b.astype(BF16), preferred_element_type=F32)


def _rmsnorm(x, g):
    ms = jnp.mean(x * x, axis=-1, keepdims=True)
    return x * lax.rsqrt(ms + EPS) * g


def _ada_kernel(c_ref, w_ref, b_ref, o_ref):
    c = c_ref[...]
    o_ref[...] = _dot(jax.nn.silu(c), w_ref[...]) + b_ref[...]


def _ada(c, w, b, tn=512):
    m, d = c.shape
    n = w.shape[1]
    return pl.pallas_call(
        _ada_kernel,
        grid=(n // tn,),
        in_specs=[pl.BlockSpec((m, d), lambda j: (0, 0)),
                  pl.BlockSpec((d, tn), lambda j: (0, j)),
                  pl.BlockSpec((1, tn), lambda j: (0, j))],
        out_specs=pl.BlockSpec((m, tn), lambda j: (0, j)),
        out_shape=jax.ShapeDtypeStruct((m, n), F32),
        compiler_params=_params(1),
        name="ada_mod",
    )(c, w, b)


def _premix_common(x_ref, sh_ref, sc_ref, g_ref, win_ref, lng_ref, lnb_ref, sw):
    x = x_ref[...]
    h = _rmsnorm(x, g_ref[...]) * (1.0 + sc_ref[...]) + sh_ref[...]
    proj = _dot(h, win_ref[...])
    u_a = proj[:, :sw]
    z = jax.nn.gelu(proj[:, sw:])
    gw = z.shape[1] // 2
    u_b = z[:, :gw]
    v_b = z[:, gw:]
    hd = gw // GM_HEADS
    parts = []
    for k in range(GM_HEADS):
        vh = v_b[:, k * hd:(k + 1) * hd]
        mu = jnp.mean(vh, axis=-1, keepdims=True)
        dv = vh - mu
        var = jnp.mean(dv * dv, axis=-1, keepdims=True)
        parts.append(dv * lax.rsqrt(var + EPS))
    vn = jnp.concatenate(parts, axis=-1) * lng_ref[...] + lnb_ref[...]
    return u_a, u_b, vn


def _premix_seq_kernel(x_ref, sh_ref, sc_ref, g_ref, win_ref, lng_ref, lnb_ref,
                       ws_ref, bs_ref, ua_ref, yb_ref, *, sw):
    u_a, u_b, vn = _premix_common(x_ref, sh_ref, sc_ref, g_ref, win_ref, lng_ref, lnb_ref, sw)
    ua_ref[...] = u_a
    tm, gw = u_b.shape
    hd = gw // GM_HEADS
    vnb = vn.astype(BF16)
    for c in range(tm // CHUNK):
        rows = slice(c * CHUNK, (c + 1) * CHUNK)
        for k in range(GM_HEADS):
            cols = slice(k * hd, (k + 1) * hd)
            s = jnp.dot(ws_ref[k], vnb[rows, cols], preferred_element_type=F32)
            yb_ref[rows, cols] = u_b[rows, cols] * (s + bs_ref[:, cols])


def _premix_step_kernel(x_ref, sh_ref, sc_ref, g_ref, win_ref, lng_ref, lnb_ref,
                        w0_ref, b0_ref, ua_ref, yb_ref, vn_ref, *, sw):
    u_a, u_b, vn = _premix_common(x_ref, sh_ref, sc_ref, g_ref, win_ref, lng_ref, lnb_ref, sw)
    ua_ref[...] = u_a
    vn_ref[...] = vn
    yb_ref[...] = u_b * (vn * w0_ref[...] + b0_ref[...])


def _row_spec(tm, d):
    return pl.BlockSpec((tm, d), lambda i: (i, 0))


def _full_spec(shape):
    nd = len(shape)
    return pl.BlockSpec(shape, lambda i: (0,) * nd)


def _mod_spec(per_row, tm, d, tiles_per_batch):
    if per_row:
        return pl.BlockSpec((tm, d), lambda i: (i, 0))
    return pl.BlockSpec((None, 1, d), lambda i: (i // tiles_per_batch, 0, 0))


def _premix_seq(x, sh, sc, g, w_in, ln_g, ln_b, ws_tril, bs_full, seq_len, tm):
    t, d = x.shape
    sw = ln_g.shape[1]
    tpb = seq_len // tm
    mod = _mod_spec(False, tm, d, tpb)
    return pl.pallas_call(
        functools.partial(_premix_seq_kernel, sw=sw),
        grid=(t // tm,),
        in_specs=[_row_spec(tm, d), mod, mod, _full_spec(g.shape), _full_spec(w_in.shape),
                  _full_spec(ln_g.shape), _full_spec(ln_b.shape), _full_spec(ws_tril.shape),
                  _full_spec(bs_full.shape)],
        out_specs=[_row_spec(tm, sw), _row_spec(tm, sw)],
        out_shape=[jax.ShapeDtypeStruct((t, sw), F32), jax.ShapeDtypeStruct((t, sw), F32)],
        compiler_params=_params(1),
        name="premix_seq",
    )(x, sh, sc, g, w_in, ln_g, ln_b, ws_tril, bs_full)


def _premix_step(x, sh, sc, g, w_in, ln_g, ln_b, w0, b0):
    t, d = x.shape
    sw = ln_g.shape[1]
    mod = _mod_spec(True, t, d, 1)
    return pl.pallas_call(
        functools.partial(_premix_step_kernel, sw=sw),
        grid=(1,),
        in_specs=[_row_spec(t, d), mod, mod, _full_spec(g.shape), _full_spec(w_in.shape),
                  _full_spec(ln_g.shape), _full_spec(ln_b.shape), _full_spec(w0.shape),
                  _full_spec(b0.shape)],
        out_specs=[_row_spec(t, sw)] * 3,
        out_shape=[jax.ShapeDtypeStruct((t, sw), F32)] * 3,
        compiler_params=_params(1),
        name="premix_step",
    )(x, sh, sc, g, w_in, ln_g, ln_b, w0, b0)


def _s5_output(xs, u, cj_ref, d_ref, wglu_ref, bglu_ref, n_chunks):
    ys = [jnp.dot(xs(j).astype(BF16), cj_ref[j], preferred_element_type=F32)
          for j in range(n_chunks)]
    y = jnp.concatenate(ys, axis=-1) + d_ref[...] * u
    y = jax.nn.gelu(y)
    gate = jax.nn.sigmoid(_dot(y, wglu_ref[...]) + bglu_ref[...])
    return y * gate


def _s5_seq_kernel(u_ref, bj_ref, cj_ref, are_ref, aim_ref, d_ref, wglu_ref, bglu_ref,
                   y_ref, sre_ref, sim_ref, bu_ref, st_ref, *, nb, tl, n_chunks, unroll):
    i = pl.program_id(0)
    n_tiles = bu_ref.shape[0]
    tpc = n_tiles // n_chunks
    hpc = tpc // 2
    hw = hpc * LANES
    kw = u_ref.shape[2] // n_chunks

    @pl.when(i == 0)
    def _():
        st_ref[...] = jnp.zeros_like(st_ref)

    u_all = u_ref[...].reshape(nb * tl, u_ref.shape[2])
    ub = u_all.astype(BF16)
    for j in range(n_chunks):
        bu = jnp.dot(ub[:, j * kw:(j + 1) * kw], bj_ref[j], preferred_element_type=F32)
        for b in range(nb):
            for c in range(tpc):
                bu_ref[j * tpc + c, pl.ds(b, tl, stride=nb), :] = \
                    bu[b * tl:(b + 1) * tl, c * LANES:(c + 1) * LANES]

    for j in range(n_chunks):
        re = slice(j * 2 * hw, j * 2 * hw + hw)
        im = slice(j * 2 * hw + hw, (j + 1) * 2 * hw)
        ar = jnp.broadcast_to(are_ref[:, j * hw:(j + 1) * hw], (nb, hw))
        ai = jnp.broadcast_to(aim_ref[:, j * hw:(j + 1) * hw], (nb, hw))
        re_tiles = [j * tpc + c for c in range(hpc)]
        im_tiles = [j * tpc + hpc + c for c in range(hpc)]

        def body(s, carry, ar=ar, ai=ai, re_tiles=re_tiles, im_tiles=im_tiles):
            xr, xi = carry
            for k in range(unroll):
                r0 = pl.multiple_of((s * unroll + k) * nb, nb)
                br = jnp.concatenate([bu_ref[c, pl.ds(r0, nb), :] for c in re_tiles], axis=-1)
                bi = jnp.concatenate([bu_ref[c, pl.ds(r0, nb), :] for c in im_tiles], axis=-1)
                nr = ar * xr - ai * xi + br
                ni = ar * xi + ai * xr + bi
                for q, c in enumerate(re_tiles):
                    bu_ref[c, pl.ds(r0, nb), :] = nr[:, q * LANES:(q + 1) * LANES]
                for q, c in enumerate(im_tiles):
                    bu_ref[c, pl.ds(r0, nb), :] = ni[:, q * LANES:(q + 1) * LANES]
                xr, xi = nr, ni
            return xr, xi

        xr, xi = lax.fori_loop(0, tl // unroll, body, (st_ref[:, re], st_ref[:, im]))
        st_ref[:, re] = xr
        st_ref[:, im] = xi
        sre_ref[:, j * hw:(j + 1) * hw] = xr
        sim_ref[:, j * hw:(j + 1) * hw] = xi

    def state_chunk(j):
        return jnp.concatenate(
            [jnp.concatenate([bu_ref[j * tpc + c, pl.ds(b, tl, stride=nb), :]
                              for c in range(tpc)], axis=-1) for b in range(nb)], axis=0)

    y = _s5_output(state_chunk, u_all, cj_ref, d_ref, wglu_ref, bglu_ref, n_chunks)
    y_ref[...] = y.reshape(y_ref.shape)


def _s5_seq(u, bj, cj, a_re, a_im, d_vec, w_glu, b_glu, tl, unroll=4):
    nb, seq_len, sw = u.shape
    n_chunks, _, cw = bj.shape
    ns = a_re.shape[1]
    kern = functools.partial(_s5_seq_kernel, nb=nb, tl=tl, n_chunks=n_chunks, unroll=unroll)
    return pl.pallas_call(
        kern,
        grid=(seq_len // tl,),
        in_specs=[pl.BlockSpec((nb, tl, sw), lambda i: (0, i, 0)),
                  _full_spec(bj.shape), _full_spec(cj.shape), _full_spec(a_re.shape),
                  _full_spec(a_im.shape), _full_spec(d_vec.shape), _full_spec(w_glu.shape),
                  _full_spec(b_glu.shape)],
        out_specs=[pl.BlockSpec((nb, tl, sw), lambda i: (0, i, 0)),
                   _full_spec((nb, ns)), _full_spec((nb, ns))],
        out_shape=[jax.ShapeDtypeStruct((nb, seq_len, sw), F32),
                   jax.ShapeDtypeStruct((nb, ns), F32), jax.ShapeDtypeStruct((nb, ns), F32)],
        scratch_shapes=[pltpu.VMEM((n_chunks * cw // LANES, tl * nb, LANES), F32),
                        pltpu.VMEM((nb, n_chunks * cw), F32)],
        compiler_params=_params(1),
        name="s5_seq",
    )(u, bj, cj, a_re, a_im, d_vec, w_glu, b_glu)


def _s5_step_kernel(u_ref, hre_ref, him_ref, bj_ref, cj_ref, are_ref, aim_ref, d_ref,
                    wglu_ref, bglu_ref, y_ref, sre_ref, sim_ref, *, n_chunks):
    u = u_ref[...]
    ub = u.astype(BF16)
    kw = u.shape[1] // n_chunks
    hw = bj_ref.shape[2] // 2
    chunks = []
    for j in range(n_chunks):
        st = slice(j * hw, (j + 1) * hw)
        bu = jnp.dot(ub[:, j * kw:(j + 1) * kw], bj_ref[j], preferred_element_type=F32)
        ar, ai = are_ref[:, st], aim_ref[:, st]
        hr, hi = hre_ref[:, st], him_ref[:, st]
        xr = ar * hr - ai * hi + bu[:, :hw]
        xi = ar * hi + ai * hr + bu[:, hw:]
        sre_ref[:, st] = xr
        sim_ref[:, st] = xi
        chunks.append(jnp.concatenate([xr, xi], axis=-1))
    y_ref[...] = _s5_output(chunks.__getitem__, u, cj_ref, d_ref, wglu_ref, bglu_ref, n_chunks)


def _s5_step(u, h_re, h_im, bj, cj, a_re, a_im, d_vec, w_glu, b_glu):
    t, sw = u.shape
    ns = a_re.shape[1]
    args = (u, h_re, h_im, bj, cj, a_re, a_im, d_vec, w_glu, b_glu)
    return pl.pallas_call(
        functools.partial(_s5_step_kernel, n_chunks=bj.shape[0]),
        grid=(1,),
        in_specs=[_full_spec(a.shape) for a in args],
        out_specs=[_full_spec((t, sw)), _full_spec((t, ns)), _full_spec((t, ns))],
        out_shape=[jax.ShapeDtypeStruct((t, sw), F32), jax.ShapeDtypeStruct((t, ns), F32),
                   jax.ShapeDtypeStruct((t, ns), F32)],
        compiler_params=_params(1),
        name="s5_step",
    )(*args)


def _group_reduce(x, lane, width, op):
    n = x.shape[-1]
    s = 1
    while s < width:
        up = pltpu.roll(x, s, 1)
        dn = pltpu.roll(x, n - s, 1)
        x = op(x, jnp.where((lane & s) != 0, up, dn))
        s *= 2
    return x


def _first_max(x, lane, big):
    m = jnp.max(x, axis=-1, keepdims=True)
    return jnp.min(jnp.where(x == m, lane, big), axis=-1, keepdims=True)


def _route(logits, bias):
    rows, n = logits.shape
    lane = lax.broadcasted_iota(jnp.int32, (rows, n), 1)
    lane_f = lane.astype(F32)
    neg = jnp.float32(-jnp.inf)
    big = jnp.float32(n)
    gsz = N_EXPERTS // N_ROUTE_GROUPS
    group_start = (lane - (lane & (gsz - 1))).astype(F32)
    scores = jax.nn.sigmoid(logits)
    biased = scores + bias
    m1 = _group_reduce(biased, lane, gsz, jnp.maximum)
    first = _group_reduce(jnp.where(biased == m1, lane_f, big), lane, gsz, jnp.minimum)
    m2 = _group_reduce(jnp.where(lane_f == first, neg, biased), lane, gsz, jnp.maximum)
    gs = m1 + m2
    gsel = jnp.zeros((rows, n), F32)
    work = gs
    for _ in range(TOPK_ROUTE_GROUPS):
        hit = group_start == _first_max(work, lane_f, big)
        gsel = jnp.where(hit, 1.0, gsel)
        work = jnp.where(hit, neg, work)
    work = jnp.where(gsel > 0.0, biased, neg)
    w = jnp.zeros((rows, n), F32)
    for _ in range(TOP_K):
        hit = lane_f == _first_max(work, lane_f, big)
        w = jnp.where(hit, scores, w)
        work = jnp.where(hit, neg, work)
    return w / jnp.sum(w, axis=-1, keepdims=True) * ROUTED_SCALE


def _split_bf16(x):
    hi = x.astype(BF16)
    lo = (x - hi.astype(F32)).astype(BF16)
    return hi, lo


def _postmix_kernel(x_ref, ya_ref, yb_ref, gt1_ref, sh2_ref, sc2_ref, gt2_ref, g2_ref,
                    wout_ref, rwh_ref, rwl_ref, rb_ref, swg_ref, swu_ref, swd_ref,
                    xacc_ref, h2_ref, wd_ref, *, token_tiles):
    y = jnp.concatenate([ya_ref[...], yb_ref[...]], axis=-1)
    x1 = x_ref[...] + gt1_ref[...] * _dot(y, wout_ref[...])
    h2 = _rmsnorm(x1, g2_ref[...]) * (1.0 + sc2_ref[...]) + sh2_ref[...]
    hb = h2.astype(BF16)
    if token_tiles:
        tm, d = h2.shape
        nj = d // LANES
        for j in range(nj):
            h2_ref[pl.ds(j, tm, stride=nj), :] = h2[:, j * LANES:(j + 1) * LANES]
    else:
        h2_ref[...] = hb
    h_hi, h_lo = _split_bf16(h2)
    logits = (jnp.dot(h_hi, rwh_ref[...], preferred_element_type=F32)
              + jnp.dot(h_hi, rwl_ref[...], preferred_element_type=F32)
              + jnp.dot(h_lo, rwh_ref[...], preferred_element_type=F32))
    wd_ref[...] = _route(logits, rb_ref[...])
    hid = jax.nn.silu(jnp.dot(hb, swg_ref[...], preferred_element_type=F32)) * \
        jnp.dot(hb, swu_ref[...], preferred_element_type=F32)
    shared = _dot(hid, swd_ref[...])
    xacc_ref[...] = x1 + gt2_ref[...] * shared


def _postmix(x, ya, yb, gt1, sh2, sc2, gt2, g2, w_out, rw_hi, rw_lo, rb, swg, swu, swd,
             per_row, seq_len, tm, token_tiles):
    t, d = x.shape
    sw = ya.shape[1]
    ne = rw_hi.shape[1]
    nj = d // LANES
    mod = _mod_spec(per_row, tm, d, max(seq_len // tm, 1))
    weights = (g2, w_out, rw_hi, rw_lo, rb, swg, swu, swd)
    if token_tiles:
        h2_spec, h2_shape = _row_spec(tm * nj, LANES), jax.ShapeDtypeStruct((t * nj, LANES), F32)
    else:
        h2_spec, h2_shape = _row_spec(tm, d), jax.ShapeDtypeStruct((t, d), BF16)
    return pl.pallas_call(
        functools.partial(_postmix_kernel, token_tiles=token_tiles),
        grid=(t // tm,),
        in_specs=[_row_spec(tm, d), _row_spec(tm, sw), _row_spec(tm, sw), mod, mod, mod, mod]
        + [_full_spec(w.shape) for w in weights],
        out_specs=[_row_spec(tm, d), h2_spec, _row_spec(tm, ne)],
        out_shape=[jax.ShapeDtypeStruct((t, d), F32), h2_shape,
                   jax.ShapeDtypeStruct((t, ne), F32)],
        compiler_params=_params(1),
        name="postmix",
    )(x, ya, yb, gt1, sh2, sc2, gt2, *weights)


def _moe_dense_kernel(h_ref, wd_ref, xacc_ref, gt2_ref, gf_ref, wg_ref, wu_ref, wdn_ref,
                      o_ref, acc_ref, *, final_norm):
    e = pl.program_id(1)

    @pl.when(e == 0)
    def _():
        acc_ref[...] = jnp.zeros_like(acc_ref)

    hb = h_ref[...]
    wd = wd_ref[...]
    lane = lax.broadcasted_iota(jnp.int32, wd.shape, 1)
    col = jnp.sum(jnp.where(lane == e, wd, 0.0), axis=-1, keepdims=True)
    hid = jax.nn.silu(jnp.dot(hb, wg_ref[...], preferred_element_type=F32)) * \
        jnp.dot(hb, wu_ref[...], preferred_element_type=F32)
    acc_ref[...] += col * _dot(hid, wdn_ref[...])

    @pl.when(e == pl.num_programs(1) - 1)
    def _():
        xo = xacc_ref[...] + gt2_ref[...] * acc_ref[...]
        o_ref[...] = _rmsnorm(xo, gf_ref[...]) if final_norm else xo


def _moe_dense(h2, wd, xacc, gt2, g_final, wg, wu, wdn, per_row, seq_len, tm, final_norm):
    t, d = xacc.shape
    ne, _, eh = wg.shape
    nl = wd.shape[1]
    tpb = max(seq_len // tm, 1)
    if per_row:
        mod = pl.BlockSpec((tm, d), lambda i, e: (i, 0))
    else:
        mod = pl.BlockSpec((None, 1, d), lambda i, e: (i // tpb, 0, 0))
    row = lambda w: pl.BlockSpec((tm, w), lambda i, e: (i, 0))
    return pl.pallas_call(
        functools.partial(_moe_dense_kernel, final_norm=final_norm),
        grid=(t // tm, ne),
        in_specs=[row(d), row(nl), row(d), mod,
                  pl.BlockSpec(g_final.shape, lambda i, e: (0, 0)),
                  pl.BlockSpec((None, d, eh), lambda i, e: (e, 0, 0)),
                  pl.BlockSpec((None, d, eh), lambda i, e: (e, 0, 0)),
                  pl.BlockSpec((None, eh, d), lambda i, e: (e, 0, 0))],
        out_specs=row(d),
        out_shape=jax.ShapeDtypeStruct((t, d), F32),
        scratch_shapes=[pltpu.VMEM((tm, d), F32)],
        compiler_params=_params(2),
        name="moe_dense",
    )(h2, wd, xacc, gt2, g_final, wg, wu, wdn)


TOKEN_BITS = 12


def _route_plan_kernel(wd_ref, rows_ref, wl_ref, cnt_ref, *, tg, nj):
    wt = wd_ref[...].T[:N_EXPERTS]
    sel = wt > 0.0
    lane = lax.broadcasted_iota(jnp.int32, wt.shape, 1)
    c = sel.astype(jnp.int32)
    s = 1
    while s < tg:
        c = c + jnp.where(lane >= s, pltpu.roll(c, s, 1), 0)
        s *= 2
    cnt_ref[...] = jnp.broadcast_to(c[:, tg - 1:tg], cnt_ref.shape)
    dist = lane - (c - 1)
    key = jnp.where(sel, (dist << TOKEN_BITS) | lane, -1)
    w = jnp.where(sel, wt, 0.0)
    s = 1
    while s < tg:
        leaving = (key >= 0) & (((key >> TOKEN_BITS) & s) != 0)
        key_in = pltpu.roll(key, tg - s, 1)
        w_in = pltpu.roll(w, tg - s, 1)
        arriving = (key_in >= 0) & (((key_in >> TOKEN_BITS) & s) != 0)
        key = jnp.where(arriving, key_in, jnp.where(leaving, -1, key))
        w = jnp.where(arriving, w_in, jnp.where(leaving, 0.0, w))
        s *= 2
    rows_ref[...] = jnp.where(key >= 0, key & ((1 << TOKEN_BITS) - 1), 0) * nj
    wl_ref[...] = w


def _route_plan(wd, tg, nj):
    t, nl = wd.shape
    assert tg <= (1 << TOKEN_BITS) and t % tg == 0
    g = t // tg
    grp = lambda w: pl.BlockSpec((None, N_EXPERTS, w), lambda i: (i, 0, 0))
    return pl.pallas_call(
        functools.partial(_route_plan_kernel, tg=tg, nj=nj),
        grid=(g,),
        in_specs=[pl.BlockSpec((tg, nl), lambda i: (i, 0))],
        out_specs=[grp(tg), grp(tg), grp(LANES)],
        out_shape=[jax.ShapeDtypeStruct((g, N_EXPERTS, tg), jnp.int32),
                   jax.ShapeDtypeStruct((g, N_EXPERTS, tg), F32),
                   jax.ShapeDtypeStruct((g, N_EXPERTS, LANES), jnp.int32)],
        compiler_params=_params(1),
        name="route_plan",
    )(wd)


def _block_table(counts, tg, bm, nb_max):
    g, ne = counts.shape
    i32 = jnp.int32
    nblk = (counts + bm - 1) // bm
    bend = jnp.cumsum(nblk, axis=1)
    bstart = bend - nblk
    nb = bend[:, -1]
    k = jnp.arange(nb_max, dtype=i32)[None, :]
    kc = jnp.minimum(k, jnp.maximum(nb[:, None] - 1, 0))
    e_k = jnp.minimum(jnp.sum(bend[:, None, :] <= kc[:, :, None], axis=2), ne - 1).astype(i32)
    b_k = kc - jnp.take_along_axis(bstart, e_k, axis=1)
    first = (b_k == 0) & (k < nb[:, None])
    active = nblk > 0
    parity = (jnp.take_along_axis(jnp.cumsum(active.astype(i32), axis=1) - 1, e_k, axis=1)
              % MOE_WEIGHT_BUFFERS)
    ids = jnp.where(active, jnp.arange(ne, dtype=i32)[None, :], ne)
    first_active_from = jnp.flip(lax.cummin(jnp.flip(ids, 1), axis=1), 1)
    nxt = jnp.concatenate([first_active_from[:, 1:], jnp.full((g, 1), ne, i32)], axis=1)
    nxt = jnp.where(nxt >= ne, -1, nxt)
    nxt2 = jnp.where(nxt < 0, -1, jnp.take_along_axis(nxt, jnp.maximum(nxt, 0), axis=1))
    next_k = jnp.take_along_axis(nxt, e_k, axis=1)
    next2_k = jnp.take_along_axis(nxt2, e_k, axis=1)
    row = (jnp.arange(g, dtype=i32)[:, None] * ne + e_k) * (tg // bm) + b_k
    flat = lambda a: a.reshape(-1).astype(i32)
    table = (nb.astype(i32), flat(e_k), flat(first), flat(parity), flat(next_k), flat(next2_k))
    return table, flat(row)


def _moe_group_kernel(nb_ref, exp_ref, first_ref, par_ref, next_ref, next2_ref,
                      h_ref, rows_ref, wl_ref, wg_hbm, wu_hbm, wdn_hbm, o_ref,
                      wg_buf, wu_buf, wdn_buf, wsem, xt_a, xt_b, hid_a, hid_b, yt_a, yt_b,
                      *, bm, nj, batch, nb_max):
    g = pl.program_id(0)
    nb = nb_ref[g]
    tb = g * nb_max
    pitch = bm + 1
    o_ref[...] = jnp.zeros_like(o_ref)

    def weight_copies(e, slot):
        return (pltpu.make_async_copy(wg_hbm.at[e], wg_buf.at[slot], wsem.at[slot]),
                pltpu.make_async_copy(wu_hbm.at[e], wu_buf.at[slot], wsem.at[slot]),
                pltpu.make_async_copy(wdn_hbm.at[e], wdn_buf.at[slot], wsem.at[slot]))

    n_wbuf = wg_buf.shape[0]

    def handoff(k):
        @pl.when(first_ref[tb + k] == 1)
        def _():
            slot = par_ref[tb + k]
            for c in weight_copies(exp_ref[tb + k], slot):
                c.wait()
            ahead = next2_ref[tb + k]

            @pl.when(ahead >= 0)
            def _():
                for c in weight_copies(ahead, (slot + 2) % n_wbuf):
                    c.start()

    def gather_rows(k, xt):
        rows = rows_ref.at[pl.ds(pl.multiple_of(k * bm, bm), bm)]

        def emit(lo, hi):
            for m in range(lo, hi):
                r0 = pl.multiple_of(rows[m], nj)
                xt[pl.ds(m, nj, stride=pitch), :] = h_ref[pl.ds(r0, nj), :]
        return emit

    def scatter_rows(k, yt):
        base = pl.multiple_of(k * bm, bm)
        rows = rows_ref.at[pl.ds(base, bm)]
        wts = wl_ref.at[pl.ds(base, bm)]

        def emit(lo, hi):
            for m0 in range(lo, hi, batch):
                updates = []
                for m in range(m0, min(m0 + batch, hi)):
                    r0 = pl.multiple_of(rows[m], nj)
                    updates.append((r0, o_ref[pl.ds(r0, nj), :]
                                    + wts[m] * yt[pl.ds(m, nj, stride=pitch), :]))
                for r0, v in reversed(updates):
                    o_ref[pl.ds(r0, nj), :] = v
        return emit

    n_chunks = nj // 2
    kc = 2 * LANES
    n_pieces = 2 * n_chunks + 1
    cuts = [bm * i // n_pieces for i in range(n_pieces + 1)]

    def step(up_blk, down_blk, side_work):
        piece = 0

        def side():
            nonlocal piece
            for emit in side_work:
                emit(cuts[piece], cuts[piece + 1])
            piece += 1

        gate = up = None
        for c in range(n_chunks):
            if up_blk is not None:
                k, xt, _ = up_blk
                slot = par_ref[tb + k]
                xc = jnp.concatenate([xt[j * pitch:j * pitch + bm, :] for j in (2 * c, 2 * c + 1)],
                                     axis=-1).astype(BF16)
                pg = jnp.dot(xc, wg_buf[slot, c * kc:(c + 1) * kc, :], preferred_element_type=F32)
                pu = jnp.dot(xc, wu_buf[slot, c * kc:(c + 1) * kc, :], preferred_element_type=F32)
                gate = pg if gate is None else gate + pg
                up = pu if up is None else up + pu
            side()
            if down_blk is not None:
                k, hid, yt = down_blk
                slot = par_ref[tb + k]
                yc = jnp.dot(hid[...], wdn_buf[slot, :, c * kc:(c + 1) * kc],
                             preferred_element_type=F32)
                for q in range(2):
                    j = 2 * c + q
                    yt[j * pitch:j * pitch + bm, :] = yc[:, q * LANES:(q + 1) * LANES]
            side()
        if up_blk is not None:
            up_blk[2][...] = (jax.nn.silu(gate) * up).astype(BF16)
        side()

    @pl.when(nb > 0)
    def _():
        for c in weight_copies(exp_ref[tb], 0):
            c.start()
        second = next_ref[tb]

        @pl.when(second >= 0)
        def _():
            for c in weight_copies(second, 1):
                c.start()
        gather_rows(0, xt_a)(0, bm)
        handoff(0)
        step((0, xt_a, hid_a), None, [gather_rows(1, xt_b)])
        handoff(1)
        step((1, xt_b, hid_b), (0, hid_a, yt_a), [gather_rows(2, xt_a)])

        def trip(i, carry):
            t = 2 * i + 1
            handoff(t + 1)
            step((t + 1, xt_a, hid_a), (t, hid_b, yt_b),
                 [gather_rows(t + 2, xt_b), scatter_rows(t - 1, yt_a)])
            handoff(t + 2)
            step((t + 2, xt_b, hid_b), (t + 1, hid_a, yt_a),
                 [gather_rows(t + 3, xt_a), scatter_rows(t, yt_b)])
            return carry

        lax.fori_loop(0, nb // 2, trip, 0)

        @pl.when(nb % 2 == 1)
        def _():
            scatter_rows(nb - 1, yt_a)(0, bm)


def _moe_group(table, h2t, rows, wl, wg, wu, wdn, tg, bm, nb_max, batch=16):
    ne, d, eh = wg.shape
    nj = d // LANES
    assert nj == SUBLANES
    g = h2t.shape[0] // (tg * nj)
    kern = functools.partial(_moe_group_kernel, bm=bm, nj=nj, batch=batch, nb_max=nb_max)
    slots = pl.BlockSpec((nb_max * bm,), lambda gi, *_: (gi,), memory_space=pltpu.SMEM)
    hbm = pl.BlockSpec(memory_space=pl.ANY)
    pitch_rows = nj * (bm + 1)
    grid_spec = pltpu.PrefetchScalarGridSpec(
        num_scalar_prefetch=len(table),
        grid=(g,),
        in_specs=[pl.BlockSpec((tg * nj, LANES), lambda gi, *_: (gi, 0)),
                  slots, slots, hbm, hbm, hbm],
        out_specs=pl.BlockSpec((tg * nj, LANES), lambda gi, *_: (gi, 0)),
        scratch_shapes=[pltpu.VMEM((MOE_WEIGHT_BUFFERS, d, eh), BF16),
                        pltpu.VMEM((MOE_WEIGHT_BUFFERS, d, eh), BF16),
                        pltpu.VMEM((MOE_WEIGHT_BUFFERS, eh, d), BF16),
                        pltpu.SemaphoreType.DMA((MOE_WEIGHT_BUFFERS,)),
                        pltpu.VMEM((pitch_rows, LANES), F32), pltpu.VMEM((pitch_rows, LANES), F32),
                        pltpu.VMEM((bm, eh), BF16), pltpu.VMEM((bm, eh), BF16),
                        pltpu.VMEM((pitch_rows, LANES), F32), pltpu.VMEM((pitch_rows, LANES), F32)],
    )
    return pl.pallas_call(
        kern,
        grid_spec=grid_spec,
        out_shape=jax.ShapeDtypeStruct((g * tg * nj, LANES), F32),
        compiler_params=_params(1),
        name="moe_group",
    )(*table, h2t, rows, wl, wg, wu, wdn)


def _finalize_kernel(r_ref, xacc_ref, gt2_ref, gf_ref, o_ref, *, final_norm):
    tm, d = xacc_ref.shape
    nj = d // LANES
    routed = jnp.concatenate([r_ref[pl.ds(j, tm, stride=nj), :] for j in range(nj)], axis=-1)
    xo = xacc_ref[...] + gt2_ref[...] * routed
    o_ref[...] = _rmsnorm(xo, gf_ref[...]) if final_norm else xo


def _finalize(routed, xacc, gt2, g_final, seq_len, tm, final_norm):
    t, d = xacc.shape
    nj = d // LANES
    return pl.pallas_call(
        functools.partial(_finalize_kernel, final_norm=final_norm),
        grid=(t // tm,),
        in_specs=[_row_spec(tm * nj, LANES),
                  _row_spec(tm, d), _mod_spec(False, tm, d, seq_len // tm),
                  _full_spec(g_final.shape)],
        out_specs=_row_spec(tm, d),
        out_shape=jax.ShapeDtypeStruct((t, d), F32),
        compiler_params=_params(1),
        name="finalize",
    )(routed, xacc, gt2, g_final)


def _s5_discretize(lam_re, lam_im, log_dt, b_re, b_im, c_re, c_im):
    g, n = lam_re.shape
    hg = b_re.shape[2]
    lr = jnp.minimum(lam_re.astype(F32), -1e-4)
    li = lam_im.astype(F32)
    dt = jnp.exp(log_dt.astype(F32))[:, None]
    mag = jnp.exp(lr * dt)
    abar_r, abar_i = mag * jnp.cos(li * dt), mag * jnp.sin(li * dt)
    den = lr * lr + li * li
    nr, ni = abar_r - 1.0, abar_i
    f_r = (nr * lr + ni * li) / den
    f_i = (ni * lr - nr * li) / den
    br_, bi_ = b_re.astype(F32), b_im.astype(F32)
    bbar_r = f_r[..., None] * br_ - f_i[..., None] * bi_
    bbar_i = f_r[..., None] * bi_ + f_i[..., None] * br_
    gpc = LANES // hg
    n_chunks = g // gpc
    eye = jnp.eye(gpc, dtype=F32)

    def b_chunks(bb):
        bb = bb.reshape(n_chunks, gpc, n, hg)
        m = jnp.einsum('jgnh,gk->jghkn', bb, eye)
        return m.reshape(n_chunks, gpc * hg, gpc * n)

    def c_chunks(cc):
        cc = cc.astype(F32).reshape(n_chunks, gpc, hg, n)
        m = jnp.einsum('jghn,gk->jgnkh', cc, eye)
        return m.reshape(n_chunks, gpc * n, gpc * hg)

    bj = jnp.concatenate([b_chunks(bbar_r), b_chunks(bbar_i)], axis=2).astype(BF16)
    cj = jnp.concatenate([c_chunks(c_re), -c_chunks(c_im)], axis=1).astype(BF16)
    return abar_r.reshape(1, g * n), abar_i.reshape(1, g * n), bj, cj


def kernel(x_prompt, x_sample, c_prompt, c_sample, state_ssm_re, state_ssm_im, g_norm1, g_norm2, w_ada, b_ada, w_in, w_out, ssm_lambda_re, ssm_lambda_im, ssm_log_dt, ssm_b_re, ssm_b_im, ssm_c_re, ssm_c_im, ssm_d, ssm_w_glu, ssm_b_glu, gm_ln_g, gm_ln_b, gm_w_s, gm_b_s, router_w, router_bias, moe_w_gate, moe_w_up, moe_w_down, shared_w_gate, shared_w_up, shared_w_down, g_final):
    bp, seq_len, d = x_prompt.shape
    bs, dec_len, _ = x_sample.shape
    depth = w_ada.shape[0]
    assert dec_len == 1 and seq_len % CHUNK == 0 and bp % SUBLANES == 0
    groups, n_state = ssm_lambda_re.shape[1:]
    sw = groups * SSM_GROUP
    gw = gm_ln_g.shape[1]
    hd = gw // GM_HEADS

    xp = x_prompt.reshape(bp * seq_len, d)
    xs = x_sample.reshape(bs, d)
    c_all = jnp.concatenate([c_prompt, c_sample], axis=0)
    gf = g_final.reshape(1, d)

    p_re, p_im, s_re, s_im, s_v = [], [], [], [], []
    for l in range(depth):
        mod = _ada(c_all, w_ada[l].astype(BF16), b_ada[l].reshape(1, -1))
        mod_p = [m.reshape(bp, 1, d) for m in jnp.split(mod[:bp], 6, axis=-1)]
        mod_s = jnp.split(mod[bp:], 6, axis=-1)

        g1 = g_norm1[l].reshape(1, d)
        g2 = g_norm2[l].reshape(1, d)
        w_in_b = w_in[l].astype(BF16)
        w_out_b = w_out[l].astype(BF16)
        ln_g = gm_ln_g[l].reshape(1, gw)
        ln_b = gm_ln_b[l].reshape(1, gw)
        ws_tril = (gm_w_s[l] * jnp.tril(jnp.ones((CHUNK, CHUNK), F32))[None]).astype(BF16)
        bs_full = jnp.repeat(gm_b_s[l].T, hd, axis=1)
        w0 = jnp.repeat(gm_w_s[l][:, 0, 0], hd).reshape(1, gw)
        b0 = jnp.repeat(gm_b_s[l][:, 0], hd).reshape(1, gw)
        a_re, a_im, bj, cj = _s5_discretize(ssm_lambda_re[l], ssm_lambda_im[l], ssm_log_dt[l],
                                            ssm_b_re[l], ssm_b_im[l], ssm_c_re[l], ssm_c_im[l])
        d_vec = ssm_d[l].reshape(1, sw)
        w_glu = ssm_w_glu[l].astype(BF16)
        b_glu = ssm_b_glu[l].reshape(1, sw)
        pad = LANES - N_EXPERTS
        rw = jnp.pad(router_w[l], ((0, 0), (0, pad)))
        rw_hi = rw.astype(BF16)
        rw_lo = (rw - rw_hi.astype(F32)).astype(BF16)
        rb = jnp.pad(router_bias[l], (0, pad), constant_values=-jnp.inf).reshape(1, LANES)
        swg, swu, swd = (shared_w_gate[l].astype(BF16), shared_w_up[l].astype(BF16),
                         shared_w_down[l].astype(BF16))
        ewg, ewu, ewd = (moe_w_gate[l].astype(BF16), moe_w_up[l].astype(BF16),
                         moe_w_down[l].astype(BF16))

        tm_row = min(ROW_TILE, seq_len)
        ua, yb = _premix_seq(xp, mod_p[0], mod_p[1], g1, w_in_b, ln_g, ln_b, ws_tril, bs_full,
                             seq_len, tm=tm_row)
        ya, hpr, hpi = _s5_seq(ua.reshape(bp, seq_len, sw), bj, cj, a_re, a_im, d_vec,
                               w_glu, b_glu, tl=min(SCAN_TILE, seq_len))
        xacc, h2, wd = _postmix(xp, ya.reshape(bp * seq_len, sw), yb, mod_p[2], mod_p[3],
                                mod_p[4], mod_p[5], g2, w_out_b, rw_hi, rw_lo, rb, swg, swu, swd,
                                per_row=False, seq_len=seq_len, tm=tm_row, token_tiles=True)
        last = l == depth - 1
        tg = min(MOE_GROUP, seq_len)
        bm = min(MOE_BLOCK, tg)
        nb_max = -(-(tg * TOP_K // bm + N_EXPERTS + 3) // SUBLANES) * SUBLANES
        slot_rows, slot_w, counts = _route_plan(wd, tg, d // LANES)
        table, block_row = _block_table(counts[:, :, 0], tg, bm, nb_max)
        by_block = lambda a: jnp.take(a.reshape(-1, bm), block_row, axis=0).reshape(-1)
        routed = _moe_group(table, h2, by_block(slot_rows), by_block(slot_w),
                            ewg, ewu, ewd, tg, bm, nb_max)
        xp = _finalize(routed, xacc, mod_p[5], gf, seq_len, tm_row, final_norm=last)
        p_re.append(hpr.reshape(bp, groups, n_state))
        p_im.append(hpi.reshape(bp, groups, n_state))

        ua, yb, vn = _premix_step(xs, mod_s[0], mod_s[1], g1, w_in_b, ln_g, ln_b, w0, b0)
        ya, hsr, hsi = _s5_step(ua, state_ssm_re[l].reshape(bs, groups * n_state),
                                state_ssm_im[l].reshape(bs, groups * n_state),
                                bj, cj, a_re, a_im, d_vec, w_glu, b_glu)
        xacc, h2, wd = _postmix(xs, ya, yb, mod_s[2], mod_s[3], mod_s[4], mod_s[5], g2, w_out_b,
                                rw_hi, rw_lo, rb, swg, swu, swd, per_row=True, seq_len=1, tm=bs,
                                token_tiles=False)
        xs = _moe_dense(h2, wd, xacc, mod_s[5], gf, ewg, ewu, ewd, per_row=True, seq_len=1,
                        tm=bs, final_norm=last)
        s_re.append(hsr.reshape(bs, groups, n_state))
        s_im.append(hsi.reshape(bs, groups, n_state))
        s_v.append(vn.reshape(bs, 1, gw))

    y_prompt = xp.reshape(bp, seq_len, d)
    y_sample = xs.reshape(bs, 1, d)
    return (y_prompt, y_sample, jnp.stack(p_re), jnp.stack(p_im), jnp.stack(s_re),
            jnp.stack(s_im), jnp.stack(s_v))
```

```python
import functools

import jax
import jax.numpy as jnp
from jax import lax
from jax.experimental import pallas as pl
from jax.experimental.pallas import tpu as pltpu

F32 = jnp.float32
BF16 = jnp.bfloat16

EPS = 1e-6
SSM_GROUP = 16
SSM_STATE = 64
GM_HEADS = 4
CHUNK = 128
N_EXPERTS = 64
TOP_K = 8
N_ROUTE_GROUPS = 8
TOPK_ROUTE_GROUPS = 4
ROUTED_SCALE = 2.5

LANES = 128
SUBLANES = 8
VMEM_LIMIT = 48 * 1024 * 1024
ROW_TILE = 512
SCAN_TILE = 128
MOE_GROUP = 2048
MOE_BLOCK = 128
MOE_WEIGHT_BUFFERS = 4


def _params(n_axes):
    return pltpu.CompilerParams(dimension_semantics=("arbitrary",) * n_axes,
                                vmem_limit_bytes=VMEM_LIMIT)


def _dot(a, b):
    return jnp.dot(a.astype(BF16), b.astype(BF16), preferred_element_type=F32)


def _rmsnorm(x, g):
    ms = jnp.mean(x * x, axis=-1, keepdims=True)
    return x * lax.rsqrt(ms + EPS) * g


def _ada_kernel(c_ref, w_ref, b_ref, o_ref):
    c = c_ref[...]
    o_ref[...] = _dot(jax.nn.silu(c), w_ref[...]) + b_ref[...]


def _ada(c, w, b, tn=512):
    m, d = c.shape
    n = w.shape[1]
    return pl.pallas_call(
        _ada_kernel,
        grid=(n // tn,),
        in_specs=[pl.BlockSpec((m, d), lambda j: (0, 0)),
                  pl.BlockSpec((d, tn), lambda j: (0, j)),
                  pl.BlockSpec((1, tn), lambda j: (0, j))],
        out_specs=pl.BlockSpec((m, tn), lambda j: (0, j)),
        out_shape=jax.ShapeDtypeStruct((m, n), F32),
        compiler_params=_params(1),
        name="ada_mod",
    )(c, w, b)


def _premix_common(x_ref, sh_ref, sc_ref, g_ref, win_ref, lng_ref, lnb_ref, sw):
    x = x_ref[...]
    h = _rmsnorm(x, g_ref[...]) * (1.0 + sc_ref[...]) + sh_ref[...]
    proj = _dot(h, win_ref[...])
    u_a = proj[:, :sw]
    z = jax.nn.gelu(proj[:, sw:])
    gw = z.shape[1] // 2
    u_b = z[:, :gw]
    v_b = z[:, gw:]
    hd = gw // GM_HEADS
    parts = []
    for k in range(GM_HEADS):
        vh = v_b[:, k * hd:(k + 1) * hd]
        mu = jnp.mean(vh, axis=-1, keepdims=True)
        dv = vh - mu
        var = jnp.mean(dv * dv, axis=-1, keepdims=True)
        parts.append(dv * lax.rsqrt(var + EPS))
    vn = jnp.concatenate(parts, axis=-1) * lng_ref[...] + lnb_ref[...]
    return u_a, u_b, vn


def _premix_seq_kernel(x_ref, sh_ref, sc_ref, g_ref, win_ref, lng_ref, lnb_ref,
                       ws_ref, bs_ref, ua_ref, yb_ref, *, sw):
    u_a, u_b, vn = _premix_common(x_ref, sh_ref, sc_ref, g_ref, win_ref, lng_ref, lnb_ref, sw)
    ua_ref[...] = u_a
    tm, gw = u_b.shape
    hd = gw // GM_HEADS
    vnb = vn.astype(BF16)
    for c in range(tm // CHUNK):
        rows = slice(c * CHUNK, (c + 1) * CHUNK)
        for k in range(GM_HEADS):
            cols = slice(k * hd, (k + 1) * hd)
            s = jnp.dot(ws_ref[k], vnb[rows, cols], preferred_element_type=F32)
            yb_ref[rows, cols] = u_b[rows, cols] * (s + bs_ref[:, cols])


def _premix_step_kernel(x_ref, sh_ref, sc_ref, g_ref, win_ref, lng_ref, lnb_ref,
                        w0_ref, b0_ref, ua_ref, yb_ref, vn_ref, *, sw):
    u_a, u_b, vn = _premix_common(x_ref, sh_ref, sc_ref, g_ref, win_ref, lng_ref, lnb_ref, sw)
    ua_ref[...] = u_a
    vn_ref[...] = vn
    yb_ref[...] = u_b * (vn * w0_ref[...] + b0_ref[...])


def _row_spec(tm, d):
    return pl.BlockSpec((tm, d), lambda i: (i, 0))


def _full_spec(shape):
    nd = len(shape)
    return pl.BlockSpec(shape, lambda i: (0,) * nd)


def _mod_spec(per_row, tm, d, tiles_per_batch):
    if per_row:
        return pl.BlockSpec((tm, d), lambda i: (i, 0))
    return pl.BlockSpec((None, 1, d), lambda i: (i // tiles_per_batch, 0, 0))


def _premix_seq(x, sh, sc, g, w_in, ln_g, ln_b, ws_tril, bs_full, seq_len, tm):
    t, d = x.shape
    sw = ln_g.shape[1]
    tpb = seq_len // tm
    mod = _mod_spec(False, tm, d, tpb)
    return pl.pallas_call(
        functools.partial(_premix_seq_kernel, sw=sw),
        grid=(t // tm,),
        in_specs=[_row_spec(tm, d), mod, mod, _full_spec(g.shape), _full_spec(w_in.shape),
                  _full_spec(ln_g.shape), _full_spec(ln_b.shape), _full_spec(ws_tril.shape),
                  _full_spec(bs_full.shape)],
        out_specs=[_row_spec(tm, sw), _row_spec(tm, sw)],
        out_shape=[jax.ShapeDtypeStruct((t, sw), F32), jax.ShapeDtypeStruct((t, sw), F32)],
        compiler_params=_params(1),
        name="premix_seq",
    )(x, sh, sc, g, w_in, ln_g, ln_b, ws_tril, bs_full)


def _premix_step(x, sh, sc, g, w_in, ln_g, ln_b, w0, b0):
    t, d = x.shape
    sw = ln_g.shape[1]
    mod = _mod_spec(True, t, d, 1)
    return pl.pallas_call(
        functools.partial(_premix_step_kernel, sw=sw),
        grid=(1,),
        in_specs=[_row_spec(t, d), mod, mod, _full_spec(g.shape), _full_spec(w_in.shape),
                  _full_spec(ln_g.shape), _full_spec(ln_b.shape), _full_spec(w0.shape),
                  _full_spec(b0.shape)],
        out_specs=[_row_spec(t, sw)] * 3,
        out_shape=[jax.ShapeDtypeStruct((t, sw), F32)] * 3,
        compiler_params=_params(1),
        name="premix_step",
    )(x, sh, sc, g, w_in, ln_g, ln_b, w0, b0)


def _s5_output(xs, u, cj_ref, d_ref, wglu_ref, bglu_ref, n_chunks):
    ys = [jnp.dot(xs(j).astype(BF16), cj_ref[j], preferred_element_type=F32)
          for j in range(n_chunks)]
    y = jnp.concatenate(ys, axis=-1) + d_ref[...] * u
    y = jax.nn.gelu(y)
    gate = jax.nn.sigmoid(_dot(y, wglu_ref[...]) + bglu_ref[...])
    return y * gate


def _s5_seq_kernel(u_ref, bj_ref, cj_ref, are_ref, aim_ref, d_ref, wglu_ref, bglu_ref,
                   y_ref, sre_ref, sim_ref, bu_ref, st_ref, *, nb, tl, n_chunks, unroll):
    i = pl.program_id(0)
    n_tiles = bu_ref.shape[0]
    tpc = n_tiles // n_chunks
    hpc = tpc // 2
    hw = hpc * LANES
    kw = u_ref.shape[2] // n_chunks

    @pl.when(i == 0)
    def _():
        st_ref[...] = jnp.zeros_like(st_ref)

    u_all = u_ref[...].reshape(nb * tl, u_ref.shape[2])
    ub = u_all.astype(BF16)
    for j in range(n_chunks):
        bu = jnp.dot(ub[:, j * kw:(j + 1) * kw], bj_ref[j], preferred_element_type=F32)
        for b in range(nb):
            for c in range(tpc):
                bu_ref[j * tpc + c, pl.ds(b, tl, stride=nb), :] = \
                    bu[b * tl:(b + 1) * tl, c * LANES:(c + 1) * LANES]

    for j in range(n_chunks):
        re = slice(j * 2 * hw, j * 2 * hw + hw)
        im = slice(j * 2 * hw + hw, (j + 1) * 2 * hw)
        ar = jnp.broadcast_to(are_ref[:, j * hw:(j + 1) * hw], (nb, hw))
        ai = jnp.broadcast_to(aim_ref[:, j * hw:(j + 1) * hw], (nb, hw))
        re_tiles = [j * tpc + c for c in range(hpc)]
        im_tiles = [j * tpc + hpc + c for c in range(hpc)]

        def body(s, carry, ar=ar, ai=ai, re_tiles=re_tiles, im_tiles=im_tiles):
            xr, xi = carry
            for k in range(unroll):
                r0 = pl.multiple_of((s * unroll + k) * nb, nb)
                br = jnp.concatenate([bu_ref[c, pl.ds(r0, nb), :] for c in re_tiles], axis=-1)
                bi = jnp.concatenate([bu_ref[c, pl.ds(r0, nb), :] for c in im_tiles], axis=-1)
                nr = ar * xr - ai * xi + br
                ni = ar * xi + ai * xr + bi
                for q, c in enumerate(re_tiles):
                    bu_ref[c, pl.ds(r0, nb), :] = nr[:, q * LANES:(q + 1) * LANES]
                for q, c in enumerate(im_tiles):
                    bu_ref[c, pl.ds(r0, nb), :] = ni[:, q * LANES:(q + 1) * LANES]
                xr, xi = nr, ni
            return xr, xi

        xr, xi = lax.fori_loop(0, tl // unroll, body, (st_ref[:, re], st_ref[:, im]))
        st_ref[:, re] = xr
        st_ref[:, im] = xi
        sre_ref[:, j * hw:(j + 1) * hw] = xr
        sim_ref[:, j * hw:(j + 1) * hw] = xi

    def state_chunk(j):
        return jnp.concatenate(
            [jnp.concatenate([bu_ref[j * tpc + c, pl.ds(b, tl, stride=nb), :]
                              for c in range(tpc)], axis=-1) for b in range(nb)], axis=0)

    y = _s5_output(state_chunk, u_all, cj_ref, d_ref, wglu_ref, bglu_ref, n_chunks)
    y_ref[...] = y.reshape(y_ref.shape)


def _s5_seq(u, bj, cj, a_re, a_im, d_vec, w_glu, b_glu, tl, unroll=4):
    nb, seq_len, sw = u.shape
    n_chunks, _, cw = bj.shape
    ns = a_re.shape[1]
    kern = functools.partial(_s5_seq_kernel, nb=nb, tl=tl, n_chunks=n_chunks, unroll=unroll)
    return pl.pallas_call(
        kern,
        grid=(seq_len // tl,),
        in_specs=[pl.BlockSpec((nb, tl, sw), lambda i: (0, i, 0)),
                  _full_spec(bj.shape), _full_spec(cj.shape), _full_spec(a_re.shape),
                  _full_spec(a_im.shape), _full_spec(d_vec.shape), _full_spec(w_glu.shape),
                  _full_spec(b_glu.shape)],
        out_specs=[pl.BlockSpec((nb, tl, sw), lambda i: (0, i, 0)),
                   _full_spec((nb, ns)), _full_spec((nb, ns))],
        out_shape=[jax.ShapeDtypeStruct((nb, seq_len, sw), F32),
                   jax.ShapeDtypeStruct((nb, ns), F32), jax.ShapeDtypeStruct((nb, ns), F32)],
        scratch_shapes=[pltpu.VMEM((n_chunks * cw // LANES, tl * nb, LANES), F32),
                        pltpu.VMEM((nb, n_chunks * cw), F32)],
        compiler_params=_params(1),
        name="s5_seq",
    )(u, bj, cj, a_re, a_im, d_vec, w_glu, b_glu)


def _s5_step_kernel(u_ref, hre_ref, him_ref, bj_ref, cj_ref, are_ref, aim_ref, d_ref,
                    wglu_ref, bglu_ref, y_ref, sre_ref, sim_ref, *, n_chunks):
    u = u_ref[...]
    ub = u.astype(BF16)
    kw = u.shape[1] // n_chunks
    hw = bj_ref.shape[2] // 2
    chunks = []
    for j in range(n_chunks):
        st = slice(j * hw, (j + 1) * hw)
        bu = jnp.dot(ub[:, j * kw:(j + 1) * kw], bj_ref[j], preferred_element_type=F32)
        ar, ai = are_ref[:, st], aim_ref[:, st]
        hr, hi = hre_ref[:, st], him_ref[:, st]
        xr = ar * hr - ai * hi + bu[:, :hw]
        xi = ar * hi + ai * hr + bu[:, hw:]
        sre_ref[:, st] = xr
        sim_ref[:, st] = xi
        chunks.append(jnp.concatenate([xr, xi], axis=-1))
    y_ref[...] = _s5_output(chunks.__getitem__, u, cj_ref, d_ref, wglu_ref, bglu_ref, n_chunks)


def _s5_step(u, h_re, h_im, bj, cj, a_re, a_im, d_vec, w_glu, b_glu):
    t, sw = u.shape
    ns = a_re.shape[1]
    args = (u, h_re, h_im, bj, cj, a_re, a_im, d_vec, w_glu, b_glu)
    return pl.pallas_call(
        functools.partial(_s5_step_kernel, n_chunks=bj.shape[0]),
        grid=(1,),
        in_specs=[_full_spec(a.shape) for a in args],
        out_specs=[_full_spec((t, sw)), _full_spec((t, ns)), _full_spec((t, ns))],
        out_shape=[jax.ShapeDtypeStruct((t, sw), F32), jax.ShapeDtypeStruct((t, ns), F32),
                   jax.ShapeDtypeStruct((t, ns), F32)],
        compiler_params=_params(1),
        name="s5_step",
    )(*args)


def _group_reduce(x, lane, width, op):
    n = x.shape[-1]
    s = 1
    while s < width:
        up = pltpu.roll(x, s, 1)
        dn = pltpu.roll(x, n - s, 1)
        x = op(x, jnp.where((lane & s) != 0, up, dn))
        s *= 2
    return x


def _first_max(x, lane, big):
    m = jnp.max(x, axis=-1, keepdims=True)
    return jnp.min(jnp.where(x == m, lane, big), axis=-1, keepdims=True)


def _route(logits, bias):
    rows, n = logits.shape
    lane = lax.broadcasted_iota(jnp.int32, (rows, n), 1)
    lane_f = lane.astype(F32)
    neg = jnp.float32(-jnp.inf)
    big = jnp.float32(n)
    gsz = N_EXPERTS // N_ROUTE_GROUPS
    group_start = (lane - (lane & (gsz - 1))).astype(F32)
    scores = jax.nn.sigmoid(logits)
    biased = scores + bias
    m1 = _group_reduce(biased, lane, gsz, jnp.maximum)
    first = _group_reduce(jnp.where(biased == m1, lane_f, big), lane, gsz, jnp.minimum)
    m2 = _group_reduce(jnp.where(lane_f == first, neg, biased), lane, gsz, jnp.maximum)
    gs = m1 + m2
    gsel = jnp.zeros((rows, n), F32)
    work = gs
    for _ in range(TOPK_ROUTE_GROUPS):
        hit = group_start == _first_max(work, lane_f, big)
        gsel = jnp.where(hit, 1.0, gsel)
        work = jnp.where(hit, neg, work)
    work = jnp.where(gsel > 0.0, biased, neg)
    w = jnp.zeros((rows, n), F32)
    for _ in range(TOP_K):
        hit = lane_f == _first_max(work, lane_f, big)
        w = jnp.where(hit, scores, w)
        work = jnp.where(hit, neg, work)
    return w / jnp.sum(w, axis=-1, keepdims=True) * ROUTED_SCALE


def _route_by_expert(logits, bias):
    ne, tm = logits.shape
    ng = N_ROUTE_GROUPS
    gsz = ne // ng
    neg = jnp.float32(-jnp.inf)
    shape = (ng, gsz, tm)
    scores = jax.nn.sigmoid(logits)
    biased = (scores + bias).reshape(shape)
    scores = scores.reshape(shape)
    sub = lax.broadcasted_iota(jnp.int32, shape, 1)
    grp = lax.broadcasted_iota(jnp.int32, (ng, 1, tm), 0)
    eid = lax.broadcasted_iota(jnp.int32, shape, 0) * gsz + sub
    m1 = jnp.max(biased, axis=1, keepdims=True)
    first = jnp.min(jnp.where(biased == m1, sub, gsz), axis=1, keepdims=True)
    m2 = jnp.max(jnp.where(sub == first, neg, biased), axis=1, keepdims=True)
    work = m1 + m2
    gsel = jnp.zeros((ng, 1, tm), jnp.bool_)
    for _ in range(TOPK_ROUTE_GROUPS):
        m = jnp.max(work, axis=0, keepdims=True)
        hit = grp == jnp.min(jnp.where(work == m, grp, ng), axis=0, keepdims=True)
        gsel = jnp.logical_or(gsel, hit)
        work = jnp.where(hit, neg, work)
    work = jnp.where(gsel, biased, neg)
    w = jnp.zeros(shape, F32)
    for _ in range(TOP_K):
        m = jnp.max(jnp.max(work, axis=0, keepdims=True), axis=1, keepdims=True)
        first = jnp.min(jnp.min(jnp.where(work == m, eid, ne), axis=0, keepdims=True),
                        axis=1, keepdims=True)
        hit = eid == first
        w = jnp.where(hit, scores, w)
        work = jnp.where(hit, neg, work)
    denom = jnp.sum(jnp.sum(w, axis=0, keepdims=True), axis=1, keepdims=True)
    return (w / denom * ROUTED_SCALE).reshape(ne, tm)


def _split_bf16(x):
    hi = x.astype(BF16)
    lo = (x - hi.astype(F32)).astype(BF16)
    return hi, lo


def _postmix_kernel(x_ref, ya_ref, yb_ref, gt1_ref, sh2_ref, sc2_ref, gt2_ref, g2_ref,
                    wout_ref, rwh_ref, rwl_ref, rb_ref, swg_ref, swu_ref, swd_ref,
                    xacc_ref, h2_ref, wd_ref, *, token_tiles):
    y = jnp.concatenate([ya_ref[...], yb_ref[...]], axis=-1)
    x1 = x_ref[...] + gt1_ref[...] * _dot(y, wout_ref[...])
    h2 = _rmsnorm(x1, g2_ref[...]) * (1.0 + sc2_ref[...]) + sh2_ref[...]
    hb = h2.astype(BF16)
    if token_tiles:
        tm, d = h2.shape
        nj = d // LANES
        for j in range(nj):
            h2_ref[pl.ds(j, tm, stride=nj), :] = h2[:, j * LANES:(j + 1) * LANES]
    else:
        h2_ref[...] = hb
    h_hi, h_lo = _split_bf16(h2)
    if token_tiles:
        nt = (((1,), (1,)), ((), ()))
        logits = (lax.dot_general(rwh_ref[...], h_hi, nt, preferred_element_type=F32)
                  + lax.dot_general(rwl_ref[...], h_hi, nt, preferred_element_type=F32)
                  + lax.dot_general(rwh_ref[...], h_lo, nt, preferred_element_type=F32))
        wd_ref[...] = _route_by_expert(logits, rb_ref[...])
    else:
        logits = (jnp.dot(h_hi, rwh_ref[...], preferred_element_type=F32)
                  + jnp.dot(h_hi, rwl_ref[...], preferred_element_type=F32)
                  + jnp.dot(h_lo, rwh_ref[...], preferred_element_type=F32))
        wd_ref[...] = _route(logits, rb_ref[...])
    hid = jax.nn.silu(jnp.dot(hb, swg_ref[...], preferred_element_type=F32)) * \
        jnp.dot(hb, swu_ref[...], preferred_element_type=F32)
    shared = _dot(hid, swd_ref[...])
    xacc_ref[...] = x1 + gt2_ref[...] * shared


def _postmix(x, ya, yb, gt1, sh2, sc2, gt2, g2, w_out, rw_hi, rw_lo, rb, swg, swu, swd,
             per_row, seq_len, tm, token_tiles):
    t, d = x.shape
    sw = ya.shape[1]
    nj = d // LANES
    mod = _mod_spec(per_row, tm, d, max(seq_len // tm, 1))
    weights = (g2, w_out, rw_hi, rw_lo, rb, swg, swu, swd)
    if token_tiles:
        ne = rw_hi.shape[0]
        h2_spec, h2_shape = _row_spec(tm * nj, LANES), jax.ShapeDtypeStruct((t * nj, LANES), F32)
        wd_spec, wd_shape = pl.BlockSpec((ne, tm), lambda i: (0, i)), (ne, t)
    else:
        ne = rw_hi.shape[1]
        h2_spec, h2_shape = _row_spec(tm, d), jax.ShapeDtypeStruct((t, d), BF16)
        wd_spec, wd_shape = _row_spec(tm, ne), (t, ne)
    return pl.pallas_call(
        functools.partial(_postmix_kernel, token_tiles=token_tiles),
        grid=(t // tm,),
        in_specs=[_row_spec(tm, d), _row_spec(tm, sw), _row_spec(tm, sw), mod, mod, mod, mod]
        + [_full_spec(w.shape) for w in weights],
        out_specs=[_row_spec(tm, d), h2_spec, wd_spec],
        out_shape=[jax.ShapeDtypeStruct((t, d), F32), h2_shape,
                   jax.ShapeDtypeStruct(wd_shape, F32)],
        compiler_params=_params(1),
        name="postmix",
    )(x, ya, yb, gt1, sh2, sc2, gt2, *weights)


def _moe_dense_kernel(h_ref, wd_ref, xacc_ref, gt2_ref, gf_ref, wg_ref, wu_ref, wdn_ref,
                      o_ref, acc_ref, *, final_norm):
    e = pl.program_id(1)

    @pl.when(e == 0)
    def _():
        acc_ref[...] = jnp.zeros_like(acc_ref)

    hb = h_ref[...]
    wd = wd_ref[...]
    lane = lax.broadcasted_iota(jnp.int32, wd.shape, 1)
    col = jnp.sum(jnp.where(lane == e, wd, 0.0), axis=-1, keepdims=True)
    hid = jax.nn.silu(jnp.dot(hb, wg_ref[...], preferred_element_type=F32)) * \
        jnp.dot(hb, wu_ref[...], preferred_element_type=F32)
    acc_ref[...] += col * _dot(hid, wdn_ref[...])

    @pl.when(e == pl.num_programs(1) - 1)
    def _():
        xo = xacc_ref[...] + gt2_ref[...] * acc_ref[...]
        o_ref[...] = _rmsnorm(xo, gf_ref[...]) if final_norm else xo


def _moe_dense(h2, wd, xacc, gt2, g_final, wg, wu, wdn, per_row, seq_len, tm, final_norm):
    t, d = xacc.shape
    ne, _, eh = wg.shape
    nl = wd.shape[1]
    tpb = max(seq_len // tm, 1)
    if per_row:
        mod = pl.BlockSpec((tm, d), lambda i, e: (i, 0))
    else:
        mod = pl.BlockSpec((None, 1, d), lambda i, e: (i // tpb, 0, 0))
    row = lambda w: pl.BlockSpec((tm, w), lambda i, e: (i, 0))
    return pl.pallas_call(
        functools.partial(_moe_dense_kernel, final_norm=final_norm),
        grid=(t // tm, ne),
        in_specs=[row(d), row(nl), row(d), mod,
                  pl.BlockSpec(g_final.shape, lambda i, e: (0, 0)),
                  pl.BlockSpec((None, d, eh), lambda i, e: (e, 0, 0)),
                  pl.BlockSpec((None, d, eh), lambda i, e: (e, 0, 0)),
                  pl.BlockSpec((None, eh, d), lambda i, e: (e, 0, 0))],
        out_specs=row(d),
        out_shape=jax.ShapeDtypeStruct((t, d), F32),
        scratch_shapes=[pltpu.VMEM((tm, d), F32)],
        compiler_params=_params(2),
        name="moe_dense",
    )(h2, wd, xacc, gt2, g_final, wg, wu, wdn)


TOKEN_BITS = 12


def _route_plan_kernel(wd_ref, rows_ref, wl_ref, cnt_ref, *, tg, nj):
    wt = wd_ref[...]
    sel = wt > 0.0
    lane = lax.broadcasted_iota(jnp.int32, wt.shape, 1)
    c = sel.astype(jnp.int32)
    s = 1
    while s < tg:
        c = c + jnp.where(lane >= s, pltpu.roll(c, s, 1), 0)
        s *= 2
    cnt_ref[...] = jnp.broadcast_to(c[:, tg - 1:tg], cnt_ref.shape)
    dist = lane - (c - 1)
    key = jnp.where(sel, (dist << TOKEN_BITS) | lane, -1)
    w = jnp.where(sel, wt, 0.0)
    s = 1
    while s < tg:
        leaving = (key >= 0) & (((key >> TOKEN_BITS) & s) != 0)
        key_in = pltpu.roll(key, tg - s, 1)
        w_in = pltpu.roll(w, tg - s, 1)
        arriving = (key_in >= 0) & (((key_in >> TOKEN_BITS) & s) != 0)
        key = jnp.where(arriving, key_in, jnp.where(leaving, -1, key))
        w = jnp.where(arriving, w_in, jnp.where(leaving, 0.0, w))
        s *= 2
    rows_ref[...] = jnp.where(key >= 0, key & ((1 << TOKEN_BITS) - 1), 0) * nj
    wl_ref[...] = w


def _route_plan(wd, tg, nj):
    ne, t = wd.shape
    assert ne == N_EXPERTS and tg <= (1 << TOKEN_BITS) and t % tg == 0
    g = t // tg
    grp = lambda w: pl.BlockSpec((None, N_EXPERTS, w), lambda i: (i, 0, 0))
    return pl.pallas_call(
        functools.partial(_route_plan_kernel, tg=tg, nj=nj),
        grid=(g,),
        in_specs=[pl.BlockSpec((ne, tg), lambda i: (0, i))],
        out_specs=[grp(tg), grp(tg), grp(LANES)],
        out_shape=[jax.ShapeDtypeStruct((g, N_EXPERTS, tg), jnp.int32),
                   jax.ShapeDtypeStruct((g, N_EXPERTS, tg), F32),
                   jax.ShapeDtypeStruct((g, N_EXPERTS, LANES), jnp.int32)],
        compiler_params=_params(1),
        name="route_plan",
    )(wd)


def _block_table(counts, tg, bm, nb_max):
    g, ne = counts.shape
    i32 = jnp.int32
    nblk = (counts + bm - 1) // bm
    bend = jnp.cumsum(nblk, axis=1)
    bstart = bend - nblk
    nb = bend[:, -1]
    k = jnp.arange(nb_max, dtype=i32)[None, :]
    kc = jnp.minimum(k, jnp.maximum(nb[:, None] - 1, 0))
    e_k = jnp.minimum(jnp.sum(bend[:, None, :] <= kc[:, :, None], axis=2), ne - 1).astype(i32)
    b_k = kc - jnp.take_along_axis(bstart, e_k, axis=1)
    first = (b_k == 0) & (k < nb[:, None])
    active = nblk > 0
    parity = (jnp.take_along_axis(jnp.cumsum(active.astype(i32), axis=1) - 1, e_k, axis=1)
              % MOE_WEIGHT_BUFFERS)
    ids = jnp.where(active, jnp.arange(ne, dtype=i32)[None, :], ne)
    first_active_from = jnp.flip(lax.cummin(jnp.flip(ids, 1), axis=1), 1)
    nxt = jnp.concatenate([first_active_from[:, 1:], jnp.full((g, 1), ne, i32)], axis=1)
    nxt = jnp.where(nxt >= ne, -1, nxt)
    nxt2 = jnp.where(nxt < 0, -1, jnp.take_along_axis(nxt, jnp.maximum(nxt, 0), axis=1))
    next_k = jnp.take_along_axis(nxt, e_k, axis=1)
    next2_k = jnp.take_along_axis(nxt2, e_k, axis=1)
    row = (jnp.arange(g, dtype=i32)[:, None] * ne + e_k) * (tg // bm) + b_k
    flat = lambda a: a.reshape(-1).astype(i32)
    table = (nb.astype(i32), flat(e_k), flat(first), flat(parity), flat(next_k), flat(next2_k))
    return table, flat(row)


def _moe_group_kernel(nb_ref, exp_ref, first_ref, par_ref, next_ref, next2_ref,
                      h_ref, rows_ref, wl_ref, wg_hbm, wu_hbm, wdn_hbm, o_ref,
                      wg_buf, wu_buf, wdn_buf, wsem, xt_a, xt_b, hid_a, hid_b, yt_a, yt_b,
                      *, bm, nj, batch, nb_max):
    g = pl.program_id(0)
    nb = nb_ref[g]
    tb = g * nb_max
    pitch = bm + 1
    o_ref[...] = jnp.zeros_like(o_ref)

    def weight_copies(e, slot):
        return (pltpu.make_async_copy(wg_hbm.at[e], wg_buf.at[slot], wsem.at[slot]),
                pltpu.make_async_copy(wu_hbm.at[e], wu_buf.at[slot], wsem.at[slot]),
                pltpu.make_async_copy(wdn_hbm.at[e], wdn_buf.at[slot], wsem.at[slot]))

    n_wbuf = wg_buf.shape[0]

    def handoff(k):
        @pl.when(first_ref[tb + k] == 1)
        def _():
            slot = par_ref[tb + k]
            for c in weight_copies(exp_ref[tb + k], slot):
                c.wait()
            ahead = next2_ref[tb + k]

            @pl.when(ahead >= 0)
            def _():
                for c in weight_copies(ahead, (slot + 2) % n_wbuf):
                    c.start()

    def gather_rows(k, xt):
        rows = rows_ref.at[pl.ds(pl.multiple_of(k * bm, bm), bm)]

        def emit(lo, hi):
            for m in range(lo, hi):
                r0 = pl.multiple_of(rows[m], nj)
                xt[pl.ds(m, nj, stride=pitch), :] = h_ref[pl.ds(r0, nj), :]
        return emit

    def scatter_rows(k, yt):
        base = pl.multiple_of(k * bm, bm)
        rows = rows_ref.at[pl.ds(base, bm)]
        wts = wl_ref.at[pl.ds(base, bm)]

        def emit(lo, hi):
            for m0 in range(lo, hi, batch):
                updates = []
                for m in range(m0, min(m0 + batch, hi)):
                    r0 = pl.multiple_of(rows[m], nj)
                    updates.append((r0, o_ref[pl.ds(r0, nj), :]
                                    + wts[m] * yt[pl.ds(m, nj, stride=pitch), :]))
                for r0, v in reversed(updates):
                    o_ref[pl.ds(r0, nj), :] = v
        return emit

    n_chunks = nj // 2
    kc = 2 * LANES
    n_pieces = 2 * n_chunks + 1
    cuts = [bm * i // n_pieces for i in range(n_pieces + 1)]

    def step(up_blk, down_blk, side_work):
        piece = 0

        def side():
            nonlocal piece
            for emit in side_work:
                emit(cuts[piece], cuts[piece + 1])
            piece += 1

        gate = up = None
        for c in range(n_chunks):
            if up_blk is not None:
                k, xt, _ = up_blk
                slot = par_ref[tb + k]
                xc = jnp.concatenate([xt[j * pitch:j * pitch + bm, :] for j in (2 * c, 2 * c + 1)],
                                     axis=-1).astype(BF16)
                pg = jnp.dot(xc, wg_buf[slot, c * kc:(c + 1) * kc, :], preferred_element_type=F32)
                pu = jnp.dot(xc, wu_buf[slot, c * kc:(c + 1) * kc, :], preferred_element_type=F32)
                gate = pg if gate is None else gate + pg
                up = pu if up is None else up + pu
            side()
            if down_blk is not None:
                k, hid, yt = down_blk
                slot = par_ref[tb + k]
                yc = jnp.dot(hid[...], wdn_buf[slot, :, c * kc:(c + 1) * kc],
                             preferred_element_type=F32)
                for q in range(2):
                    j = 2 * c + q
                    yt[j * pitch:j * pitch + bm, :] = yc[:, q * LANES:(q + 1) * LANES]
            side()
        if up_blk is not None:
            up_blk[2][...] = (jax.nn.silu(gate) * up).astype(BF16)
        side()

    @pl.when(nb > 0)
    def _():
        for c in weight_copies(exp_ref[tb], 0):
            c.start()
        second = next_ref[tb]

        @pl.when(second >= 0)
        def _():
            for c in weight_copies(second, 1):
                c.start()
        gather_rows(0, xt_a)(0, bm)
        handoff(0)
        step((0, xt_a, hid_a), None, [gather_rows(1, xt_b)])
        handoff(1)
        step((1, xt_b, hid_b), (0, hid_a, yt_a), [gather_rows(2, xt_a)])

        def trip(i, carry):
            t = 2 * i + 1
            handoff(t + 1)
            step((t + 1, xt_a, hid_a), (t, hid_b, yt_b),
                 [gather_rows(t + 2, xt_b), scatter_rows(t - 1, yt_a)])
            handoff(t + 2)
            step((t + 2, xt_b, hid_b), (t + 1, hid_a, yt_a),
                 [gather_rows(t + 3, xt_a), scatter_rows(t, yt_b)])
            return carry

        lax.fori_loop(0, nb // 2, trip, 0)

        @pl.when(nb % 2 == 1)
        def _():
            scatter_rows(nb - 1, yt_a)(0, bm)


def _moe_group(table, h2t, rows, wl, wg, wu, wdn, tg, bm, nb_max, batch=4):
    ne, d, eh = wg.shape
    nj = d // LANES
    assert nj == SUBLANES
    g = h2t.shape[0] // (tg * nj)
    kern = functools.partial(_moe_group_kernel, bm=bm, nj=nj, batch=batch, nb_max=nb_max)
    slots = pl.BlockSpec((nb_max * bm,), lambda gi, *_: (gi,), memory_space=pltpu.SMEM)
    hbm = pl.BlockSpec(memory_space=pl.ANY)
    pitch_rows = nj * (bm + 1)
    grid_spec = pltpu.PrefetchScalarGridSpec(
        num_scalar_prefetch=len(table),
        grid=(g,),
        in_specs=[pl.BlockSpec((tg * nj, LANES), lambda gi, *_: (gi, 0)),
                  slots, slots, hbm, hbm, hbm],
        out_specs=pl.BlockSpec((tg * nj, LANES), lambda gi, *_: (gi, 0)),
        scratch_shapes=[pltpu.VMEM((MOE_WEIGHT_BUFFERS, d, eh), BF16),
                        pltpu.VMEM((MOE_WEIGHT_BUFFERS, d, eh), BF16),
                        pltpu.VMEM((MOE_WEIGHT_BUFFERS, eh, d), BF16),
                        pltpu.SemaphoreType.DMA((MOE_WEIGHT_BUFFERS,)),
                        pltpu.VMEM((pitch_rows, LANES), F32), pltpu.VMEM((pitch_rows, LANES), F32),
                        pltpu.VMEM((bm, eh), BF16), pltpu.VMEM((bm, eh), BF16),
                        pltpu.VMEM((pitch_rows, LANES), F32), pltpu.VMEM((pitch_rows, LANES), F32)],
    )
    return pl.pallas_call(
        kern,
        grid_spec=grid_spec,
        out_shape=jax.ShapeDtypeStruct((g * tg * nj, LANES), F32),
        compiler_params=_params(1),
        name="moe_group",
    )(*table, h2t, rows, wl, wg, wu, wdn)


def _finalize_kernel(r_ref, xacc_ref, gt2_ref, gf_ref, o_ref, *, final_norm):
    tm, d = xacc_ref.shape
    nj = d // LANES
    routed = jnp.concatenate([r_ref[pl.ds(j, tm, stride=nj), :] for j in range(nj)], axis=-1)
    xo = xacc_ref[...] + gt2_ref[...] * routed
    o_ref[...] = _rmsnorm(xo, gf_ref[...]) if final_norm else xo


def _finalize(routed, xacc, gt2, g_final, seq_len, tm, final_norm):
    t, d = xacc.shape
    nj = d // LANES
    return pl.pallas_call(
        functools.partial(_finalize_kernel, final_norm=final_norm),
        grid=(t // tm,),
        in_specs=[_row_spec(tm * nj, LANES),
                  _row_spec(tm, d), _mod_spec(False, tm, d, seq_len // tm),
                  _full_spec(g_final.shape)],
        out_specs=_row_spec(tm, d),
        out_shape=jax.ShapeDtypeStruct((t, d), F32),
        compiler_params=_params(1),
        name="finalize",
    )(routed, xacc, gt2, g_final)


def _s5_discretize(lam_re, lam_im, log_dt, b_re, b_im, c_re, c_im):
    g, n = lam_re.shape
    hg = b_re.shape[2]
    lr = jnp.minimum(lam_re.astype(F32), -1e-4)
    li = lam_im.astype(F32)
    dt = jnp.exp(log_dt.astype(F32))[:, None]
    mag = jnp.exp(lr * dt)
    abar_r, abar_i = mag * jnp.cos(li * dt), mag * jnp.sin(li * dt)
    den = lr * lr + li * li
    nr, ni = abar_r - 1.0, abar_i
    f_r = (nr * lr + ni * li) / den
    f_i = (ni * lr - nr * li) / den
    br_, bi_ = b_re.astype(F32), b_im.astype(F32)
    bbar_r = f_r[..., None] * br_ - f_i[..., None] * bi_
    bbar_i = f_r[..., None] * bi_ + f_i[..., None] * br_
    gpc = LANES // hg
    n_chunks = g // gpc
    eye = jnp.eye(gpc, dtype=F32)

    def b_chunks(bb):
        bb = bb.reshape(n_chunks, gpc, n, hg)
        m = jnp.einsum('jgnh,gk->jghkn', bb, eye)
        return m.reshape(n_chunks, gpc * hg, gpc * n)

    def c_chunks(cc):
        cc = cc.astype(F32).reshape(n_chunks, gpc, hg, n)
        m = jnp.einsum('jghn,gk->jgnkh', cc, eye)
        return m.reshape(n_chunks, gpc * n, gpc * hg)

    bj = jnp.concatenate([b_chunks(bbar_r), b_chunks(bbar_i)], axis=2).astype(BF16)
    cj = jnp.concatenate([c_chunks(c_re), -c_chunks(c_im)], axis=1).astype(BF16)
    return abar_r.reshape(1, g * n), abar_i.reshape(1, g * n), bj, cj


def kernel(x_prompt, x_sample, c_prompt, c_sample, state_ssm_re, state_ssm_im, g_norm1, g_norm2, w_ada, b_ada, w_in, w_out, ssm_lambda_re, ssm_lambda_im, ssm_log_dt, ssm_b_re, ssm_b_im, ssm_c_re, ssm_c_im, ssm_d, ssm_w_glu, ssm_b_glu, gm_ln_g, gm_ln_b, gm_w_s, gm_b_s, router_w, router_bias, moe_w_gate, moe_w_up, moe_w_down, shared_w_gate, shared_w_up, shared_w_down, g_final):
    bp, seq_len, d = x_prompt.shape
    bs, dec_len, _ = x_sample.shape
    depth = w_ada.shape[0]
    assert dec_len == 1 and seq_len % CHUNK == 0 and bp % SUBLANES == 0
    groups, n_state = ssm_lambda_re.shape[1:]
    sw = groups * SSM_GROUP
    gw = gm_ln_g.shape[1]
    hd = gw // GM_HEADS

    xp = x_prompt.reshape(bp * seq_len, d)
    xs = x_sample.reshape(bs, d)
    c_all = jnp.concatenate([c_prompt, c_sample], axis=0)
    gf = g_final.reshape(1, d)

    p_re, p_im, s_re, s_im, s_v = [], [], [], [], []
    for l in range(depth):
        mod = _ada(c_all, w_ada[l].astype(BF16), b_ada[l].reshape(1, -1))
        mod_p = [m.reshape(bp, 1, d) for m in jnp.split(mod[:bp], 6, axis=-1)]
        mod_s = jnp.split(mod[bp:], 6, axis=-1)

        g1 = g_norm1[l].reshape(1, d)
        g2 = g_norm2[l].reshape(1, d)
        w_in_b = w_in[l].astype(BF16)
        w_out_b = w_out[l].astype(BF16)
        ln_g = gm_ln_g[l].reshape(1, gw)
        ln_b = gm_ln_b[l].reshape(1, gw)
        ws_tril = (gm_w_s[l] * jnp.tril(jnp.ones((CHUNK, CHUNK), F32))[None]).astype(BF16)
        bs_full = jnp.repeat(gm_b_s[l].T, hd, axis=1)
        w0 = jnp.repeat(gm_w_s[l][:, 0, 0], hd).reshape(1, gw)
        b0 = jnp.repeat(gm_b_s[l][:, 0], hd).reshape(1, gw)
        a_re, a_im, bj, cj = _s5_discretize(ssm_lambda_re[l], ssm_lambda_im[l], ssm_log_dt[l],
                                            ssm_b_re[l], ssm_b_im[l], ssm_c_re[l], ssm_c_im[l])
        d_vec = ssm_d[l].reshape(1, sw)
        w_glu = ssm_w_glu[l].astype(BF16)
        b_glu = ssm_b_glu[l].reshape(1, sw)
        pad = LANES - N_EXPERTS
        rw = jnp.pad(router_w[l], ((0, 0), (0, pad)))
        rw_hi = rw.astype(BF16)
        rw_lo = (rw - rw_hi.astype(F32)).astype(BF16)
        rb = jnp.pad(router_bias[l], (0, pad), constant_values=-jnp.inf).reshape(1, LANES)
        rwt_hi, rwt_lo = rw_hi[:, :N_EXPERTS].T, rw_lo[:, :N_EXPERTS].T
        rb_col = router_bias[l].reshape(N_EXPERTS, 1)
        swg, swu, swd = (shared_w_gate[l].astype(BF16), shared_w_up[l].astype(BF16),
                         shared_w_down[l].astype(BF16))
        ewg, ewu, ewd = (moe_w_gate[l].astype(BF16), moe_w_up[l].astype(BF16),
                         moe_w_down[l].astype(BF16))

        tm_row = min(ROW_TILE, seq_len)
        ua, yb = _premix_seq(xp, mod_p[0], mod_p[1], g1, w_in_b, ln_g, ln_b, ws_tril, bs_full,
                             seq_len, tm=tm_row)
        ya, hpr, hpi = _s5_seq(ua.reshape(bp, seq_len, sw), bj, cj, a_re, a_im, d_vec,
                               w_glu, b_glu, tl=min(SCAN_TILE, seq_len))
        xacc, h2, wd = _postmix(xp, ya.reshape(bp * seq_len, sw), yb, mod_p[2], mod_p[3],
                                mod_p[4], mod_p[5], g2, w_out_b, rwt_hi, rwt_lo, rb_col,
                                swg, swu, swd,
                                per_row=False, seq_len=seq_len, tm=tm_row, token_tiles=True)
        last = l == depth - 1
        tg = min(MOE_GROUP, seq_len)
        bm = min(MOE_BLOCK, tg)
        nb_max = -(-(tg * TOP_K // bm + N_EXPERTS + 3) // SUBLANES) * SUBLANES
        slot_rows, slot_w, counts = _route_plan(wd, tg, d // LANES)
        table, block_row = _block_table(counts[:, :, 0], tg, bm, nb_max)
        by_block = lambda a: jnp.take(a.reshape(-1, bm), block_row, axis=0).reshape(-1)
        routed = _moe_group(table, h2, by_block(slot_rows), by_block(slot_w),
                            ewg, ewu, ewd, tg, bm, nb_max)
        xp = _finalize(routed, xacc, mod_p[5], gf, seq_len, tm_row, final_norm=last)
        p_re.append(hpr.reshape(bp, groups, n_state))
        p_im.append(hpi.reshape(bp, groups, n_state))

        ua, yb, vn = _premix_step(xs, mod_s[0], mod_s[1], g1, w_in_b, ln_g, ln_b, w0, b0)
        ya, hsr, hsi = _s5_step(ua, state_ssm_re[l].reshape(bs, groups * n_state),
                                state_ssm_im[l].reshape(bs, groups * n_state),
                                bj, cj, a_re, a_im, d_vec, w_glu, b_glu)
        xacc, h2, wd = _postmix(xs, ya, yb, mod_s[2], mod_s[3], mod_s[4], mod_s[5], g2, w_out_b,
                                rw_hi, rw_lo, rb, swg, swu, swd, per_row=True, seq_len=1, tm=bs,
                                token_tiles=False)
        xs = _moe_dense(h2, wd, xacc, mod_s[5], gf, ewg, ewu, ewd, per_row=True, seq_len=1,
                        tm=bs, final_norm=last)
        s_re.append(hsr.reshape(bs, groups, n_state))
        s_im.append(hsi.reshape(bs, groups, n_state))
        s_v.append(vn.reshape(bs, 1, gw))

    y_prompt = xp.reshape(bp, seq_len, d)
    y_sample = xs.reshape(bs, 1, d)
    return (y_prompt, y_sample, jnp.stack(p_re), jnp.stack(p_im), jnp.stack(s_re),
            jnp.stack(s_im), jnp.stack(s_v))
```

```python
import functools

import jax
import jax.numpy as jnp
from jax import lax
from jax.experimental import pallas as pl
from jax.experimental.pallas import tpu as pltpu

F32 = jnp.float32
BF16 = jnp.bfloat16

EPS = 1e-6
SSM_GROUP = 16
SSM_STATE = 64
GM_HEADS = 4
CHUNK = 128
N_EXPERTS = 64
TOP_K = 8
N_ROUTE_GROUPS = 8
TOPK_ROUTE_GROUPS = 4
ROUTED_SCALE = 2.5

LANES = 128
SUBLANES = 8
VMEM_LIMIT = 48 * 1024 * 1024
ROW_TILE = 512
SCAN_TILE = 128
MOE_GROUP = 2048
MOE_BLOCK = 128
MOE_EARLY_ROW_PERCENT = 56
MOE_WEIGHT_BUFFERS = 4


def _params(n_axes):
    return pltpu.CompilerParams(dimension_semantics=("arbitrary",) * n_axes,
                                vmem_limit_bytes=VMEM_LIMIT)


def _dot(a, b):
    return jnp.dot(a.astype(BF16), b.astype(BF16), preferred_element_type=F32)


def _rmsnorm(x, g):
    ms = jnp.mean(x * x, axis=-1, keepdims=True)
    return x * lax.rsqrt(ms + EPS) * g


def _ada_kernel(c_ref, w_ref, b_ref, o_ref):
    c = c_ref[...]
    o_ref[...] = _dot(jax.nn.silu(c), w_ref[...]) + b_ref[...]


def _ada(c, w, b, tn=512):
    m, d = c.shape
    n = w.shape[1]
    return pl.pallas_call(
        _ada_kernel,
        grid=(n // tn,),
        in_specs=[pl.BlockSpec((m, d), lambda j: (0, 0)),
                  pl.BlockSpec((d, tn), lambda j: (0, j)),
                  pl.BlockSpec((1, tn), lambda j: (0, j))],
        out_specs=pl.BlockSpec((m, tn), lambda j: (0, j)),
        out_shape=jax.ShapeDtypeStruct((m, n), F32),
        compiler_params=_params(1),
        name="ada_mod",
    )(c, w, b)


def _premix_common(x_ref, sh_ref, sc_ref, g_ref, win_ref, lng_ref, lnb_ref, sw):
    x = x_ref[...]
    h = _rmsnorm(x, g_ref[...]) * (1.0 + sc_ref[...]) + sh_ref[...]
    proj = _dot(h, win_ref[...])
    u_a = proj[:, :sw]
    z = jax.nn.gelu(proj[:, sw:])
    gw = z.shape[1] // 2
    u_b = z[:, :gw]
    v_b = z[:, gw:]
    hd = gw // GM_HEADS
    parts = []
    for k in range(GM_HEADS):
        vh = v_b[:, k * hd:(k + 1) * hd]
        mu = jnp.mean(vh, axis=-1, keepdims=True)
        dv = vh - mu
        var = jnp.mean(dv * dv, axis=-1, keepdims=True)
        parts.append(dv * lax.rsqrt(var + EPS))
    vn = jnp.concatenate(parts, axis=-1) * lng_ref[...] + lnb_ref[...]
    return u_a, u_b, vn


def _premix_seq_kernel(x_ref, sh_ref, sc_ref, g_ref, win_ref, lng_ref, lnb_ref,
                       ws_ref, bs_ref, ua_ref, yb_ref, *, sw):
    u_a, u_b, vn = _premix_common(x_ref, sh_ref, sc_ref, g_ref, win_ref, lng_ref, lnb_ref, sw)
    ua_ref[...] = u_a
    tm, gw = u_b.shape
    hd = gw // GM_HEADS
    vnb = vn.astype(BF16)
    for c in range(tm // CHUNK):
        rows = slice(c * CHUNK, (c + 1) * CHUNK)
        for k in range(GM_HEADS):
            cols = slice(k * hd, (k + 1) * hd)
            s = jnp.dot(ws_ref[k], vnb[rows, cols], preferred_element_type=F32)
            yb_ref[rows, cols] = u_b[rows, cols] * (s + bs_ref[:, cols])


def _premix_step_kernel(x_ref, sh_ref, sc_ref, g_ref, win_ref, lng_ref, lnb_ref,
                        w0_ref, b0_ref, ua_ref, yb_ref, vn_ref, *, sw):
    u_a, u_b, vn = _premix_common(x_ref, sh_ref, sc_ref, g_ref, win_ref, lng_ref, lnb_ref, sw)
    ua_ref[...] = u_a
    vn_ref[...] = vn
    yb_ref[...] = u_b * (vn * w0_ref[...] + b0_ref[...])


def _row_spec(tm, d):
    return pl.BlockSpec((tm, d), lambda i: (i, 0))


def _full_spec(shape):
    nd = len(shape)
    return pl.BlockSpec(shape, lambda i: (0,) * nd)


def _mod_spec(per_row, tm, d, tiles_per_batch):
    if per_row:
        return pl.BlockSpec((tm, d), lambda i: (i, 0))
    return pl.BlockSpec((None, 1, d), lambda i: (i // tiles_per_batch, 0, 0))


def _premix_seq(x, sh, sc, g, w_in, ln_g, ln_b, ws_tril, bs_full, seq_len, tm):
    t, d = x.shape
    sw = ln_g.shape[1]
    tpb = seq_len // tm
    mod = _mod_spec(False, tm, d, tpb)
    return pl.pallas_call(
        functools.partial(_premix_seq_kernel, sw=sw),
        grid=(t // tm,),
        in_specs=[_row_spec(tm, d), mod, mod, _full_spec(g.shape), _full_spec(w_in.shape),
                  _full_spec(ln_g.shape), _full_spec(ln_b.shape), _full_spec(ws_tril.shape),
                  _full_spec(bs_full.shape)],
        out_specs=[_row_spec(tm, sw), _row_spec(tm, sw)],
        out_shape=[jax.ShapeDtypeStruct((t, sw), F32), jax.ShapeDtypeStruct((t, sw), F32)],
        compiler_params=_params(1),
        name="premix_seq",
    )(x, sh, sc, g, w_in, ln_g, ln_b, ws_tril, bs_full)


def _premix_step(x, sh, sc, g, w_in, ln_g, ln_b, w0, b0):
    t, d = x.shape
    sw = ln_g.shape[1]
    mod = _mod_spec(True, t, d, 1)
    return pl.pallas_call(
        functools.partial(_premix_step_kernel, sw=sw),
        grid=(1,),
        in_specs=[_row_spec(t, d), mod, mod, _full_spec(g.shape), _full_spec(w_in.shape),
                  _full_spec(ln_g.shape), _full_spec(ln_b.shape), _full_spec(w0.shape),
                  _full_spec(b0.shape)],
        out_specs=[_row_spec(t, sw)] * 3,
        out_shape=[jax.ShapeDtypeStruct((t, sw), F32)] * 3,
        compiler_params=_params(1),
        name="premix_step",
    )(x, sh, sc, g, w_in, ln_g, ln_b, w0, b0)


def _s5_output(xs, u, cj_ref, d_ref, wglu_ref, bglu_ref, n_chunks):
    ys = [jnp.dot(xs(j).astype(BF16), cj_ref[j], preferred_element_type=F32)
          for j in range(n_chunks)]
    y = jnp.concatenate(ys, axis=-1) + d_ref[...] * u
    y = jax.nn.gelu(y)
    gate = jax.nn.sigmoid(_dot(y, wglu_ref[...]) + bglu_ref[...])
    return y * gate


def _s5_seq_kernel(u_ref, bj_ref, cj_ref, are_ref, aim_ref, d_ref, wglu_ref, bglu_ref,
                   y_ref, sre_ref, sim_ref, bu_ref, st_ref, *, nb, tl, n_chunks, unroll):
    i = pl.program_id(0)
    n_tiles = bu_ref.shape[0]
    tpc = n_tiles // n_chunks
    hpc = tpc // 2
    hw = hpc * LANES
    kw = u_ref.shape[2] // n_chunks

    @pl.when(i == 0)
    def _():
        st_ref[...] = jnp.zeros_like(st_ref)

    u_all = u_ref[...].reshape(nb * tl, u_ref.shape[2])
    ub = u_all.astype(BF16)
    for j in range(n_chunks):
        bu = jnp.dot(ub[:, j * kw:(j + 1) * kw], bj_ref[j], preferred_element_type=F32)
        for b in range(nb):
            for c in range(tpc):
                bu_ref[j * tpc + c, pl.ds(b, tl, stride=nb), :] = \
                    bu[b * tl:(b + 1) * tl, c * LANES:(c + 1) * LANES]

    for j in range(n_chunks):
        re = slice(j * 2 * hw, j * 2 * hw + hw)
        im = slice(j * 2 * hw + hw, (j + 1) * 2 * hw)
        ar = jnp.broadcast_to(are_ref[:, j * hw:(j + 1) * hw], (nb, hw))
        ai = jnp.broadcast_to(aim_ref[:, j * hw:(j + 1) * hw], (nb, hw))
        re_tiles = [j * tpc + c for c in range(hpc)]
        im_tiles = [j * tpc + hpc + c for c in range(hpc)]

        def body(s, carry, ar=ar, ai=ai, re_tiles=re_tiles, im_tiles=im_tiles):
            xr, xi = carry
            for k in range(unroll):
                r0 = pl.multiple_of((s * unroll + k) * nb, nb)
                br = jnp.concatenate([bu_ref[c, pl.ds(r0, nb), :] for c in re_tiles], axis=-1)
                bi = jnp.concatenate([bu_ref[c, pl.ds(r0, nb), :] for c in im_tiles], axis=-1)
                nr = ar * xr - ai * xi + br
                ni = ar * xi + ai * xr + bi
                for q, c in enumerate(re_tiles):
                    bu_ref[c, pl.ds(r0, nb), :] = nr[:, q * LANES:(q + 1) * LANES]
                for q, c in enumerate(im_tiles):
                    bu_ref[c, pl.ds(r0, nb), :] = ni[:, q * LANES:(q + 1) * LANES]
                xr, xi = nr, ni
            return xr, xi

        xr, xi = lax.fori_loop(0, tl // unroll, body, (st_ref[:, re], st_ref[:, im]))
        st_ref[:, re] = xr
        st_ref[:, im] = xi
        sre_ref[:, j * hw:(j + 1) * hw] = xr
        sim_ref[:, j * hw:(j + 1) * hw] = xi

    def state_chunk(j):
        return jnp.concatenate(
            [jnp.concatenate([bu_ref[j * tpc + c, pl.ds(b, tl, stride=nb), :]
                              for c in range(tpc)], axis=-1) for b in range(nb)], axis=0)

    y = _s5_output(state_chunk, u_all, cj_ref, d_ref, wglu_ref, bglu_ref, n_chunks)
    y_ref[...] = y.reshape(y_ref.shape)


def _s5_seq(u, bj, cj, a_re, a_im, d_vec, w_glu, b_glu, tl, unroll=4):
    nb, seq_len, sw = u.shape
    n_chunks, _, cw = bj.shape
    ns = a_re.shape[1]
    kern = functools.partial(_s5_seq_kernel, nb=nb, tl=tl, n_chunks=n_chunks, unroll=unroll)
    return pl.pallas_call(
        kern,
        grid=(seq_len // tl,),
        in_specs=[pl.BlockSpec((nb, tl, sw), lambda i: (0, i, 0)),
                  _full_spec(bj.shape), _full_spec(cj.shape), _full_spec(a_re.shape),
                  _full_spec(a_im.shape), _full_spec(d_vec.shape), _full_spec(w_glu.shape),
                  _full_spec(b_glu.shape)],
        out_specs=[pl.BlockSpec((nb, tl, sw), lambda i: (0, i, 0)),
                   _full_spec((nb, ns)), _full_spec((nb, ns))],
        out_shape=[jax.ShapeDtypeStruct((nb, seq_len, sw), F32),
                   jax.ShapeDtypeStruct((nb, ns), F32), jax.ShapeDtypeStruct((nb, ns), F32)],
        scratch_shapes=[pltpu.VMEM((n_chunks * cw // LANES, tl * nb, LANES), F32),
                        pltpu.VMEM((nb, n_chunks * cw), F32)],
        compiler_params=_params(1),
        name="s5_seq",
    )(u, bj, cj, a_re, a_im, d_vec, w_glu, b_glu)


def _s5_step_kernel(u_ref, hre_ref, him_ref, bj_ref, cj_ref, are_ref, aim_ref, d_ref,
                    wglu_ref, bglu_ref, y_ref, sre_ref, sim_ref, *, n_chunks):
    u = u_ref[...]
    ub = u.astype(BF16)
    kw = u.shape[1] // n_chunks
    hw = bj_ref.shape[2] // 2
    chunks = []
    for j in range(n_chunks):
        st = slice(j * hw, (j + 1) * hw)
        bu = jnp.dot(ub[:, j * kw:(j + 1) * kw], bj_ref[j], preferred_element_type=F32)
        ar, ai = are_ref[:, st], aim_ref[:, st]
        hr, hi = hre_ref[:, st], him_ref[:, st]
        xr = ar * hr - ai * hi + bu[:, :hw]
        xi = ar * hi + ai * hr + bu[:, hw:]
        sre_ref[:, st] = xr
        sim_ref[:, st] = xi
        chunks.append(jnp.concatenate([xr, xi], axis=-1))
    y_ref[...] = _s5_output(chunks.__getitem__, u, cj_ref, d_ref, wglu_ref, bglu_ref, n_chunks)


def _s5_step(u, h_re, h_im, bj, cj, a_re, a_im, d_vec, w_glu, b_glu):
    t, sw = u.shape
    ns = a_re.shape[1]
    args = (u, h_re, h_im, bj, cj, a_re, a_im, d_vec, w_glu, b_glu)
    return pl.pallas_call(
        functools.partial(_s5_step_kernel, n_chunks=bj.shape[0]),
        grid=(1,),
        in_specs=[_full_spec(a.shape) for a in args],
        out_specs=[_full_spec((t, sw)), _full_spec((t, ns)), _full_spec((t, ns))],
        out_shape=[jax.ShapeDtypeStruct((t, sw), F32), jax.ShapeDtypeStruct((t, ns), F32),
                   jax.ShapeDtypeStruct((t, ns), F32)],
        compiler_params=_params(1),
        name="s5_step",
    )(*args)


def _group_reduce(x, lane, width, op):
    n = x.shape[-1]
    s = 1
    while s < width:
        up = pltpu.roll(x, s, 1)
        dn = pltpu.roll(x, n - s, 1)
        x = op(x, jnp.where((lane & s) != 0, up, dn))
        s *= 2
    return x


def _first_max(x, lane, big):
    m = jnp.max(x, axis=-1, keepdims=True)
    return jnp.min(jnp.where(x == m, lane, big), axis=-1, keepdims=True)


def _route(logits, bias):
    rows, n = logits.shape
    lane = lax.broadcasted_iota(jnp.int32, (rows, n), 1)
    lane_f = lane.astype(F32)
    neg = jnp.float32(-jnp.inf)
    big = jnp.float32(n)
    gsz = N_EXPERTS // N_ROUTE_GROUPS
    group_start = (lane - (lane & (gsz - 1))).astype(F32)
    scores = jax.nn.sigmoid(logits)
    biased = scores + bias
    m1 = _group_reduce(biased, lane, gsz, jnp.maximum)
    first = _group_reduce(jnp.where(biased == m1, lane_f, big), lane, gsz, jnp.minimum)
    m2 = _group_reduce(jnp.where(lane_f == first, neg, biased), lane, gsz, jnp.maximum)
    gs = m1 + m2
    gsel = jnp.zeros((rows, n), F32)
    work = gs
    for _ in range(TOPK_ROUTE_GROUPS):
        hit = group_start == _first_max(work, lane_f, big)
        gsel = jnp.where(hit, 1.0, gsel)
        work = jnp.where(hit, neg, work)
    work = jnp.where(gsel > 0.0, biased, neg)
    w = jnp.zeros((rows, n), F32)
    for _ in range(TOP_K):
        hit = lane_f == _first_max(work, lane_f, big)
        w = jnp.where(hit, scores, w)
        work = jnp.where(hit, neg, work)
    return w / jnp.sum(w, axis=-1, keepdims=True) * ROUTED_SCALE


def _route_by_expert(logits, bias):
    ne, tm = logits.shape
    ng = N_ROUTE_GROUPS
    gsz = ne // ng
    neg = jnp.float32(-jnp.inf)
    shape = (ng, gsz, tm)
    scores = jax.nn.sigmoid(logits)
    biased = (scores + bias).reshape(shape)
    scores = scores.reshape(shape)
    sub = lax.broadcasted_iota(jnp.int32, shape, 1)
    grp = lax.broadcasted_iota(jnp.int32, (ng, 1, tm), 0)
    eid = lax.broadcasted_iota(jnp.int32, shape, 0) * gsz + sub
    m1 = jnp.max(biased, axis=1, keepdims=True)
    first = jnp.min(jnp.where(biased == m1, sub, gsz), axis=1, keepdims=True)
    m2 = jnp.max(jnp.where(sub == first, neg, biased), axis=1, keepdims=True)
    work = m1 + m2
    gsel = jnp.zeros((ng, 1, tm), jnp.bool_)
    for _ in range(TOPK_ROUTE_GROUPS):
        m = jnp.max(work, axis=0, keepdims=True)
        hit = grp == jnp.min(jnp.where(work == m, grp, ng), axis=0, keepdims=True)
        gsel = jnp.logical_or(gsel, hit)
        work = jnp.where(hit, neg, work)
    work = jnp.where(gsel, biased, neg)
    w = jnp.zeros(shape, F32)
    for _ in range(TOP_K):
        m = jnp.max(jnp.max(work, axis=0, keepdims=True), axis=1, keepdims=True)
        first = jnp.min(jnp.min(jnp.where(work == m, eid, ne), axis=0, keepdims=True),
                        axis=1, keepdims=True)
        hit = eid == first
        w = jnp.where(hit, scores, w)
        work = jnp.where(hit, neg, work)
    denom = jnp.sum(jnp.sum(w, axis=0, keepdims=True), axis=1, keepdims=True)
    return (w / denom * ROUTED_SCALE).reshape(ne, tm)


def _split_bf16(x):
    hi = x.astype(BF16)
    lo = (x - hi.astype(F32)).astype(BF16)
    return hi, lo


def _postmix_kernel(x_ref, ya_ref, yb_ref, gt1_ref, sh2_ref, sc2_ref, gt2_ref, g2_ref,
                    wout_ref, rwh_ref, rwl_ref, rb_ref, swg_ref, swu_ref, swd_ref,
                    xacc_ref, h2_ref, wd_ref, *, token_tiles):
    y = jnp.concatenate([ya_ref[...], yb_ref[...]], axis=-1)
    x1 = x_ref[...] + gt1_ref[...] * _dot(y, wout_ref[...])
    h2 = _rmsnorm(x1, g2_ref[...]) * (1.0 + sc2_ref[...]) + sh2_ref[...]
    hb = h2.astype(BF16)
    if token_tiles:
        tm, d = h2.shape
        nj = d // LANES
        for j in range(nj):
            h2_ref[pl.ds(j, tm, stride=nj), :] = h2[:, j * LANES:(j + 1) * LANES]
    else:
        h2_ref[...] = hb
    h_hi, h_lo = _split_bf16(h2)
    if token_tiles:
        nt = (((1,), (1,)), ((), ()))
        logits = (lax.dot_general(rwh_ref[...], h_hi, nt, preferred_element_type=F32)
                  + lax.dot_general(rwl_ref[...], h_hi, nt, preferred_element_type=F32)
                  + lax.dot_general(rwh_ref[...], h_lo, nt, preferred_element_type=F32))
        wd_ref[...] = _route_by_expert(logits, rb_ref[...])
    else:
        logits = (jnp.dot(h_hi, rwh_ref[...], preferred_element_type=F32)
                  + jnp.dot(h_hi, rwl_ref[...], preferred_element_type=F32)
                  + jnp.dot(h_lo, rwh_ref[...], preferred_element_type=F32))
        wd_ref[...] = _route(logits, rb_ref[...])
    hid = jax.nn.silu(jnp.dot(hb, swg_ref[...], preferred_element_type=F32)) * \
        jnp.dot(hb, swu_ref[...], preferred_element_type=F32)
    shared = _dot(hid, swd_ref[...])
    xacc_ref[...] = x1 + gt2_ref[...] * shared


def _postmix(x, ya, yb, gt1, sh2, sc2, gt2, g2, w_out, rw_hi, rw_lo, rb, swg, swu, swd,
             per_row, seq_len, tm, token_tiles):
    t, d = x.shape
    sw = ya.shape[1]
    nj = d // LANES
    mod = _mod_spec(per_row, tm, d, max(seq_len // tm, 1))
    weights = (g2, w_out, rw_hi, rw_lo, rb, swg, swu, swd)
    if token_tiles:
        ne = rw_hi.shape[0]
        h2_spec, h2_shape = _row_spec(tm * nj, LANES), jax.ShapeDtypeStruct((t * nj, LANES), F32)
        wd_spec, wd_shape = pl.BlockSpec((ne, tm), lambda i: (0, i)), (ne, t)
    else:
        ne = rw_hi.shape[1]
        h2_spec, h2_shape = _row_spec(tm, d), jax.ShapeDtypeStruct((t, d), BF16)
        wd_spec, wd_shape = _row_spec(tm, ne), (t, ne)
    return pl.pallas_call(
        functools.partial(_postmix_kernel, token_tiles=token_tiles),
        grid=(t // tm,),
        in_specs=[_row_spec(tm, d), _row_spec(tm, sw), _row_spec(tm, sw), mod, mod, mod, mod]
        + [_full_spec(w.shape) for w in weights],
        out_specs=[_row_spec(tm, d), h2_spec, wd_spec],
        out_shape=[jax.ShapeDtypeStruct((t, d), F32), h2_shape,
                   jax.ShapeDtypeStruct(wd_shape, F32)],
        compiler_params=_params(1),
        name="postmix",
    )(x, ya, yb, gt1, sh2, sc2, gt2, *weights)


def _moe_dense_kernel(h_ref, wd_ref, xacc_ref, gt2_ref, gf_ref, wg_ref, wu_ref, wdn_ref,
                      o_ref, acc_ref, *, final_norm):
    e = pl.program_id(1)

    @pl.when(e == 0)
    def _():
        acc_ref[...] = jnp.zeros_like(acc_ref)

    hb = h_ref[...]
    wd = wd_ref[...]
    lane = lax.broadcasted_iota(jnp.int32, wd.shape, 1)
    col = jnp.sum(jnp.where(lane == e, wd, 0.0), axis=-1, keepdims=True)
    hid = jax.nn.silu(jnp.dot(hb, wg_ref[...], preferred_element_type=F32)) * \
        jnp.dot(hb, wu_ref[...], preferred_element_type=F32)
    acc_ref[...] += col * _dot(hid, wdn_ref[...])

    @pl.when(e == pl.num_programs(1) - 1)
    def _():
        xo = xacc_ref[...] + gt2_ref[...] * acc_ref[...]
        o_ref[...] = _rmsnorm(xo, gf_ref[...]) if final_norm else xo


def _moe_dense(h2, wd, xacc, gt2, g_final, wg, wu, wdn, per_row, seq_len, tm, final_norm):
    t, d = xacc.shape
    ne, _, eh = wg.shape
    nl = wd.shape[1]
    tpb = max(seq_len // tm, 1)
    if per_row:
        mod = pl.BlockSpec((tm, d), lambda i, e: (i, 0))
    else:
        mod = pl.BlockSpec((None, 1, d), lambda i, e: (i // tpb, 0, 0))
    row = lambda w: pl.BlockSpec((tm, w), lambda i, e: (i, 0))
    return pl.pallas_call(
        functools.partial(_moe_dense_kernel, final_norm=final_norm),
        grid=(t // tm, ne),
        in_specs=[row(d), row(nl), row(d), mod,
                  pl.BlockSpec(g_final.shape, lambda i, e: (0, 0)),
                  pl.BlockSpec((None, d, eh), lambda i, e: (e, 0, 0)),
                  pl.BlockSpec((None, d, eh), lambda i, e: (e, 0, 0)),
                  pl.BlockSpec((None, eh, d), lambda i, e: (e, 0, 0))],
        out_specs=row(d),
        out_shape=jax.ShapeDtypeStruct((t, d), F32),
        scratch_shapes=[pltpu.VMEM((tm, d), F32)],
        compiler_params=_params(2),
        name="moe_dense",
    )(h2, wd, xacc, gt2, g_final, wg, wu, wdn)


TOKEN_BITS = 12


def _route_plan_kernel(wd_ref, rows_ref, wl_ref, cnt_ref, *, tg, nj):
    wt = wd_ref[...]
    sel = wt > 0.0
    lane = lax.broadcasted_iota(jnp.int32, wt.shape, 1)
    c = sel.astype(jnp.int32)
    s = 1
    while s < tg:
        c = c + jnp.where(lane >= s, pltpu.roll(c, s, 1), 0)
        s *= 2
    cnt_ref[...] = jnp.broadcast_to(c[:, tg - 1:tg], cnt_ref.shape)
    dist = lane - (c - 1)
    key = jnp.where(sel, (dist << TOKEN_BITS) | lane, -1)
    w = jnp.where(sel, wt, 0.0)
    s = 1
    while s < tg:
        leaving = (key >= 0) & (((key >> TOKEN_BITS) & s) != 0)
        key_in = pltpu.roll(key, tg - s, 1)
        w_in = pltpu.roll(w, tg - s, 1)
        arriving = (key_in >= 0) & (((key_in >> TOKEN_BITS) & s) != 0)
        key = jnp.where(arriving, key_in, jnp.where(leaving, -1, key))
        w = jnp.where(arriving, w_in, jnp.where(leaving, 0.0, w))
        s *= 2
    rows_ref[...] = jnp.where(key >= 0, key & ((1 << TOKEN_BITS) - 1), 0) * nj
    wl_ref[...] = w


def _route_plan(wd, tg, nj):
    ne, t = wd.shape
    assert ne == N_EXPERTS and tg <= (1 << TOKEN_BITS) and t % tg == 0
    g = t // tg
    grp = lambda w: pl.BlockSpec((None, N_EXPERTS, w), lambda i: (i, 0, 0))
    return pl.pallas_call(
        functools.partial(_route_plan_kernel, tg=tg, nj=nj),
        grid=(g,),
        in_specs=[pl.BlockSpec((ne, tg), lambda i: (0, i))],
        out_specs=[grp(tg), grp(tg), grp(LANES)],
        out_shape=[jax.ShapeDtypeStruct((g, N_EXPERTS, tg), jnp.int32),
                   jax.ShapeDtypeStruct((g, N_EXPERTS, tg), F32),
                   jax.ShapeDtypeStruct((g, N_EXPERTS, LANES), jnp.int32)],
        compiler_params=_params(1),
        name="route_plan",
    )(wd)


def _block_table(counts, tg, bm, nb_max):
    g, ne = counts.shape
    i32 = jnp.int32
    nblk = (counts + bm - 1) // bm
    bend = jnp.cumsum(nblk, axis=1)
    bstart = bend - nblk
    nb = bend[:, -1]
    k = jnp.arange(nb_max, dtype=i32)[None, :]
    kc = jnp.minimum(k, jnp.maximum(nb[:, None] - 1, 0))
    e_k = jnp.minimum(jnp.sum(bend[:, None, :] <= kc[:, :, None], axis=2), ne - 1).astype(i32)
    b_k = kc - jnp.take_along_axis(bstart, e_k, axis=1)
    first = (b_k == 0) & (k < nb[:, None])
    active = nblk > 0
    parity = (jnp.take_along_axis(jnp.cumsum(active.astype(i32), axis=1) - 1, e_k, axis=1)
              % MOE_WEIGHT_BUFFERS)
    ids = jnp.where(active, jnp.arange(ne, dtype=i32)[None, :], ne)
    first_active_from = jnp.flip(lax.cummin(jnp.flip(ids, 1), axis=1), 1)
    nxt = jnp.concatenate([first_active_from[:, 1:], jnp.full((g, 1), ne, i32)], axis=1)
    nxt = jnp.where(nxt >= ne, -1, nxt)
    nxt2 = jnp.where(nxt < 0, -1, jnp.take_along_axis(nxt, jnp.maximum(nxt, 0), axis=1))
    next_k = jnp.take_along_axis(nxt, e_k, axis=1)
    next2_k = jnp.take_along_axis(nxt2, e_k, axis=1)
    row = (jnp.arange(g, dtype=i32)[:, None] * ne + e_k) * (tg // bm) + b_k
    flat = lambda a: a.reshape(-1).astype(i32)
    table = (nb.astype(i32), flat(e_k), flat(first), flat(parity), flat(next_k), flat(next2_k))
    return table, flat(row)


def _moe_group_kernel(nb_ref, exp_ref, first_ref, par_ref, next_ref, next2_ref,
                      h_ref, rows_ref, wl_ref, wg_hbm, wu_hbm, wdn_hbm, o_ref,
                      wg_buf, wu_buf, wdn_buf, wsem, xt_a, xt_b, hid_a, hid_b, yt_a, yt_b,
                      *, bm, nj, batch, nb_max):
    g = pl.program_id(0)
    nb = nb_ref[g]
    tb = g * nb_max
    pitch = bm + 1
    o_ref[...] = jnp.zeros_like(o_ref)

    def weight_copies(e, slot):
        return (pltpu.make_async_copy(wg_hbm.at[e], wg_buf.at[slot], wsem.at[slot]),
                pltpu.make_async_copy(wu_hbm.at[e], wu_buf.at[slot], wsem.at[slot]),
                pltpu.make_async_copy(wdn_hbm.at[e], wdn_buf.at[slot], wsem.at[slot]))

    n_wbuf = wg_buf.shape[0]

    def handoff(k):
        @pl.when(first_ref[tb + k] == 1)
        def _():
            slot = par_ref[tb + k]
            for c in weight_copies(exp_ref[tb + k], slot):
                c.wait()
            ahead = next2_ref[tb + k]

            @pl.when(ahead >= 0)
            def _():
                for c in weight_copies(ahead, (slot + 2) % n_wbuf):
                    c.start()

    def gather_rows(k, xt):
        rows = rows_ref.at[pl.ds(pl.multiple_of(k * bm, bm), bm)]

        def emit(lo, hi):
            for m in range(lo, hi):
                r0 = pl.multiple_of(rows[m], nj)
                xt[pl.ds(m, nj, stride=pitch), :] = h_ref[pl.ds(r0, nj), :]
        return emit

    def scatter_rows(k, yt):
        rows = rows_ref.at[pl.ds(pl.multiple_of(k * bm, bm), bm)]

        def emit(lo, hi):
            for m0 in range(lo, hi, batch):
                updates = []
                for m in range(m0, min(m0 + batch, hi)):
                    r0 = pl.multiple_of(rows[m], nj)
                    updates.append((r0, o_ref[pl.ds(r0, nj), :]
                                    + yt[pl.ds(m, nj, stride=pitch), :]))
                for r0, v in reversed(updates):
                    o_ref[pl.ds(r0, nj), :] = v
        return emit

    n_chunks = nj // 2
    kc = 2 * LANES
    n_pieces = 2 * n_chunks + 1
    early = bm * MOE_EARLY_ROW_PERCENT // 100
    cuts = [early * i // (n_pieces - 1) for i in range(n_pieces)] + [bm]

    def step(up_blk, down_blk, side_work):
        piece = 0

        def side():
            nonlocal piece
            for emit in side_work:
                emit(cuts[piece], cuts[piece + 1])
            piece += 1

        gate = up = None
        for c in range(n_chunks):
            if up_blk is not None:
                k, xt, _ = up_blk
                slot = par_ref[tb + k]
                xc = jnp.concatenate([xt[j * pitch:j * pitch + bm, :] for j in (2 * c, 2 * c + 1)],
                                     axis=-1).astype(BF16)
                pg = jnp.dot(xc, wg_buf[slot, c * kc:(c + 1) * kc, :], preferred_element_type=F32)
                pu = jnp.dot(xc, wu_buf[slot, c * kc:(c + 1) * kc, :], preferred_element_type=F32)
                gate = pg if gate is None else gate + pg
                up = pu if up is None else up + pu
            side()
        for c in range(n_chunks):
            if down_blk is not None:
                k, hid, yt = down_blk
                slot = par_ref[tb + k]
                yc = jnp.dot(hid[...], wdn_buf[slot, :, c * kc:(c + 1) * kc],
                             preferred_element_type=F32)
                for q in range(2):
                    j = 2 * c + q
                    yt[j * pitch:j * pitch + bm, :] = yc[:, q * LANES:(q + 1) * LANES]
            side()
        side()
        if up_blk is not None:
            k, _, hid = up_blk
            w_col = jnp.broadcast_to(wl_ref[k], (SUBLANES, bm)).T[:, :1]
            hid[...] = (jax.nn.silu(gate) * up * w_col).astype(BF16)

    @pl.when(nb > 0)
    def _():
        for c in weight_copies(exp_ref[tb], 0):
            c.start()
        second = next_ref[tb]

        @pl.when(second >= 0)
        def _():
            for c in weight_copies(second, 1):
                c.start()
        gather_rows(0, xt_a)(0, bm)
        handoff(0)
        step((0, xt_a, hid_a), None, [gather_rows(1, xt_b)])
        handoff(1)
        step((1, xt_b, hid_b), (0, hid_a, yt_a), [gather_rows(2, xt_a)])

        def trip(i, carry):
            t = 2 * i + 1
            handoff(t + 1)
            step((t + 1, xt_a, hid_a), (t, hid_b, yt_b),
                 [gather_rows(t + 2, xt_b), scatter_rows(t - 1, yt_a)])
            handoff(t + 2)
            step((t + 2, xt_b, hid_b), (t + 1, hid_a, yt_a),
                 [gather_rows(t + 3, xt_a), scatter_rows(t, yt_b)])
            return carry

        lax.fori_loop(0, nb // 2, trip, 0)

        @pl.when(nb % 2 == 1)
        def _():
            scatter_rows(nb - 1, yt_a)(0, bm)


def _moe_group(table, h2t, rows, wl, wg, wu, wdn, tg, bm, nb_max, batch=4):
    ne, d, eh = wg.shape
    nj = d // LANES
    assert nj == SUBLANES
    g = h2t.shape[0] // (tg * nj)
    kern = functools.partial(_moe_group_kernel, bm=bm, nj=nj, batch=batch, nb_max=nb_max)
    slots = pl.BlockSpec((nb_max * bm,), lambda gi, *_: (gi,), memory_space=pltpu.SMEM)
    block_w = pl.BlockSpec((nb_max, 1, bm), lambda gi, *_: (gi, 0, 0))
    hbm = pl.BlockSpec(memory_space=pl.ANY)
    pitch_rows = nj * (bm + 1)
    grid_spec = pltpu.PrefetchScalarGridSpec(
        num_scalar_prefetch=len(table),
        grid=(g,),
        in_specs=[pl.BlockSpec((tg * nj, LANES), lambda gi, *_: (gi, 0)),
                  slots, block_w, hbm, hbm, hbm],
        out_specs=pl.BlockSpec((tg * nj, LANES), lambda gi, *_: (gi, 0)),
        scratch_shapes=[pltpu.VMEM((MOE_WEIGHT_BUFFERS, d, eh), BF16),
                        pltpu.VMEM((MOE_WEIGHT_BUFFERS, d, eh), BF16),
                        pltpu.VMEM((MOE_WEIGHT_BUFFERS, eh, d), BF16),
                        pltpu.SemaphoreType.DMA((MOE_WEIGHT_BUFFERS,)),
                        pltpu.VMEM((pitch_rows, LANES), F32), pltpu.VMEM((pitch_rows, LANES), F32),
                        pltpu.VMEM((bm, eh), BF16), pltpu.VMEM((bm, eh), BF16),
                        pltpu.VMEM((pitch_rows, LANES), F32), pltpu.VMEM((pitch_rows, LANES), F32)],
    )
    return pl.pallas_call(
        kern,
        grid_spec=grid_spec,
        out_shape=jax.ShapeDtypeStruct((g * tg * nj, LANES), F32),
        compiler_params=_params(1),
        name="moe_group",
    )(*table, h2t, rows, wl.reshape(g * nb_max, 1, bm), wg, wu, wdn)


def _finalize_kernel(r_ref, xacc_ref, gt2_ref, gf_ref, o_ref, *, final_norm):
    tm, d = xacc_ref.shape
    nj = d // LANES
    routed = jnp.concatenate([r_ref[pl.ds(j, tm, stride=nj), :] for j in range(nj)], axis=-1)
    xo = xacc_ref[...] + gt2_ref[...] * routed
    o_ref[...] = _rmsnorm(xo, gf_ref[...]) if final_norm else xo


def _finalize(routed, xacc, gt2, g_final, seq_len, tm, final_norm):
    t, d = xacc.shape
    nj = d // LANES
    return pl.pallas_call(
        functools.partial(_finalize_kernel, final_norm=final_norm),
        grid=(t // tm,),
        in_specs=[_row_spec(tm * nj, LANES),
                  _row_spec(tm, d), _mod_spec(False, tm, d, seq_len // tm),
                  _full_spec(g_final.shape)],
        out_specs=_row_spec(tm, d),
        out_shape=jax.ShapeDtypeStruct((t, d), F32),
        compiler_params=_params(1),
        name="finalize",
    )(routed, xacc, gt2, g_final)


def _s5_discretize(lam_re, lam_im, log_dt, b_re, b_im, c_re, c_im):
    g, n = lam_re.shape
    hg = b_re.shape[2]
    lr = jnp.minimum(lam_re.astype(F32), -1e-4)
    li = lam_im.astype(F32)
    dt = jnp.exp(log_dt.astype(F32))[:, None]
    mag = jnp.exp(lr * dt)
    abar_r, abar_i = mag * jnp.cos(li * dt), mag * jnp.sin(li * dt)
    den = lr * lr + li * li
    nr, ni = abar_r - 1.0, abar_i
    f_r = (nr * lr + ni * li) / den
    f_i = (ni * lr - nr * li) / den
    br_, bi_ = b_re.astype(F32), b_im.astype(F32)
    bbar_r = f_r[..., None] * br_ - f_i[..., None] * bi_
    bbar_i = f_r[..., None] * bi_ + f_i[..., None] * br_
    gpc = LANES // hg
    n_chunks = g // gpc
    eye = jnp.eye(gpc, dtype=F32)

    def b_chunks(bb):
        bb = bb.reshape(n_chunks, gpc, n, hg)
        m = jnp.einsum('jgnh,gk->jghkn', bb, eye)
        return m.reshape(n_chunks, gpc * hg, gpc * n)

    def c_chunks(cc):
        cc = cc.astype(F32).reshape(n_chunks, gpc, hg, n)
        m = jnp.einsum('jghn,gk->jgnkh', cc, eye)
        return m.reshape(n_chunks, gpc * n, gpc * hg)

    bj = jnp.concatenate([b_chunks(bbar_r), b_chunks(bbar_i)], axis=2).astype(BF16)
    cj = jnp.concatenate([c_chunks(c_re), -c_chunks(c_im)], axis=1).astype(BF16)
    return abar_r.reshape(1, g * n), abar_i.reshape(1, g * n), bj, cj


def kernel(x_prompt, x_sample, c_prompt, c_sample, state_ssm_re, state_ssm_im, g_norm1, g_norm2, w_ada, b_ada, w_in, w_out, ssm_lambda_re, ssm_lambda_im, ssm_log_dt, ssm_b_re, ssm_b_im, ssm_c_re, ssm_c_im, ssm_d, ssm_w_glu, ssm_b_glu, gm_ln_g, gm_ln_b, gm_w_s, gm_b_s, router_w, router_bias, moe_w_gate, moe_w_up, moe_w_down, shared_w_gate, shared_w_up, shared_w_down, g_final):
    bp, seq_len, d = x_prompt.shape
    bs, dec_len, _ = x_sample.shape
    depth = w_ada.shape[0]
    assert dec_len == 1 and seq_len % CHUNK == 0 and bp % SUBLANES == 0
    groups, n_state = ssm_lambda_re.shape[1:]
    sw = groups * SSM_GROUP
    gw = gm_ln_g.shape[1]
    hd = gw // GM_HEADS

    xp = x_prompt.reshape(bp * seq_len, d)
    xs = x_sample.reshape(bs, d)
    c_all = jnp.concatenate([c_prompt, c_sample], axis=0)
    gf = g_final.reshape(1, d)

    p_re, p_im, s_re, s_im, s_v = [], [], [], [], []
    for l in range(depth):
        mod = _ada(c_all, w_ada[l].astype(BF16), b_ada[l].reshape(1, -1))
        mod_p = [m.reshape(bp, 1, d) for m in jnp.split(mod[:bp], 6, axis=-1)]
        mod_s = jnp.split(mod[bp:], 6, axis=-1)

        g1 = g_norm1[l].reshape(1, d)
        g2 = g_norm2[l].reshape(1, d)
        w_in_b = w_in[l].astype(BF16)
        w_out_b = w_out[l].astype(BF16)
        ln_g = gm_ln_g[l].reshape(1, gw)
        ln_b = gm_ln_b[l].reshape(1, gw)
        ws_tril = (gm_w_s[l] * jnp.tril(jnp.ones((CHUNK, CHUNK), F32))[None]).astype(BF16)
        bs_full = jnp.repeat(gm_b_s[l].T, hd, axis=1)
        w0 = jnp.repeat(gm_w_s[l][:, 0, 0], hd).reshape(1, gw)
        b0 = jnp.repeat(gm_b_s[l][:, 0], hd).reshape(1, gw)
        a_re, a_im, bj, cj = _s5_discretize(ssm_lambda_re[l], ssm_lambda_im[l], ssm_log_dt[l],
                                            ssm_b_re[l], ssm_b_im[l], ssm_c_re[l], ssm_c_im[l])
        d_vec = ssm_d[l].reshape(1, sw)
        w_glu = ssm_w_glu[l].astype(BF16)
        b_glu = ssm_b_glu[l].reshape(1, sw)
        pad = LANES - N_EXPERTS
        rw = jnp.pad(router_w[l], ((0, 0), (0, pad)))
        rw_hi = rw.astype(BF16)
        rw_lo = (rw - rw_hi.astype(F32)).astype(BF16)
        rb = jnp.pad(router_bias[l], (0, pad), constant_values=-jnp.inf).reshape(1, LANES)
        rwt_hi, rwt_lo = rw_hi[:, :N_EXPERTS].T, rw_lo[:, :N_EXPERTS].T
        rb_col = router_bias[l].reshape(N_EXPERTS, 1)
        swg, swu, swd = (shared_w_gate[l].astype(BF16), shared_w_up[l].astype(BF16),
                         shared_w_down[l].astype(BF16))
        ewg, ewu, ewd = (moe_w_gate[l].astype(BF16), moe_w_up[l].astype(BF16),
                         moe_w_down[l].astype(BF16))

        tm_row = min(ROW_TILE, seq_len)
        ua, yb = _premix_seq(xp, mod_p[0], mod_p[1], g1, w_in_b, ln_g, ln_b, ws_tril, bs_full,
                             seq_len, tm=tm_row)
        ya, hpr, hpi = _s5_seq(ua.reshape(bp, seq_len, sw), bj, cj, a_re, a_im, d_vec,
                               w_glu, b_glu, tl=min(SCAN_TILE, seq_len))
        xacc, h2, wd = _postmix(xp, ya.reshape(bp * seq_len, sw), yb, mod_p[2], mod_p[3],
                                mod_p[4], mod_p[5], g2, w_out_b, rwt_hi, rwt_lo, rb_col,
                                swg, swu, swd,
                                per_row=False, seq_len=seq_len, tm=tm_row, token_tiles=True)
        last = l == depth - 1
        tg = min(MOE_GROUP, seq_len)
        bm = min(MOE_BLOCK, tg)
        nb_max = -(-(tg * TOP_K // bm + N_EXPERTS + 3) // SUBLANES) * SUBLANES
        slot_rows, slot_w, counts = _route_plan(wd, tg, d // LANES)
        table, block_row = _block_table(counts[:, :, 0], tg, bm, nb_max)
        by_block = lambda a: jnp.take(a.reshape(-1, bm), block_row, axis=0).reshape(-1)
        routed = _moe_group(table, h2, by_block(slot_rows), by_block(slot_w),
                            ewg, ewu, ewd, tg, bm, nb_max)
        xp = _finalize(routed, xacc, mod_p[5], gf, seq_len, tm_row, final_norm=last)
        p_re.append(hpr.reshape(bp, groups, n_state))
        p_im.append(hpi.reshape(bp, groups, n_state))

        ua, yb, vn = _premix_step(xs, mod_s[0], mod_s[1], g1, w_in_b, ln_g, ln_b, w0, b0)
        ya, hsr, hsi = _s5_step(ua, state_ssm_re[l].reshape(bs, groups * n_state),
                                state_ssm_im[l].reshape(bs, groups * n_state),
                                bj, cj, a_re, a_im, d_vec, w_glu, b_glu)
        xacc, h2, wd = _postmix(xs, ya, yb, mod_s[2], mod_s[3], mod_s[4], mod_s[5], g2, w_out_b,
                                rw_hi, rw_lo, rb, swg, swu, swd, per_row=True, seq_len=1, tm=bs,
                                token_tiles=False)
        xs = _moe_dense(h2, wd, xacc, mod_s[5], gf, ewg, ewu, ewd, per_row=True, seq_len=1,
                        tm=bs, final_norm=last)
        s_re.append(hsr.reshape(bs, groups, n_state))
        s_im.append(hsi.reshape(bs, groups, n_state))
        s_v.append(vn.reshape(bs, 1, gw))

    y_prompt = xp.reshape(bp, seq_len, d)
    y_sample = xs.reshape(bs, 1, d)
    return (y_prompt, y_sample, jnp.stack(p_re), jnp.stack(p_im), jnp.stack(s_re),
            jnp.stack(s_im), jnp.stack(s_v))
```

```python
import functools

import jax
import jax.numpy as jnp
from jax import lax
from jax.experimental import pallas as pl
from jax.experimental.pallas import tpu as pltpu

F32 = jnp.float32
BF16 = jnp.bfloat16

EPS = 1e-6
SSM_GROUP = 16
SSM_STATE = 64
GM_HEADS = 4
CHUNK = 128
N_EXPERTS = 64
TOP_K = 8
N_ROUTE_GROUPS = 8
TOPK_ROUTE_GROUPS = 4
ROUTED_SCALE = 2.5

LANES = 128
SUBLANES = 8
VMEM_LIMIT = 48 * 1024 * 1024
ROW_TILE = 512
SCAN_TILE = 128
MOE_GROUP = 4096
SMEM_TILE = 1024
MOE_BLOCK = 128
MOE_EARLY_ROW_PERCENT = 56
MOE_WEIGHT_BUFFERS = 4


def _params(n_axes):
    return pltpu.CompilerParams(dimension_semantics=("arbitrary",) * n_axes,
                                vmem_limit_bytes=VMEM_LIMIT)


def _dot(a, b):
    return jnp.dot(a.astype(BF16), b.astype(BF16), preferred_element_type=F32)


def _rmsnorm(x, g):
    ms = jnp.mean(x * x, axis=-1, keepdims=True)
    return x * lax.rsqrt(ms + EPS) * g


def _ada_kernel(c_ref, w_ref, b_ref, o_ref):
    c = c_ref[...]
    o_ref[...] = _dot(jax.nn.silu(c), w_ref[...]) + b_ref[...]


def _ada(c, w, b, tn=512):
    m, d = c.shape
    n = w.shape[1]
    return pl.pallas_call(
        _ada_kernel,
        grid=(n // tn,),
        in_specs=[pl.BlockSpec((m, d), lambda j: (0, 0)),
                  pl.BlockSpec((d, tn), lambda j: (0, j)),
                  pl.BlockSpec((1, tn), lambda j: (0, j))],
        out_specs=pl.BlockSpec((m, tn), lambda j: (0, j)),
        out_shape=jax.ShapeDtypeStruct((m, n), F32),
        compiler_params=_params(1),
        name="ada_mod",
    )(c, w, b)


def _premix_common(x_ref, sh_ref, sc_ref, g_ref, win_ref, lng_ref, lnb_ref, sw):
    x = x_ref[...]
    h = _rmsnorm(x, g_ref[...]) * (1.0 + sc_ref[...]) + sh_ref[...]
    proj = _dot(h, win_ref[...])
    u_a = proj[:, :sw]
    z = jax.nn.gelu(proj[:, sw:])
    gw = z.shape[1] // 2
    u_b = z[:, :gw]
    v_b = z[:, gw:]
    hd = gw // GM_HEADS
    parts = []
    for k in range(GM_HEADS):
        vh = v_b[:, k * hd:(k + 1) * hd]
        mu = jnp.mean(vh, axis=-1, keepdims=True)
        dv = vh - mu
        var = jnp.mean(dv * dv, axis=-1, keepdims=True)
        parts.append(dv * lax.rsqrt(var + EPS))
    vn = jnp.concatenate(parts, axis=-1) * lng_ref[...] + lnb_ref[...]
    return u_a, u_b, vn


def _premix_seq_kernel(x_ref, sh_ref, sc_ref, g_ref, win_ref, lng_ref, lnb_ref,
                       ws_ref, bs_ref, ua_ref, yb_ref, *, sw):
    u_a, u_b, vn = _premix_common(x_ref, sh_ref, sc_ref, g_ref, win_ref, lng_ref, lnb_ref, sw)
    ua_ref[...] = u_a
    tm, gw = u_b.shape
    hd = gw // GM_HEADS
    vnb = vn.astype(BF16)
    for c in range(tm // CHUNK):
        rows = slice(c * CHUNK, (c + 1) * CHUNK)
        for k in range(GM_HEADS):
            cols = slice(k * hd, (k + 1) * hd)
            s = jnp.dot(ws_ref[k], vnb[rows, cols], preferred_element_type=F32)
            yb_ref[rows, cols] = u_b[rows, cols] * (s + bs_ref[:, cols])


def _premix_step_kernel(x_ref, sh_ref, sc_ref, g_ref, win_ref, lng_ref, lnb_ref,
                        w0_ref, b0_ref, ua_ref, yb_ref, vn_ref, *, sw):
    u_a, u_b, vn = _premix_common(x_ref, sh_ref, sc_ref, g_ref, win_ref, lng_ref, lnb_ref, sw)
    ua_ref[...] = u_a
    vn_ref[...] = vn
    yb_ref[...] = u_b * (vn * w0_ref[...] + b0_ref[...])


def _row_spec(tm, d):
    return pl.BlockSpec((tm, d), lambda i: (i, 0))


def _full_spec(shape):
    nd = len(shape)
    return pl.BlockSpec(shape, lambda i: (0,) * nd)


def _mod_spec(per_row, tm, d, tiles_per_batch):
    if per_row:
        return pl.BlockSpec((tm, d), lambda i: (i, 0))
    return pl.BlockSpec((None, 1, d), lambda i: (i // tiles_per_batch, 0, 0))


def _premix_seq(x, sh, sc, g, w_in, ln_g, ln_b, ws_tril, bs_full, seq_len, tm):
    t, d = x.shape
    sw = ln_g.shape[1]
    tpb = seq_len // tm
    mod = _mod_spec(False, tm, d, tpb)
    return pl.pallas_call(
        functools.partial(_premix_seq_kernel, sw=sw),
        grid=(t // tm,),
        in_specs=[_row_spec(tm, d), mod, mod, _full_spec(g.shape), _full_spec(w_in.shape),
                  _full_spec(ln_g.shape), _full_spec(ln_b.shape), _full_spec(ws_tril.shape),
                  _full_spec(bs_full.shape)],
        out_specs=[_row_spec(tm, sw), _row_spec(tm, sw)],
        out_shape=[jax.ShapeDtypeStruct((t, sw), F32), jax.ShapeDtypeStruct((t, sw), F32)],
        compiler_params=_params(1),
        name="premix_seq",
    )(x, sh, sc, g, w_in, ln_g, ln_b, ws_tril, bs_full)


def _premix_step(x, sh, sc, g, w_in, ln_g, ln_b, w0, b0):
    t, d = x.shape
    sw = ln_g.shape[1]
    mod = _mod_spec(True, t, d, 1)
    return pl.pallas_call(
        functools.partial(_premix_step_kernel, sw=sw),
        grid=(1,),
        in_specs=[_row_spec(t, d), mod, mod, _full_spec(g.shape), _full_spec(w_in.shape),
                  _full_spec(ln_g.shape), _full_spec(ln_b.shape), _full_spec(w0.shape),
                  _full_spec(b0.shape)],
        out_specs=[_row_spec(t, sw)] * 3,
        out_shape=[jax.ShapeDtypeStruct((t, sw), F32)] * 3,
        compiler_params=_params(1),
        name="premix_step",
    )(x, sh, sc, g, w_in, ln_g, ln_b, w0, b0)


def _s5_output(xs, u, cj_ref, d_ref, wglu_ref, bglu_ref, n_chunks):
    ys = [jnp.dot(xs(j).astype(BF16), cj_ref[j], preferred_element_type=F32)
          for j in range(n_chunks)]
    y = jnp.concatenate(ys, axis=-1) + d_ref[...] * u
    y = jax.nn.gelu(y)
    gate = jax.nn.sigmoid(_dot(y, wglu_ref[...]) + bglu_ref[...])
    return y * gate


def _s5_seq_kernel(u_ref, bj_ref, cj_ref, are_ref, aim_ref, d_ref, wglu_ref, bglu_ref,
                   y_ref, sre_ref, sim_ref, bu_ref, st_ref, *, nb, tl, n_chunks, unroll):
    i = pl.program_id(0)
    n_tiles = bu_ref.shape[0]
    tpc = n_tiles // n_chunks
    hpc = tpc // 2
    hw = hpc * LANES
    kw = u_ref.shape[2] // n_chunks

    @pl.when(i == 0)
    def _():
        st_ref[...] = jnp.zeros_like(st_ref)

    u_all = u_ref[...].reshape(nb * tl, u_ref.shape[2])
    ub = u_all.astype(BF16)
    for j in range(n_chunks):
        bu = jnp.dot(ub[:, j * kw:(j + 1) * kw], bj_ref[j], preferred_element_type=F32)
        for b in range(nb):
            for c in range(tpc):
                bu_ref[j * tpc + c, pl.ds(b, tl, stride=nb), :] = \
                    bu[b * tl:(b + 1) * tl, c * LANES:(c + 1) * LANES]

    for j in range(n_chunks):
        re = slice(j * 2 * hw, j * 2 * hw + hw)
        im = slice(j * 2 * hw + hw, (j + 1) * 2 * hw)
        ar = jnp.broadcast_to(are_ref[:, j * hw:(j + 1) * hw], (nb, hw))
        ai = jnp.broadcast_to(aim_ref[:, j * hw:(j + 1) * hw], (nb, hw))
        re_tiles = [j * tpc + c for c in range(hpc)]
        im_tiles = [j * tpc + hpc + c for c in range(hpc)]

        def body(s, carry, ar=ar, ai=ai, re_tiles=re_tiles, im_tiles=im_tiles):
            xr, xi = carry
            for k in range(unroll):
                r0 = pl.multiple_of((s * unroll + k) * nb, nb)
                br = jnp.concatenate([bu_ref[c, pl.ds(r0, nb), :] for c in re_tiles], axis=-1)
                bi = jnp.concatenate([bu_ref[c, pl.ds(r0, nb), :] for c in im_tiles], axis=-1)
                nr = ar * xr - ai * xi + br
                ni = ar * xi + ai * xr + bi
                for q, c in enumerate(re_tiles):
                    bu_ref[c, pl.ds(r0, nb), :] = nr[:, q * LANES:(q + 1) * LANES]
                for q, c in enumerate(im_tiles):
                    bu_ref[c, pl.ds(r0, nb), :] = ni[:, q * LANES:(q + 1) * LANES]
                xr, xi = nr, ni
            return xr, xi

        xr, xi = lax.fori_loop(0, tl // unroll, body, (st_ref[:, re], st_ref[:, im]))
        st_ref[:, re] = xr
        st_ref[:, im] = xi
        sre_ref[:, j * hw:(j + 1) * hw] = xr
        sim_ref[:, j * hw:(j + 1) * hw] = xi

    def state_chunk(j):
        return jnp.concatenate(
            [jnp.concatenate([bu_ref[j * tpc + c, pl.ds(b, tl, stride=nb), :]
                              for c in range(tpc)], axis=-1) for b in range(nb)], axis=0)

    y = _s5_output(state_chunk, u_all, cj_ref, d_ref, wglu_ref, bglu_ref, n_chunks)
    y_ref[...] = y.reshape(y_ref.shape)


def _s5_seq(u, bj, cj, a_re, a_im, d_vec, w_glu, b_glu, tl, unroll=4):
    nb, seq_len, sw = u.shape
    n_chunks, _, cw = bj.shape
    ns = a_re.shape[1]
    kern = functools.partial(_s5_seq_kernel, nb=nb, tl=tl, n_chunks=n_chunks, unroll=unroll)
    return pl.pallas_call(
        kern,
        grid=(seq_len // tl,),
        in_specs=[pl.BlockSpec((nb, tl, sw), lambda i: (0, i, 0)),
                  _full_spec(bj.shape), _full_spec(cj.shape), _full_spec(a_re.shape),
                  _full_spec(a_im.shape), _full_spec(d_vec.shape), _full_spec(w_glu.shape),
                  _full_spec(b_glu.shape)],
        out_specs=[pl.BlockSpec((nb, tl, sw), lambda i: (0, i, 0)),
                   _full_spec((nb, ns)), _full_spec((nb, ns))],
        out_shape=[jax.ShapeDtypeStruct((nb, seq_len, sw), F32),
                   jax.ShapeDtypeStruct((nb, ns), F32), jax.ShapeDtypeStruct((nb, ns), F32)],
        scratch_shapes=[pltpu.VMEM((n_chunks * cw // LANES, tl * nb, LANES), F32),
                        pltpu.VMEM((nb, n_chunks * cw), F32)],
        compiler_params=_params(1),
        name="s5_seq",
    )(u, bj, cj, a_re, a_im, d_vec, w_glu, b_glu)


def _s5_step_kernel(u_ref, hre_ref, him_ref, bj_ref, cj_ref, are_ref, aim_ref, d_ref,
                    wglu_ref, bglu_ref, y_ref, sre_ref, sim_ref, *, n_chunks):
    u = u_ref[...]
    ub = u.astype(BF16)
    kw = u.shape[1] // n_chunks
    hw = bj_ref.shape[2] // 2
    chunks = []
    for j in range(n_chunks):
        st = slice(j * hw, (j + 1) * hw)
        bu = jnp.dot(ub[:, j * kw:(j + 1) * kw], bj_ref[j], preferred_element_type=F32)
        ar, ai = are_ref[:, st], aim_ref[:, st]
        hr, hi = hre_ref[:, st], him_ref[:, st]
        xr = ar * hr - ai * hi + bu[:, :hw]
        xi = ar * hi + ai * hr + bu[:, hw:]
        sre_ref[:, st] = xr
        sim_ref[:, st] = xi
        chunks.append(jnp.concatenate([xr, xi], axis=-1))
    y_ref[...] = _s5_output(chunks.__getitem__, u, cj_ref, d_ref, wglu_ref, bglu_ref, n_chunks)


def _s5_step(u, h_re, h_im, bj, cj, a_re, a_im, d_vec, w_glu, b_glu):
    t, sw = u.shape
    ns = a_re.shape[1]
    args = (u, h_re, h_im, bj, cj, a_re, a_im, d_vec, w_glu, b_glu)
    return pl.pallas_call(
        functools.partial(_s5_step_kernel, n_chunks=bj.shape[0]),
        grid=(1,),
        in_specs=[_full_spec(a.shape) for a in args],
        out_specs=[_full_spec((t, sw)), _full_spec((t, ns)), _full_spec((t, ns))],
        out_shape=[jax.ShapeDtypeStruct((t, sw), F32), jax.ShapeDtypeStruct((t, ns), F32),
                   jax.ShapeDtypeStruct((t, ns), F32)],
        compiler_params=_params(1),
        name="s5_step",
    )(*args)


def _group_reduce(x, lane, width, op):
    n = x.shape[-1]
    s = 1
    while s < width:
        up = pltpu.roll(x, s, 1)
        dn = pltpu.roll(x, n - s, 1)
        x = op(x, jnp.where((lane & s) != 0, up, dn))
        s *= 2
    return x


def _first_max(x, lane, big):
    m = jnp.max(x, axis=-1, keepdims=True)
    return jnp.min(jnp.where(x == m, lane, big), axis=-1, keepdims=True)


def _route(logits, bias):
    rows, n = logits.shape
    lane = lax.broadcasted_iota(jnp.int32, (rows, n), 1)
    lane_f = lane.astype(F32)
    neg = jnp.float32(-jnp.inf)
    big = jnp.float32(n)
    gsz = N_EXPERTS // N_ROUTE_GROUPS
    group_start = (lane - (lane & (gsz - 1))).astype(F32)
    scores = jax.nn.sigmoid(logits)
    biased = scores + bias
    m1 = _group_reduce(biased, lane, gsz, jnp.maximum)
    first = _group_reduce(jnp.where(biased == m1, lane_f, big), lane, gsz, jnp.minimum)
    m2 = _group_reduce(jnp.where(lane_f == first, neg, biased), lane, gsz, jnp.maximum)
    gs = m1 + m2
    gsel = jnp.zeros((rows, n), F32)
    work = gs
    for _ in range(TOPK_ROUTE_GROUPS):
        hit = group_start == _first_max(work, lane_f, big)
        gsel = jnp.where(hit, 1.0, gsel)
        work = jnp.where(hit, neg, work)
    work = jnp.where(gsel > 0.0, biased, neg)
    w = jnp.zeros((rows, n), F32)
    for _ in range(TOP_K):
        hit = lane_f == _first_max(work, lane_f, big)
        w = jnp.where(hit, scores, w)
        work = jnp.where(hit, neg, work)
    return w / jnp.sum(w, axis=-1, keepdims=True) * ROUTED_SCALE


def _route_by_expert(logits, bias):
    ne, tm = logits.shape
    ng = N_ROUTE_GROUPS
    gsz = ne // ng
    neg = jnp.float32(-jnp.inf)
    shape = (ng, gsz, tm)
    scores = jax.nn.sigmoid(logits)
    biased = (scores + bias).reshape(shape)
    scores = scores.reshape(shape)
    sub = lax.broadcasted_iota(jnp.int32, shape, 1)
    grp = lax.broadcasted_iota(jnp.int32, (ng, 1, tm), 0)
    eid = lax.broadcasted_iota(jnp.int32, shape, 0) * gsz + sub
    m1 = jnp.max(biased, axis=1, keepdims=True)
    first = jnp.min(jnp.where(biased == m1, sub, gsz), axis=1, keepdims=True)
    m2 = jnp.max(jnp.where(sub == first, neg, biased), axis=1, keepdims=True)
    work = m1 + m2
    gsel = jnp.zeros((ng, 1, tm), jnp.bool_)
    for _ in range(TOPK_ROUTE_GROUPS):
        m = jnp.max(work, axis=0, keepdims=True)
        hit = grp == jnp.min(jnp.where(work == m, grp, ng), axis=0, keepdims=True)
        gsel = jnp.logical_or(gsel, hit)
        work = jnp.where(hit, neg, work)
    work = jnp.where(gsel, biased, neg)
    w = jnp.zeros(shape, F32)
    for _ in range(TOP_K):
        m = jnp.max(jnp.max(work, axis=0, keepdims=True), axis=1, keepdims=True)
        first = jnp.min(jnp.min(jnp.where(work == m, eid, ne), axis=0, keepdims=True),
                        axis=1, keepdims=True)
        hit = eid == first
        w = jnp.where(hit, scores, w)
        work = jnp.where(hit, neg, work)
    denom = jnp.sum(jnp.sum(w, axis=0, keepdims=True), axis=1, keepdims=True)
    return (w / denom * ROUTED_SCALE).reshape(ne, tm)


def _split_bf16(x):
    hi = x.astype(BF16)
    lo = (x - hi.astype(F32)).astype(BF16)
    return hi, lo


def _postmix_kernel(x_ref, ya_ref, yb_ref, gt1_ref, sh2_ref, sc2_ref, gt2_ref, g2_ref,
                    wout_ref, rwh_ref, rwl_ref, rb_ref, swg_ref, swu_ref, swd_ref,
                    xacc_ref, h2_ref, wd_ref, *, token_tiles):
    y = jnp.concatenate([ya_ref[...], yb_ref[...]], axis=-1)
    x1 = x_ref[...] + gt1_ref[...] * _dot(y, wout_ref[...])
    h2 = _rmsnorm(x1, g2_ref[...]) * (1.0 + sc2_ref[...]) + sh2_ref[...]
    hb = h2.astype(BF16)
    if token_tiles:
        tm, d = h2.shape
        nj = d // LANES
        for j in range(nj):
            h2_ref[pl.ds(j, tm, stride=nj), :] = h2[:, j * LANES:(j + 1) * LANES]
    else:
        h2_ref[...] = hb
    h_hi, h_lo = _split_bf16(h2)
    if token_tiles:
        nt = (((1,), (1,)), ((), ()))
        logits = (lax.dot_general(rwh_ref[...], h_hi, nt, preferred_element_type=F32)
                  + lax.dot_general(rwl_ref[...], h_hi, nt, preferred_element_type=F32)
                  + lax.dot_general(rwh_ref[...], h_lo, nt, preferred_element_type=F32))
        wd_ref[...] = _route_by_expert(logits, rb_ref[...])
    else:
        logits = (jnp.dot(h_hi, rwh_ref[...], preferred_element_type=F32)
                  + jnp.dot(h_hi, rwl_ref[...], preferred_element_type=F32)
                  + jnp.dot(h_lo, rwh_ref[...], preferred_element_type=F32))
        wd_ref[...] = _route(logits, rb_ref[...])
    hid = jax.nn.silu(jnp.dot(hb, swg_ref[...], preferred_element_type=F32)) * \
        jnp.dot(hb, swu_ref[...], preferred_element_type=F32)
    shared = _dot(hid, swd_ref[...])
    xacc_ref[...] = x1 + gt2_ref[...] * shared


def _postmix(x, ya, yb, gt1, sh2, sc2, gt2, g2, w_out, rw_hi, rw_lo, rb, swg, swu, swd,
             per_row, seq_len, tm, token_tiles):
    t, d = x.shape
    sw = ya.shape[1]
    nj = d // LANES
    mod = _mod_spec(per_row, tm, d, max(seq_len // tm, 1))
    weights = (g2, w_out, rw_hi, rw_lo, rb, swg, swu, swd)
    if token_tiles:
        ne = rw_hi.shape[0]
        h2_spec, h2_shape = _row_spec(tm * nj, LANES), jax.ShapeDtypeStruct((t * nj, LANES), F32)
        wd_spec, wd_shape = pl.BlockSpec((ne, tm), lambda i: (0, i)), (ne, t)
    else:
        ne = rw_hi.shape[1]
        h2_spec, h2_shape = _row_spec(tm, d), jax.ShapeDtypeStruct((t, d), BF16)
        wd_spec, wd_shape = _row_spec(tm, ne), (t, ne)
    return pl.pallas_call(
        functools.partial(_postmix_kernel, token_tiles=token_tiles),
        grid=(t // tm,),
        in_specs=[_row_spec(tm, d), _row_spec(tm, sw), _row_spec(tm, sw), mod, mod, mod, mod]
        + [_full_spec(w.shape) for w in weights],
        out_specs=[_row_spec(tm, d), h2_spec, wd_spec],
        out_shape=[jax.ShapeDtypeStruct((t, d), F32), h2_shape,
                   jax.ShapeDtypeStruct(wd_shape, F32)],
        compiler_params=_params(1),
        name="postmix",
    )(x, ya, yb, gt1, sh2, sc2, gt2, *weights)


def _moe_dense_kernel(h_ref, wd_ref, xacc_ref, gt2_ref, gf_ref, wg_ref, wu_ref, wdn_ref,
                      o_ref, acc_ref, *, final_norm):
    e = pl.program_id(1)

    @pl.when(e == 0)
    def _():
        acc_ref[...] = jnp.zeros_like(acc_ref)

    hb = h_ref[...]
    wd = wd_ref[...]
    lane = lax.broadcasted_iota(jnp.int32, wd.shape, 1)
    col = jnp.sum(jnp.where(lane == e, wd, 0.0), axis=-1, keepdims=True)
    hid = jax.nn.silu(jnp.dot(hb, wg_ref[...], preferred_element_type=F32)) * \
        jnp.dot(hb, wu_ref[...], preferred_element_type=F32)
    acc_ref[...] += col * _dot(hid, wdn_ref[...])

    @pl.when(e == pl.num_programs(1) - 1)
    def _():
        xo = xacc_ref[...] + gt2_ref[...] * acc_ref[...]
        o_ref[...] = _rmsnorm(xo, gf_ref[...]) if final_norm else xo


def _moe_dense(h2, wd, xacc, gt2, g_final, wg, wu, wdn, per_row, seq_len, tm, final_norm):
    t, d = xacc.shape
    ne, _, eh = wg.shape
    nl = wd.shape[1]
    tpb = max(seq_len // tm, 1)
    if per_row:
        mod = pl.BlockSpec((tm, d), lambda i, e: (i, 0))
    else:
        mod = pl.BlockSpec((None, 1, d), lambda i, e: (i // tpb, 0, 0))
    row = lambda w: pl.BlockSpec((tm, w), lambda i, e: (i, 0))
    return pl.pallas_call(
        functools.partial(_moe_dense_kernel, final_norm=final_norm),
        grid=(t // tm, ne),
        in_specs=[row(d), row(nl), row(d), mod,
                  pl.BlockSpec(g_final.shape, lambda i, e: (0, 0)),
                  pl.BlockSpec((None, d, eh), lambda i, e: (e, 0, 0)),
                  pl.BlockSpec((None, d, eh), lambda i, e: (e, 0, 0)),
                  pl.BlockSpec((None, eh, d), lambda i, e: (e, 0, 0))],
        out_specs=row(d),
        out_shape=jax.ShapeDtypeStruct((t, d), F32),
        scratch_shapes=[pltpu.VMEM((tm, d), F32)],
        compiler_params=_params(2),
        name="moe_dense",
    )(h2, wd, xacc, gt2, g_final, wg, wu, wdn)


TOKEN_BITS = 13


def _route_plan_kernel(wd_ref, rows_ref, wl_ref, cnt_ref, *, tg, nj):
    wt = wd_ref[...]
    sel = wt > 0.0
    lane = lax.broadcasted_iota(jnp.int32, wt.shape, 1)
    c = sel.astype(jnp.int32)
    s = 1
    while s < tg:
        c = c + jnp.where(lane >= s, pltpu.roll(c, s, 1), 0)
        s *= 2
    cnt_ref[...] = jnp.broadcast_to(c[:, tg - 1:tg], cnt_ref.shape)
    dist = lane - (c - 1)
    key = jnp.where(sel, (dist << TOKEN_BITS) | lane, -1)
    w = jnp.where(sel, wt, 0.0)
    s = 1
    while s < tg:
        leaving = (key >= 0) & (((key >> TOKEN_BITS) & s) != 0)
        key_in = pltpu.roll(key, tg - s, 1)
        w_in = pltpu.roll(w, tg - s, 1)
        arriving = (key_in >= 0) & (((key_in >> TOKEN_BITS) & s) != 0)
        key = jnp.where(arriving, key_in, jnp.where(leaving, -1, key))
        w = jnp.where(arriving, w_in, jnp.where(leaving, 0.0, w))
        s *= 2
    rows_ref[...] = jnp.where(key >= 0, key & ((1 << TOKEN_BITS) - 1), 0) * nj
    wl_ref[...] = w


def _route_plan(wd, tg, nj):
    ne, t = wd.shape
    assert ne == N_EXPERTS and tg <= (1 << TOKEN_BITS) and t % tg == 0
    g = t // tg
    grp = lambda w: pl.BlockSpec((None, N_EXPERTS, w), lambda i: (i, 0, 0))
    return pl.pallas_call(
        functools.partial(_route_plan_kernel, tg=tg, nj=nj),
        grid=(g,),
        in_specs=[pl.BlockSpec((ne, tg), lambda i: (0, i))],
        out_specs=[grp(tg), grp(tg), grp(LANES)],
        out_shape=[jax.ShapeDtypeStruct((g, N_EXPERTS, tg), jnp.int32),
                   jax.ShapeDtypeStruct((g, N_EXPERTS, tg), F32),
                   jax.ShapeDtypeStruct((g, N_EXPERTS, LANES), jnp.int32)],
        compiler_params=_params(1),
        name="route_plan",
    )(wd)


def _block_table(counts, tg, bm, nb_max):
    g, ne = counts.shape
    i32 = jnp.int32
    nblk = (counts + bm - 1) // bm
    bend = jnp.cumsum(nblk, axis=1)
    bstart = bend - nblk
    nb = bend[:, -1]
    k = jnp.arange(nb_max, dtype=i32)[None, :]
    kc = jnp.minimum(k, jnp.maximum(nb[:, None] - 1, 0))
    e_k = jnp.minimum(jnp.sum(bend[:, None, :] <= kc[:, :, None], axis=2), ne - 1).astype(i32)
    b_k = kc - jnp.take_along_axis(bstart, e_k, axis=1)
    first = (b_k == 0) & (k < nb[:, None])
    active = nblk > 0
    parity = (jnp.take_along_axis(jnp.cumsum(active.astype(i32), axis=1) - 1, e_k, axis=1)
              % MOE_WEIGHT_BUFFERS)
    ids = jnp.where(active, jnp.arange(ne, dtype=i32)[None, :], ne)
    first_active_from = jnp.flip(lax.cummin(jnp.flip(ids, 1), axis=1), 1)
    nxt = jnp.concatenate([first_active_from[:, 1:], jnp.full((g, 1), ne, i32)], axis=1)
    nxt = jnp.where(nxt >= ne, -1, nxt)
    nxt2 = jnp.where(nxt < 0, -1, jnp.take_along_axis(nxt, jnp.maximum(nxt, 0), axis=1))
    next_k = jnp.take_along_axis(nxt, e_k, axis=1)
    next2_k = jnp.take_along_axis(nxt2, e_k, axis=1)
    row = (jnp.arange(g, dtype=i32)[:, None] * ne + e_k) * (tg // bm) + b_k
    flat = lambda a: a.reshape(-1).astype(i32)
    table = (nb.astype(i32), flat(e_k), flat(first), flat(parity), flat(next_k), flat(next2_k))
    return table, flat(row)


def _moe_group_kernel(nb_ref, exp_ref, first_ref, par_ref, next_ref, next2_ref,
                      h_ref, rows_ref, wl_ref, wg_hbm, wu_hbm, wdn_hbm, o_ref,
                      wg_buf, wu_buf, wdn_buf, wsem, xt_a, xt_b, hid_a, hid_b, yt_a, yt_b,
                      *, bm, nj, batch, nb_max):
    g = pl.program_id(0)
    nb = nb_ref[g]
    tb = g * nb_max
    pitch = bm + 1
    o_ref[...] = jnp.zeros_like(o_ref)

    def weight_copies(e, slot):
        return (pltpu.make_async_copy(wg_hbm.at[e], wg_buf.at[slot], wsem.at[slot]),
                pltpu.make_async_copy(wu_hbm.at[e], wu_buf.at[slot], wsem.at[slot]),
                pltpu.make_async_copy(wdn_hbm.at[e], wdn_buf.at[slot], wsem.at[slot]))

    n_wbuf = wg_buf.shape[0]

    def handoff(k):
        @pl.when(first_ref[tb + k] == 1)
        def _():
            slot = par_ref[tb + k]
            for c in weight_copies(exp_ref[tb + k], slot):
                c.wait()
            ahead = next2_ref[tb + k]

            @pl.when(ahead >= 0)
            def _():
                for c in weight_copies(ahead, (slot + 2) % n_wbuf):
                    c.start()

    def gather_rows(k, xt):
        rows = rows_ref.at[pl.ds(pl.multiple_of(k * bm, bm), bm)]

        def emit(lo, hi):
            for m in range(lo, hi):
                r0 = pl.multiple_of(rows[m], nj)
                xt[pl.ds(m, nj, stride=pitch), :] = h_ref[pl.ds(r0, nj), :]
        return emit

    def scatter_rows(k, yt):
        rows = rows_ref.at[pl.ds(pl.multiple_of(k * bm, bm), bm)]

        def emit(lo, hi):
            for m0 in range(lo, hi, batch):
                updates = []
                for m in range(m0, min(m0 + batch, hi)):
                    r0 = pl.multiple_of(rows[m], nj)
                    updates.append((r0, o_ref[pl.ds(r0, nj), :]
                                    + yt[pl.ds(m, nj, stride=pitch), :]))
                for r0, v in reversed(updates):
                    o_ref[pl.ds(r0, nj), :] = v
        return emit

    n_chunks = nj // 2
    kc = 2 * LANES
    n_pieces = 2 * n_chunks + 1
    early = bm * MOE_EARLY_ROW_PERCENT // 100
    cuts = [early * i // (n_pieces - 1) for i in range(n_pieces)] + [bm]

    def step(up_blk, down_blk, side_work):
        piece = 0

        def side():
            nonlocal piece
            for emit in side_work:
                emit(cuts[piece], cuts[piece + 1])
            piece += 1

        gate = up = None
        for c in range(n_chunks):
            if up_blk is not None:
                k, xt, _ = up_blk
                slot = par_ref[tb + k]
                xc = jnp.concatenate([xt[j * pitch:j * pitch + bm, :] for j in (2 * c, 2 * c + 1)],
                                     axis=-1).astype(BF16)
                pg = jnp.dot(xc, wg_buf[slot, c * kc:(c + 1) * kc, :], preferred_element_type=F32)
                pu = jnp.dot(xc, wu_buf[slot, c * kc:(c + 1) * kc, :], preferred_element_type=F32)
                gate = pg if gate is None else gate + pg
                up = pu if up is None else up + pu
            side()
        for c in range(n_chunks):
            if down_blk is not None:
                k, hid, yt = down_blk
                slot = par_ref[tb + k]
                yc = jnp.dot(hid[...], wdn_buf[slot, :, c * kc:(c + 1) * kc],
                             preferred_element_type=F32)
                for q in range(2):
                    j = 2 * c + q
                    yt[j * pitch:j * pitch + bm, :] = yc[:, q * LANES:(q + 1) * LANES]
            side()
        side()
        if up_blk is not None:
            k, _, hid = up_blk
            w_col = jnp.broadcast_to(wl_ref[k], (SUBLANES, bm)).T[:, :1]
            hid[...] = (jax.nn.silu(gate) * up * w_col).astype(BF16)

    @pl.when(nb > 0)
    def _():
        for c in weight_copies(exp_ref[tb], 0):
            c.start()
        second = next_ref[tb]

        @pl.when(second >= 0)
        def _():
            for c in weight_copies(second, 1):
                c.start()
        gather_rows(0, xt_a)(0, bm)
        handoff(0)
        step((0, xt_a, hid_a), None, [gather_rows(1, xt_b)])
        handoff(1)
        step((1, xt_b, hid_b), (0, hid_a, yt_a), [gather_rows(2, xt_a)])

        def trip(i, carry):
            t = 2 * i + 1
            handoff(t + 1)
            step((t + 1, xt_a, hid_a), (t, hid_b, yt_b),
                 [gather_rows(t + 2, xt_b), scatter_rows(t - 1, yt_a)])
            handoff(t + 2)
            step((t + 2, xt_b, hid_b), (t + 1, hid_a, yt_a),
                 [gather_rows(t + 3, xt_a), scatter_rows(t, yt_b)])
            return carry

        lax.fori_loop(0, nb // 2, trip, 0)

        @pl.when(nb % 2 == 1)
        def _():
            scatter_rows(nb - 1, yt_a)(0, bm)


def _moe_group(table, h2t, rows, wl, wg, wu, wdn, tg, bm, nb_max, batch=4):
    ne, d, eh = wg.shape
    nj = d // LANES
    assert nj == SUBLANES
    g = h2t.shape[0] // (tg * nj)
    kern = functools.partial(_moe_group_kernel, bm=bm, nj=nj, batch=batch, nb_max=nb_max)
    slots = pl.BlockSpec((nb_max * bm,), lambda gi, *_: (gi,), memory_space=pltpu.SMEM)
    block_w = pl.BlockSpec((nb_max, 1, bm), lambda gi, *_: (gi, 0, 0))
    hbm = pl.BlockSpec(memory_space=pl.ANY)
    pitch_rows = nj * (bm + 1)
    grid_spec = pltpu.PrefetchScalarGridSpec(
        num_scalar_prefetch=len(table),
        grid=(g,),
        in_specs=[pl.BlockSpec((tg * nj, LANES), lambda gi, *_: (gi, 0),
                               pipeline_mode=pl.Buffered(1)),
                  slots, block_w, hbm, hbm, hbm],
        out_specs=pl.BlockSpec((tg * nj, LANES), lambda gi, *_: (gi, 0),
                               pipeline_mode=pl.Buffered(1)),
        scratch_shapes=[pltpu.VMEM((MOE_WEIGHT_BUFFERS, d, eh), BF16),
                        pltpu.VMEM((MOE_WEIGHT_BUFFERS, d, eh), BF16),
                        pltpu.VMEM((MOE_WEIGHT_BUFFERS, eh, d), BF16),
                        pltpu.SemaphoreType.DMA((MOE_WEIGHT_BUFFERS,)),
                        pltpu.VMEM((pitch_rows, LANES), F32), pltpu.VMEM((pitch_rows, LANES), F32),
                        pltpu.VMEM((bm, eh), BF16), pltpu.VMEM((bm, eh), BF16),
                        pltpu.VMEM((pitch_rows, LANES), F32), pltpu.VMEM((pitch_rows, LANES), F32)],
    )
    return pl.pallas_call(
        kern,
        grid_spec=grid_spec,
        out_shape=jax.ShapeDtypeStruct((g * tg * nj, LANES), F32),
        compiler_params=_params(1),
        name="moe_group",
    )(*table, h2t, rows, wl.reshape(g * nb_max, 1, bm), wg, wu, wdn)


def _finalize_kernel(r_ref, xacc_ref, gt2_ref, gf_ref, o_ref, *, final_norm):
    tm, d = xacc_ref.shape
    nj = d // LANES
    routed = jnp.concatenate([r_ref[pl.ds(j, tm, stride=nj), :] for j in range(nj)], axis=-1)
    xo = xacc_ref[...] + gt2_ref[...] * routed
    o_ref[...] = _rmsnorm(xo, gf_ref[...]) if final_norm else xo


def _finalize(routed, xacc, gt2, g_final, seq_len, tm, final_norm):
    t, d = xacc.shape
    nj = d // LANES
    return pl.pallas_call(
        functools.partial(_finalize_kernel, final_norm=final_norm),
        grid=(t // tm,),
        in_specs=[_row_spec(tm * nj, LANES),
                  _row_spec(tm, d), _mod_spec(False, tm, d, seq_len // tm),
                  _full_spec(g_final.shape)],
        out_specs=_row_spec(tm, d),
        out_shape=jax.ShapeDtypeStruct((t, d), F32),
        compiler_params=_params(1),
        name="finalize",
    )(routed, xacc, gt2, g_final)


def _s5_discretize(lam_re, lam_im, log_dt, b_re, b_im, c_re, c_im):
    g, n = lam_re.shape
    hg = b_re.shape[2]
    lr = jnp.minimum(lam_re.astype(F32), -1e-4)
    li = lam_im.astype(F32)
    dt = jnp.exp(log_dt.astype(F32))[:, None]
    mag = jnp.exp(lr * dt)
    abar_r, abar_i = mag * jnp.cos(li * dt), mag * jnp.sin(li * dt)
    den = lr * lr + li * li
    nr, ni = abar_r - 1.0, abar_i
    f_r = (nr * lr + ni * li) / den
    f_i = (ni * lr - nr * li) / den
    br_, bi_ = b_re.astype(F32), b_im.astype(F32)
    bbar_r = f_r[..., None] * br_ - f_i[..., None] * bi_
    bbar_i = f_r[..., None] * bi_ + f_i[..., None] * br_
    gpc = LANES // hg
    n_chunks = g // gpc
    eye = jnp.eye(gpc, dtype=F32)

    def b_chunks(bb):
        bb = bb.reshape(n_chunks, gpc, n, hg)
        m = jnp.einsum('jgnh,gk->jghkn', bb, eye)
        return m.reshape(n_chunks, gpc * hg, gpc * n)

    def c_chunks(cc):
        cc = cc.astype(F32).reshape(n_chunks, gpc, hg, n)
        m = jnp.einsum('jghn,gk->jgnkh', cc, eye)
        return m.reshape(n_chunks, gpc * n, gpc * hg)

    bj = jnp.concatenate([b_chunks(bbar_r), b_chunks(bbar_i)], axis=2).astype(BF16)
    cj = jnp.concatenate([c_chunks(c_re), -c_chunks(c_im)], axis=1).astype(BF16)
    return abar_r.reshape(1, g * n), abar_i.reshape(1, g * n), bj, cj


def kernel(x_prompt, x_sample, c_prompt, c_sample, state_ssm_re, state_ssm_im, g_norm1, g_norm2, w_ada, b_ada, w_in, w_out, ssm_lambda_re, ssm_lambda_im, ssm_log_dt, ssm_b_re, ssm_b_im, ssm_c_re, ssm_c_im, ssm_d, ssm_w_glu, ssm_b_glu, gm_ln_g, gm_ln_b, gm_w_s, gm_b_s, router_w, router_bias, moe_w_gate, moe_w_up, moe_w_down, shared_w_gate, shared_w_up, shared_w_down, g_final):
    bp, seq_len, d = x_prompt.shape
    bs, dec_len, _ = x_sample.shape
    depth = w_ada.shape[0]
    assert dec_len == 1 and seq_len % CHUNK == 0 and bp % SUBLANES == 0
    groups, n_state = ssm_lambda_re.shape[1:]
    sw = groups * SSM_GROUP
    gw = gm_ln_g.shape[1]
    hd = gw // GM_HEADS

    xp = x_prompt.reshape(bp * seq_len, d)
    xs = x_sample.reshape(bs, d)
    c_all = jnp.concatenate([c_prompt, c_sample], axis=0)
    gf = g_final.reshape(1, d)

    p_re, p_im, s_re, s_im, s_v = [], [], [], [], []
    for l in range(depth):
        mod = _ada(c_all, w_ada[l].astype(BF16), b_ada[l].reshape(1, -1))
        mod_p = [m.reshape(bp, 1, d) for m in jnp.split(mod[:bp], 6, axis=-1)]
        mod_s = jnp.split(mod[bp:], 6, axis=-1)

        g1 = g_norm1[l].reshape(1, d)
        g2 = g_norm2[l].reshape(1, d)
        w_in_b = w_in[l].astype(BF16)
        w_out_b = w_out[l].astype(BF16)
        ln_g = gm_ln_g[l].reshape(1, gw)
        ln_b = gm_ln_b[l].reshape(1, gw)
        ws_tril = (gm_w_s[l] * jnp.tril(jnp.ones((CHUNK, CHUNK), F32))[None]).astype(BF16)
        bs_full = jnp.repeat(gm_b_s[l].T, hd, axis=1)
        w0 = jnp.repeat(gm_w_s[l][:, 0, 0], hd).reshape(1, gw)
        b0 = jnp.repeat(gm_b_s[l][:, 0], hd).reshape(1, gw)
        a_re, a_im, bj, cj = _s5_discretize(ssm_lambda_re[l], ssm_lambda_im[l], ssm_log_dt[l],
                                            ssm_b_re[l], ssm_b_im[l], ssm_c_re[l], ssm_c_im[l])
        d_vec = ssm_d[l].reshape(1, sw)
        w_glu = ssm_w_glu[l].astype(BF16)
        b_glu = ssm_b_glu[l].reshape(1, sw)
        pad = LANES - N_EXPERTS
        rw = jnp.pad(router_w[l], ((0, 0), (0, pad)))
        rw_hi = rw.astype(BF16)
        rw_lo = (rw - rw_hi.astype(F32)).astype(BF16)
        rb = jnp.pad(router_bias[l], (0, pad), constant_values=-jnp.inf).reshape(1, LANES)
        rwt_hi, rwt_lo = rw_hi[:, :N_EXPERTS].T, rw_lo[:, :N_EXPERTS].T
        rb_col = router_bias[l].reshape(N_EXPERTS, 1)
        swg, swu, swd = (shared_w_gate[l].astype(BF16), shared_w_up[l].astype(BF16),
                         shared_w_down[l].astype(BF16))
        ewg, ewu, ewd = (moe_w_gate[l].astype(BF16), moe_w_up[l].astype(BF16),
                         moe_w_down[l].astype(BF16))

        tm_row = min(ROW_TILE, seq_len)
        ua, yb = _premix_seq(xp, mod_p[0], mod_p[1], g1, w_in_b, ln_g, ln_b, ws_tril, bs_full,
                             seq_len, tm=tm_row)
        ya, hpr, hpi = _s5_seq(ua.reshape(bp, seq_len, sw), bj, cj, a_re, a_im, d_vec,
                               w_glu, b_glu, tl=min(SCAN_TILE, seq_len))
        xacc, h2, wd = _postmix(xp, ya.reshape(bp * seq_len, sw), yb, mod_p[2], mod_p[3],
                                mod_p[4], mod_p[5], g2, w_out_b, rwt_hi, rwt_lo, rb_col,
                                swg, swu, swd,
                                per_row=False, seq_len=seq_len, tm=tm_row, token_tiles=True)
        last = l == depth - 1
        tg = min(MOE_GROUP, bp * seq_len)
        assert (bp * seq_len) % tg == 0
        bm = min(MOE_BLOCK, tg)
        per_tile = max(SMEM_TILE // bm, 1)
        nb_max = -(-(tg * TOP_K // bm + N_EXPERTS + 3) // per_tile) * per_tile
        slot_rows, slot_w, counts = _route_plan(wd, tg, d // LANES)
        table, block_row = _block_table(counts[:, :, 0], tg, bm, nb_max)
        by_block = lambda a: jnp.take(a.reshape(-1, bm), block_row, axis=0).reshape(-1)
        routed = _moe_group(table, h2, by_block(slot_rows), by_block(slot_w),
                            ewg, ewu, ewd, tg, bm, nb_max)
        xp = _finalize(routed, xacc, mod_p[5], gf, seq_len, tm_row, final_norm=last)
        p_re.append(hpr.reshape(bp, groups, n_state))
        p_im.append(hpi.reshape(bp, groups, n_state))

        ua, yb, vn = _premix_step(xs, mod_s[0], mod_s[1], g1, w_in_b, ln_g, ln_b, w0, b0)
        ya, hsr, hsi = _s5_step(ua, state_ssm_re[l].reshape(bs, groups * n_state),
                                state_ssm_im[l].reshape(bs, groups * n_state),
                                bj, cj, a_re, a_im, d_vec, w_glu, b_glu)
        xacc, h2, wd = _postmix(xs, ya, yb, mod_s[2], mod_s[3], mod_s[4], mod_s[5], g2, w_out_b,
                                rw_hi, rw_lo, rb, swg, swu, swd, per_row=True, seq_len=1, tm=bs,
                                token_tiles=False)
        xs = _moe_dense(h2, wd, xacc, mod_s[5], gf, ewg, ewu, ewd, per_row=True, seq_len=1,
                        tm=bs, final_norm=last)
        s_re.append(hsr.reshape(bs, groups, n_state))
        s_im.append(hsi.reshape(bs, groups, n_state))
        s_v.append(vn.reshape(bs, 1, gw))

    y_prompt = xp.reshape(bp, seq_len, d)
    y_sample = xs.reshape(bs, 1, d)
    return (y_prompt, y_sample, jnp.stack(p_re), jnp.stack(p_im), jnp.stack(s_re),
            jnp.stack(s_im), jnp.stack(s_v))
```

```python
import functools

import jax
import jax.numpy as jnp
from jax import lax
from jax.experimental import pallas as pl
from jax.experimental.pallas import tpu as pltpu

F32 = jnp.float32
BF16 = jnp.bfloat16

EPS = 1e-6
SSM_GROUP = 16
SSM_STATE = 64
GM_HEADS = 4
CHUNK = 128
N_EXPERTS = 64
TOP_K = 8
N_ROUTE_GROUPS = 8
TOPK_ROUTE_GROUPS = 4
ROUTED_SCALE = 2.5

LANES = 128
SUBLANES = 8
VMEM_LIMIT = 48 * 1024 * 1024
ROW_TILE = 512
PREMIX_TILE = 1024
DENSE_EXPERTS_PER_STEP = 4
SCAN_TILE = 128
MOE_GROUP = 4096
SMEM_TILE = 1024
MOE_BLOCK = 128
MOE_EARLY_ROW_PERCENT = 56
MOE_WEIGHT_BUFFERS = 4


def _params(n_axes):
    return pltpu.CompilerParams(dimension_semantics=("arbitrary",) * n_axes,
                                vmem_limit_bytes=VMEM_LIMIT)


def _dot(a, b):
    return jnp.dot(a.astype(BF16), b.astype(BF16), preferred_element_type=F32)


def _rmsnorm(x, g):
    ms = jnp.mean(x * x, axis=-1, keepdims=True)
    return x * lax.rsqrt(ms + EPS) * g


def _ada_kernel(c_ref, w_ref, b_ref, o_ref):
    c = c_ref[...]
    o_ref[...] = _dot(jax.nn.silu(c), w_ref[...]) + b_ref[...]


def _ada(c, w, b, tn=512):
    m, d = c.shape
    n = w.shape[1]
    return pl.pallas_call(
        _ada_kernel,
        grid=(n // tn,),
        in_specs=[pl.BlockSpec((m, d), lambda j: (0, 0)),
                  pl.BlockSpec((d, tn), lambda j: (0, j)),
                  pl.BlockSpec((1, tn), lambda j: (0, j))],
        out_specs=pl.BlockSpec((m, tn), lambda j: (0, j)),
        out_shape=jax.ShapeDtypeStruct((m, n), F32),
        compiler_params=_params(1),
        name="ada_mod",
    )(c, w, b)


def _premix_common(x_ref, sh_ref, sc_ref, g_ref, win_ref, lng_ref, lnb_ref, sw):
    x = x_ref[...]
    h = _rmsnorm(x, g_ref[...]) * (1.0 + sc_ref[...]) + sh_ref[...]
    proj = _dot(h, win_ref[...])
    u_a = proj[:, :sw]
    z = jax.nn.gelu(proj[:, sw:])
    gw = z.shape[1] // 2
    u_b = z[:, :gw]
    v_b = z[:, gw:]
    hd = gw // GM_HEADS
    parts = []
    for k in range(GM_HEADS):
        vh = v_b[:, k * hd:(k + 1) * hd]
        mu = jnp.mean(vh, axis=-1, keepdims=True)
        dv = vh - mu
        var = jnp.mean(dv * dv, axis=-1, keepdims=True)
        parts.append(dv * lax.rsqrt(var + EPS))
    vn = jnp.concatenate(parts, axis=-1) * lng_ref[...] + lnb_ref[...]
    return u_a, u_b, vn


def _premix_seq_kernel(x_ref, sh_ref, sc_ref, g_ref, win_ref, lng_ref, lnb_ref,
                       ws_ref, bs_ref, ua_ref, yb_ref, *, sw):
    u_a, u_b, vn = _premix_common(x_ref, sh_ref, sc_ref, g_ref, win_ref, lng_ref, lnb_ref, sw)
    ua_ref[...] = u_a
    tm, gw = u_b.shape
    hd = gw // GM_HEADS
    vnb = vn.astype(BF16)
    for c in range(tm // CHUNK):
        rows = slice(c * CHUNK, (c + 1) * CHUNK)
        for k in range(GM_HEADS):
            cols = slice(k * hd, (k + 1) * hd)
            s = jnp.dot(ws_ref[k], vnb[rows, cols], preferred_element_type=F32)
            yb_ref[rows, cols] = u_b[rows, cols] * (s + bs_ref[:, cols])


def _premix_step_kernel(x_ref, sh_ref, sc_ref, g_ref, win_ref, lng_ref, lnb_ref,
                        w0_ref, b0_ref, ua_ref, yb_ref, vn_ref, *, sw):
    u_a, u_b, vn = _premix_common(x_ref, sh_ref, sc_ref, g_ref, win_ref, lng_ref, lnb_ref, sw)
    ua_ref[...] = u_a
    vn_ref[...] = vn
    yb_ref[...] = u_b * (vn * w0_ref[...] + b0_ref[...])


def _row_spec(tm, d):
    return pl.BlockSpec((tm, d), lambda i: (i, 0))


def _full_spec(shape):
    nd = len(shape)
    return pl.BlockSpec(shape, lambda i: (0,) * nd)


def _mod_spec(per_row, tm, d, tiles_per_batch):
    if per_row:
        return pl.BlockSpec((tm, d), lambda i: (i, 0))
    return pl.BlockSpec((None, 1, d), lambda i: (i // tiles_per_batch, 0, 0))


def _premix_seq(x, sh, sc, g, w_in, ln_g, ln_b, ws_tril, bs_full, seq_len, tm):
    t, d = x.shape
    sw = ln_g.shape[1]
    tpb = seq_len // tm
    mod = _mod_spec(False, tm, d, tpb)
    return pl.pallas_call(
        functools.partial(_premix_seq_kernel, sw=sw),
        grid=(t // tm,),
        in_specs=[_row_spec(tm, d), mod, mod, _full_spec(g.shape), _full_spec(w_in.shape),
                  _full_spec(ln_g.shape), _full_spec(ln_b.shape), _full_spec(ws_tril.shape),
                  _full_spec(bs_full.shape)],
        out_specs=[_row_spec(tm, sw), _row_spec(tm, sw)],
        out_shape=[jax.ShapeDtypeStruct((t, sw), F32), jax.ShapeDtypeStruct((t, sw), F32)],
        compiler_params=_params(1),
        name="premix_seq",
    )(x, sh, sc, g, w_in, ln_g, ln_b, ws_tril, bs_full)


def _premix_step(x, sh, sc, g, w_in, ln_g, ln_b, w0, b0):
    t, d = x.shape
    sw = ln_g.shape[1]
    mod = _mod_spec(True, t, d, 1)
    return pl.pallas_call(
        functools.partial(_premix_step_kernel, sw=sw),
        grid=(1,),
        in_specs=[_row_spec(t, d), mod, mod, _full_spec(g.shape), _full_spec(w_in.shape),
                  _full_spec(ln_g.shape), _full_spec(ln_b.shape), _full_spec(w0.shape),
                  _full_spec(b0.shape)],
        out_specs=[_row_spec(t, sw)] * 3,
        out_shape=[jax.ShapeDtypeStruct((t, sw), F32)] * 3,
        compiler_params=_params(1),
        name="premix_step",
    )(x, sh, sc, g, w_in, ln_g, ln_b, w0, b0)


def _s5_output(xs, u, cj_ref, d_ref, wglu_ref, bglu_ref, n_chunks):
    ys = [jnp.dot(xs(j).astype(BF16), cj_ref[j], preferred_element_type=F32)
          for j in range(n_chunks)]
    y = jnp.concatenate(ys, axis=-1) + d_ref[...] * u
    y = jax.nn.gelu(y)
    gate = jax.nn.sigmoid(_dot(y, wglu_ref[...]) + bglu_ref[...])
    return y * gate


def _s5_seq_kernel(u_ref, bj_ref, cj_ref, are_ref, aim_ref, d_ref, wglu_ref, bglu_ref,
                   y_ref, sre_ref, sim_ref, bu_ref, st_ref, *, nb, tl, n_chunks, unroll):
    i = pl.program_id(0)
    n_tiles = bu_ref.shape[0]
    tpc = n_tiles // n_chunks
    hpc = tpc // 2
    hw = hpc * LANES
    kw = u_ref.shape[2] // n_chunks

    @pl.when(i == 0)
    def _():
        st_ref[...] = jnp.zeros_like(st_ref)

    u_all = u_ref[...].reshape(nb * tl, u_ref.shape[2])
    ub = u_all.astype(BF16)
    for j in range(n_chunks):
        bu = jnp.dot(ub[:, j * kw:(j + 1) * kw], bj_ref[j], preferred_element_type=F32)
        for b in range(nb):
            for c in range(tpc):
                bu_ref[j * tpc + c, pl.ds(b, tl, stride=nb), :] = \
                    bu[b * tl:(b + 1) * tl, c * LANES:(c + 1) * LANES]

    for j in range(n_chunks):
        re = slice(j * 2 * hw, j * 2 * hw + hw)
        im = slice(j * 2 * hw + hw, (j + 1) * 2 * hw)
        ar = jnp.broadcast_to(are_ref[:, j * hw:(j + 1) * hw], (nb, hw))
        ai = jnp.broadcast_to(aim_ref[:, j * hw:(j + 1) * hw], (nb, hw))
        re_tiles = [j * tpc + c for c in range(hpc)]
        im_tiles = [j * tpc + hpc + c for c in range(hpc)]

        def body(s, carry, ar=ar, ai=ai, re_tiles=re_tiles, im_tiles=im_tiles):
            xr, xi = carry
            for k in range(unroll):
                r0 = pl.multiple_of((s * unroll + k) * nb, nb)
                br = jnp.concatenate([bu_ref[c, pl.ds(r0, nb), :] for c in re_tiles], axis=-1)
                bi = jnp.concatenate([bu_ref[c, pl.ds(r0, nb), :] for c in im_tiles], axis=-1)
                nr = ar * xr - ai * xi + br
                ni = ar * xi + ai * xr + bi
                for q, c in enumerate(re_tiles):
                    bu_ref[c, pl.ds(r0, nb), :] = nr[:, q * LANES:(q + 1) * LANES]
                for q, c in enumerate(im_tiles):
                    bu_ref[c, pl.ds(r0, nb), :] = ni[:, q * LANES:(q + 1) * LANES]
                xr, xi = nr, ni
            return xr, xi

        xr, xi = lax.fori_loop(0, tl // unroll, body, (st_ref[:, re], st_ref[:, im]))
        st_ref[:, re] = xr
        st_ref[:, im] = xi
        sre_ref[:, j * hw:(j + 1) * hw] = xr
        sim_ref[:, j * hw:(j + 1) * hw] = xi

    def state_chunk(j):
        return jnp.concatenate(
            [jnp.concatenate([bu_ref[j * tpc + c, pl.ds(b, tl, stride=nb), :]
                              for c in range(tpc)], axis=-1) for b in range(nb)], axis=0)

    y = _s5_output(state_chunk, u_all, cj_ref, d_ref, wglu_ref, bglu_ref, n_chunks)
    y_ref[...] = y.reshape(y_ref.shape)


def _s5_seq(u, bj, cj, a_re, a_im, d_vec, w_glu, b_glu, tl, unroll=4):
    nb, seq_len, sw = u.shape
    n_chunks, _, cw = bj.shape
    ns = a_re.shape[1]
    kern = functools.partial(_s5_seq_kernel, nb=nb, tl=tl, n_chunks=n_chunks, unroll=unroll)
    return pl.pallas_call(
        kern,
        grid=(seq_len // tl,),
        in_specs=[pl.BlockSpec((nb, tl, sw), lambda i: (0, i, 0)),
                  _full_spec(bj.shape), _full_spec(cj.shape), _full_spec(a_re.shape),
                  _full_spec(a_im.shape), _full_spec(d_vec.shape), _full_spec(w_glu.shape),
                  _full_spec(b_glu.shape)],
        out_specs=[pl.BlockSpec((nb, tl, sw), lambda i: (0, i, 0)),
                   _full_spec((nb, ns)), _full_spec((nb, ns))],
        out_shape=[jax.ShapeDtypeStruct((nb, seq_len, sw), F32),
                   jax.ShapeDtypeStruct((nb, ns), F32), jax.ShapeDtypeStruct((nb, ns), F32)],
        scratch_shapes=[pltpu.VMEM((n_chunks * cw // LANES, tl * nb, LANES), F32),
                        pltpu.VMEM((nb, n_chunks * cw), F32)],
        compiler_params=_params(1),
        name="s5_seq",
    )(u, bj, cj, a_re, a_im, d_vec, w_glu, b_glu)


def _s5_step_kernel(u_ref, hre_ref, him_ref, bj_ref, cj_ref, are_ref, aim_ref, d_ref,
                    wglu_ref, bglu_ref, y_ref, sre_ref, sim_ref, *, n_chunks):
    u = u_ref[...]
    ub = u.astype(BF16)
    kw = u.shape[1] // n_chunks
    hw = bj_ref.shape[2] // 2
    chunks = []
    for j in range(n_chunks):
        st = slice(j * hw, (j + 1) * hw)
        bu = jnp.dot(ub[:, j * kw:(j + 1) * kw], bj_ref[j], preferred_element_type=F32)
        ar, ai = are_ref[:, st], aim_ref[:, st]
        hr, hi = hre_ref[:, st], him_ref[:, st]
        xr = ar * hr - ai * hi + bu[:, :hw]
        xi = ar * hi + ai * hr + bu[:, hw:]
        sre_ref[:, st] = xr
        sim_ref[:, st] = xi
        chunks.append(jnp.concatenate([xr, xi], axis=-1))
    y_ref[...] = _s5_output(chunks.__getitem__, u, cj_ref, d_ref, wglu_ref, bglu_ref, n_chunks)


def _s5_step(u, h_re, h_im, bj, cj, a_re, a_im, d_vec, w_glu, b_glu):
    t, sw = u.shape
    ns = a_re.shape[1]
    args = (u, h_re, h_im, bj, cj, a_re, a_im, d_vec, w_glu, b_glu)
    return pl.pallas_call(
        functools.partial(_s5_step_kernel, n_chunks=bj.shape[0]),
        grid=(1,),
        in_specs=[_full_spec(a.shape) for a in args],
        out_specs=[_full_spec((t, sw)), _full_spec((t, ns)), _full_spec((t, ns))],
        out_shape=[jax.ShapeDtypeStruct((t, sw), F32), jax.ShapeDtypeStruct((t, ns), F32),
                   jax.ShapeDtypeStruct((t, ns), F32)],
        compiler_params=_params(1),
        name="s5_step",
    )(*args)


def _group_reduce(x, lane, width, op):
    n = x.shape[-1]
    s = 1
    while s < width:
        up = pltpu.roll(x, s, 1)
        dn = pltpu.roll(x, n - s, 1)
        x = op(x, jnp.where((lane & s) != 0, up, dn))
        s *= 2
    return x


def _first_max(x, lane, big):
    m = jnp.max(x, axis=-1, keepdims=True)
    return jnp.min(jnp.where(x == m, lane, big), axis=-1, keepdims=True)


def _route(logits, bias):
    rows, n = logits.shape
    lane = lax.broadcasted_iota(jnp.int32, (rows, n), 1)
    lane_f = lane.astype(F32)
    neg = jnp.float32(-jnp.inf)
    big = jnp.float32(n)
    gsz = N_EXPERTS // N_ROUTE_GROUPS
    group_start = (lane - (lane & (gsz - 1))).astype(F32)
    scores = jax.nn.sigmoid(logits)
    biased = scores + bias
    m1 = _group_reduce(biased, lane, gsz, jnp.maximum)
    first = _group_reduce(jnp.where(biased == m1, lane_f, big), lane, gsz, jnp.minimum)
    m2 = _group_reduce(jnp.where(lane_f == first, neg, biased), lane, gsz, jnp.maximum)
    gs = m1 + m2
    gsel = jnp.zeros((rows, n), F32)
    work = gs
    for _ in range(TOPK_ROUTE_GROUPS):
        hit = group_start == _first_max(work, lane_f, big)
        gsel = jnp.where(hit, 1.0, gsel)
        work = jnp.where(hit, neg, work)
    work = jnp.where(gsel > 0.0, biased, neg)
    w = jnp.zeros((rows, n), F32)
    for _ in range(TOP_K):
        hit = lane_f == _first_max(work, lane_f, big)
        w = jnp.where(hit, scores, w)
        work = jnp.where(hit, neg, work)
    return w / jnp.sum(w, axis=-1, keepdims=True) * ROUTED_SCALE


def _route_by_expert(logits, bias):
    ne, tm = logits.shape
    ng = N_ROUTE_GROUPS
    gsz = ne // ng
    neg = jnp.float32(-jnp.inf)
    shape = (ng, gsz, tm)
    scores = jax.nn.sigmoid(logits)
    biased = (scores + bias).reshape(shape)
    scores = scores.reshape(shape)
    sub = lax.broadcasted_iota(jnp.int32, shape, 1)
    grp = lax.broadcasted_iota(jnp.int32, (ng, 1, tm), 0)
    eid = lax.broadcasted_iota(jnp.int32, shape, 0) * gsz + sub
    m1 = jnp.max(biased, axis=1, keepdims=True)
    first = jnp.min(jnp.where(biased == m1, sub, gsz), axis=1, keepdims=True)
    m2 = jnp.max(jnp.where(sub == first, neg, biased), axis=1, keepdims=True)
    work = m1 + m2
    gsel = jnp.zeros((ng, 1, tm), jnp.bool_)
    for _ in range(TOPK_ROUTE_GROUPS):
        m = jnp.max(work, axis=0, keepdims=True)
        hit = grp == jnp.min(jnp.where(work == m, grp, ng), axis=0, keepdims=True)
        gsel = jnp.logical_or(gsel, hit)
        work = jnp.where(hit, neg, work)
    work = jnp.where(gsel, biased, neg)
    w = jnp.zeros(shape, F32)
    for _ in range(TOP_K):
        m = jnp.max(jnp.max(work, axis=0, keepdims=True), axis=1, keepdims=True)
        first = jnp.min(jnp.min(jnp.where(work == m, eid, ne), axis=0, keepdims=True),
                        axis=1, keepdims=True)
        hit = eid == first
        w = jnp.where(hit, scores, w)
        work = jnp.where(hit, neg, work)
    denom = jnp.sum(jnp.sum(w, axis=0, keepdims=True), axis=1, keepdims=True)
    return (w / denom * ROUTED_SCALE).reshape(ne, tm)


def _split_bf16(x):
    hi = x.astype(BF16)
    lo = (x - hi.astype(F32)).astype(BF16)
    return hi, lo


def _postmix_kernel(x_ref, ya_ref, yb_ref, gt1_ref, sh2_ref, sc2_ref, gt2_ref, g2_ref,
                    wout_ref, rwh_ref, rwl_ref, rb_ref, swg_ref, swu_ref, swd_ref,
                    xacc_ref, h2_ref, wd_ref, *, token_tiles):
    y = jnp.concatenate([ya_ref[...], yb_ref[...]], axis=-1)
    x1 = x_ref[...] + gt1_ref[...] * _dot(y, wout_ref[...])
    h2 = _rmsnorm(x1, g2_ref[...]) * (1.0 + sc2_ref[...]) + sh2_ref[...]
    hb = h2.astype(BF16)
    if token_tiles:
        tm, d = h2.shape
        nj = d // LANES
        for j in range(nj):
            h2_ref[pl.ds(j, tm, stride=nj), :] = h2[:, j * LANES:(j + 1) * LANES]
    else:
        h2_ref[...] = hb
    h_hi, h_lo = _split_bf16(h2)
    if token_tiles:
        nt = (((1,), (1,)), ((), ()))
        logits = (lax.dot_general(rwh_ref[...], h_hi, nt, preferred_element_type=F32)
                  + lax.dot_general(rwl_ref[...], h_hi, nt, preferred_element_type=F32)
                  + lax.dot_general(rwh_ref[...], h_lo, nt, preferred_element_type=F32))
        wd_ref[...] = _route_by_expert(logits, rb_ref[...])
    else:
        logits = (jnp.dot(h_hi, rwh_ref[...], preferred_element_type=F32)
                  + jnp.dot(h_hi, rwl_ref[...], preferred_element_type=F32)
                  + jnp.dot(h_lo, rwh_ref[...], preferred_element_type=F32))
        wd_ref[...] = _route(logits, rb_ref[...])
    hid = jax.nn.silu(jnp.dot(hb, swg_ref[...], preferred_element_type=F32)) * \
        jnp.dot(hb, swu_ref[...], preferred_element_type=F32)
    shared = _dot(hid, swd_ref[...])
    xacc_ref[...] = x1 + gt2_ref[...] * shared


def _postmix(x, ya, yb, gt1, sh2, sc2, gt2, g2, w_out, rw_hi, rw_lo, rb, swg, swu, swd,
             per_row, seq_len, tm, token_tiles):
    t, d = x.shape
    sw = ya.shape[1]
    nj = d // LANES
    mod = _mod_spec(per_row, tm, d, max(seq_len // tm, 1))
    weights = (g2, w_out, rw_hi, rw_lo, rb, swg, swu, swd)
    if token_tiles:
        ne = rw_hi.shape[0]
        h2_spec, h2_shape = _row_spec(tm * nj, LANES), jax.ShapeDtypeStruct((t * nj, LANES), F32)
        wd_spec, wd_shape = pl.BlockSpec((ne, tm), lambda i: (0, i)), (ne, t)
    else:
        ne = rw_hi.shape[1]
        h2_spec, h2_shape = _row_spec(tm, d), jax.ShapeDtypeStruct((t, d), BF16)
        wd_spec, wd_shape = _row_spec(tm, ne), (t, ne)
    return pl.pallas_call(
        functools.partial(_postmix_kernel, token_tiles=token_tiles),
        grid=(t // tm,),
        in_specs=[_row_spec(tm, d), _row_spec(tm, sw), _row_spec(tm, sw), mod, mod, mod, mod]
        + [_full_spec(w.shape) for w in weights],
        out_specs=[_row_spec(tm, d), h2_spec, wd_spec],
        out_shape=[jax.ShapeDtypeStruct((t, d), F32), h2_shape,
                   jax.ShapeDtypeStruct(wd_shape, F32)],
        compiler_params=_params(1),
        name="postmix",
    )(x, ya, yb, gt1, sh2, sc2, gt2, *weights)


def _moe_dense_kernel(h_ref, wd_ref, xacc_ref, gt2_ref, gf_ref, wg_ref, wu_ref, wdn_ref,
                      o_ref, acc_ref, *, final_norm):
    step = pl.program_id(1)
    per_step = wg_ref.shape[0]

    @pl.when(step == 0)
    def _():
        acc_ref[...] = jnp.zeros_like(acc_ref)

    hb = h_ref[...]
    wd = wd_ref[...]
    lane = lax.broadcasted_iota(jnp.int32, wd.shape, 1)
    total = None
    for q in range(per_step):
        col = jnp.sum(jnp.where(lane == step * per_step + q, wd, 0.0), axis=-1, keepdims=True)
        hid = jax.nn.silu(jnp.dot(hb, wg_ref[q], preferred_element_type=F32)) * \
            jnp.dot(hb, wu_ref[q], preferred_element_type=F32)
        part = col * _dot(hid, wdn_ref[q])
        total = part if total is None else total + part
    acc_ref[...] += total

    @pl.when(step == pl.num_programs(1) - 1)
    def _():
        xo = xacc_ref[...] + gt2_ref[...] * acc_ref[...]
        o_ref[...] = _rmsnorm(xo, gf_ref[...]) if final_norm else xo


def _moe_dense(h2, wd, xacc, gt2, g_final, wg, wu, wdn, per_row, seq_len, tm, final_norm):
    t, d = xacc.shape
    ne, _, eh = wg.shape
    nl = wd.shape[1]
    tpb = max(seq_len // tm, 1)
    if per_row:
        mod = pl.BlockSpec((tm, d), lambda i, e: (i, 0))
    else:
        mod = pl.BlockSpec((None, 1, d), lambda i, e: (i // tpb, 0, 0))
    row = lambda w: pl.BlockSpec((tm, w), lambda i, e: (i, 0))
    per_step = DENSE_EXPERTS_PER_STEP
    assert ne % per_step == 0
    return pl.pallas_call(
        functools.partial(_moe_dense_kernel, final_norm=final_norm),
        grid=(t // tm, ne // per_step),
        in_specs=[row(d), row(nl), row(d), mod,
                  pl.BlockSpec(g_final.shape, lambda i, e: (0, 0)),
                  pl.BlockSpec((per_step, d, eh), lambda i, e: (e, 0, 0)),
                  pl.BlockSpec((per_step, d, eh), lambda i, e: (e, 0, 0)),
                  pl.BlockSpec((per_step, eh, d), lambda i, e: (e, 0, 0))],
        out_specs=row(d),
        out_shape=jax.ShapeDtypeStruct((t, d), F32),
        scratch_shapes=[pltpu.VMEM((tm, d), F32)],
        compiler_params=_params(2),
        name="moe_dense",
    )(h2, wd, xacc, gt2, g_final, wg, wu, wdn)


TOKEN_BITS = 13


def _route_plan_kernel(wd_ref, rows_ref, wl_ref, cnt_ref, *, tg, nj):
    wt = wd_ref[...]
    sel = wt > 0.0
    lane = lax.broadcasted_iota(jnp.int32, wt.shape, 1)
    c = sel.astype(jnp.int32)
    s = 1
    while s < tg:
        c = c + jnp.where(lane >= s, pltpu.roll(c, s, 1), 0)
        s *= 2
    cnt_ref[...] = jnp.broadcast_to(c[:, tg - 1:tg], cnt_ref.shape)
    dist = lane - (c - 1)
    key = jnp.where(sel, (dist << TOKEN_BITS) | lane, -1)
    w = jnp.where(sel, wt, 0.0)
    s = 1
    while s < tg:
        leaving = (key >= 0) & (((key >> TOKEN_BITS) & s) != 0)
        key_in = pltpu.roll(key, tg - s, 1)
        w_in = pltpu.roll(w, tg - s, 1)
        arriving = (key_in >= 0) & (((key_in >> TOKEN_BITS) & s) != 0)
        key = jnp.where(arriving, key_in, jnp.where(leaving, -1, key))
        w = jnp.where(arriving, w_in, jnp.where(leaving, 0.0, w))
        s *= 2
    rows_ref[...] = jnp.where(key >= 0, key & ((1 << TOKEN_BITS) - 1), 0) * nj
    wl_ref[...] = w


def _route_plan(wd, tg, nj):
    ne, t = wd.shape
    assert ne == N_EXPERTS and tg <= (1 << TOKEN_BITS) and t % tg == 0
    g = t // tg
    grp = lambda w: pl.BlockSpec((None, N_EXPERTS, w), lambda i: (i, 0, 0))
    return pl.pallas_call(
        functools.partial(_route_plan_kernel, tg=tg, nj=nj),
        grid=(g,),
        in_specs=[pl.BlockSpec((ne, tg), lambda i: (0, i))],
        out_specs=[grp(tg), grp(tg), grp(LANES)],
        out_shape=[jax.ShapeDtypeStruct((g, N_EXPERTS, tg), jnp.int32),
                   jax.ShapeDtypeStruct((g, N_EXPERTS, tg), F32),
                   jax.ShapeDtypeStruct((g, N_EXPERTS, LANES), jnp.int32)],
        compiler_params=_params(1),
        name="route_plan",
    )(wd)


def _block_table(counts, tg, bm, nb_max):
    g, ne = counts.shape
    i32 = jnp.int32
    nblk = (counts + bm - 1) // bm
    bend = jnp.cumsum(nblk, axis=1)
    bstart = bend - nblk
    nb = bend[:, -1]
    k = jnp.arange(nb_max, dtype=i32)[None, :]
    kc = jnp.minimum(k, jnp.maximum(nb[:, None] - 1, 0))
    e_k = jnp.minimum(jnp.sum(bend[:, None, :] <= kc[:, :, None], axis=2), ne - 1).astype(i32)
    b_k = kc - jnp.take_along_axis(bstart, e_k, axis=1)
    first = (b_k == 0) & (k < nb[:, None])
    active = nblk > 0
    parity = (jnp.take_along_axis(jnp.cumsum(active.astype(i32), axis=1) - 1, e_k, axis=1)
              % MOE_WEIGHT_BUFFERS)
    ids = jnp.where(active, jnp.arange(ne, dtype=i32)[None, :], ne)
    first_active_from = jnp.flip(lax.cummin(jnp.flip(ids, 1), axis=1), 1)
    nxt = jnp.concatenate([first_active_from[:, 1:], jnp.full((g, 1), ne, i32)], axis=1)
    nxt = jnp.where(nxt >= ne, -1, nxt)
    nxt2 = jnp.where(nxt < 0, -1, jnp.take_along_axis(nxt, jnp.maximum(nxt, 0), axis=1))
    next_k = jnp.take_along_axis(nxt, e_k, axis=1)
    next2_k = jnp.take_along_axis(nxt2, e_k, axis=1)
    row = (jnp.arange(g, dtype=i32)[:, None] * ne + e_k) * (tg // bm) + b_k
    flat = lambda a: a.reshape(-1).astype(i32)
    table = (nb.astype(i32), flat(e_k), flat(first), flat(parity), flat(next_k), flat(next2_k))
    return table, flat(row)


def _moe_group_kernel(nb_ref, exp_ref, first_ref, par_ref, next_ref, next2_ref,
                      h_ref, rows_ref, wl_ref, wg_hbm, wu_hbm, wdn_hbm, o_ref,
                      wg_buf, wu_buf, wdn_buf, wsem, xt_a, xt_b, hid_a, hid_b, yt_a, yt_b,
                      *, bm, nj, batch, nb_max):
    g = pl.program_id(0)
    nb = nb_ref[g]
    tb = g * nb_max
    pitch = bm + 1
    o_ref[...] = jnp.zeros_like(o_ref)

    def weight_copies(e, slot):
        return (pltpu.make_async_copy(wg_hbm.at[e], wg_buf.at[slot], wsem.at[slot]),
                pltpu.make_async_copy(wu_hbm.at[e], wu_buf.at[slot], wsem.at[slot]),
                pltpu.make_async_copy(wdn_hbm.at[e], wdn_buf.at[slot], wsem.at[slot]))

    n_wbuf = wg_buf.shape[0]

    def handoff(k):
        @pl.when(first_ref[tb + k] == 1)
        def _():
            slot = par_ref[tb + k]
            for c in weight_copies(exp_ref[tb + k], slot):
                c.wait()
            ahead = next2_ref[tb + k]

            @pl.when(ahead >= 0)
            def _():
                for c in weight_copies(ahead, (slot + 2) % n_wbuf):
                    c.start()

    def gather_rows(k, xt):
        rows = rows_ref.at[pl.ds(pl.multiple_of(k * bm, bm), bm)]

        def emit(lo, hi):
            for m in range(lo, hi):
                r0 = pl.multiple_of(rows[m], nj)
                xt[pl.ds(m, nj, stride=pitch), :] = h_ref[pl.ds(r0, nj), :]
        return emit

    def scatter_rows(k, yt):
        rows = rows_ref.at[pl.ds(pl.multiple_of(k * bm, bm), bm)]

        def emit(lo, hi):
            for m0 in range(lo, hi, batch):
                updates = []
                for m in range(m0, min(m0 + batch, hi)):
                    r0 = pl.multiple_of(rows[m], nj)
                    updates.append((r0, o_ref[pl.ds(r0, nj), :]
                                    + yt[pl.ds(m, nj, stride=pitch), :]))
                for r0, v in reversed(updates):
                    o_ref[pl.ds(r0, nj), :] = v
        return emit

    n_chunks = nj // 2
    kc = 2 * LANES
    n_pieces = 2 * n_chunks + 1
    early = bm * MOE_EARLY_ROW_PERCENT // 100
    cuts = [early * i // (n_pieces - 1) for i in range(n_pieces)] + [bm]

    def step(up_blk, down_blk, side_work):
        piece = 0

        def side():
            nonlocal piece
            for emit in side_work:
                emit(cuts[piece], cuts[piece + 1])
            piece += 1

        gate = up = None
        for c in range(n_chunks):
            if up_blk is not None:
                k, xt, _ = up_blk
                slot = par_ref[tb + k]
                xc = jnp.concatenate([xt[j * pitch:j * pitch + bm, :] for j in (2 * c, 2 * c + 1)],
                                     axis=-1).astype(BF16)
                pg = jnp.dot(xc, wg_buf[slot, c * kc:(c + 1) * kc, :], preferred_element_type=F32)
                pu = jnp.dot(xc, wu_buf[slot, c * kc:(c + 1) * kc, :], preferred_element_type=F32)
                gate = pg if gate is None else gate + pg
                up = pu if up is None else up + pu
            side()
        for c in range(n_chunks):
            if down_blk is not None:
                k, hid, yt = down_blk
                slot = par_ref[tb + k]
                yc = jnp.dot(hid[...], wdn_buf[slot, :, c * kc:(c + 1) * kc],
                             preferred_element_type=F32)
                for q in range(2):
                    j = 2 * c + q
                    yt[j * pitch:j * pitch + bm, :] = yc[:, q * LANES:(q + 1) * LANES]
            side()
        side()
        if up_blk is not None:
            k, _, hid = up_blk
            w_col = jnp.broadcast_to(wl_ref[k], (SUBLANES, bm)).T[:, :1]
            hid[...] = (jax.nn.silu(gate) * up * w_col).astype(BF16)

    @pl.when(nb > 0)
    def _():
        for c in weight_copies(exp_ref[tb], 0):
            c.start()
        second = next_ref[tb]

        @pl.when(second >= 0)
        def _():
            for c in weight_copies(second, 1):
                c.start()
        gather_rows(0, xt_a)(0, bm)
        handoff(0)
        step((0, xt_a, hid_a), None, [gather_rows(1, xt_b)])
        handoff(1)
        step((1, xt_b, hid_b), (0, hid_a, yt_a), [gather_rows(2, xt_a)])

        def trip(i, carry):
            t = 2 * i + 1
            handoff(t + 1)
            step((t + 1, xt_a, hid_a), (t, hid_b, yt_b),
                 [gather_rows(t + 2, xt_b), scatter_rows(t - 1, yt_a)])
            handoff(t + 2)
            step((t + 2, xt_b, hid_b), (t + 1, hid_a, yt_a),
                 [gather_rows(t + 3, xt_a), scatter_rows(t, yt_b)])
            return carry

        lax.fori_loop(0, nb // 2, trip, 0)

        @pl.when(nb % 2 == 1)
        def _():
            scatter_rows(nb - 1, yt_a)(0, bm)


def _moe_group(table, h2t, rows, wl, wg, wu, wdn, tg, bm, nb_max, batch=4):
    ne, d, eh = wg.shape
    nj = d // LANES
    assert nj == SUBLANES
    g = h2t.shape[0] // (tg * nj)
    kern = functools.partial(_moe_group_kernel, bm=bm, nj=nj, batch=batch, nb_max=nb_max)
    slots = pl.BlockSpec((nb_max * bm,), lambda gi, *_: (gi,), memory_space=pltpu.SMEM)
    block_w = pl.BlockSpec((nb_max, 1, bm), lambda gi, *_: (gi, 0, 0))
    hbm = pl.BlockSpec(memory_space=pl.ANY)
    pitch_rows = nj * (bm + 1)
    grid_spec = pltpu.PrefetchScalarGridSpec(
        num_scalar_prefetch=len(table),
        grid=(g,),
        in_specs=[pl.BlockSpec((tg * nj, LANES), lambda gi, *_: (gi, 0),
                               pipeline_mode=pl.Buffered(1)),
                  slots, block_w, hbm, hbm, hbm],
        out_specs=pl.BlockSpec((tg * nj, LANES), lambda gi, *_: (gi, 0),
                               pipeline_mode=pl.Buffered(1)),
        scratch_shapes=[pltpu.VMEM((MOE_WEIGHT_BUFFERS, d, eh), BF16),
                        pltpu.VMEM((MOE_WEIGHT_BUFFERS, d, eh), BF16),
                        pltpu.VMEM((MOE_WEIGHT_BUFFERS, eh, d), BF16),
                        pltpu.SemaphoreType.DMA((MOE_WEIGHT_BUFFERS,)),
                        pltpu.VMEM((pitch_rows, LANES), F32), pltpu.VMEM((pitch_rows, LANES), F32),
                        pltpu.VMEM((bm, eh), BF16), pltpu.VMEM((bm, eh), BF16),
                        pltpu.VMEM((pitch_rows, LANES), F32), pltpu.VMEM((pitch_rows, LANES), F32)],
    )
    return pl.pallas_call(
        kern,
        grid_spec=grid_spec,
        out_shape=jax.ShapeDtypeStruct((g * tg * nj, LANES), F32),
        compiler_params=_params(1),
        name="moe_group",
    )(*table, h2t, rows, wl.reshape(g * nb_max, 1, bm), wg, wu, wdn)


def _finalize_kernel(r_ref, xacc_ref, gt2_ref, gf_ref, o_ref, *, final_norm):
    tm, d = xacc_ref.shape
    nj = d // LANES
    routed = jnp.concatenate([r_ref[pl.ds(j, tm, stride=nj), :] for j in range(nj)], axis=-1)
    xo = xacc_ref[...] + gt2_ref[...] * routed
    o_ref[...] = _rmsnorm(xo, gf_ref[...]) if final_norm else xo


def _finalize(routed, xacc, gt2, g_final, seq_len, tm, final_norm):
    t, d = xacc.shape
    nj = d // LANES
    return pl.pallas_call(
        functools.partial(_finalize_kernel, final_norm=final_norm),
        grid=(t // tm,),
        in_specs=[_row_spec(tm * nj, LANES),
                  _row_spec(tm, d), _mod_spec(False, tm, d, seq_len // tm),
                  _full_spec(g_final.shape)],
        out_specs=_row_spec(tm, d),
        out_shape=jax.ShapeDtypeStruct((t, d), F32),
        compiler_params=_params(1),
        name="finalize",
    )(routed, xacc, gt2, g_final)


def _s5_discretize(lam_re, lam_im, log_dt, b_re, b_im, c_re, c_im):
    g, n = lam_re.shape
    hg = b_re.shape[2]
    lr = jnp.minimum(lam_re.astype(F32), -1e-4)
    li = lam_im.astype(F32)
    dt = jnp.exp(log_dt.astype(F32))[:, None]
    mag = jnp.exp(lr * dt)
    abar_r, abar_i = mag * jnp.cos(li * dt), mag * jnp.sin(li * dt)
    den = lr * lr + li * li
    nr, ni = abar_r - 1.0, abar_i
    f_r = (nr * lr + ni * li) / den
    f_i = (ni * lr - nr * li) / den
    br_, bi_ = b_re.astype(F32), b_im.astype(F32)
    bbar_r = f_r[..., None] * br_ - f_i[..., None] * bi_
    bbar_i = f_r[..., None] * bi_ + f_i[..., None] * br_
    gpc = LANES // hg
    n_chunks = g // gpc
    eye = jnp.eye(gpc, dtype=F32)

    def b_chunks(bb):
        bb = bb.reshape(n_chunks, gpc, n, hg)
        m = jnp.einsum('jgnh,gk->jghkn', bb, eye)
        return m.reshape(n_chunks, gpc * hg, gpc * n)

    def c_chunks(cc):
        cc = cc.astype(F32).reshape(n_chunks, gpc, hg, n)
        m = jnp.einsum('jghn,gk->jgnkh', cc, eye)
        return m.reshape(n_chunks, gpc * n, gpc * hg)

    bj = jnp.concatenate([b_chunks(bbar_r), b_chunks(bbar_i)], axis=2).astype(BF16)
    cj = jnp.concatenate([c_chunks(c_re), -c_chunks(c_im)], axis=1).astype(BF16)
    return abar_r.reshape(1, g * n), abar_i.reshape(1, g * n), bj, cj


def kernel(x_prompt, x_sample, c_prompt, c_sample, state_ssm_re, state_ssm_im, g_norm1, g_norm2, w_ada, b_ada, w_in, w_out, ssm_lambda_re, ssm_lambda_im, ssm_log_dt, ssm_b_re, ssm_b_im, ssm_c_re, ssm_c_im, ssm_d, ssm_w_glu, ssm_b_glu, gm_ln_g, gm_ln_b, gm_w_s, gm_b_s, router_w, router_bias, moe_w_gate, moe_w_up, moe_w_down, shared_w_gate, shared_w_up, shared_w_down, g_final):
    bp, seq_len, d = x_prompt.shape
    bs, dec_len, _ = x_sample.shape
    depth = w_ada.shape[0]
    assert dec_len == 1 and seq_len % CHUNK == 0 and bp % SUBLANES == 0
    groups, n_state = ssm_lambda_re.shape[1:]
    sw = groups * SSM_GROUP
    gw = gm_ln_g.shape[1]
    hd = gw // GM_HEADS

    xp = x_prompt.reshape(bp * seq_len, d)
    xs = x_sample.reshape(bs, d)
    c_all = jnp.concatenate([c_prompt, c_sample], axis=0)
    gf = g_final.reshape(1, d)

    p_re, p_im, s_re, s_im, s_v = [], [], [], [], []
    for l in range(depth):
        mod = _ada(c_all, w_ada[l].astype(BF16), b_ada[l].reshape(1, -1))
        mod_p = [m.reshape(bp, 1, d) for m in jnp.split(mod[:bp], 6, axis=-1)]
        mod_s = jnp.split(mod[bp:], 6, axis=-1)

        g1 = g_norm1[l].reshape(1, d)
        g2 = g_norm2[l].reshape(1, d)
        w_in_b = w_in[l].astype(BF16)
        w_out_b = w_out[l].astype(BF16)
        ln_g = gm_ln_g[l].reshape(1, gw)
        ln_b = gm_ln_b[l].reshape(1, gw)
        ws_tril = (gm_w_s[l] * jnp.tril(jnp.ones((CHUNK, CHUNK), F32))[None]).astype(BF16)
        bs_full = jnp.repeat(gm_b_s[l].T, hd, axis=1)
        w0 = jnp.repeat(gm_w_s[l][:, 0, 0], hd).reshape(1, gw)
        b0 = jnp.repeat(gm_b_s[l][:, 0], hd).reshape(1, gw)
        a_re, a_im, bj, cj = _s5_discretize(ssm_lambda_re[l], ssm_lambda_im[l], ssm_log_dt[l],
                                            ssm_b_re[l], ssm_b_im[l], ssm_c_re[l], ssm_c_im[l])
        d_vec = ssm_d[l].reshape(1, sw)
        w_glu = ssm_w_glu[l].astype(BF16)
        b_glu = ssm_b_glu[l].reshape(1, sw)
        pad = LANES - N_EXPERTS
        rw = jnp.pad(router_w[l], ((0, 0), (0, pad)))
        rw_hi = rw.astype(BF16)
        rw_lo = (rw - rw_hi.astype(F32)).astype(BF16)
        rb = jnp.pad(router_bias[l], (0, pad), constant_values=-jnp.inf).reshape(1, LANES)
        rwt_hi, rwt_lo = rw_hi[:, :N_EXPERTS].T, rw_lo[:, :N_EXPERTS].T
        rb_col = router_bias[l].reshape(N_EXPERTS, 1)
        swg, swu, swd = (shared_w_gate[l].astype(BF16), shared_w_up[l].astype(BF16),
                         shared_w_down[l].astype(BF16))
        ewg, ewu, ewd = (moe_w_gate[l].astype(BF16), moe_w_up[l].astype(BF16),
                         moe_w_down[l].astype(BF16))

        tm_row = min(ROW_TILE, seq_len)
        ua, yb = _premix_seq(xp, mod_p[0], mod_p[1], g1, w_in_b, ln_g, ln_b, ws_tril, bs_full,
                             seq_len, tm=min(PREMIX_TILE, seq_len))
        ya, hpr, hpi = _s5_seq(ua.reshape(bp, seq_len, sw), bj, cj, a_re, a_im, d_vec,
                               w_glu, b_glu, tl=min(SCAN_TILE, seq_len))
        xacc, h2, wd = _postmix(xp, ya.reshape(bp * seq_len, sw), yb, mod_p[2], mod_p[3],
                                mod_p[4], mod_p[5], g2, w_out_b, rwt_hi, rwt_lo, rb_col,
                                swg, swu, swd,
                                per_row=False, seq_len=seq_len, tm=tm_row, token_tiles=True)
        last = l == depth - 1
        tg = min(MOE_GROUP, bp * seq_len)
        assert (bp * seq_len) % tg == 0
        bm = min(MOE_BLOCK, tg)
        per_tile = max(SMEM_TILE // bm, 1)
        nb_max = -(-(tg * TOP_K // bm + N_EXPERTS + 3) // per_tile) * per_tile
        slot_rows, slot_w, counts = _route_plan(wd, tg, d // LANES)
        table, block_row = _block_table(counts[:, :, 0], tg, bm, nb_max)
        by_block = lambda a: jnp.take(a.reshape(-1, bm), block_row, axis=0).reshape(-1)
        routed = _moe_group(table, h2, by_block(slot_rows), by_block(slot_w),
                            ewg, ewu, ewd, tg, bm, nb_max)
        xp = _finalize(routed, xacc, mod_p[5], gf, seq_len, tm_row, final_norm=last)
        p_re.append(hpr.reshape(bp, groups, n_state))
        p_im.append(hpi.reshape(bp, groups, n_state))

        ua, yb, vn = _premix_step(xs, mod_s[0], mod_s[1], g1, w_in_b, ln_g, ln_b, w0, b0)
        ya, hsr, hsi = _s5_step(ua, state_ssm_re[l].reshape(bs, groups * n_state),
                                state_ssm_im[l].reshape(bs, groups * n_state),
                                bj, cj, a_re, a_im, d_vec, w_glu, b_glu)
        xacc, h2, wd = _postmix(xs, ya, yb, mod_s[2], mod_s[3], mod_s[4], mod_s[5], g2, w_out_b,
                                rw_hi, rw_lo, rb, swg, swu, swd, per_row=True, seq_len=1, tm=bs,
                                token_tiles=False)
        xs = _moe_dense(h2, wd, xacc, mod_s[5], gf, ewg, ewu, ewd, per_row=True, seq_len=1,
                        tm=bs, final_norm=last)
        s_re.append(hsr.reshape(bs, groups, n_state))
        s_im.append(hsi.reshape(bs, groups, n_state))
        s_v.append(vn.reshape(bs, 1, gw))

    y_prompt = xp.reshape(bp, seq_len, d)
    y_sample = xs.reshape(bs, 1, d)
    return (y_prompt, y_sample, jnp.stack(p_re), jnp.stack(p_im), jnp.stack(s_re),
            jnp.stack(s_im), jnp.stack(s_v))
```

```python
import functools

import jax
import jax.numpy as jnp
from jax import lax
from jax.experimental import pallas as pl
from jax.experimental.pallas import tpu as pltpu

F32 = jnp.float32
BF16 = jnp.bfloat16

EPS = 1e-6
SSM_GROUP = 16
SSM_STATE = 64
GM_HEADS = 4
CHUNK = 128
N_EXPERTS = 64
TOP_K = 8
N_ROUTE_GROUPS = 8
TOPK_ROUTE_GROUPS = 4
ROUTED_SCALE = 2.5

LANES = 128
SUBLANES = 8
VMEM_LIMIT = 48 * 1024 * 1024
ROW_TILE = 512
PREMIX_TILE = 1024
DENSE_EXPERTS_PER_STEP = 8
SCAN_TILE = 128
MOE_GROUP = 4096
SMEM_TILE = 1024
MOE_BLOCK = 128
MOE_EARLY_ROW_PERCENT = 56
MOE_WEIGHT_BUFFERS = 4


def _params(n_axes):
    return pltpu.CompilerParams(dimension_semantics=("arbitrary",) * n_axes,
                                vmem_limit_bytes=VMEM_LIMIT)


def _dot(a, b):
    return jnp.dot(a.astype(BF16), b.astype(BF16), preferred_element_type=F32)


def _rmsnorm(x, g):
    ms = jnp.mean(x * x, axis=-1, keepdims=True)
    return x * lax.rsqrt(ms + EPS) * g


def _ada_kernel(c_ref, w_ref, b_ref, o_ref):
    c = c_ref[...]
    o_ref[...] = _dot(jax.nn.silu(c), w_ref[...]) + b_ref[...]


def _ada(c, w, b, tn=512):
    m, d = c.shape
    n = w.shape[1]
    return pl.pallas_call(
        _ada_kernel,
        grid=(n // tn,),
        in_specs=[pl.BlockSpec((m, d), lambda j: (0, 0)),
                  pl.BlockSpec((d, tn), lambda j: (0, j)),
                  pl.BlockSpec((1, tn), lambda j: (0, j))],
        out_specs=pl.BlockSpec((m, tn), lambda j: (0, j)),
        out_shape=jax.ShapeDtypeStruct((m, n), F32),
        compiler_params=_params(1),
        name="ada_mod",
    )(c, w, b)


def _premix_common(x_ref, sh_ref, sc_ref, g_ref, win_ref, lng_ref, lnb_ref, sw):
    x = x_ref[...]
    h = _rmsnorm(x, g_ref[...]) * (1.0 + sc_ref[...]) + sh_ref[...]
    proj = _dot(h, win_ref[...])
    u_a = proj[:, :sw]
    z = jax.nn.gelu(proj[:, sw:])
    gw = z.shape[1] // 2
    u_b = z[:, :gw]
    v_b = z[:, gw:]
    hd = gw // GM_HEADS
    parts = []
    for k in range(GM_HEADS):
        vh = v_b[:, k * hd:(k + 1) * hd]
        mu = jnp.mean(vh, axis=-1, keepdims=True)
        dv = vh - mu
        var = jnp.mean(dv * dv, axis=-1, keepdims=True)
        parts.append(dv * lax.rsqrt(var + EPS))
    vn = jnp.concatenate(parts, axis=-1) * lng_ref[...] + lnb_ref[...]
    return u_a, u_b, vn


def _premix_seq_kernel(x_ref, sh_ref, sc_ref, g_ref, win_ref, lng_ref, lnb_ref,
                       ws_ref, bs_ref, ua_ref, yb_ref, *, sw):
    u_a, u_b, vn = _premix_common(x_ref, sh_ref, sc_ref, g_ref, win_ref, lng_ref, lnb_ref, sw)
    ua_ref[...] = u_a
    tm, gw = u_b.shape
    hd = gw // GM_HEADS
    vnb = vn.astype(BF16)
    for c in range(tm // CHUNK):
        rows = slice(c * CHUNK, (c + 1) * CHUNK)
        for k in range(GM_HEADS):
            cols = slice(k * hd, (k + 1) * hd)
            s = jnp.dot(ws_ref[k], vnb[rows, cols], preferred_element_type=F32)
            yb_ref[rows, cols] = u_b[rows, cols] * (s + bs_ref[:, cols])


def _premix_step_kernel(x_ref, sh_ref, sc_ref, g_ref, win_ref, lng_ref, lnb_ref,
                        w0_ref, b0_ref, ua_ref, yb_ref, vn_ref, *, sw):
    u_a, u_b, vn = _premix_common(x_ref, sh_ref, sc_ref, g_ref, win_ref, lng_ref, lnb_ref, sw)
    ua_ref[...] = u_a
    vn_ref[...] = vn
    yb_ref[...] = u_b * (vn * w0_ref[...] + b0_ref[...])


def _row_spec(tm, d):
    return pl.BlockSpec((tm, d), lambda i: (i, 0))


def _full_spec(shape):
    nd = len(shape)
    return pl.BlockSpec(shape, lambda i: (0,) * nd)


def _mod_spec(per_row, tm, d, tiles_per_batch):
    if per_row:
        return pl.BlockSpec((tm, d), lambda i: (i, 0))
    return pl.BlockSpec((None, 1, d), lambda i: (i // tiles_per_batch, 0, 0))


def _premix_seq(x, sh, sc, g, w_in, ln_g, ln_b, ws_tril, bs_full, seq_len, tm):
    t, d = x.shape
    sw = ln_g.shape[1]
    tpb = seq_len // tm
    mod = _mod_spec(False, tm, d, tpb)
    return pl.pallas_call(
        functools.partial(_premix_seq_kernel, sw=sw),
        grid=(t // tm,),
        in_specs=[_row_spec(tm, d), mod, mod, _full_spec(g.shape), _full_spec(w_in.shape),
                  _full_spec(ln_g.shape), _full_spec(ln_b.shape), _full_spec(ws_tril.shape),
                  _full_spec(bs_full.shape)],
        out_specs=[_row_spec(tm, sw), _row_spec(tm, sw)],
        out_shape=[jax.ShapeDtypeStruct((t, sw), F32), jax.ShapeDtypeStruct((t, sw), F32)],
        compiler_params=_params(1),
        name="premix_seq",
    )(x, sh, sc, g, w_in, ln_g, ln_b, ws_tril, bs_full)


def _premix_step(x, sh, sc, g, w_in, ln_g, ln_b, w0, b0):
    t, d = x.shape
    sw = ln_g.shape[1]
    mod = _mod_spec(True, t, d, 1)
    return pl.pallas_call(
        functools.partial(_premix_step_kernel, sw=sw),
        grid=(1,),
        in_specs=[_row_spec(t, d), mod, mod, _full_spec(g.shape), _full_spec(w_in.shape),
                  _full_spec(ln_g.shape), _full_spec(ln_b.shape), _full_spec(w0.shape),
                  _full_spec(b0.shape)],
        out_specs=[_row_spec(t, sw)] * 3,
        out_shape=[jax.ShapeDtypeStruct((t, sw), F32)] * 3,
        compiler_params=_params(1),
        name="premix_step",
    )(x, sh, sc, g, w_in, ln_g, ln_b, w0, b0)


def _s5_output(xs, u, cj_ref, d_ref, wglu_ref, bglu_ref, n_chunks):
    ys = [jnp.dot(xs(j).astype(BF16), cj_ref[j], preferred_element_type=F32)
          for j in range(n_chunks)]
    y = jnp.concatenate(ys, axis=-1) + d_ref[...] * u
    y = jax.nn.gelu(y)
    gate = jax.nn.sigmoid(_dot(y, wglu_ref[...]) + bglu_ref[...])
    return y * gate


def _s5_seq_kernel(u_ref, bj_ref, cj_ref, are_ref, aim_ref, d_ref, wglu_ref, bglu_ref,
                   y_ref, sre_ref, sim_ref, bu_ref, st_ref, *, nb, tl, n_chunks, unroll):
    i = pl.program_id(0)
    n_tiles = bu_ref.shape[0]
    tpc = n_tiles // n_chunks
    hpc = tpc // 2
    hw = hpc * LANES
    kw = u_ref.shape[2] // n_chunks

    @pl.when(i == 0)
    def _():
        st_ref[...] = jnp.zeros_like(st_ref)

    u_all = u_ref[...].reshape(nb * tl, u_ref.shape[2])
    ub = u_all.astype(BF16)
    for j in range(n_chunks):
        bu = jnp.dot(ub[:, j * kw:(j + 1) * kw], bj_ref[j], preferred_element_type=F32)
        for b in range(nb):
            for c in range(tpc):
                bu_ref[j * tpc + c, pl.ds(b, tl, stride=nb), :] = \
                    bu[b * tl:(b + 1) * tl, c * LANES:(c + 1) * LANES]

    for j in range(n_chunks):
        re = slice(j * 2 * hw, j * 2 * hw + hw)
        im = slice(j * 2 * hw + hw, (j + 1) * 2 * hw)
        ar = jnp.broadcast_to(are_ref[:, j * hw:(j + 1) * hw], (nb, hw))
        ai = jnp.broadcast_to(aim_ref[:, j * hw:(j + 1) * hw], (nb, hw))
        re_tiles = [j * tpc + c for c in range(hpc)]
        im_tiles = [j * tpc + hpc + c for c in range(hpc)]

        def body(s, carry, ar=ar, ai=ai, re_tiles=re_tiles, im_tiles=im_tiles):
            xr, xi = carry
            for k in range(unroll):
                r0 = pl.multiple_of((s * unroll + k) * nb, nb)
                br = jnp.concatenate([bu_ref[c, pl.ds(r0, nb), :] for c in re_tiles], axis=-1)
                bi = jnp.concatenate([bu_ref[c, pl.ds(r0, nb), :] for c in im_tiles], axis=-1)
                nr = ar * xr - ai * xi + br
                ni = ar * xi + ai * xr + bi
                for q, c in enumerate(re_tiles):
                    bu_ref[c, pl.ds(r0, nb), :] = nr[:, q * LANES:(q + 1) * LANES]
                for q, c in enumerate(im_tiles):
                    bu_ref[c, pl.ds(r0, nb), :] = ni[:, q * LANES:(q + 1) * LANES]
                xr, xi = nr, ni
            return xr, xi

        xr, xi = lax.fori_loop(0, tl // unroll, body, (st_ref[:, re], st_ref[:, im]))
        st_ref[:, re] = xr
        st_ref[:, im] = xi
        sre_ref[:, j * hw:(j + 1) * hw] = xr
        sim_ref[:, j * hw:(j + 1) * hw] = xi

    def state_chunk(j):
        return jnp.concatenate(
            [jnp.concatenate([bu_ref[j * tpc + c, pl.ds(b, tl, stride=nb), :]
                              for c in range(tpc)], axis=-1) for b in range(nb)], axis=0)

    y = _s5_output(state_chunk, u_all, cj_ref, d_ref, wglu_ref, bglu_ref, n_chunks)
    y_ref[...] = y.reshape(y_ref.shape)


def _s5_seq(u, bj, cj, a_re, a_im, d_vec, w_glu, b_glu, tl, unroll=4):
    nb, seq_len, sw = u.shape
    n_chunks, _, cw = bj.shape
    ns = a_re.shape[1]
    kern = functools.partial(_s5_seq_kernel, nb=nb, tl=tl, n_chunks=n_chunks, unroll=unroll)
    return pl.pallas_call(
        kern,
        grid=(seq_len // tl,),
        in_specs=[pl.BlockSpec((nb, tl, sw), lambda i: (0, i, 0)),
                  _full_spec(bj.shape), _full_spec(cj.shape), _full_spec(a_re.shape),
                  _full_spec(a_im.shape), _full_spec(d_vec.shape), _full_spec(w_glu.shape),
                  _full_spec(b_glu.shape)],
        out_specs=[pl.BlockSpec((nb, tl, sw), lambda i: (0, i, 0)),
                   _full_spec((nb, ns)), _full_spec((nb, ns))],
        out_shape=[jax.ShapeDtypeStruct((nb, seq_len, sw), F32),
                   jax.ShapeDtypeStruct((nb, ns), F32), jax.ShapeDtypeStruct((nb, ns), F32)],
        scratch_shapes=[pltpu.VMEM((n_chunks * cw // LANES, tl * nb, LANES), F32),
                        pltpu.VMEM((nb, n_chunks * cw), F32)],
        compiler_params=_params(1),
        name="s5_seq",
    )(u, bj, cj, a_re, a_im, d_vec, w_glu, b_glu)


def _s5_step_kernel(u_ref, hre_ref, him_ref, bj_ref, cj_ref, are_ref, aim_ref, d_ref,
                    wglu_ref, bglu_ref, y_ref, sre_ref, sim_ref, *, n_chunks):
    u = u_ref[...]
    ub = u.astype(BF16)
    kw = u.shape[1] // n_chunks
    hw = bj_ref.shape[2] // 2
    chunks = []
    for j in range(n_chunks):
        st = slice(j * hw, (j + 1) * hw)
        bu = jnp.dot(ub[:, j * kw:(j + 1) * kw], bj_ref[j], preferred_element_type=F32)
        ar, ai = are_ref[:, st], aim_ref[:, st]
        hr, hi = hre_ref[:, st], him_ref[:, st]
        xr = ar * hr - ai * hi + bu[:, :hw]
        xi = ar * hi + ai * hr + bu[:, hw:]
        sre_ref[:, st] = xr
        sim_ref[:, st] = xi
        chunks.append(jnp.concatenate([xr, xi], axis=-1))
    y_ref[...] = _s5_output(chunks.__getitem__, u, cj_ref, d_ref, wglu_ref, bglu_ref, n_chunks)


def _s5_step(u, h_re, h_im, bj, cj, a_re, a_im, d_vec, w_glu, b_glu):
    t, sw = u.shape
    ns = a_re.shape[1]
    args = (u, h_re, h_im, bj, cj, a_re, a_im, d_vec, w_glu, b_glu)
    return pl.pallas_call(
        functools.partial(_s5_step_kernel, n_chunks=bj.shape[0]),
        grid=(1,),
        in_specs=[_full_spec(a.shape) for a in args],
        out_specs=[_full_spec((t, sw)), _full_spec((t, ns)), _full_spec((t, ns))],
        out_shape=[jax.ShapeDtypeStruct((t, sw), F32), jax.ShapeDtypeStruct((t, ns), F32),
                   jax.ShapeDtypeStruct((t, ns), F32)],
        compiler_params=_params(1),
        name="s5_step",
    )(*args)


def _group_reduce(x, lane, width, op):
    n = x.shape[-1]
    s = 1
    while s < width:
        up = pltpu.roll(x, s, 1)
        dn = pltpu.roll(x, n - s, 1)
        x = op(x, jnp.where((lane & s) != 0, up, dn))
        s *= 2
    return x


def _first_max(x, lane, big):
    m = jnp.max(x, axis=-1, keepdims=True)
    return jnp.min(jnp.where(x == m, lane, big), axis=-1, keepdims=True)


def _route(logits, bias):
    rows, n = logits.shape
    lane = lax.broadcasted_iota(jnp.int32, (rows, n), 1)
    lane_f = lane.astype(F32)
    neg = jnp.float32(-jnp.inf)
    big = jnp.float32(n)
    gsz = N_EXPERTS // N_ROUTE_GROUPS
    group_start = (lane - (lane & (gsz - 1))).astype(F32)
    scores = jax.nn.sigmoid(logits)
    biased = scores + bias
    m1 = _group_reduce(biased, lane, gsz, jnp.maximum)
    first = _group_reduce(jnp.where(biased == m1, lane_f, big), lane, gsz, jnp.minimum)
    m2 = _group_reduce(jnp.where(lane_f == first, neg, biased), lane, gsz, jnp.maximum)
    gs = m1 + m2
    gsel = jnp.zeros((rows, n), F32)
    work = gs
    for _ in range(TOPK_ROUTE_GROUPS):
        hit = group_start == _first_max(work, lane_f, big)
        gsel = jnp.where(hit, 1.0, gsel)
        work = jnp.where(hit, neg, work)
    work = jnp.where(gsel > 0.0, biased, neg)
    w = jnp.zeros((rows, n), F32)
    for _ in range(TOP_K):
        hit = lane_f == _first_max(work, lane_f, big)
        w = jnp.where(hit, scores, w)
        work = jnp.where(hit, neg, work)
    return w / jnp.sum(w, axis=-1, keepdims=True) * ROUTED_SCALE


def _route_by_expert(logits, bias):
    ne, tm = logits.shape
    ng = N_ROUTE_GROUPS
    gsz = ne // ng
    neg = jnp.float32(-jnp.inf)
    shape = (ng, gsz, tm)
    scores = jax.nn.sigmoid(logits)
    biased = (scores + bias).reshape(shape)
    scores = scores.reshape(shape)
    sub = lax.broadcasted_iota(jnp.int32, shape, 1)
    grp = lax.broadcasted_iota(jnp.int32, (ng, 1, tm), 0)
    eid = lax.broadcasted_iota(jnp.int32, shape, 0) * gsz + sub
    m1 = jnp.max(biased, axis=1, keepdims=True)
    first = jnp.min(jnp.where(biased == m1, sub, gsz), axis=1, keepdims=True)
    m2 = jnp.max(jnp.where(sub == first, neg, biased), axis=1, keepdims=True)
    work = m1 + m2
    gsel = jnp.zeros((ng, 1, tm), jnp.bool_)
    for _ in range(TOPK_ROUTE_GROUPS):
        m = jnp.max(work, axis=0, keepdims=True)
        hit = grp == jnp.min(jnp.where(work == m, grp, ng), axis=0, keepdims=True)
        gsel = jnp.logical_or(gsel, hit)
        work = jnp.where(hit, neg, work)
    work = jnp.where(gsel, biased, neg)
    w = jnp.zeros(shape, F32)
    for _ in range(TOP_K):
        m = jnp.max(jnp.max(work, axis=0, keepdims=True), axis=1, keepdims=True)
        first = jnp.min(jnp.min(jnp.where(work == m, eid, ne), axis=0, keepdims=True),
                        axis=1, keepdims=True)
        hit = eid == first
        w = jnp.where(hit, scores, w)
        work = jnp.where(hit, neg, work)
    denom = jnp.sum(jnp.sum(w, axis=0, keepdims=True), axis=1, keepdims=True)
    return (w / denom * ROUTED_SCALE).reshape(ne, tm)


def _split_bf16(x):
    hi = x.astype(BF16)
    lo = (x - hi.astype(F32)).astype(BF16)
    return hi, lo


def _postmix_kernel(x_ref, ya_ref, yb_ref, gt1_ref, sh2_ref, sc2_ref, gt2_ref, g2_ref,
                    wout_ref, rwh_ref, rwl_ref, rb_ref, swg_ref, swu_ref, swd_ref,
                    xacc_ref, h2_ref, wd_ref, *, token_tiles):
    y = jnp.concatenate([ya_ref[...], yb_ref[...]], axis=-1)
    x1 = x_ref[...] + gt1_ref[...] * _dot(y, wout_ref[...])
    h2 = _rmsnorm(x1, g2_ref[...]) * (1.0 + sc2_ref[...]) + sh2_ref[...]
    hb = h2.astype(BF16)
    if token_tiles:
        tm, d = h2.shape
        nj = d // LANES
        for j in range(nj):
            h2_ref[pl.ds(j, tm, stride=nj), :] = h2[:, j * LANES:(j + 1) * LANES]
    else:
        h2_ref[...] = hb
    h_hi, h_lo = _split_bf16(h2)
    if token_tiles:
        nt = (((1,), (1,)), ((), ()))
        logits = (lax.dot_general(rwh_ref[...], h_hi, nt, preferred_element_type=F32)
                  + lax.dot_general(rwl_ref[...], h_hi, nt, preferred_element_type=F32)
                  + lax.dot_general(rwh_ref[...], h_lo, nt, preferred_element_type=F32))
        wd_ref[...] = _route_by_expert(logits, rb_ref[...])
    else:
        logits = (jnp.dot(h_hi, rwh_ref[...], preferred_element_type=F32)
                  + jnp.dot(h_hi, rwl_ref[...], preferred_element_type=F32)
                  + jnp.dot(h_lo, rwh_ref[...], preferred_element_type=F32))
        wd_ref[...] = _route(logits, rb_ref[...])
    hid = jax.nn.silu(jnp.dot(hb, swg_ref[...], preferred_element_type=F32)) * \
        jnp.dot(hb, swu_ref[...], preferred_element_type=F32)
    shared = _dot(hid, swd_ref[...])
    xacc_ref[...] = x1 + gt2_ref[...] * shared


def _postmix(x, ya, yb, gt1, sh2, sc2, gt2, g2, w_out, rw_hi, rw_lo, rb, swg, swu, swd,
             per_row, seq_len, tm, token_tiles):
    t, d = x.shape
    sw = ya.shape[1]
    nj = d // LANES
    mod = _mod_spec(per_row, tm, d, max(seq_len // tm, 1))
    weights = (g2, w_out, rw_hi, rw_lo, rb, swg, swu, swd)
    if token_tiles:
        ne = rw_hi.shape[0]
        h2_spec, h2_shape = _row_spec(tm * nj, LANES), jax.ShapeDtypeStruct((t * nj, LANES), F32)
        wd_spec, wd_shape = pl.BlockSpec((ne, tm), lambda i: (0, i)), (ne, t)
    else:
        ne = rw_hi.shape[1]
        h2_spec, h2_shape = _row_spec(tm, d), jax.ShapeDtypeStruct((t, d), BF16)
        wd_spec, wd_shape = _row_spec(tm, ne), (t, ne)
    return pl.pallas_call(
        functools.partial(_postmix_kernel, token_tiles=token_tiles),
        grid=(t // tm,),
        in_specs=[_row_spec(tm, d), _row_spec(tm, sw), _row_spec(tm, sw), mod, mod, mod, mod]
        + [_full_spec(w.shape) for w in weights],
        out_specs=[_row_spec(tm, d), h2_spec, wd_spec],
        out_shape=[jax.ShapeDtypeStruct((t, d), F32), h2_shape,
                   jax.ShapeDtypeStruct(wd_shape, F32)],
        compiler_params=_params(1),
        name="postmix",
    )(x, ya, yb, gt1, sh2, sc2, gt2, *weights)


def _moe_dense_kernel(h_ref, wd_ref, xacc_ref, gt2_ref, gf_ref, wg_ref, wu_ref, wdn_ref,
                      o_ref, acc_ref, *, final_norm):
    step = pl.program_id(1)
    per_step = wg_ref.shape[0]

    @pl.when(step == 0)
    def _():
        acc_ref[...] = jnp.zeros_like(acc_ref)

    hb = h_ref[...]
    wd = wd_ref[...]
    lane = lax.broadcasted_iota(jnp.int32, wd.shape, 1)
    total = None
    for q in range(per_step):
        col = jnp.sum(jnp.where(lane == step * per_step + q, wd, 0.0), axis=-1, keepdims=True)
        hid = jax.nn.silu(jnp.dot(hb, wg_ref[q], preferred_element_type=F32)) * \
            jnp.dot(hb, wu_ref[q], preferred_element_type=F32)
        part = col * _dot(hid, wdn_ref[q])
        total = part if total is None else total + part
    acc_ref[...] += total

    @pl.when(step == pl.num_programs(1) - 1)
    def _():
        xo = xacc_ref[...] + gt2_ref[...] * acc_ref[...]
        o_ref[...] = _rmsnorm(xo, gf_ref[...]) if final_norm else xo


def _moe_dense(h2, wd, xacc, gt2, g_final, wg, wu, wdn, per_row, seq_len, tm, final_norm):
    t, d = xacc.shape
    ne, _, eh = wg.shape
    nl = wd.shape[1]
    tpb = max(seq_len // tm, 1)
    if per_row:
        mod = pl.BlockSpec((tm, d), lambda i, e: (i, 0))
    else:
        mod = pl.BlockSpec((None, 1, d), lambda i, e: (i // tpb, 0, 0))
    row = lambda w: pl.BlockSpec((tm, w), lambda i, e: (i, 0))
    per_step = DENSE_EXPERTS_PER_STEP
    assert ne % per_step == 0
    return pl.pallas_call(
        functools.partial(_moe_dense_kernel, final_norm=final_norm),
        grid=(t // tm, ne // per_step),
        in_specs=[row(d), row(nl), row(d), mod,
                  pl.BlockSpec(g_final.shape, lambda i, e: (0, 0)),
                  pl.BlockSpec((per_step, d, eh), lambda i, e: (e, 0, 0)),
                  pl.BlockSpec((per_step, d, eh), lambda i, e: (e, 0, 0)),
                  pl.BlockSpec((per_step, eh, d), lambda i, e: (e, 0, 0))],
        out_specs=row(d),
        out_shape=jax.ShapeDtypeStruct((t, d), F32),
        scratch_shapes=[pltpu.VMEM((tm, d), F32)],
        compiler_params=_params(2),
        name="moe_dense",
    )(h2, wd, xacc, gt2, g_final, wg, wu, wdn)


TOKEN_BITS = 13


def _route_plan_kernel(wd_ref, rows_ref, wl_ref, cnt_ref, *, tg, nj):
    wt = wd_ref[...]
    sel = wt > 0.0
    lane = lax.broadcasted_iota(jnp.int32, wt.shape, 1)
    c = sel.astype(jnp.int32)
    s = 1
    while s < tg:
        c = c + jnp.where(lane >= s, pltpu.roll(c, s, 1), 0)
        s *= 2
    cnt_ref[...] = jnp.broadcast_to(c[:, tg - 1:tg], cnt_ref.shape)
    dist = lane - (c - 1)
    key = jnp.where(sel, (dist << TOKEN_BITS) | lane, -1)
    w = jnp.where(sel, wt, 0.0)
    s = 1
    while s < tg:
        leaving = (key >= 0) & (((key >> TOKEN_BITS) & s) != 0)
        key_in = pltpu.roll(key, tg - s, 1)
        w_in = pltpu.roll(w, tg - s, 1)
        arriving = (key_in >= 0) & (((key_in >> TOKEN_BITS) & s) != 0)
        key = jnp.where(arriving, key_in, jnp.where(leaving, -1, key))
        w = jnp.where(arriving, w_in, jnp.where(leaving, 0.0, w))
        s *= 2
    rows_ref[...] = jnp.where(key >= 0, key & ((1 << TOKEN_BITS) - 1), 0) * nj
    wl_ref[...] = w


def _route_plan(wd, tg, nj):
    ne, t = wd.shape
    assert ne == N_EXPERTS and tg <= (1 << TOKEN_BITS) and t % tg == 0
    g = t // tg
    grp = lambda w: pl.BlockSpec((None, N_EXPERTS, w), lambda i: (i, 0, 0))
    return pl.pallas_call(
        functools.partial(_route_plan_kernel, tg=tg, nj=nj),
        grid=(g,),
        in_specs=[pl.BlockSpec((ne, tg), lambda i: (0, i))],
        out_specs=[grp(tg), grp(tg), grp(LANES)],
        out_shape=[jax.ShapeDtypeStruct((g, N_EXPERTS, tg), jnp.int32),
                   jax.ShapeDtypeStruct((g, N_EXPERTS, tg), F32),
                   jax.ShapeDtypeStruct((g, N_EXPERTS, LANES), jnp.int32)],
        compiler_params=_params(1),
        name="route_plan",
    )(wd)


def _block_table(counts, tg, bm, nb_max):
    g, ne = counts.shape
    i32 = jnp.int32
    nblk = (counts + bm - 1) // bm
    bend = jnp.cumsum(nblk, axis=1)
    bstart = bend - nblk
    nb = bend[:, -1]
    k = jnp.arange(nb_max, dtype=i32)[None, :]
    kc = jnp.minimum(k, jnp.maximum(nb[:, None] - 1, 0))
    e_k = jnp.minimum(jnp.sum(bend[:, None, :] <= kc[:, :, None], axis=2), ne - 1).astype(i32)
    b_k = kc - jnp.take_along_axis(bstart, e_k, axis=1)
    first = (b_k == 0) & (k < nb[:, None])
    active = nblk > 0
    parity = (jnp.take_along_axis(jnp.cumsum(active.astype(i32), axis=1) - 1, e_k, axis=1)
              % MOE_WEIGHT_BUFFERS)
    ids = jnp.where(active, jnp.arange(ne, dtype=i32)[None, :], ne)
    first_active_from = jnp.flip(lax.cummin(jnp.flip(ids, 1), axis=1), 1)
    nxt = jnp.concatenate([first_active_from[:, 1:], jnp.full((g, 1), ne, i32)], axis=1)
    nxt = jnp.where(nxt >= ne, -1, nxt)
    nxt2 = jnp.where(nxt < 0, -1, jnp.take_along_axis(nxt, jnp.maximum(nxt, 0), axis=1))
    next_k = jnp.take_along_axis(nxt, e_k, axis=1)
    next2_k = jnp.take_along_axis(nxt2, e_k, axis=1)
    row = (jnp.arange(g, dtype=i32)[:, None] * ne + e_k) * (tg // bm) + b_k
    flat = lambda a: a.reshape(-1).astype(i32)
    table = (nb.astype(i32), flat(e_k), flat(first), flat(parity), flat(next_k), flat(next2_k))
    return table, flat(row)


def _moe_group_kernel(nb_ref, exp_ref, first_ref, par_ref, next_ref, next2_ref,
                      h_ref, rows_ref, wl_ref, wg_hbm, wu_hbm, wdn_hbm, o_ref,
                      wg_buf, wu_buf, wdn_buf, wsem, xt_a, xt_b, hid_a, hid_b, yt_a, yt_b,
                      *, bm, nj, batch, nb_max):
    g = pl.program_id(0)
    nb = nb_ref[g]
    tb = g * nb_max
    pitch = bm + 1
    o_ref[...] = jnp.zeros_like(o_ref)

    def weight_copies(e, slot):
        return (pltpu.make_async_copy(wg_hbm.at[e], wg_buf.at[slot], wsem.at[slot]),
                pltpu.make_async_copy(wu_hbm.at[e], wu_buf.at[slot], wsem.at[slot]),
                pltpu.make_async_copy(wdn_hbm.at[e], wdn_buf.at[slot], wsem.at[slot]))

    n_wbuf = wg_buf.shape[0]

    def handoff(k):
        @pl.when(first_ref[tb + k] == 1)
        def _():
            slot = par_ref[tb + k]
            for c in weight_copies(exp_ref[tb + k], slot):
                c.wait()
            ahead = next2_ref[tb + k]

            @pl.when(ahead >= 0)
            def _():
                for c in weight_copies(ahead, (slot + 2) % n_wbuf):
                    c.start()

    def gather_rows(k, xt):
        rows = rows_ref.at[pl.ds(pl.multiple_of(k * bm, bm), bm)]

        def emit(lo, hi):
            for m in range(lo, hi):
                r0 = pl.multiple_of(rows[m], nj)
                xt[pl.ds(m, nj, stride=pitch), :] = h_ref[pl.ds(r0, nj), :]
        return emit

    def scatter_rows(k, yt):
        rows = rows_ref.at[pl.ds(pl.multiple_of(k * bm, bm), bm)]

        def emit(lo, hi):
            for m0 in range(lo, hi, batch):
                updates = []
                for m in range(m0, min(m0 + batch, hi)):
                    r0 = pl.multiple_of(rows[m], nj)
                    updates.append((r0, o_ref[pl.ds(r0, nj), :]
                                    + yt[pl.ds(m, nj, stride=pitch), :]))
                for r0, v in reversed(updates):
                    o_ref[pl.ds(r0, nj), :] = v
        return emit

    n_chunks = nj // 2
    kc = 2 * LANES
    n_pieces = 2 * n_chunks + 1
    early = bm * MOE_EARLY_ROW_PERCENT // 100
    cuts = [early * i // (n_pieces - 1) for i in range(n_pieces)] + [bm]

    def step(up_blk, down_blk, side_work):
        piece = 0

        def side():
            nonlocal piece
            for emit in side_work:
                emit(cuts[piece], cuts[piece + 1])
            piece += 1

        gate = up = None
        for c in range(n_chunks):
            if up_blk is not None:
                k, xt, _ = up_blk
                slot = par_ref[tb + k]
                xc = jnp.concatenate([xt[j * pitch:j * pitch + bm, :] for j in (2 * c, 2 * c + 1)],
                                     axis=-1).astype(BF16)
                pg = jnp.dot(xc, wg_buf[slot, c * kc:(c + 1) * kc, :], preferred_element_type=F32)
                pu = jnp.dot(xc, wu_buf[slot, c * kc:(c + 1) * kc, :], preferred_element_type=F32)
                gate = pg if gate is None else gate + pg
                up = pu if up is None else up + pu
            side()
        for c in range(n_chunks):
            if down_blk is not None:
                k, hid, yt = down_blk
                slot = par_ref[tb + k]
                yc = jnp.dot(hid[...], wdn_buf[slot, :, c * kc:(c + 1) * kc],
                             preferred_element_type=F32)
                for q in range(2):
                    j = 2 * c + q
                    yt[j * pitch:j * pitch + bm, :] = yc[:, q * LANES:(q + 1) * LANES]
            side()
        side()
        if up_blk is not None:
            k, _, hid = up_blk
            w_col = jnp.broadcast_to(wl_ref[k], (SUBLANES, bm)).T[:, :1]
            hid[...] = (jax.nn.silu(gate) * up * w_col).astype(BF16)

    @pl.when(nb > 0)
    def _():
        for c in weight_copies(exp_ref[tb], 0):
            c.start()
        second = next_ref[tb]

        @pl.when(second >= 0)
        def _():
            for c in weight_copies(second, 1):
                c.start()
        gather_rows(0, xt_a)(0, bm)
        handoff(0)
        step((0, xt_a, hid_a), None, [gather_rows(1, xt_b)])
        handoff(1)
        step((1, xt_b, hid_b), (0, hid_a, yt_a), [gather_rows(2, xt_a)])

        def trip(i, carry):
            t = 2 * i + 1
            handoff(t + 1)
            step((t + 1, xt_a, hid_a), (t, hid_b, yt_b),
                 [gather_rows(t + 2, xt_b), scatter_rows(t - 1, yt_a)])
            handoff(t + 2)
            step((t + 2, xt_b, hid_b), (t + 1, hid_a, yt_a),
                 [gather_rows(t + 3, xt_a), scatter_rows(t, yt_b)])
            return carry

        lax.fori_loop(0, nb // 2, trip, 0)

        @pl.when(nb % 2 == 1)
        def _():
            scatter_rows(nb - 1, yt_a)(0, bm)


def _moe_group(table, h2t, rows, wl, wg, wu, wdn, tg, bm, nb_max, batch=4):
    ne, d, eh = wg.shape
    nj = d // LANES
    assert nj == SUBLANES
    g = h2t.shape[0] // (tg * nj)
    kern = functools.partial(_moe_group_kernel, bm=bm, nj=nj, batch=batch, nb_max=nb_max)
    slots = pl.BlockSpec((nb_max * bm,), lambda gi, *_: (gi,), memory_space=pltpu.SMEM)
    block_w = pl.BlockSpec((nb_max, 1, bm), lambda gi, *_: (gi, 0, 0))
    hbm = pl.BlockSpec(memory_space=pl.ANY)
    pitch_rows = nj * (bm + 1)
    grid_spec = pltpu.PrefetchScalarGridSpec(
        num_scalar_prefetch=len(table),
        grid=(g,),
        in_specs=[pl.BlockSpec((tg * nj, LANES), lambda gi, *_: (gi, 0),
                               pipeline_mode=pl.Buffered(1)),
                  slots, block_w, hbm, hbm, hbm],
        out_specs=pl.BlockSpec((tg * nj, LANES), lambda gi, *_: (gi, 0),
                               pipeline_mode=pl.Buffered(1)),
        scratch_shapes=[pltpu.VMEM((MOE_WEIGHT_BUFFERS, d, eh), BF16),
                        pltpu.VMEM((MOE_WEIGHT_BUFFERS, d, eh), BF16),
                        pltpu.VMEM((MOE_WEIGHT_BUFFERS, eh, d), BF16),
                        pltpu.SemaphoreType.DMA((MOE_WEIGHT_BUFFERS,)),
                        pltpu.VMEM((pitch_rows, LANES), F32), pltpu.VMEM((pitch_rows, LANES), F32),
                        pltpu.VMEM((bm, eh), BF16), pltpu.VMEM((bm, eh), BF16),
                        pltpu.VMEM((pitch_rows, LANES), F32), pltpu.VMEM((pitch_rows, LANES), F32)],
    )
    return pl.pallas_call(
        kern,
        grid_spec=grid_spec,
        out_shape=jax.ShapeDtypeStruct((g * tg * nj, LANES), F32),
        compiler_params=_params(1),
        name="moe_group",
    )(*table, h2t, rows, wl.reshape(g * nb_max, 1, bm), wg, wu, wdn)


def _finalize_kernel(r_ref, xacc_ref, gt2_ref, gf_ref, o_ref, *, final_norm):
    tm, d = xacc_ref.shape
    nj = d // LANES
    routed = jnp.concatenate([r_ref[pl.ds(j, tm, stride=nj), :] for j in range(nj)], axis=-1)
    xo = xacc_ref[...] + gt2_ref[...] * routed
    o_ref[...] = _rmsnorm(xo, gf_ref[...]) if final_norm else xo


def _finalize(routed, xacc, gt2, g_final, seq_len, tm, final_norm):
    t, d = xacc.shape
    nj = d // LANES
    return pl.pallas_call(
        functools.partial(_finalize_kernel, final_norm=final_norm),
        grid=(t // tm,),
        in_specs=[_row_spec(tm * nj, LANES),
                  _row_spec(tm, d), _mod_spec(False, tm, d, seq_len // tm),
                  _full_spec(g_final.shape)],
        out_specs=_row_spec(tm, d),
        out_shape=jax.ShapeDtypeStruct((t, d), F32),
        compiler_params=_params(1),
        name="finalize",
    )(routed, xacc, gt2, g_final)


def _s5_discretize(lam_re, lam_im, log_dt, b_re, b_im, c_re, c_im):
    g, n = lam_re.shape
    hg = b_re.shape[2]
    lr = jnp.minimum(lam_re.astype(F32), -1e-4)
    li = lam_im.astype(F32)
    dt = jnp.exp(log_dt.astype(F32))[:, None]
    mag = jnp.exp(lr * dt)
    abar_r, abar_i = mag * jnp.cos(li * dt), mag * jnp.sin(li * dt)
    den = lr * lr + li * li
    nr, ni = abar_r - 1.0, abar_i
    f_r = (nr * lr + ni * li) / den
    f_i = (ni * lr - nr * li) / den
    br_, bi_ = b_re.astype(F32), b_im.astype(F32)
    bbar_r = f_r[..., None] * br_ - f_i[..., None] * bi_
    bbar_i = f_r[..., None] * bi_ + f_i[..., None] * br_
    gpc = LANES // hg
    n_chunks = g // gpc
    eye = jnp.eye(gpc, dtype=F32)

    def b_chunks(bb):
        bb = bb.reshape(n_chunks, gpc, n, hg)
        m = jnp.einsum('jgnh,gk->jghkn', bb, eye)
        return m.reshape(n_chunks, gpc * hg, gpc * n)

    def c_chunks(cc):
        cc = cc.astype(F32).reshape(n_chunks, gpc, hg, n)
        m = jnp.einsum('jghn,gk->jgnkh', cc, eye)
        return m.reshape(n_chunks, gpc * n, gpc * hg)

    bj = jnp.concatenate([b_chunks(bbar_r), b_chunks(bbar_i)], axis=2).astype(BF16)
    cj = jnp.concatenate([c_chunks(c_re), -c_chunks(c_im)], axis=1).astype(BF16)
    return abar_r.reshape(1, g * n), abar_i.reshape(1, g * n), bj, cj


def kernel(x_prompt, x_sample, c_prompt, c_sample, state_ssm_re, state_ssm_im, g_norm1, g_norm2, w_ada, b_ada, w_in, w_out, ssm_lambda_re, ssm_lambda_im, ssm_log_dt, ssm_b_re, ssm_b_im, ssm_c_re, ssm_c_im, ssm_d, ssm_w_glu, ssm_b_glu, gm_ln_g, gm_ln_b, gm_w_s, gm_b_s, router_w, router_bias, moe_w_gate, moe_w_up, moe_w_down, shared_w_gate, shared_w_up, shared_w_down, g_final):
    bp, seq_len, d = x_prompt.shape
    bs, dec_len, _ = x_sample.shape
    depth = w_ada.shape[0]
    assert dec_len == 1 and seq_len % CHUNK == 0 and bp % SUBLANES == 0
    groups, n_state = ssm_lambda_re.shape[1:]
    sw = groups * SSM_GROUP
    gw = gm_ln_g.shape[1]
    hd = gw // GM_HEADS

    xp = x_prompt.reshape(bp * seq_len, d)
    xs = x_sample.reshape(bs, d)
    c_all = jnp.concatenate([c_prompt, c_sample], axis=0)
    gf = g_final.reshape(1, d)

    p_re, p_im, s_re, s_im, s_v = [], [], [], [], []
    for l in range(depth):
        mod = _ada(c_all, w_ada[l], b_ada[l].reshape(1, -1))
        mod_p = [m.reshape(bp, 1, d) for m in jnp.split(mod[:bp], 6, axis=-1)]
        mod_s = jnp.split(mod[bp:], 6, axis=-1)

        g1 = g_norm1[l].reshape(1, d)
        g2 = g_norm2[l].reshape(1, d)
        w_in_b = w_in[l].astype(BF16)
        w_out_b = w_out[l].astype(BF16)
        ln_g = gm_ln_g[l].reshape(1, gw)
        ln_b = gm_ln_b[l].reshape(1, gw)
        ws_tril = (gm_w_s[l] * jnp.tril(jnp.ones((CHUNK, CHUNK), F32))[None]).astype(BF16)
        bs_full = jnp.repeat(gm_b_s[l].T, hd, axis=1)
        w0 = jnp.repeat(gm_w_s[l][:, 0, 0], hd).reshape(1, gw)
        b0 = jnp.repeat(gm_b_s[l][:, 0], hd).reshape(1, gw)
        a_re, a_im, bj, cj = _s5_discretize(ssm_lambda_re[l], ssm_lambda_im[l], ssm_log_dt[l],
                                            ssm_b_re[l], ssm_b_im[l], ssm_c_re[l], ssm_c_im[l])
        d_vec = ssm_d[l].reshape(1, sw)
        w_glu = ssm_w_glu[l].astype(BF16)
        b_glu = ssm_b_glu[l].reshape(1, sw)
        pad = LANES - N_EXPERTS
        rw = jnp.pad(router_w[l], ((0, 0), (0, pad)))
        rw_hi = rw.astype(BF16)
        rw_lo = (rw - rw_hi.astype(F32)).astype(BF16)
        rb = jnp.pad(router_bias[l], (0, pad), constant_values=-jnp.inf).reshape(1, LANES)
        rwt_hi, rwt_lo = rw_hi[:, :N_EXPERTS].T, rw_lo[:, :N_EXPERTS].T
        rb_col = router_bias[l].reshape(N_EXPERTS, 1)
        swg, swu, swd = (shared_w_gate[l].astype(BF16), shared_w_up[l].astype(BF16),
                         shared_w_down[l].astype(BF16))
        ewg, ewu, ewd = (moe_w_gate[l].astype(BF16), moe_w_up[l].astype(BF16),
                         moe_w_down[l].astype(BF16))

        tm_row = min(ROW_TILE, seq_len)
        ua, yb = _premix_seq(xp, mod_p[0], mod_p[1], g1, w_in_b, ln_g, ln_b, ws_tril, bs_full,
                             seq_len, tm=min(PREMIX_TILE, seq_len))
        ya, hpr, hpi = _s5_seq(ua.reshape(bp, seq_len, sw), bj, cj, a_re, a_im, d_vec,
                               w_glu, b_glu, tl=min(SCAN_TILE, seq_len))
        xacc, h2, wd = _postmix(xp, ya.reshape(bp * seq_len, sw), yb, mod_p[2], mod_p[3],
                                mod_p[4], mod_p[5], g2, w_out_b, rwt_hi, rwt_lo, rb_col,
                                swg, swu, swd,
                                per_row=False, seq_len=seq_len, tm=tm_row, token_tiles=True)
        last = l == depth - 1
        tg = min(MOE_GROUP, bp * seq_len)
        assert (bp * seq_len) % tg == 0
        bm = min(MOE_BLOCK, tg)
        per_tile = max(SMEM_TILE // bm, 1)
        nb_max = -(-(tg * TOP_K // bm + N_EXPERTS + 3) // per_tile) * per_tile
        slot_rows, slot_w, counts = _route_plan(wd, tg, d // LANES)
        table, block_row = _block_table(counts[:, :, 0], tg, bm, nb_max)
        by_block = lambda a: jnp.take(a.reshape(-1, bm), block_row, axis=0).reshape(-1)
        routed = _moe_group(table, h2, by_block(slot_rows), by_block(slot_w),
                            ewg, ewu, ewd, tg, bm, nb_max)
        xp = _finalize(routed, xacc, mod_p[5], gf, seq_len, tm_row, final_norm=last)
        p_re.append(hpr.reshape(bp, groups, n_state))
        p_im.append(hpi.reshape(bp, groups, n_state))

        ua, yb, vn = _premix_step(xs, mod_s[0], mod_s[1], g1, w_in_b, ln_g, ln_b, w0, b0)
        ya, hsr, hsi = _s5_step(ua, state_ssm_re[l].reshape(bs, groups * n_state),
                                state_ssm_im[l].reshape(bs, groups * n_state),
                                bj, cj, a_re, a_im, d_vec, w_glu, b_glu)
        xacc, h2, wd = _postmix(xs, ya, yb, mod_s[2], mod_s[3], mod_s[4], mod_s[5], g2, w_out_b,
                                rw_hi, rw_lo, rb, swg, swu, swd, per_row=True, seq_len=1, tm=bs,
                                token_tiles=False)
        xs = _moe_dense(h2, wd, xacc, mod_s[5], gf, ewg, ewu, ewd, per_row=True, seq_len=1,
                        tm=bs, final_norm=last)
        s_re.append(hsr.reshape(bs, groups, n_state))
        s_im.append(hsi.reshape(bs, groups, n_state))
        s_v.append(vn.reshape(bs, 1, gw))

    y_prompt = xp.reshape(bp, seq_len, d)
    y_sample = xs.reshape(bs, 1, d)
    return (y_prompt, y_sample, jnp.stack(p_re), jnp.stack(p_im), jnp.stack(s_re),
            jnp.stack(s_im), jnp.stack(s_v))
```

```python
import functools

import jax
import jax.numpy as jnp
from jax import lax
from jax.experimental import pallas as pl
from jax.experimental.pallas import tpu as pltpu

F32 = jnp.float32
BF16 = jnp.bfloat16

EPS = 1e-6
SSM_GROUP = 16
SSM_STATE = 64
GM_HEADS = 4
CHUNK = 128
N_EXPERTS = 64
TOP_K = 8
N_ROUTE_GROUPS = 8
TOPK_ROUTE_GROUPS = 4
ROUTED_SCALE = 2.5

LANES = 128
SUBLANES = 8
VMEM_LIMIT = 48 * 1024 * 1024
ROW_TILE = 512
PREMIX_TILE = 1024
DENSE_EXPERTS_PER_STEP = 8
SCAN_TILE = 128
MOE_GROUP = 4096
SMEM_TILE = 1024
MOE_BLOCK = 128
MOE_EARLY_ROW_PERCENT = 56
MOE_WEIGHT_BUFFERS = 4


def _params(n_axes):
    return pltpu.CompilerParams(dimension_semantics=("arbitrary",) * n_axes,
                                vmem_limit_bytes=VMEM_LIMIT)


def _dot(a, b):
    return jnp.dot(a.astype(BF16), b.astype(BF16), preferred_element_type=F32)


def _rmsnorm(x, g):
    ms = jnp.mean(x * x, axis=-1, keepdims=True)
    return x * lax.rsqrt(ms + EPS) * g


def _ada_kernel(c_ref, w_ref, b_ref, o_ref):
    c = c_ref[...]
    o_ref[...] = _dot(jax.nn.silu(c), w_ref[...]) + b_ref[...]


def _ada(c, w, b, tn=512):
    m, d = c.shape
    n = w.shape[1]
    return pl.pallas_call(
        _ada_kernel,
        grid=(n // tn,),
        in_specs=[pl.BlockSpec((m, d), lambda j: (0, 0)),
                  pl.BlockSpec((d, tn), lambda j: (0, j)),
                  pl.BlockSpec((1, tn), lambda j: (0, j))],
        out_specs=pl.BlockSpec((m, tn), lambda j: (0, j)),
        out_shape=jax.ShapeDtypeStruct((m, n), F32),
        compiler_params=_params(1),
        name="ada_mod",
    )(c, w, b)


def _premix_common(x_ref, sh_ref, sc_ref, g_ref, win_ref, lng_ref, lnb_ref, sw):
    x = x_ref[...]
    h = _rmsnorm(x, g_ref[...]) * (1.0 + sc_ref[...]) + sh_ref[...]
    proj = _dot(h, win_ref[...])
    u_a = proj[:, :sw]
    z = jax.nn.gelu(proj[:, sw:])
    gw = z.shape[1] // 2
    u_b = z[:, :gw]
    v_b = z[:, gw:]
    hd = gw // GM_HEADS
    parts = []
    for k in range(GM_HEADS):
        vh = v_b[:, k * hd:(k + 1) * hd]
        mu = jnp.mean(vh, axis=-1, keepdims=True)
        dv = vh - mu
        var = jnp.mean(dv * dv, axis=-1, keepdims=True)
        parts.append(dv * lax.rsqrt(var + EPS))
    vn = jnp.concatenate(parts, axis=-1) * lng_ref[...] + lnb_ref[...]
    return u_a, u_b, vn


def _premix_seq_kernel(x_ref, sh_ref, sc_ref, g_ref, win_ref, lng_ref, lnb_ref,
                       ws_ref, bs_ref, ua_ref, yb_ref, *, sw):
    u_a, u_b, vn = _premix_common(x_ref, sh_ref, sc_ref, g_ref, win_ref, lng_ref, lnb_ref, sw)
    ua_ref[...] = u_a
    tm, gw = u_b.shape
    hd = gw // GM_HEADS
    vnb = vn.astype(BF16)
    for c in range(tm // CHUNK):
        rows = slice(c * CHUNK, (c + 1) * CHUNK)
        for k in range(GM_HEADS):
            cols = slice(k * hd, (k + 1) * hd)
            s = jnp.dot(ws_ref[k], vnb[rows, cols], preferred_element_type=F32)
            yb_ref[rows, cols] = u_b[rows, cols] * (s + bs_ref[:, cols])


def _premix_step_kernel(x_ref, sh_ref, sc_ref, g_ref, win_ref, lng_ref, lnb_ref,
                        w0_ref, b0_ref, ua_ref, yb_ref, vn_ref, *, sw):
    u_a, u_b, vn = _premix_common(x_ref, sh_ref, sc_ref, g_ref, win_ref, lng_ref, lnb_ref, sw)
    ua_ref[...] = u_a
    vn_ref[...] = vn
    yb_ref[...] = u_b * (vn * w0_ref[...] + b0_ref[...])


def _row_spec(tm, d):
    return pl.BlockSpec((tm, d), lambda i: (i, 0))


def _full_spec(shape):
    nd = len(shape)
    return pl.BlockSpec(shape, lambda i: (0,) * nd)


def _mod_spec(per_row, tm, d, tiles_per_batch):
    if per_row:
        return pl.BlockSpec((tm, d), lambda i: (i, 0))
    return pl.BlockSpec((None, 1, d), lambda i: (i // tiles_per_batch, 0, 0))


def _premix_seq(x, sh, sc, g, w_in, ln_g, ln_b, ws_tril, bs_full, seq_len, tm):
    t, d = x.shape
    sw = ln_g.shape[1]
    tpb = seq_len // tm
    mod = _mod_spec(False, tm, d, tpb)
    return pl.pallas_call(
        functools.partial(_premix_seq_kernel, sw=sw),
        grid=(t // tm,),
        in_specs=[_row_spec(tm, d), mod, mod, _full_spec(g.shape), _full_spec(w_in.shape),
                  _full_spec(ln_g.shape), _full_spec(ln_b.shape), _full_spec(ws_tril.shape),
                  _full_spec(bs_full.shape)],
        out_specs=[_row_spec(tm, sw), _row_spec(tm, sw)],
        out_shape=[jax.ShapeDtypeStruct((t, sw), F32), jax.ShapeDtypeStruct((t, sw), F32)],
        compiler_params=_params(1),
        name="premix_seq",
    )(x, sh, sc, g, w_in, ln_g, ln_b, ws_tril, bs_full)


def _premix_step(x, sh, sc, g, w_in, ln_g, ln_b, w0, b0):
    t, d = x.shape
    sw = ln_g.shape[1]
    mod = _mod_spec(True, t, d, 1)
    return pl.pallas_call(
        functools.partial(_premix_step_kernel, sw=sw),
        grid=(1,),
        in_specs=[_row_spec(t, d), mod, mod, _full_spec(g.shape), _full_spec(w_in.shape),
                  _full_spec(ln_g.shape), _full_spec(ln_b.shape), _full_spec(w0.shape),
                  _full_spec(b0.shape)],
        out_specs=[_row_spec(t, sw)] * 3,
        out_shape=[jax.ShapeDtypeStruct((t, sw), F32)] * 3,
        compiler_params=_params(1),
        name="premix_step",
    )(x, sh, sc, g, w_in, ln_g, ln_b, w0, b0)


def _s5_gate(y, u, d_ref, wglu_ref, bglu_ref):
    y = jax.nn.gelu(y + d_ref[...] * u)
    gate = jax.nn.sigmoid(_dot(y, wglu_ref[...]) + bglu_ref[...])
    return y * gate


def _s5_seq_kernel(u_ref, bj_ref, cj_ref, are_ref, aim_ref, d_ref, wglu_ref, bglu_ref,
                   y_ref, sre_ref, sim_ref, bu_ref, st_ref, ut_ref, yt_ref,
                   *, nb, tl, n_chunks, unroll):
    i = pl.program_id(0)
    n_tiles = bu_ref.shape[0]
    tpc = n_tiles // n_chunks
    hpc = tpc // 2
    hw = hpc * LANES
    assert u_ref.shape[2] == n_chunks * LANES

    @pl.when(i == 0)
    def _():
        st_ref[...] = jnp.zeros_like(st_ref)

    for b in range(nb):
        ub = u_ref[b]
        for j in range(n_chunks):
            ut_ref[j, pl.ds(b, tl, stride=nb), :] = ub[:, j * LANES:(j + 1) * LANES]
    for j in range(n_chunks):
        bu = jnp.dot(ut_ref[j].astype(BF16), bj_ref[j], preferred_element_type=F32)
        for c in range(tpc):
            bu_ref[j * tpc + c] = bu[:, c * LANES:(c + 1) * LANES]

    for j in range(n_chunks):
        re = slice(j * 2 * hw, j * 2 * hw + hw)
        im = slice(j * 2 * hw + hw, (j + 1) * 2 * hw)
        ar = jnp.broadcast_to(are_ref[:, j * hw:(j + 1) * hw], (nb, hw))
        ai = jnp.broadcast_to(aim_ref[:, j * hw:(j + 1) * hw], (nb, hw))
        re_tiles = [j * tpc + c for c in range(hpc)]
        im_tiles = [j * tpc + hpc + c for c in range(hpc)]

        def body(s, carry, ar=ar, ai=ai, re_tiles=re_tiles, im_tiles=im_tiles):
            xr, xi = carry
            for k in range(unroll):
                r0 = pl.multiple_of((s * unroll + k) * nb, nb)
                br = jnp.concatenate([bu_ref[c, pl.ds(r0, nb), :] for c in re_tiles], axis=-1)
                bi = jnp.concatenate([bu_ref[c, pl.ds(r0, nb), :] for c in im_tiles], axis=-1)
                nr = ar * xr - ai * xi + br
                ni = ar * xi + ai * xr + bi
                for q, c in enumerate(re_tiles):
                    bu_ref[c, pl.ds(r0, nb), :] = nr[:, q * LANES:(q + 1) * LANES]
                for q, c in enumerate(im_tiles):
                    bu_ref[c, pl.ds(r0, nb), :] = ni[:, q * LANES:(q + 1) * LANES]
                xr, xi = nr, ni
            return xr, xi

        xr, xi = lax.fori_loop(0, tl // unroll, body, (st_ref[:, re], st_ref[:, im]))
        st_ref[:, re] = xr
        st_ref[:, im] = xi
        sre_ref[:, j * hw:(j + 1) * hw] = xr
        sim_ref[:, j * hw:(j + 1) * hw] = xi

    for j in range(n_chunks):
        xs = jnp.concatenate([bu_ref[j * tpc + c] for c in range(tpc)], axis=-1)
        yt_ref[j] = jnp.dot(xs.astype(BF16), cj_ref[j], preferred_element_type=F32)
    y = jnp.concatenate(
        [jnp.concatenate([yt_ref[j, pl.ds(b, tl, stride=nb), :] for j in range(n_chunks)], axis=-1)
         for b in range(nb)], axis=0)
    u_all = u_ref[...].reshape(nb * tl, u_ref.shape[2])
    y_ref[...] = _s5_gate(y, u_all, d_ref, wglu_ref, bglu_ref).reshape(y_ref.shape)


def _s5_seq(u, bj, cj, a_re, a_im, d_vec, w_glu, b_glu, tl, unroll=4):
    nb, seq_len, sw = u.shape
    n_chunks, _, cw = bj.shape
    ns = a_re.shape[1]
    kern = functools.partial(_s5_seq_kernel, nb=nb, tl=tl, n_chunks=n_chunks, unroll=unroll)
    return pl.pallas_call(
        kern,
        grid=(seq_len // tl,),
        in_specs=[pl.BlockSpec((nb, tl, sw), lambda i: (0, i, 0)),
                  _full_spec(bj.shape), _full_spec(cj.shape), _full_spec(a_re.shape),
                  _full_spec(a_im.shape), _full_spec(d_vec.shape), _full_spec(w_glu.shape),
                  _full_spec(b_glu.shape)],
        out_specs=[pl.BlockSpec((nb, tl, sw), lambda i: (0, i, 0)),
                   _full_spec((nb, ns)), _full_spec((nb, ns))],
        out_shape=[jax.ShapeDtypeStruct((nb, seq_len, sw), F32),
                   jax.ShapeDtypeStruct((nb, ns), F32), jax.ShapeDtypeStruct((nb, ns), F32)],
        scratch_shapes=[pltpu.VMEM((n_chunks * cw // LANES, tl * nb, LANES), F32),
                        pltpu.VMEM((nb, n_chunks * cw), F32),
                        pltpu.VMEM((n_chunks, tl * nb, LANES), F32),
                        pltpu.VMEM((n_chunks, tl * nb, LANES), F32)],
        compiler_params=_params(1),
        name="s5_seq",
    )(u, bj, cj, a_re, a_im, d_vec, w_glu, b_glu)


def _s5_step_kernel(u_ref, hre_ref, him_ref, bj_ref, cj_ref, are_ref, aim_ref, d_ref,
                    wglu_ref, bglu_ref, y_ref, sre_ref, sim_ref, *, n_chunks):
    u = u_ref[...]
    ub = u.astype(BF16)
    kw = u.shape[1] // n_chunks
    hw = bj_ref.shape[2] // 2
    chunks = []
    for j in range(n_chunks):
        st = slice(j * hw, (j + 1) * hw)
        bu = jnp.dot(ub[:, j * kw:(j + 1) * kw], bj_ref[j], preferred_element_type=F32)
        ar, ai = are_ref[:, st], aim_ref[:, st]
        hr, hi = hre_ref[:, st], him_ref[:, st]
        xr = ar * hr - ai * hi + bu[:, :hw]
        xi = ar * hi + ai * hr + bu[:, hw:]
        sre_ref[:, st] = xr
        sim_ref[:, st] = xi
        chunks.append(jnp.concatenate([xr, xi], axis=-1))
    y = jnp.concatenate([jnp.dot(chunks[j].astype(BF16), cj_ref[j], preferred_element_type=F32)
                         for j in range(n_chunks)], axis=-1)
    y_ref[...] = _s5_gate(y, u, d_ref, wglu_ref, bglu_ref)


def _s5_step(u, h_re, h_im, bj, cj, a_re, a_im, d_vec, w_glu, b_glu):
    t, sw = u.shape
    ns = a_re.shape[1]
    args = (u, h_re, h_im, bj, cj, a_re, a_im, d_vec, w_glu, b_glu)
    return pl.pallas_call(
        functools.partial(_s5_step_kernel, n_chunks=bj.shape[0]),
        grid=(1,),
        in_specs=[_full_spec(a.shape) for a in args],
        out_specs=[_full_spec((t, sw)), _full_spec((t, ns)), _full_spec((t, ns))],
        out_shape=[jax.ShapeDtypeStruct((t, sw), F32), jax.ShapeDtypeStruct((t, ns), F32),
                   jax.ShapeDtypeStruct((t, ns), F32)],
        compiler_params=_params(1),
        name="s5_step",
    )(*args)


def _group_reduce(x, lane, width, op):
    n = x.shape[-1]
    s = 1
    while s < width:
        up = pltpu.roll(x, s, 1)
        dn = pltpu.roll(x, n - s, 1)
        x = op(x, jnp.where((lane & s) != 0, up, dn))
        s *= 2
    return x


def _first_max(x, lane, big):
    m = jnp.max(x, axis=-1, keepdims=True)
    return jnp.min(jnp.where(x == m, lane, big), axis=-1, keepdims=True)


def _route(logits, bias):
    rows, n = logits.shape
    lane = lax.broadcasted_iota(jnp.int32, (rows, n), 1)
    lane_f = lane.astype(F32)
    neg = jnp.float32(-jnp.inf)
    big = jnp.float32(n)
    gsz = N_EXPERTS // N_ROUTE_GROUPS
    group_start = (lane - (lane & (gsz - 1))).astype(F32)
    scores = jax.nn.sigmoid(logits)
    biased = scores + bias
    m1 = _group_reduce(biased, lane, gsz, jnp.maximum)
    first = _group_reduce(jnp.where(biased == m1, lane_f, big), lane, gsz, jnp.minimum)
    m2 = _group_reduce(jnp.where(lane_f == first, neg, biased), lane, gsz, jnp.maximum)
    gs = m1 + m2
    gsel = jnp.zeros((rows, n), F32)
    work = gs
    for _ in range(TOPK_ROUTE_GROUPS):
        hit = group_start == _first_max(work, lane_f, big)
        gsel = jnp.where(hit, 1.0, gsel)
        work = jnp.where(hit, neg, work)
    work = jnp.where(gsel > 0.0, biased, neg)
    w = jnp.zeros((rows, n), F32)
    for _ in range(TOP_K):
        hit = lane_f == _first_max(work, lane_f, big)
        w = jnp.where(hit, scores, w)
        work = jnp.where(hit, neg, work)
    return w / jnp.sum(w, axis=-1, keepdims=True) * ROUTED_SCALE


def _route_by_expert(logits, bias):
    ne, tm = logits.shape
    ng = N_ROUTE_GROUPS
    gsz = ne // ng
    neg = jnp.float32(-jnp.inf)
    shape = (ng, gsz, tm)
    scores = jax.nn.sigmoid(logits)
    biased = (scores + bias).reshape(shape)
    scores = scores.reshape(shape)
    sub = lax.broadcasted_iota(jnp.int32, shape, 1)
    grp = lax.broadcasted_iota(jnp.int32, (ng, 1, tm), 0)
    eid = lax.broadcasted_iota(jnp.int32, shape, 0) * gsz + sub
    m1 = jnp.max(biased, axis=1, keepdims=True)
    first = jnp.min(jnp.where(biased == m1, sub, gsz), axis=1, keepdims=True)
    m2 = jnp.max(jnp.where(sub == first, neg, biased), axis=1, keepdims=True)
    work = m1 + m2
    gsel = jnp.zeros((ng, 1, tm), jnp.bool_)
    for _ in range(TOPK_ROUTE_GROUPS):
        m = jnp.max(work, axis=0, keepdims=True)
        hit = grp == jnp.min(jnp.where(work == m, grp, ng), axis=0, keepdims=True)
        gsel = jnp.logical_or(gsel, hit)
        work = jnp.where(hit, neg, work)
    work = jnp.where(gsel, biased, neg)
    w = jnp.zeros(shape, F32)
    for _ in range(TOP_K):
        m = jnp.max(jnp.max(work, axis=0, keepdims=True), axis=1, keepdims=True)
        first = jnp.min(jnp.min(jnp.where(work == m, eid, ne), axis=0, keepdims=True),
                        axis=1, keepdims=True)
        hit = eid == first
        w = jnp.where(hit, scores, w)
        work = jnp.where(hit, neg, work)
    denom = jnp.sum(jnp.sum(w, axis=0, keepdims=True), axis=1, keepdims=True)
    return (w / denom * ROUTED_SCALE).reshape(ne, tm)


def _split_bf16(x):
    hi = x.astype(BF16)
    lo = (x - hi.astype(F32)).astype(BF16)
    return hi, lo


def _postmix_kernel(x_ref, ya_ref, yb_ref, gt1_ref, sh2_ref, sc2_ref, gt2_ref, g2_ref,
                    wout_ref, rwh_ref, rwl_ref, rb_ref, swg_ref, swu_ref, swd_ref,
                    xacc_ref, h2_ref, wd_ref, *, token_tiles):
    y = jnp.concatenate([ya_ref[...], yb_ref[...]], axis=-1)
    x1 = x_ref[...] + gt1_ref[...] * _dot(y, wout_ref[...])
    h2 = _rmsnorm(x1, g2_ref[...]) * (1.0 + sc2_ref[...]) + sh2_ref[...]
    hb = h2.astype(BF16)
    if token_tiles:
        tm, d = h2.shape
        nj = d // LANES
        for j in range(nj):
            h2_ref[pl.ds(j, tm, stride=nj), :] = h2[:, j * LANES:(j + 1) * LANES]
    else:
        h2_ref[...] = hb
    h_hi, h_lo = _split_bf16(h2)
    if token_tiles:
        nt = (((1,), (1,)), ((), ()))
        logits = (lax.dot_general(rwh_ref[...], h_hi, nt, preferred_element_type=F32)
                  + lax.dot_general(rwl_ref[...], h_hi, nt, preferred_element_type=F32)
                  + lax.dot_general(rwh_ref[...], h_lo, nt, preferred_element_type=F32))
        wd_ref[...] = _route_by_expert(logits, rb_ref[...])
    else:
        logits = (jnp.dot(h_hi, rwh_ref[...], preferred_element_type=F32)
                  + jnp.dot(h_hi, rwl_ref[...], preferred_element_type=F32)
                  + jnp.dot(h_lo, rwh_ref[...], preferred_element_type=F32))
        wd_ref[...] = _route(logits, rb_ref[...])
    hid = jax.nn.silu(jnp.dot(hb, swg_ref[...], preferred_element_type=F32)) * \
        jnp.dot(hb, swu_ref[...], preferred_element_type=F32)
    shared = _dot(hid, swd_ref[...])
    xacc_ref[...] = x1 + gt2_ref[...] * shared


def _postmix(x, ya, yb, gt1, sh2, sc2, gt2, g2, w_out, rw_hi, rw_lo, rb, swg, swu, swd,
             per_row, seq_len, tm, token_tiles):
    t, d = x.shape
    sw = ya.shape[1]
    nj = d // LANES
    mod = _mod_spec(per_row, tm, d, max(seq_len // tm, 1))
    weights = (g2, w_out, rw_hi, rw_lo, rb, swg, swu, swd)
    if token_tiles:
        ne = rw_hi.shape[0]
        h2_spec, h2_shape = _row_spec(tm * nj, LANES), jax.ShapeDtypeStruct((t * nj, LANES), F32)
        wd_spec, wd_shape = pl.BlockSpec((ne, tm), lambda i: (0, i)), (ne, t)
    else:
        ne = rw_hi.shape[1]
        h2_spec, h2_shape = _row_spec(tm, d), jax.ShapeDtypeStruct((t, d), BF16)
        wd_spec, wd_shape = _row_spec(tm, ne), (t, ne)
    return pl.pallas_call(
        functools.partial(_postmix_kernel, token_tiles=token_tiles),
        grid=(t // tm,),
        in_specs=[_row_spec(tm, d), _row_spec(tm, sw), _row_spec(tm, sw), mod, mod, mod, mod]
        + [_full_spec(w.shape) for w in weights],
        out_specs=[_row_spec(tm, d), h2_spec, wd_spec],
        out_shape=[jax.ShapeDtypeStruct((t, d), F32), h2_shape,
                   jax.ShapeDtypeStruct(wd_shape, F32)],
        compiler_params=_params(1),
        name="postmix",
    )(x, ya, yb, gt1, sh2, sc2, gt2, *weights)


def _moe_dense_kernel(h_ref, wd_ref, xacc_ref, gt2_ref, gf_ref, wg_ref, wu_ref, wdn_ref,
                      o_ref, acc_ref, *, final_norm):
    step = pl.program_id(1)
    per_step = wg_ref.shape[0]

    @pl.when(step == 0)
    def _():
        acc_ref[...] = jnp.zeros_like(acc_ref)

    hb = h_ref[...]
    wd = wd_ref[...]
    lane = lax.broadcasted_iota(jnp.int32, wd.shape, 1)
    total = None
    for q in range(per_step):
        col = jnp.sum(jnp.where(lane == step * per_step + q, wd, 0.0), axis=-1, keepdims=True)
        hid = jax.nn.silu(jnp.dot(hb, wg_ref[q], preferred_element_type=F32)) * \
            jnp.dot(hb, wu_ref[q], preferred_element_type=F32)
        part = col * _dot(hid, wdn_ref[q])
        total = part if total is None else total + part
    acc_ref[...] += total

    @pl.when(step == pl.num_programs(1) - 1)
    def _():
        xo = xacc_ref[...] + gt2_ref[...] * acc_ref[...]
        o_ref[...] = _rmsnorm(xo, gf_ref[...]) if final_norm else xo


def _moe_dense(h2, wd, xacc, gt2, g_final, wg, wu, wdn, per_row, seq_len, tm, final_norm):
    t, d = xacc.shape
    ne, _, eh = wg.shape
    nl = wd.shape[1]
    tpb = max(seq_len // tm, 1)
    if per_row:
        mod = pl.BlockSpec((tm, d), lambda i, e: (i, 0))
    else:
        mod = pl.BlockSpec((None, 1, d), lambda i, e: (i // tpb, 0, 0))
    row = lambda w: pl.BlockSpec((tm, w), lambda i, e: (i, 0))
    per_step = DENSE_EXPERTS_PER_STEP
    assert ne % per_step == 0
    return pl.pallas_call(
        functools.partial(_moe_dense_kernel, final_norm=final_norm),
        grid=(t // tm, ne // per_step),
        in_specs=[row(d), row(nl), row(d), mod,
                  pl.BlockSpec(g_final.shape, lambda i, e: (0, 0)),
                  pl.BlockSpec((per_step, d, eh), lambda i, e: (e, 0, 0)),
                  pl.BlockSpec((per_step, d, eh), lambda i, e: (e, 0, 0)),
                  pl.BlockSpec((per_step, eh, d), lambda i, e: (e, 0, 0))],
        out_specs=row(d),
        out_shape=jax.ShapeDtypeStruct((t, d), F32),
        scratch_shapes=[pltpu.VMEM((tm, d), F32)],
        compiler_params=_params(2),
        name="moe_dense",
    )(h2, wd, xacc, gt2, g_final, wg, wu, wdn)


TOKEN_BITS = 13


def _route_plan_kernel(wd_ref, rows_ref, wl_ref, cnt_ref, *, tg, nj):
    wt = wd_ref[...]
    sel = wt > 0.0
    lane = lax.broadcasted_iota(jnp.int32, wt.shape, 1)
    c = sel.astype(jnp.int32)
    s = 1
    while s < tg:
        c = c + jnp.where(lane >= s, pltpu.roll(c, s, 1), 0)
        s *= 2
    cnt_ref[...] = jnp.broadcast_to(c[:, tg - 1:tg], cnt_ref.shape)
    dist = lane - (c - 1)
    key = jnp.where(sel, (dist << TOKEN_BITS) | lane, -1)
    w = jnp.where(sel, wt, 0.0)
    s = 1
    while s < tg:
        leaving = (key >= 0) & (((key >> TOKEN_BITS) & s) != 0)
        key_in = pltpu.roll(key, tg - s, 1)
        w_in = pltpu.roll(w, tg - s, 1)
        arriving = (key_in >= 0) & (((key_in >> TOKEN_BITS) & s) != 0)
        key = jnp.where(arriving, key_in, jnp.where(leaving, -1, key))
        w = jnp.where(arriving, w_in, jnp.where(leaving, 0.0, w))
        s *= 2
    rows_ref[...] = jnp.where(key >= 0, key & ((1 << TOKEN_BITS) - 1), 0) * nj
    wl_ref[...] = w


def _route_plan(wd, tg, nj):
    ne, t = wd.shape
    assert ne == N_EXPERTS and tg <= (1 << TOKEN_BITS) and t % tg == 0
    g = t // tg
    grp = lambda w: pl.BlockSpec((None, N_EXPERTS, w), lambda i: (i, 0, 0))
    return pl.pallas_call(
        functools.partial(_route_plan_kernel, tg=tg, nj=nj),
        grid=(g,),
        in_specs=[pl.BlockSpec((ne, tg), lambda i: (0, i))],
        out_specs=[grp(tg), grp(tg), grp(LANES)],
        out_shape=[jax.ShapeDtypeStruct((g, N_EXPERTS, tg), jnp.int32),
                   jax.ShapeDtypeStruct((g, N_EXPERTS, tg), F32),
                   jax.ShapeDtypeStruct((g, N_EXPERTS, LANES), jnp.int32)],
        compiler_params=_params(1),
        name="route_plan",
    )(wd)


def _block_table(counts, tg, bm, nb_max):
    g, ne = counts.shape
    i32 = jnp.int32
    nblk = (counts + bm - 1) // bm
    bend = jnp.cumsum(nblk, axis=1)
    bstart = bend - nblk
    nb = bend[:, -1]
    k = jnp.arange(nb_max, dtype=i32)[None, :]
    kc = jnp.minimum(k, jnp.maximum(nb[:, None] - 1, 0))
    e_k = jnp.minimum(jnp.sum(bend[:, None, :] <= kc[:, :, None], axis=2), ne - 1).astype(i32)
    b_k = kc - jnp.take_along_axis(bstart, e_k, axis=1)
    first = (b_k == 0) & (k < nb[:, None])
    active = nblk > 0
    parity = (jnp.take_along_axis(jnp.cumsum(active.astype(i32), axis=1) - 1, e_k, axis=1)
              % MOE_WEIGHT_BUFFERS)
    ids = jnp.where(active, jnp.arange(ne, dtype=i32)[None, :], ne)
    first_active_from = jnp.flip(lax.cummin(jnp.flip(ids, 1), axis=1), 1)
    nxt = jnp.concatenate([first_active_from[:, 1:], jnp.full((g, 1), ne, i32)], axis=1)
    nxt = jnp.where(nxt >= ne, -1, nxt)
    nxt2 = jnp.where(nxt < 0, -1, jnp.take_along_axis(nxt, jnp.maximum(nxt, 0), axis=1))
    next_k = jnp.take_along_axis(nxt, e_k, axis=1)
    next2_k = jnp.take_along_axis(nxt2, e_k, axis=1)
    row = (jnp.arange(g, dtype=i32)[:, None] * ne + e_k) * (tg // bm) + b_k
    flat = lambda a: a.reshape(-1).astype(i32)
    table = (nb.astype(i32), flat(e_k), flat(first), flat(parity), flat(next_k), flat(next2_k))
    return table, flat(row)


def _moe_group_kernel(nb_ref, exp_ref, first_ref, par_ref, next_ref, next2_ref,
                      h_ref, rows_ref, wl_ref, wg_hbm, wu_hbm, wdn_hbm, o_ref,
                      wg_buf, wu_buf, wdn_buf, wsem, xt_a, xt_b, hid_a, hid_b, yt_a, yt_b,
                      *, bm, nj, batch, nb_max):
    g = pl.program_id(0)
    nb = nb_ref[g]
    tb = g * nb_max
    pitch = bm + 1
    o_ref[...] = jnp.zeros_like(o_ref)

    def weight_copies(e, slot):
        return (pltpu.make_async_copy(wg_hbm.at[e], wg_buf.at[slot], wsem.at[slot]),
                pltpu.make_async_copy(wu_hbm.at[e], wu_buf.at[slot], wsem.at[slot]),
                pltpu.make_async_copy(wdn_hbm.at[e], wdn_buf.at[slot], wsem.at[slot]))

    n_wbuf = wg_buf.shape[0]

    def handoff(k):
        @pl.when(first_ref[tb + k] == 1)
        def _():
            slot = par_ref[tb + k]
            for c in weight_copies(exp_ref[tb + k], slot):
                c.wait()
            ahead = next2_ref[tb + k]

            @pl.when(ahead >= 0)
            def _():
                for c in weight_copies(ahead, (slot + 2) % n_wbuf):
                    c.start()

    def gather_rows(k, xt):
        rows = rows_ref.at[pl.ds(pl.multiple_of(k * bm, bm), bm)]

        def emit(lo, hi):
            for m in range(lo, hi):
                r0 = pl.multiple_of(rows[m], nj)
                xt[pl.ds(m, nj, stride=pitch), :] = h_ref[pl.ds(r0, nj), :]
        return emit

    def scatter_rows(k, yt):
        rows = rows_ref.at[pl.ds(pl.multiple_of(k * bm, bm), bm)]

        def emit(lo, hi):
            for m0 in range(lo, hi, batch):
                updates = []
                for m in range(m0, min(m0 + batch, hi)):
                    r0 = pl.multiple_of(rows[m], nj)
                    updates.append((r0, o_ref[pl.ds(r0, nj), :]
                                    + yt[pl.ds(m, nj, stride=pitch), :]))
                for r0, v in reversed(updates):
                    o_ref[pl.ds(r0, nj), :] = v
        return emit

    n_chunks = nj // 2
    kc = 2 * LANES
    n_pieces = 2 * n_chunks + 1
    early = bm * MOE_EARLY_ROW_PERCENT // 100
    cuts = [early * i // (n_pieces - 1) for i in range(n_pieces)] + [bm]

    def step(up_blk, down_blk, side_work):
        piece = 0

        def side():
            nonlocal piece
            for emit in side_work:
                emit(cuts[piece], cuts[piece + 1])
            piece += 1

        gate = up = None
        for c in range(n_chunks):
            if up_blk is not None:
                k, xt, _ = up_blk
                slot = par_ref[tb + k]
                xc = jnp.concatenate([xt[j * pitch:j * pitch + bm, :] for j in (2 * c, 2 * c + 1)],
                                     axis=-1).astype(BF16)
                pg = jnp.dot(xc, wg_buf[slot, c * kc:(c + 1) * kc, :], preferred_element_type=F32)
                pu = jnp.dot(xc, wu_buf[slot, c * kc:(c + 1) * kc, :], preferred_element_type=F32)
                gate = pg if gate is None else gate + pg
                up = pu if up is None else up + pu
            side()
        for c in range(n_chunks):
            if down_blk is not None:
                k, hid, yt = down_blk
                slot = par_ref[tb + k]
                yc = jnp.dot(hid[...], wdn_buf[slot, :, c * kc:(c + 1) * kc],
                             preferred_element_type=F32)
                for q in range(2):
                    j = 2 * c + q
                    yt[j * pitch:j * pitch + bm, :] = yc[:, q * LANES:(q + 1) * LANES]
            side()
        side()
        if up_blk is not None:
            k, _, hid = up_blk
            w_col = jnp.broadcast_to(wl_ref[k], (SUBLANES, bm)).T[:, :1]
            hid[...] = (jax.nn.silu(gate) * up * w_col).astype(BF16)

    @pl.when(nb > 0)
    def _():
        for c in weight_copies(exp_ref[tb], 0):
            c.start()
        second = next_ref[tb]

        @pl.when(second >= 0)
        def _():
            for c in weight_copies(second, 1):
                c.start()
        gather_rows(0, xt_a)(0, bm)
        handoff(0)
        step((0, xt_a, hid_a), None, [gather_rows(1, xt_b)])
        handoff(1)
        step((1, xt_b, hid_b), (0, hid_a, yt_a), [gather_rows(2, xt_a)])

        def trip(i, carry):
            t = 2 * i + 1
            handoff(t + 1)
            step((t + 1, xt_a, hid_a), (t, hid_b, yt_b),
                 [gather_rows(t + 2, xt_b), scatter_rows(t - 1, yt_a)])
            handoff(t + 2)
            step((t + 2, xt_b, hid_b), (t + 1, hid_a, yt_a),
                 [gather_rows(t + 3, xt_a), scatter_rows(t, yt_b)])
            return carry

        lax.fori_loop(0, nb // 2, trip, 0)

        @pl.when(nb % 2 == 1)
        def _():
            scatter_rows(nb - 1, yt_a)(0, bm)


def _moe_group(table, h2t, rows, wl, wg, wu, wdn, tg, bm, nb_max, batch=4):
    ne, d, eh = wg.shape
    nj = d // LANES
    assert nj == SUBLANES
    g = h2t.shape[0] // (tg * nj)
    kern = functools.partial(_moe_group_kernel, bm=bm, nj=nj, batch=batch, nb_max=nb_max)
    slots = pl.BlockSpec((nb_max * bm,), lambda gi, *_: (gi,), memory_space=pltpu.SMEM)
    block_w = pl.BlockSpec((nb_max, 1, bm), lambda gi, *_: (gi, 0, 0))
    hbm = pl.BlockSpec(memory_space=pl.ANY)
    pitch_rows = nj * (bm + 1)
    grid_spec = pltpu.PrefetchScalarGridSpec(
        num_scalar_prefetch=len(table),
        grid=(g,),
        in_specs=[pl.BlockSpec((tg * nj, LANES), lambda gi, *_: (gi, 0),
                               pipeline_mode=pl.Buffered(1)),
                  slots, block_w, hbm, hbm, hbm],
        out_specs=pl.BlockSpec((tg * nj, LANES), lambda gi, *_: (gi, 0),
                               pipeline_mode=pl.Buffered(1)),
        scratch_shapes=[pltpu.VMEM((MOE_WEIGHT_BUFFERS, d, eh), BF16),
                        pltpu.VMEM((MOE_WEIGHT_BUFFERS, d, eh), BF16),
                        pltpu.VMEM((MOE_WEIGHT_BUFFERS, eh, d), BF16),
                        pltpu.SemaphoreType.DMA((MOE_WEIGHT_BUFFERS,)),
                        pltpu.VMEM((pitch_rows, LANES), F32), pltpu.VMEM((pitch_rows, LANES), F32),
                        pltpu.VMEM((bm, eh), BF16), pltpu.VMEM((bm, eh), BF16),
                        pltpu.VMEM((pitch_rows, LANES), F32), pltpu.VMEM((pitch_rows, LANES), F32)],
    )
    return pl.pallas_call(
        kern,
        grid_spec=grid_spec,
        out_shape=jax.ShapeDtypeStruct((g * tg * nj, LANES), F32),
        compiler_params=_params(1),
        name="moe_group",
    )(*table, h2t, rows, wl.reshape(g * nb_max, 1, bm), wg, wu, wdn)


def _finalize_kernel(r_ref, xacc_ref, gt2_ref, gf_ref, o_ref, *, final_norm):
    tm, d = xacc_ref.shape
    nj = d // LANES
    routed = jnp.concatenate([r_ref[pl.ds(j, tm, stride=nj), :] for j in range(nj)], axis=-1)
    xo = xacc_ref[...] + gt2_ref[...] * routed
    o_ref[...] = _rmsnorm(xo, gf_ref[...]) if final_norm else xo


def _finalize(routed, xacc, gt2, g_final, seq_len, tm, final_norm):
    t, d = xacc.shape
    nj = d // LANES
    return pl.pallas_call(
        functools.partial(_finalize_kernel, final_norm=final_norm),
        grid=(t // tm,),
        in_specs=[_row_spec(tm * nj, LANES),
                  _row_spec(tm, d), _mod_spec(False, tm, d, seq_len // tm),
                  _full_spec(g_final.shape)],
        out_specs=_row_spec(tm, d),
        out_shape=jax.ShapeDtypeStruct((t, d), F32),
        compiler_params=_params(1),
        name="finalize",
    )(routed, xacc, gt2, g_final)


def _s5_discretize(lam_re, lam_im, log_dt, b_re, b_im, c_re, c_im):
    g, n = lam_re.shape
    hg = b_re.shape[2]
    lr = jnp.minimum(lam_re.astype(F32), -1e-4)
    li = lam_im.astype(F32)
    dt = jnp.exp(log_dt.astype(F32))[:, None]
    mag = jnp.exp(lr * dt)
    abar_r, abar_i = mag * jnp.cos(li * dt), mag * jnp.sin(li * dt)
    den = lr * lr + li * li
    nr, ni = abar_r - 1.0, abar_i
    f_r = (nr * lr + ni * li) / den
    f_i = (ni * lr - nr * li) / den
    br_, bi_ = b_re.astype(F32), b_im.astype(F32)
    bbar_r = f_r[..., None] * br_ - f_i[..., None] * bi_
    bbar_i = f_r[..., None] * bi_ + f_i[..., None] * br_
    gpc = LANES // hg
    n_chunks = g // gpc
    eye = jnp.eye(gpc, dtype=F32)

    def b_chunks(bb):
        bb = bb.reshape(n_chunks, gpc, n, hg)
        m = jnp.einsum('jgnh,gk->jghkn', bb, eye)
        return m.reshape(n_chunks, gpc * hg, gpc * n)

    def c_chunks(cc):
        cc = cc.astype(F32).reshape(n_chunks, gpc, hg, n)
        m = jnp.einsum('jghn,gk->jgnkh', cc, eye)
        return m.reshape(n_chunks, gpc * n, gpc * hg)

    bj = jnp.concatenate([b_chunks(bbar_r), b_chunks(bbar_i)], axis=2).astype(BF16)
    cj = jnp.concatenate([c_chunks(c_re), -c_chunks(c_im)], axis=1).astype(BF16)
    return abar_r.reshape(1, g * n), abar_i.reshape(1, g * n), bj, cj


def kernel(x_prompt, x_sample, c_prompt, c_sample, state_ssm_re, state_ssm_im, g_norm1, g_norm2, w_ada, b_ada, w_in, w_out, ssm_lambda_re, ssm_lambda_im, ssm_log_dt, ssm_b_re, ssm_b_im, ssm_c_re, ssm_c_im, ssm_d, ssm_w_glu, ssm_b_glu, gm_ln_g, gm_ln_b, gm_w_s, gm_b_s, router_w, router_bias, moe_w_gate, moe_w_up, moe_w_down, shared_w_gate, shared_w_up, shared_w_down, g_final):
    bp, seq_len, d = x_prompt.shape
    bs, dec_len, _ = x_sample.shape
    depth = w_ada.shape[0]
    assert dec_len == 1 and seq_len % CHUNK == 0 and bp % SUBLANES == 0
    groups, n_state = ssm_lambda_re.shape[1:]
    sw = groups * SSM_GROUP
    gw = gm_ln_g.shape[1]
    hd = gw // GM_HEADS

    xp = x_prompt.reshape(bp * seq_len, d)
    xs = x_sample.reshape(bs, d)
    c_all = jnp.concatenate([c_prompt, c_sample], axis=0)
    gf = g_final.reshape(1, d)

    p_re, p_im, s_re, s_im, s_v = [], [], [], [], []
    for l in range(depth):
        mod = _ada(c_all, w_ada[l], b_ada[l].reshape(1, -1))
        mod_p = [m.reshape(bp, 1, d) for m in jnp.split(mod[:bp], 6, axis=-1)]
        mod_s = jnp.split(mod[bp:], 6, axis=-1)

        g1 = g_norm1[l].reshape(1, d)
        g2 = g_norm2[l].reshape(1, d)
        w_in_b = w_in[l].astype(BF16)
        w_out_b = w_out[l].astype(BF16)
        ln_g = gm_ln_g[l].reshape(1, gw)
        ln_b = gm_ln_b[l].reshape(1, gw)
        ws_tril = (gm_w_s[l] * jnp.tril(jnp.ones((CHUNK, CHUNK), F32))[None]).astype(BF16)
        bs_full = jnp.repeat(gm_b_s[l].T, hd, axis=1)
        w0 = jnp.repeat(gm_w_s[l][:, 0, 0], hd).reshape(1, gw)
        b0 = jnp.repeat(gm_b_s[l][:, 0], hd).reshape(1, gw)
        a_re, a_im, bj, cj = _s5_discretize(ssm_lambda_re[l], ssm_lambda_im[l], ssm_log_dt[l],
                                            ssm_b_re[l], ssm_b_im[l], ssm_c_re[l], ssm_c_im[l])
        d_vec = ssm_d[l].reshape(1, sw)
        w_glu = ssm_w_glu[l].astype(BF16)
        b_glu = ssm_b_glu[l].reshape(1, sw)
        pad = LANES - N_EXPERTS
        rw = jnp.pad(router_w[l], ((0, 0), (0, pad)))
        rw_hi = rw.astype(BF16)
        rw_lo = (rw - rw_hi.astype(F32)).astype(BF16)
        rb = jnp.pad(router_bias[l], (0, pad), constant_values=-jnp.inf).reshape(1, LANES)
        rwt_hi, rwt_lo = rw_hi[:, :N_EXPERTS].T, rw_lo[:, :N_EXPERTS].T
        rb_col = router_bias[l].reshape(N_EXPERTS, 1)
        swg, swu, swd = (shared_w_gate[l].astype(BF16), shared_w_up[l].astype(BF16),
                         shared_w_down[l].astype(BF16))
        ewg, ewu, ewd = (moe_w_gate[l].astype(BF16), moe_w_up[l].astype(BF16),
                         moe_w_down[l].astype(BF16))

        tm_row = min(ROW_TILE, seq_len)
        ua, yb = _premix_seq(xp, mod_p[0], mod_p[1], g1, w_in_b, ln_g, ln_b, ws_tril, bs_full,
                             seq_len, tm=min(PREMIX_TILE, seq_len))
        ya, hpr, hpi = _s5_seq(ua.reshape(bp, seq_len, sw), bj, cj, a_re, a_im, d_vec,
                               w_glu, b_glu, tl=min(SCAN_TILE, seq_len))
        xacc, h2, wd = _postmix(xp, ya.reshape(bp * seq_len, sw), yb, mod_p[2], mod_p[3],
                                mod_p[4], mod_p[5], g2, w_out_b, rwt_hi, rwt_lo, rb_col,
                                swg, swu, swd,
                                per_row=False, seq_len=seq_len, tm=tm_row, token_tiles=True)
        last = l == depth - 1
        tg = min(MOE_GROUP, bp * seq_len)
        assert (bp * seq_len) % tg == 0
        bm = min(MOE_BLOCK, tg)
        per_tile = max(SMEM_TILE // bm, 1)
        nb_max = -(-(tg * TOP_K // bm + N_EXPERTS + 3) // per_tile) * per_tile
        slot_rows, slot_w, counts = _route_plan(wd, tg, d // LANES)
        table, block_row = _block_table(counts[:, :, 0], tg, bm, nb_max)
        by_block = lambda a: jnp.take(a.reshape(-1, bm), block_row, axis=0).reshape(-1)
        routed = _moe_group(table, h2, by_block(slot_rows), by_block(slot_w),
                            ewg, ewu, ewd, tg, bm, nb_max)
        xp = _finalize(routed, xacc, mod_p[5], gf, seq_len, tm_row, final_norm=last)
        p_re.append(hpr.reshape(bp, groups, n_state))
        p_im.append(hpi.reshape(bp, groups, n_state))

        ua, yb, vn = _premix_step(xs, mod_s[0], mod_s[1], g1, w_in_b, ln_g, ln_b, w0, b0)
        ya, hsr, hsi = _s5_step(ua, state_ssm_re[l].reshape(bs, groups * n_state),
                                state_ssm_im[l].reshape(bs, groups * n_state),
                                bj, cj, a_re, a_im, d_vec, w_glu, b_glu)
        xacc, h2, wd = _postmix(xs, ya, yb, mod_s[2], mod_s[3], mod_s[4], mod_s[5], g2, w_out_b,
                                rw_hi, rw_lo, rb, swg, swu, swd, per_row=True, seq_len=1, tm=bs,
                                token_tiles=False)
        xs = _moe_dense(h2, wd, xacc, mod_s[5], gf, ewg, ewu, ewd, per_row=True, seq_len=1,
                        tm=bs, final_norm=last)
        s_re.append(hsr.reshape(bs, groups, n_state))
        s_im.append(hsi.reshape(bs, groups, n_state))
        s_v.append(vn.reshape(bs, 1, gw))

    y_prompt = xp.reshape(bp, seq_len, d)
    y_sample = xs.reshape(bs, 1, d)
    return (y_prompt, y_sample, jnp.stack(p_re), jnp.stack(p_im), jnp.stack(s_re),
            jnp.stack(s_im), jnp.stack(s_v))
```

```python
import functools

import jax
import jax.numpy as jnp
from jax import lax
from jax.experimental import pallas as pl
from jax.experimental.pallas import tpu as pltpu

F32 = jnp.float32
BF16 = jnp.bfloat16

EPS = 1e-6
SSM_GROUP = 16
SSM_STATE = 64
GM_HEADS = 4
CHUNK = 128
N_EXPERTS = 64
TOP_K = 8
N_ROUTE_GROUPS = 8
TOPK_ROUTE_GROUPS = 4
ROUTED_SCALE = 2.5

LANES = 128
SUBLANES = 8
VMEM_LIMIT = 48 * 1024 * 1024
ROW_TILE = 512
PREMIX_TILE = 1024
DENSE_EXPERTS_PER_STEP = 8
SCAN_TILE = 128
MOE_GROUP = 4096
SMEM_TILE = 1024
MOE_BLOCK = 128
MOE_EARLY_ROW_PERCENT = 56
MOE_WEIGHT_BUFFERS = 4


def _params(n_axes):
    return pltpu.CompilerParams(dimension_semantics=("arbitrary",) * n_axes,
                                vmem_limit_bytes=VMEM_LIMIT)


def _dot(a, b):
    return jnp.dot(a.astype(BF16), b.astype(BF16), preferred_element_type=F32)


def _rmsnorm(x, g):
    ms = jnp.mean(x * x, axis=-1, keepdims=True)
    return x * lax.rsqrt(ms + EPS) * g


def _ada_kernel(c_ref, w_ref, b_ref, o_ref):
    c = c_ref[...]
    o_ref[...] = _dot(jax.nn.silu(c), w_ref[...]) + b_ref[...]


def _ada(c, w, b, tn=512):
    m, d = c.shape
    n = w.shape[1]
    return pl.pallas_call(
        _ada_kernel,
        grid=(n // tn,),
        in_specs=[pl.BlockSpec((m, d), lambda j: (0, 0)),
                  pl.BlockSpec((d, tn), lambda j: (0, j)),
                  pl.BlockSpec((1, tn), lambda j: (0, j))],
        out_specs=pl.BlockSpec((m, tn), lambda j: (0, j)),
        out_shape=jax.ShapeDtypeStruct((m, n), F32),
        compiler_params=_params(1),
        name="ada_mod",
    )(c, w, b)


def _premix_common(x_ref, sh_ref, sc_ref, g_ref, win_ref, lng_ref, lnb_ref, sw):
    x = x_ref[...]
    h = _rmsnorm(x, g_ref[...]) * (1.0 + sc_ref[...]) + sh_ref[...]
    proj = _dot(h, win_ref[...])
    u_a = proj[:, :sw]
    z = jax.nn.gelu(proj[:, sw:])
    gw = z.shape[1] // 2
    u_b = z[:, :gw]
    v_b = z[:, gw:]
    hd = gw // GM_HEADS
    parts = []
    for k in range(GM_HEADS):
        vh = v_b[:, k * hd:(k + 1) * hd]
        mu = jnp.mean(vh, axis=-1, keepdims=True)
        dv = vh - mu
        var = jnp.mean(dv * dv, axis=-1, keepdims=True)
        parts.append(dv * lax.rsqrt(var + EPS))
    vn = jnp.concatenate(parts, axis=-1) * lng_ref[...] + lnb_ref[...]
    return u_a, u_b, vn


def _premix_seq_kernel(x_ref, sh_ref, sc_ref, g_ref, win_ref, lng_ref, lnb_ref,
                       ws_ref, bs_ref, ua_ref, yb_ref, *, sw):
    u_a, u_b, vn = _premix_common(x_ref, sh_ref, sc_ref, g_ref, win_ref, lng_ref, lnb_ref, sw)
    ua_ref[...] = u_a
    tm, gw = u_b.shape
    hd = gw // GM_HEADS
    vnb = vn.astype(BF16)
    for c in range(tm // CHUNK):
        rows = slice(c * CHUNK, (c + 1) * CHUNK)
        for k in range(GM_HEADS):
            cols = slice(k * hd, (k + 1) * hd)
            s = jnp.dot(ws_ref[k], vnb[rows, cols], preferred_element_type=F32)
            yb_ref[rows, cols] = u_b[rows, cols] * (s + bs_ref[:, cols])


def _premix_step_kernel(x_ref, sh_ref, sc_ref, g_ref, win_ref, lng_ref, lnb_ref,
                        w0_ref, b0_ref, ua_ref, yb_ref, vn_ref, *, sw):
    u_a, u_b, vn = _premix_common(x_ref, sh_ref, sc_ref, g_ref, win_ref, lng_ref, lnb_ref, sw)
    ua_ref[...] = u_a
    vn_ref[...] = vn
    yb_ref[...] = u_b * (vn * w0_ref[...] + b0_ref[...])


def _row_spec(tm, d):
    return pl.BlockSpec((tm, d), lambda i: (i, 0))


def _full_spec(shape):
    nd = len(shape)
    return pl.BlockSpec(shape, lambda i: (0,) * nd)


def _mod_spec(per_row, tm, d, tiles_per_batch):
    if per_row:
        return pl.BlockSpec((tm, d), lambda i: (i, 0))
    return pl.BlockSpec((None, 1, d), lambda i: (i // tiles_per_batch, 0, 0))


def _premix_seq(x, sh, sc, g, w_in, ln_g, ln_b, ws_tril, bs_full, seq_len, tm):
    t, d = x.shape
    sw = ln_g.shape[1]
    tpb = seq_len // tm
    mod = _mod_spec(False, tm, d, tpb)
    return pl.pallas_call(
        functools.partial(_premix_seq_kernel, sw=sw),
        grid=(t // tm,),
        in_specs=[_row_spec(tm, d), mod, mod, _full_spec(g.shape), _full_spec(w_in.shape),
                  _full_spec(ln_g.shape), _full_spec(ln_b.shape), _full_spec(ws_tril.shape),
                  _full_spec(bs_full.shape)],
        out_specs=[_row_spec(tm, sw), _row_spec(tm, sw)],
        out_shape=[jax.ShapeDtypeStruct((t, sw), F32), jax.ShapeDtypeStruct((t, sw), F32)],
        compiler_params=_params(1),
        name="premix_seq",
    )(x, sh, sc, g, w_in, ln_g, ln_b, ws_tril, bs_full)


def _premix_step(x, sh, sc, g, w_in, ln_g, ln_b, w0, b0):
    t, d = x.shape
    sw = ln_g.shape[1]
    mod = _mod_spec(True, t, d, 1)
    return pl.pallas_call(
        functools.partial(_premix_step_kernel, sw=sw),
        grid=(1,),
        in_specs=[_row_spec(t, d), mod, mod, _full_spec(g.shape), _full_spec(w_in.shape),
                  _full_spec(ln_g.shape), _full_spec(ln_b.shape), _full_spec(w0.shape),
                  _full_spec(b0.shape)],
        out_specs=[_row_spec(t, sw)] * 3,
        out_shape=[jax.ShapeDtypeStruct((t, sw), F32)] * 3,
        compiler_params=_params(1),
        name="premix_step",
    )(x, sh, sc, g, w_in, ln_g, ln_b, w0, b0)


def _s5_gate(y, u, d_ref, wglu_ref, bglu_ref):
    y = jax.nn.gelu(y + d_ref[...] * u)
    gate = jax.nn.sigmoid(_dot(y, wglu_ref[...]) + bglu_ref[...])
    return y * gate


def _s5_seq_kernel(u_ref, bj_ref, cj_ref, are_ref, aim_ref, d_ref, wglu_ref, bglu_ref,
                   y_ref, sre_ref, sim_ref, bu_ref, st_ref, ut_ref, yt_ref,
                   *, nb, tl, n_chunks, unroll):
    i = pl.program_id(0)
    n_tiles = bu_ref.shape[0]
    tpc = n_tiles // n_chunks
    hpc = tpc // 2
    hw = hpc * LANES
    assert u_ref.shape[2] == n_chunks * LANES

    @pl.when(i == 0)
    def _():
        st_ref[...] = jnp.zeros_like(st_ref)

    for b in range(nb):
        ub = u_ref[b]
        for j in range(n_chunks):
            ut_ref[j, pl.ds(b, tl, stride=nb), :] = ub[:, j * LANES:(j + 1) * LANES]
    for j in range(n_chunks):
        bu = jnp.dot(ut_ref[j].astype(BF16), bj_ref[j], preferred_element_type=F32)
        for c in range(tpc):
            bu_ref[j * tpc + c] = bu[:, c * LANES:(c + 1) * LANES]

    for j in range(n_chunks):
        re = slice(j * 2 * hw, j * 2 * hw + hw)
        im = slice(j * 2 * hw + hw, (j + 1) * 2 * hw)
        ar = jnp.broadcast_to(are_ref[:, j * hw:(j + 1) * hw], (nb, hw))
        ai = jnp.broadcast_to(aim_ref[:, j * hw:(j + 1) * hw], (nb, hw))
        re_tiles = [j * tpc + c for c in range(hpc)]
        im_tiles = [j * tpc + hpc + c for c in range(hpc)]

        def body(s, carry, ar=ar, ai=ai, re_tiles=re_tiles, im_tiles=im_tiles):
            xr, xi = carry
            for k in range(unroll):
                r0 = pl.multiple_of((s * unroll + k) * nb, nb)
                br = jnp.concatenate([bu_ref[c, pl.ds(r0, nb), :] for c in re_tiles], axis=-1)
                bi = jnp.concatenate([bu_ref[c, pl.ds(r0, nb), :] for c in im_tiles], axis=-1)
                nr = ar * xr - ai * xi + br
                ni = ar * xi + ai * xr + bi
                for q, c in enumerate(re_tiles):
                    bu_ref[c, pl.ds(r0, nb), :] = nr[:, q * LANES:(q + 1) * LANES]
                for q, c in enumerate(im_tiles):
                    bu_ref[c, pl.ds(r0, nb), :] = ni[:, q * LANES:(q + 1) * LANES]
                xr, xi = nr, ni
            return xr, xi

        xr, xi = lax.fori_loop(0, tl // unroll, body, (st_ref[:, re], st_ref[:, im]))
        st_ref[:, re] = xr
        st_ref[:, im] = xi
        sre_ref[:, j * hw:(j + 1) * hw] = xr
        sim_ref[:, j * hw:(j + 1) * hw] = xi

    for j in range(n_chunks):
        xs = jnp.concatenate([bu_ref[j * tpc + c] for c in range(tpc)], axis=-1)
        yt_ref[j] = jnp.dot(xs.astype(BF16), cj_ref[j], preferred_element_type=F32)
    y = jnp.concatenate(
        [jnp.concatenate([yt_ref[j, pl.ds(b, tl, stride=nb), :] for j in range(n_chunks)], axis=-1)
         for b in range(nb)], axis=0)
    u_all = u_ref[...].reshape(nb * tl, u_ref.shape[2])
    y_ref[...] = _s5_gate(y, u_all, d_ref, wglu_ref, bglu_ref).reshape(y_ref.shape)


def _s5_seq(u, bj, cj, a_re, a_im, d_vec, w_glu, b_glu, tl, unroll=4):
    nb, seq_len, sw = u.shape
    n_chunks, _, cw = bj.shape
    ns = a_re.shape[1]
    kern = functools.partial(_s5_seq_kernel, nb=nb, tl=tl, n_chunks=n_chunks, unroll=unroll)
    return pl.pallas_call(
        kern,
        grid=(seq_len // tl,),
        in_specs=[pl.BlockSpec((nb, tl, sw), lambda i: (0, i, 0)),
                  _full_spec(bj.shape), _full_spec(cj.shape), _full_spec(a_re.shape),
                  _full_spec(a_im.shape), _full_spec(d_vec.shape), _full_spec(w_glu.shape),
                  _full_spec(b_glu.shape)],
        out_specs=[pl.BlockSpec((nb, tl, sw), lambda i: (0, i, 0)),
                   _full_spec((nb, ns)), _full_spec((nb, ns))],
        out_shape=[jax.ShapeDtypeStruct((nb, seq_len, sw), F32),
                   jax.ShapeDtypeStruct((nb, ns), F32), jax.ShapeDtypeStruct((nb, ns), F32)],
        scratch_shapes=[pltpu.VMEM((n_chunks * cw // LANES, tl * nb, LANES), F32),
                        pltpu.VMEM((nb, n_chunks * cw), F32),
                        pltpu.VMEM((n_chunks, tl * nb, LANES), F32),
                        pltpu.VMEM((n_chunks, tl * nb, LANES), F32)],
        compiler_params=_params(1),
        name="s5_seq",
    )(u, bj, cj, a_re, a_im, d_vec, w_glu, b_glu)


def _s5_step_kernel(u_ref, hre_ref, him_ref, bj_ref, cj_ref, are_ref, aim_ref, d_ref,
                    wglu_ref, bglu_ref, y_ref, sre_ref, sim_ref, *, n_chunks):
    u = u_ref[...]
    ub = u.astype(BF16)
    kw = u.shape[1] // n_chunks
    hw = bj_ref.shape[2] // 2
    chunks = []
    for j in range(n_chunks):
        st = slice(j * hw, (j + 1) * hw)
        bu = jnp.dot(ub[:, j * kw:(j + 1) * kw], bj_ref[j], preferred_element_type=F32)
        ar, ai = are_ref[:, st], aim_ref[:, st]
        hr, hi = hre_ref[:, st], him_ref[:, st]
        xr = ar * hr - ai * hi + bu[:, :hw]
        xi = ar * hi + ai * hr + bu[:, hw:]
        sre_ref[:, st] = xr
        sim_ref[:, st] = xi
        chunks.append(jnp.concatenate([xr, xi], axis=-1))
    y = jnp.concatenate([jnp.dot(chunks[j].astype(BF16), cj_ref[j], preferred_element_type=F32)
                         for j in range(n_chunks)], axis=-1)
    y_ref[...] = _s5_gate(y, u, d_ref, wglu_ref, bglu_ref)


def _s5_step(u, h_re, h_im, bj, cj, a_re, a_im, d_vec, w_glu, b_glu):
    t, sw = u.shape
    ns = a_re.shape[1]
    args = (u, h_re, h_im, bj, cj, a_re, a_im, d_vec, w_glu, b_glu)
    return pl.pallas_call(
        functools.partial(_s5_step_kernel, n_chunks=bj.shape[0]),
        grid=(1,),
        in_specs=[_full_spec(a.shape) for a in args],
        out_specs=[_full_spec((t, sw)), _full_spec((t, ns)), _full_spec((t, ns))],
        out_shape=[jax.ShapeDtypeStruct((t, sw), F32), jax.ShapeDtypeStruct((t, ns), F32),
                   jax.ShapeDtypeStruct((t, ns), F32)],
        compiler_params=_params(1),
        name="s5_step",
    )(*args)


def _group_reduce(x, lane, width, op):
    n = x.shape[-1]
    s = 1
    while s < width:
        up = pltpu.roll(x, s, 1)
        dn = pltpu.roll(x, n - s, 1)
        x = op(x, jnp.where((lane & s) != 0, up, dn))
        s *= 2
    return x


def _first_max(x, lane, big):
    m = jnp.max(x, axis=-1, keepdims=True)
    return jnp.min(jnp.where(x == m, lane, big), axis=-1, keepdims=True)


def _route(logits, bias):
    rows, n = logits.shape
    lane = lax.broadcasted_iota(jnp.int32, (rows, n), 1)
    lane_f = lane.astype(F32)
    neg = jnp.float32(-jnp.inf)
    big = jnp.float32(n)
    gsz = N_EXPERTS // N_ROUTE_GROUPS
    group_start = (lane - (lane & (gsz - 1))).astype(F32)
    scores = jax.nn.sigmoid(logits)
    biased = scores + bias
    m1 = _group_reduce(biased, lane, gsz, jnp.maximum)
    first = _group_reduce(jnp.where(biased == m1, lane_f, big), lane, gsz, jnp.minimum)
    m2 = _group_reduce(jnp.where(lane_f == first, neg, biased), lane, gsz, jnp.maximum)
    gs = m1 + m2
    gsel = jnp.zeros((rows, n), F32)
    work = gs
    for _ in range(TOPK_ROUTE_GROUPS):
        hit = group_start == _first_max(work, lane_f, big)
        gsel = jnp.where(hit, 1.0, gsel)
        work = jnp.where(hit, neg, work)
    work = jnp.where(gsel > 0.0, biased, neg)
    w = jnp.zeros((rows, n), F32)
    for _ in range(TOP_K):
        hit = lane_f == _first_max(work, lane_f, big)
        w = jnp.where(hit, scores, w)
        work = jnp.where(hit, neg, work)
    return w / jnp.sum(w, axis=-1, keepdims=True) * ROUTED_SCALE


def _route_by_expert(logits, bias):
    ne, tm = logits.shape
    ng = N_ROUTE_GROUPS
    gsz = ne // ng
    neg = jnp.float32(-jnp.inf)
    shape = (ng, gsz, tm)
    scores = jax.nn.sigmoid(logits)
    biased = (scores + bias).reshape(shape)
    scores = scores.reshape(shape)
    sub = lax.broadcasted_iota(jnp.int32, shape, 1)
    grp = lax.broadcasted_iota(jnp.int32, (ng, 1, tm), 0)
    eid = lax.broadcasted_iota(jnp.int32, shape, 0) * gsz + sub
    m1 = jnp.max(biased, axis=1, keepdims=True)
    first = jnp.min(jnp.where(biased == m1, sub, gsz), axis=1, keepdims=True)
    m2 = jnp.max(jnp.where(sub == first, neg, biased), axis=1, keepdims=True)
    work = m1 + m2
    gsel = jnp.zeros((ng, 1, tm), jnp.bool_)
    for _ in range(TOPK_ROUTE_GROUPS):
        m = jnp.max(work, axis=0, keepdims=True)
        hit = grp == jnp.min(jnp.where(work == m, grp, ng), axis=0, keepdims=True)
        gsel = jnp.logical_or(gsel, hit)
        work = jnp.where(hit, neg, work)
    work = jnp.where(gsel, biased, neg)
    w = jnp.zeros(shape, F32)
    for _ in range(TOP_K):
        m = jnp.max(jnp.max(work, axis=0, keepdims=True), axis=1, keepdims=True)
        first = jnp.min(jnp.min(jnp.where(work == m, eid, ne), axis=0, keepdims=True),
                        axis=1, keepdims=True)
        hit = eid == first
        w = jnp.where(hit, scores, w)
        work = jnp.where(hit, neg, work)
    denom = jnp.sum(jnp.sum(w, axis=0, keepdims=True), axis=1, keepdims=True)
    return (w / denom * ROUTED_SCALE).reshape(ne, tm)


def _split_bf16(x):
    hi = x.astype(BF16)
    lo = (x - hi.astype(F32)).astype(BF16)
    return hi, lo


def _postmix_kernel(x_ref, ya_ref, yb_ref, gt1_ref, sh2_ref, sc2_ref, gt2_ref, g2_ref,
                    wout_ref, rwh_ref, rwl_ref, rb_ref, swg_ref, swu_ref, swd_ref,
                    xacc_ref, h2_ref, wd_ref, *, token_tiles):
    y = jnp.concatenate([ya_ref[...], yb_ref[...]], axis=-1)
    x1 = x_ref[...] + gt1_ref[...] * _dot(y, wout_ref[...])
    h2 = _rmsnorm(x1, g2_ref[...]) * (1.0 + sc2_ref[...]) + sh2_ref[...]
    hb = h2.astype(BF16)
    if token_tiles:
        tm, d = h2.shape
        nj = d // LANES
        for j in range(nj):
            h2_ref[pl.ds(j, tm, stride=nj), :] = h2[:, j * LANES:(j + 1) * LANES]
    else:
        h2_ref[...] = hb
    h_hi, h_lo = _split_bf16(h2)
    if token_tiles:
        nt = (((1,), (1,)), ((), ()))
        logits = (lax.dot_general(rwh_ref[...], h_hi, nt, preferred_element_type=F32)
                  + lax.dot_general(rwl_ref[...], h_hi, nt, preferred_element_type=F32)
                  + lax.dot_general(rwh_ref[...], h_lo, nt, preferred_element_type=F32))
        wd_ref[...] = _route_by_expert(logits, rb_ref[...])
    else:
        logits = (jnp.dot(h_hi, rwh_ref[...], preferred_element_type=F32)
                  + jnp.dot(h_hi, rwl_ref[...], preferred_element_type=F32)
                  + jnp.dot(h_lo, rwh_ref[...], preferred_element_type=F32))
        wd_ref[...] = _route(logits, rb_ref[...])
    hid = jax.nn.silu(jnp.dot(hb, swg_ref[...], preferred_element_type=F32)) * \
        jnp.dot(hb, swu_ref[...], preferred_element_type=F32)
    shared = _dot(hid, swd_ref[...])
    xacc_ref[...] = x1 + gt2_ref[...] * shared


def _postmix(x, ya, yb, gt1, sh2, sc2, gt2, g2, w_out, rw_hi, rw_lo, rb, swg, swu, swd,
             per_row, seq_len, tm, token_tiles):
    t, d = x.shape
    sw = ya.shape[1]
    nj = d // LANES
    mod = _mod_spec(per_row, tm, d, max(seq_len // tm, 1))
    weights = (g2, w_out, rw_hi, rw_lo, rb, swg, swu, swd)
    if token_tiles:
        ne = rw_hi.shape[0]
        h2_spec, h2_shape = _row_spec(tm * nj, LANES), jax.ShapeDtypeStruct((t * nj, LANES), F32)
        wd_spec, wd_shape = pl.BlockSpec((ne, tm), lambda i: (0, i)), (ne, t)
    else:
        ne = rw_hi.shape[1]
        h2_spec, h2_shape = _row_spec(tm, d), jax.ShapeDtypeStruct((t, d), BF16)
        wd_spec, wd_shape = _row_spec(tm, ne), (t, ne)
    return pl.pallas_call(
        functools.partial(_postmix_kernel, token_tiles=token_tiles),
        grid=(t // tm,),
        in_specs=[_row_spec(tm, d), _row_spec(tm, sw), _row_spec(tm, sw), mod, mod, mod, mod]
        + [_full_spec(w.shape) for w in weights],
        out_specs=[_row_spec(tm, d), h2_spec, wd_spec],
        out_shape=[jax.ShapeDtypeStruct((t, d), F32), h2_shape,
                   jax.ShapeDtypeStruct(wd_shape, F32)],
        compiler_params=_params(1),
        name="postmix",
    )(x, ya, yb, gt1, sh2, sc2, gt2, *weights)


def _moe_dense_kernel(h_ref, wd_ref, xacc_ref, gt2_ref, gf_ref, wg_ref, wu_ref, wdn_ref,
                      o_ref, acc_ref, *, final_norm):
    step = pl.program_id(1)
    per_step = wg_ref.shape[0]

    @pl.when(step == 0)
    def _():
        acc_ref[...] = jnp.zeros_like(acc_ref)

    hb = h_ref[...]
    wd = wd_ref[...]
    lane = lax.broadcasted_iota(jnp.int32, wd.shape, 1)
    total = None
    for q in range(per_step):
        col = jnp.sum(jnp.where(lane == step * per_step + q, wd, 0.0), axis=-1, keepdims=True)
        hid = jax.nn.silu(jnp.dot(hb, wg_ref[q], preferred_element_type=F32)) * \
            jnp.dot(hb, wu_ref[q], preferred_element_type=F32)
        part = col * _dot(hid, wdn_ref[q])
        total = part if total is None else total + part
    acc_ref[...] += total

    @pl.when(step == pl.num_programs(1) - 1)
    def _():
        xo = xacc_ref[...] + gt2_ref[...] * acc_ref[...]
        o_ref[...] = _rmsnorm(xo, gf_ref[...]) if final_norm else xo


def _moe_dense(h2, wd, xacc, gt2, g_final, wg, wu, wdn, per_row, seq_len, tm, final_norm):
    t, d = xacc.shape
    ne, _, eh = wg.shape
    nl = wd.shape[1]
    tpb = max(seq_len // tm, 1)
    if per_row:
        mod = pl.BlockSpec((tm, d), lambda i, e: (i, 0))
    else:
        mod = pl.BlockSpec((None, 1, d), lambda i, e: (i // tpb, 0, 0))
    row = lambda w: pl.BlockSpec((tm, w), lambda i, e: (i, 0))
    per_step = DENSE_EXPERTS_PER_STEP
    assert ne % per_step == 0
    return pl.pallas_call(
        functools.partial(_moe_dense_kernel, final_norm=final_norm),
        grid=(t // tm, ne // per_step),
        in_specs=[row(d), row(nl), row(d), mod,
                  pl.BlockSpec(g_final.shape, lambda i, e: (0, 0)),
                  pl.BlockSpec((per_step, d, eh), lambda i, e: (e, 0, 0)),
                  pl.BlockSpec((per_step, d, eh), lambda i, e: (e, 0, 0)),
                  pl.BlockSpec((per_step, eh, d), lambda i, e: (e, 0, 0))],
        out_specs=row(d),
        out_shape=jax.ShapeDtypeStruct((t, d), F32),
        scratch_shapes=[pltpu.VMEM((tm, d), F32)],
        compiler_params=_params(2),
        name="moe_dense",
    )(h2, wd, xacc, gt2, g_final, wg, wu, wdn)


TOKEN_BITS = 13


def _route_plan_kernel(wd_ref, rows_ref, wl_ref, cnt_ref, *, tg, nj):
    wt = wd_ref[...]
    sel = wt > 0.0
    lane = lax.broadcasted_iota(jnp.int32, wt.shape, 1)
    c = sel.astype(jnp.int32)
    s = 1
    while s < tg:
        c = c + jnp.where(lane >= s, pltpu.roll(c, s, 1), 0)
        s *= 2
    cnt_ref[...] = jnp.broadcast_to(c[:, tg - 1:tg], cnt_ref.shape)
    dist = lane - (c - 1)
    key = jnp.where(sel, (dist << TOKEN_BITS) | lane, -1)
    w = jnp.where(sel, wt, 0.0)
    s = 1
    while s < tg:
        leaving = (key >= 0) & (((key >> TOKEN_BITS) & s) != 0)
        key_in = pltpu.roll(key, tg - s, 1)
        w_in = pltpu.roll(w, tg - s, 1)
        arriving = (key_in >= 0) & (((key_in >> TOKEN_BITS) & s) != 0)
        key = jnp.where(arriving, key_in, jnp.where(leaving, -1, key))
        w = jnp.where(arriving, w_in, jnp.where(leaving, 0.0, w))
        s *= 2
    rows_ref[...] = jnp.where(key >= 0, key & ((1 << TOKEN_BITS) - 1), 0) * nj
    wl_ref[...] = w


def _route_plan(wd, tg, nj):
    ne, t = wd.shape
    assert ne == N_EXPERTS and tg <= (1 << TOKEN_BITS) and t % tg == 0
    g = t // tg
    grp = lambda w: pl.BlockSpec((None, N_EXPERTS, w), lambda i: (i, 0, 0))
    return pl.pallas_call(
        functools.partial(_route_plan_kernel, tg=tg, nj=nj),
        grid=(g,),
        in_specs=[pl.BlockSpec((ne, tg), lambda i: (0, i))],
        out_specs=[grp(tg), grp(tg), grp(LANES)],
        out_shape=[jax.ShapeDtypeStruct((g, N_EXPERTS, tg), jnp.int32),
                   jax.ShapeDtypeStruct((g, N_EXPERTS, tg), F32),
                   jax.ShapeDtypeStruct((g, N_EXPERTS, LANES), jnp.int32)],
        compiler_params=_params(1),
        name="route_plan",
    )(wd)


def _block_table(counts, tg, bm, nb_max):
    g, ne = counts.shape
    i32 = jnp.int32
    nblk = (counts + bm - 1) // bm
    bend = jnp.cumsum(nblk, axis=1)
    bstart = bend - nblk
    nb = bend[:, -1]
    k = jnp.arange(nb_max, dtype=i32)[None, :]
    kc = jnp.minimum(k, jnp.maximum(nb[:, None] - 1, 0))
    e_k = jnp.minimum(jnp.sum(bend[:, None, :] <= kc[:, :, None], axis=2), ne - 1).astype(i32)
    b_k = kc - jnp.take_along_axis(bstart, e_k, axis=1)
    first = (b_k == 0) & (k < nb[:, None])
    active = nblk > 0
    parity = (jnp.take_along_axis(jnp.cumsum(active.astype(i32), axis=1) - 1, e_k, axis=1)
              % MOE_WEIGHT_BUFFERS)
    ids = jnp.where(active, jnp.arange(ne, dtype=i32)[None, :], ne)
    first_active_from = jnp.flip(lax.cummin(jnp.flip(ids, 1), axis=1), 1)
    nxt = jnp.concatenate([first_active_from[:, 1:], jnp.full((g, 1), ne, i32)], axis=1)
    nxt = jnp.where(nxt >= ne, -1, nxt)
    nxt2 = jnp.where(nxt < 0, -1, jnp.take_along_axis(nxt, jnp.maximum(nxt, 0), axis=1))
    next_k = jnp.take_along_axis(nxt, e_k, axis=1)
    next2_k = jnp.take_along_axis(nxt2, e_k, axis=1)
    row = (jnp.arange(g, dtype=i32)[:, None] * ne + e_k) * (tg // bm) + b_k
    flat = lambda a: a.reshape(-1).astype(i32)
    table = (nb.astype(i32), flat(e_k), flat(first), flat(parity), flat(next_k), flat(next2_k))
    return table, flat(row)


def _moe_group_kernel(nb_ref, exp_ref, first_ref, par_ref, next_ref, next2_ref,
                      h_ref, rows_ref, wl_ref, wg_hbm, wu_hbm, wdn_hbm, o_ref,
                      wg_buf, wu_buf, wdn_buf, wsem, xt_a, xt_b, hid_a, hid_b, yt_a, yt_b,
                      *, bm, nj, batch, nb_max):
    g = pl.program_id(0)
    nb = nb_ref[g]
    tb = g * nb_max
    pitch = bm + 1
    o_ref[...] = jnp.zeros_like(o_ref)

    def weight_copies(e, slot):
        return (pltpu.make_async_copy(wg_hbm.at[e], wg_buf.at[slot], wsem.at[slot]),
                pltpu.make_async_copy(wu_hbm.at[e], wu_buf.at[slot], wsem.at[slot]),
                pltpu.make_async_copy(wdn_hbm.at[e], wdn_buf.at[slot], wsem.at[slot]))

    n_wbuf = wg_buf.shape[0]

    def handoff(k):
        @pl.when(first_ref[tb + k] == 1)
        def _():
            slot = par_ref[tb + k]
            for c in weight_copies(exp_ref[tb + k], slot):
                c.wait()
            ahead = next2_ref[tb + k]

            @pl.when(ahead >= 0)
            def _():
                for c in weight_copies(ahead, (slot + 2) % n_wbuf):
                    c.start()

    def gather_rows(k, xt):
        rows = rows_ref.at[pl.ds(pl.multiple_of(k * bm, bm), bm)]

        def emit(lo, hi):
            for m in range(lo, hi):
                r0 = pl.multiple_of(rows[m], nj)
                xt[pl.ds(m, nj, stride=pitch), :] = h_ref[pl.ds(r0, nj), :]
        return emit

    def scatter_rows(k, yt):
        rows = rows_ref.at[pl.ds(pl.multiple_of(k * bm, bm), bm)]

        def emit(lo, hi):
            for m0 in range(lo, hi, batch):
                updates = []
                for m in range(m0, min(m0 + batch, hi)):
                    r0 = pl.multiple_of(rows[m], nj)
                    updates.append((r0, o_ref[pl.ds(r0, nj), :]
                                    + yt[pl.ds(m, nj, stride=pitch), :]))
                for r0, v in reversed(updates):
                    o_ref[pl.ds(r0, nj), :] = v
        return emit

    n_chunks = nj // 2
    kc = 2 * LANES
    n_pieces = 2 * n_chunks + 1
    early = bm * MOE_EARLY_ROW_PERCENT // 100
    cuts = [early * i // (n_pieces - 1) for i in range(n_pieces)] + [bm]

    def step(up_blk, down_blk, side_work):
        piece = 0

        def side():
            nonlocal piece
            for emit in side_work:
                emit(cuts[piece], cuts[piece + 1])
            piece += 1

        gate = up = None
        for c in range(n_chunks):
            if up_blk is not None:
                k, xt, _ = up_blk
                slot = par_ref[tb + k]
                xc = jnp.concatenate([xt[j * pitch:j * pitch + bm, :] for j in (2 * c, 2 * c + 1)],
                                     axis=-1).astype(BF16)
                pg = jnp.dot(xc, wg_buf[slot, c * kc:(c + 1) * kc, :], preferred_element_type=F32)
                pu = jnp.dot(xc, wu_buf[slot, c * kc:(c + 1) * kc, :], preferred_element_type=F32)
                gate = pg if gate is None else gate + pg
                up = pu if up is None else up + pu
            side()
        for c in range(n_chunks):
            if down_blk is not None:
                k, hid, yt = down_blk
                slot = par_ref[tb + k]
                yc = jnp.dot(hid[...], wdn_buf[slot, :, c * kc:(c + 1) * kc],
                             preferred_element_type=F32)
                for q in range(2):
                    j = 2 * c + q
                    yt[j * pitch:j * pitch + bm, :] = yc[:, q * LANES:(q + 1) * LANES]
            side()
        side()
        if up_blk is not None:
            k, _, hid = up_blk
            w_col = jnp.broadcast_to(wl_ref[k], (SUBLANES, bm)).T[:, :1]
            hid[...] = (jax.nn.silu(gate) * up * w_col).astype(BF16)

    @pl.when(nb > 0)
    def _():
        for c in weight_copies(exp_ref[tb], 0):
            c.start()
        second = next_ref[tb]

        @pl.when(second >= 0)
        def _():
            for c in weight_copies(second, 1):
                c.start()
        gather_rows(0, xt_a)(0, bm)
        handoff(0)
        step((0, xt_a, hid_a), None, [gather_rows(1, xt_b)])
        handoff(1)
        step((1, xt_b, hid_b), (0, hid_a, yt_a), [gather_rows(2, xt_a)])

        def trip(i, carry):
            t = 2 * i + 1
            handoff(t + 1)
            step((t + 1, xt_a, hid_a), (t, hid_b, yt_b),
                 [gather_rows(t + 2, xt_b), scatter_rows(t - 1, yt_a)])
            handoff(t + 2)
            step((t + 2, xt_b, hid_b), (t + 1, hid_a, yt_a),
                 [gather_rows(t + 3, xt_a), scatter_rows(t, yt_b)])
            return carry

        lax.fori_loop(0, nb // 2, trip, 0)

        @pl.when(nb % 2 == 1)
        def _():
            scatter_rows(nb - 1, yt_a)(0, bm)


def _moe_group(table, h2t, rows, wl, wg, wu, wdn, tg, bm, nb_max, batch=4):
    ne, d, eh = wg.shape
    nj = d // LANES
    assert nj == SUBLANES
    g = h2t.shape[0] // (tg * nj)
    kern = functools.partial(_moe_group_kernel, bm=bm, nj=nj, batch=batch, nb_max=nb_max)
    slots = pl.BlockSpec((nb_max * bm,), lambda gi, *_: (gi,), memory_space=pltpu.SMEM)
    block_w = pl.BlockSpec((nb_max, 1, bm), lambda gi, *_: (gi, 0, 0))
    hbm = pl.BlockSpec(memory_space=pl.ANY)
    pitch_rows = nj * (bm + 1)
    grid_spec = pltpu.PrefetchScalarGridSpec(
        num_scalar_prefetch=len(table),
        grid=(g,),
        in_specs=[pl.BlockSpec((tg * nj, LANES), lambda gi, *_: (gi, 0),
                               pipeline_mode=pl.Buffered(1)),
                  slots, block_w, hbm, hbm, hbm],
        out_specs=pl.BlockSpec((tg * nj, LANES), lambda gi, *_: (gi, 0),
                               pipeline_mode=pl.Buffered(1)),
        scratch_shapes=[pltpu.VMEM((MOE_WEIGHT_BUFFERS, d, eh), BF16),
                        pltpu.VMEM((MOE_WEIGHT_BUFFERS, d, eh), BF16),
                        pltpu.VMEM((MOE_WEIGHT_BUFFERS, eh, d), BF16),
                        pltpu.SemaphoreType.DMA((MOE_WEIGHT_BUFFERS,)),
                        pltpu.VMEM((pitch_rows, LANES), F32), pltpu.VMEM((pitch_rows, LANES), F32),
                        pltpu.VMEM((bm, eh), BF16), pltpu.VMEM((bm, eh), BF16),
                        pltpu.VMEM((pitch_rows, LANES), F32), pltpu.VMEM((pitch_rows, LANES), F32)],
    )
    return pl.pallas_call(
        kern,
        grid_spec=grid_spec,
        out_shape=jax.ShapeDtypeStruct((g * tg * nj, LANES), F32),
        compiler_params=_params(1),
        name="moe_group",
    )(*table, h2t, rows, wl.reshape(g * nb_max, 1, bm), wg, wu, wdn)


def _finalize_kernel(r_ref, xacc_ref, gt2_ref, gf_ref, o_ref, *, final_norm):
    tm, d = xacc_ref.shape
    nj = d // LANES
    routed = jnp.concatenate([r_ref[pl.ds(j, tm, stride=nj), :] for j in range(nj)], axis=-1)
    xo = xacc_ref[...] + gt2_ref[...] * routed
    o_ref[...] = _rmsnorm(xo, gf_ref[...]) if final_norm else xo


def _finalize(routed, xacc, gt2, g_final, seq_len, tm, final_norm):
    t, d = xacc.shape
    nj = d // LANES
    return pl.pallas_call(
        functools.partial(_finalize_kernel, final_norm=final_norm),
        grid=(t // tm,),
        in_specs=[_row_spec(tm * nj, LANES),
                  _row_spec(tm, d), _mod_spec(False, tm, d, seq_len // tm),
                  _full_spec(g_final.shape)],
        out_specs=_row_spec(tm, d),
        out_shape=jax.ShapeDtypeStruct((t, d), F32),
        compiler_params=_params(1),
        name="finalize",
    )(routed, xacc, gt2, g_final)


def _s5_discretize(lam_re, lam_im, log_dt, b_re, b_im, c_re, c_im):
    g, n = lam_re.shape
    hg = b_re.shape[2]
    lr = jnp.minimum(lam_re.astype(F32), -1e-4)
    li = lam_im.astype(F32)
    dt = jnp.exp(log_dt.astype(F32))[:, None]
    mag = jnp.exp(lr * dt)
    abar_r, abar_i = mag * jnp.cos(li * dt), mag * jnp.sin(li * dt)
    den = lr * lr + li * li
    nr, ni = abar_r - 1.0, abar_i
    f_r = (nr * lr + ni * li) / den
    f_i = (ni * lr - nr * li) / den
    br_, bi_ = b_re.astype(F32), b_im.astype(F32)
    bbar_r = f_r[..., None] * br_ - f_i[..., None] * bi_
    bbar_i = f_r[..., None] * bi_ + f_i[..., None] * br_
    gpc = LANES // hg
    n_chunks = g // gpc
    eye = jnp.eye(gpc, dtype=F32)

    def b_chunks(bb):
        bb = bb.reshape(n_chunks, gpc, n, hg)
        m = jnp.einsum('jgnh,gk->jghkn', bb, eye)
        return m.reshape(n_chunks, gpc * hg, gpc * n)

    def c_chunks(cc):
        cc = cc.astype(F32).reshape(n_chunks, gpc, hg, n)
        m = jnp.einsum('jghn,gk->jgnkh', cc, eye)
        return m.reshape(n_chunks, gpc * n, gpc * hg)

    bj = jnp.concatenate([b_chunks(bbar_r), b_chunks(bbar_i)], axis=2).astype(BF16)
    cj = jnp.concatenate([c_chunks(c_re), -c_chunks(c_im)], axis=1).astype(BF16)
    return abar_r.reshape(1, g * n), abar_i.reshape(1, g * n), bj, cj


def kernel(x_prompt, x_sample, c_prompt, c_sample, state_ssm_re, state_ssm_im, g_norm1, g_norm2, w_ada, b_ada, w_in, w_out, ssm_lambda_re, ssm_lambda_im, ssm_log_dt, ssm_b_re, ssm_b_im, ssm_c_re, ssm_c_im, ssm_d, ssm_w_glu, ssm_b_glu, gm_ln_g, gm_ln_b, gm_w_s, gm_b_s, router_w, router_bias, moe_w_gate, moe_w_up, moe_w_down, shared_w_gate, shared_w_up, shared_w_down, g_final):
    bp, seq_len, d = x_prompt.shape
    bs, dec_len, _ = x_sample.shape
    depth = w_ada.shape[0]
    assert dec_len == 1 and seq_len % CHUNK == 0 and bp % SUBLANES == 0
    groups, n_state = ssm_lambda_re.shape[1:]
    sw = groups * SSM_GROUP
    gw = gm_ln_g.shape[1]
    hd = gw // GM_HEADS

    xp = x_prompt.reshape(bp * seq_len, d)
    xs = x_sample.reshape(bs, d)
    c_all = jnp.concatenate([c_prompt, c_sample], axis=0)
    gf = g_final.reshape(1, d)

    p_re, p_im, s_re, s_im, s_v = [], [], [], [], []
    for l in range(depth):
        mod = _ada(c_all, w_ada[l], b_ada[l].reshape(1, -1))
        mod_p = [m.reshape(bp, 1, d) for m in jnp.split(mod[:bp], 6, axis=-1)]
        mod_s = jnp.split(mod[bp:], 6, axis=-1)

        g1 = g_norm1[l].reshape(1, d)
        g2 = g_norm2[l].reshape(1, d)
        w_in_b = w_in[l].astype(BF16)
        w_out_b = w_out[l].astype(BF16)
        ln_g = gm_ln_g[l].reshape(1, gw)
        ln_b = gm_ln_b[l].reshape(1, gw)
        ws_tril = (gm_w_s[l] * jnp.tril(jnp.ones((CHUNK, CHUNK), F32))[None]).astype(BF16)
        bs_full = jnp.repeat(gm_b_s[l].T, hd, axis=1)
        w0 = jnp.repeat(gm_w_s[l][:, 0, 0], hd).reshape(1, gw)
        b0 = jnp.repeat(gm_b_s[l][:, 0], hd).reshape(1, gw)
        a_re, a_im, bj, cj = _s5_discretize(ssm_lambda_re[l], ssm_lambda_im[l], ssm_log_dt[l],
                                            ssm_b_re[l], ssm_b_im[l], ssm_c_re[l], ssm_c_im[l])
        d_vec = ssm_d[l].reshape(1, sw)
        w_glu = ssm_w_glu[l].astype(BF16)
        b_glu = ssm_b_glu[l].reshape(1, sw)
        pad = LANES - N_EXPERTS
        rw = jnp.pad(router_w[l], ((0, 0), (0, pad)))
        rw_hi = rw.astype(BF16)
        rw_lo = (rw - rw_hi.astype(F32)).astype(BF16)
        rb = jnp.pad(router_bias[l], (0, pad), constant_values=-jnp.inf).reshape(1, LANES)
        rwt_hi, rwt_lo = rw_hi[:, :N_EXPERTS].T, rw_lo[:, :N_EXPERTS].T
        rb_col = router_bias[l].reshape(N_EXPERTS, 1)
        swg, swu, swd = (shared_w_gate[l].astype(BF16), shared_w_up[l].astype(BF16),
                         shared_w_down[l].astype(BF16))
        ewg, ewu, ewd = (moe_w_gate[l].astype(BF16), moe_w_up[l].astype(BF16),
                         moe_w_down[l].astype(BF16))

        tm_row = min(ROW_TILE, seq_len)
        ua, yb = _premix_seq(xp, mod_p[0], mod_p[1], g1, w_in_b, ln_g, ln_b, ws_tril, bs_full,
                             seq_len, tm=min(PREMIX_TILE, seq_len))
        ya, hpr, hpi = _s5_seq(ua.reshape(bp, seq_len, sw), bj, cj, a_re, a_im, d_vec,
                               w_glu, b_glu, tl=min(SCAN_TILE, seq_len))
        xacc, h2, wd = _postmix(xp, ya.reshape(bp * seq_len, sw), yb, mod_p[2], mod_p[3],
                                mod_p[4], mod_p[5], g2, w_out_b, rwt_hi, rwt_lo, rb_col,
                                swg, swu, swd,
                                per_row=False, seq_len=seq_len, tm=tm_row, token_tiles=True)
        last = l == depth - 1
        tg = min(MOE_GROUP, bp * seq_len)
        assert (bp * seq_len) % tg == 0
        bm = min(MOE_BLOCK, tg)
        per_tile = max(SMEM_TILE // bm, 1)
        nb_max = -(-(tg * TOP_K // bm + N_EXPERTS + 3) // per_tile) * per_tile
        slot_rows, slot_w, counts = _route_plan(wd, tg, d // LANES)
        table, block_row = _block_table(counts[:, :, 0], tg, bm, nb_max)
        by_block = lambda a: jnp.take(a.reshape(-1, bm), block_row, axis=0).reshape(-1)
        routed = _moe_group(table, h2, by_block(slot_rows), by_block(slot_w),
                            ewg, ewu, ewd, tg, bm, nb_max)
        xp = _finalize(routed, xacc, mod_p[5], gf, seq_len, min(PREMIX_TILE, seq_len),
                       final_norm=last)
        p_re.append(hpr.reshape(bp, groups, n_state))
        p_im.append(hpi.reshape(bp, groups, n_state))

        ua, yb, vn = _premix_step(xs, mod_s[0], mod_s[1], g1, w_in_b, ln_g, ln_b, w0, b0)
        ya, hsr, hsi = _s5_step(ua, state_ssm_re[l].reshape(bs, groups * n_state),
                                state_ssm_im[l].reshape(bs, groups * n_state),
                                bj, cj, a_re, a_im, d_vec, w_glu, b_glu)
        xacc, h2, wd = _postmix(xs, ya, yb, mod_s[2], mod_s[3], mod_s[4], mod_s[5], g2, w_out_b,
                                rw_hi, rw_lo, rb, swg, swu, swd, per_row=True, seq_len=1, tm=bs,
                                token_tiles=False)
        xs = _moe_dense(h2, wd, xacc, mod_s[5], gf, ewg, ewu, ewd, per_row=True, seq_len=1,
                        tm=bs, final_norm=last)
        s_re.append(hsr.reshape(bs, groups, n_state))
        s_im.append(hsi.reshape(bs, groups, n_state))
        s_v.append(vn.reshape(bs, 1, gw))

    y_prompt = xp.reshape(bp, seq_len, d)
    y_sample = xs.reshape(bs, 1, d)
    return (y_prompt, y_sample, jnp.stack(p_re), jnp.stack(p_im), jnp.stack(s_re),
            jnp.stack(s_im), jnp.stack(s_v))
```

```python
import functools

import jax
import jax.numpy as jnp
from jax import lax
from jax.experimental import pallas as pl
from jax.experimental.pallas import tpu as pltpu

F32 = jnp.float32
BF16 = jnp.bfloat16

EPS = 1e-6
SSM_GROUP = 16
SSM_STATE = 64
GM_HEADS = 4
CHUNK = 128
N_EXPERTS = 64
TOP_K = 8
N_ROUTE_GROUPS = 8
TOPK_ROUTE_GROUPS = 4
ROUTED_SCALE = 2.5

LANES = 128
SUBLANES = 8
VMEM_LIMIT = 48 * 1024 * 1024
ROW_TILE = 1024
PREMIX_TILE = 1024
DENSE_EXPERTS_PER_STEP = 8
SCAN_TILE = 128
MOE_GROUP = 4096
SMEM_TILE = 1024
MOE_BLOCK = 128
MOE_EARLY_ROW_PERCENT = 56
MOE_WEIGHT_BUFFERS = 4


def _params(n_axes):
    return pltpu.CompilerParams(dimension_semantics=("arbitrary",) * n_axes,
                                vmem_limit_bytes=VMEM_LIMIT)


def _dot(a, b):
    return jnp.dot(a.astype(BF16), b.astype(BF16), preferred_element_type=F32)


def _rmsnorm(x, g):
    ms = jnp.mean(x * x, axis=-1, keepdims=True)
    return x * lax.rsqrt(ms + EPS) * g


def _ada_kernel(c_ref, w_ref, b_ref, o_ref):
    c = c_ref[...]
    o_ref[...] = _dot(jax.nn.silu(c), w_ref[...]) + b_ref[...]


def _ada(c, w, b, tn=512):
    m, d = c.shape
    n = w.shape[1]
    return pl.pallas_call(
        _ada_kernel,
        grid=(n // tn,),
        in_specs=[pl.BlockSpec((m, d), lambda j: (0, 0)),
                  pl.BlockSpec((d, tn), lambda j: (0, j)),
                  pl.BlockSpec((1, tn), lambda j: (0, j))],
        out_specs=pl.BlockSpec((m, tn), lambda j: (0, j)),
        out_shape=jax.ShapeDtypeStruct((m, n), F32),
        compiler_params=_params(1),
        name="ada_mod",
    )(c, w, b)


def _premix_common(x_ref, sh_ref, sc_ref, g_ref, win_ref, lng_ref, lnb_ref, sw):
    x = x_ref[...]
    h = _rmsnorm(x, g_ref[...]) * (1.0 + sc_ref[...]) + sh_ref[...]
    proj = _dot(h, win_ref[...])
    u_a = proj[:, :sw]
    z = jax.nn.gelu(proj[:, sw:])
    gw = z.shape[1] // 2
    u_b = z[:, :gw]
    v_b = z[:, gw:]
    hd = gw // GM_HEADS
    parts = []
    for k in range(GM_HEADS):
        vh = v_b[:, k * hd:(k + 1) * hd]
        mu = jnp.mean(vh, axis=-1, keepdims=True)
        dv = vh - mu
        var = jnp.mean(dv * dv, axis=-1, keepdims=True)
        parts.append(dv * lax.rsqrt(var + EPS))
    vn = jnp.concatenate(parts, axis=-1) * lng_ref[...] + lnb_ref[...]
    return u_a, u_b, vn


def _premix_seq_kernel(x_ref, sh_ref, sc_ref, g_ref, win_ref, lng_ref, lnb_ref,
                       ws_ref, bs_ref, ua_ref, yb_ref, *, sw):
    u_a, u_b, vn = _premix_common(x_ref, sh_ref, sc_ref, g_ref, win_ref, lng_ref, lnb_ref, sw)
    ua_ref[...] = u_a
    tm, gw = u_b.shape
    hd = gw // GM_HEADS
    vnb = vn.astype(BF16)
    for c in range(tm // CHUNK):
        rows = slice(c * CHUNK, (c + 1) * CHUNK)
        for k in range(GM_HEADS):
            cols = slice(k * hd, (k + 1) * hd)
            s = jnp.dot(ws_ref[k], vnb[rows, cols], preferred_element_type=F32)
            yb_ref[rows, cols] = u_b[rows, cols] * (s + bs_ref[:, cols])


def _premix_step_kernel(x_ref, sh_ref, sc_ref, g_ref, win_ref, lng_ref, lnb_ref,
                        w0_ref, b0_ref, ua_ref, yb_ref, vn_ref, *, sw):
    u_a, u_b, vn = _premix_common(x_ref, sh_ref, sc_ref, g_ref, win_ref, lng_ref, lnb_ref, sw)
    ua_ref[...] = u_a
    vn_ref[...] = vn
    yb_ref[...] = u_b * (vn * w0_ref[...] + b0_ref[...])


def _row_spec(tm, d):
    return pl.BlockSpec((tm, d), lambda i: (i, 0))


def _full_spec(shape):
    nd = len(shape)
    return pl.BlockSpec(shape, lambda i: (0,) * nd)


def _mod_spec(per_row, tm, d, tiles_per_batch):
    if per_row:
        return pl.BlockSpec((tm, d), lambda i: (i, 0))
    return pl.BlockSpec((None, 1, d), lambda i: (i // tiles_per_batch, 0, 0))


def _premix_seq(x, sh, sc, g, w_in, ln_g, ln_b, ws_tril, bs_full, seq_len, tm):
    t, d = x.shape
    sw = ln_g.shape[1]
    tpb = seq_len // tm
    mod = _mod_spec(False, tm, d, tpb)
    return pl.pallas_call(
        functools.partial(_premix_seq_kernel, sw=sw),
        grid=(t // tm,),
        in_specs=[_row_spec(tm, d), mod, mod, _full_spec(g.shape), _full_spec(w_in.shape),
                  _full_spec(ln_g.shape), _full_spec(ln_b.shape), _full_spec(ws_tril.shape),
                  _full_spec(bs_full.shape)],
        out_specs=[_row_spec(tm, sw), _row_spec(tm, sw)],
        out_shape=[jax.ShapeDtypeStruct((t, sw), F32), jax.ShapeDtypeStruct((t, sw), F32)],
        compiler_params=_params(1),
        name="premix_seq",
    )(x, sh, sc, g, w_in, ln_g, ln_b, ws_tril, bs_full)


def _premix_step(x, sh, sc, g, w_in, ln_g, ln_b, w0, b0):
    t, d = x.shape
    sw = ln_g.shape[1]
    mod = _mod_spec(True, t, d, 1)
    return pl.pallas_call(
        functools.partial(_premix_step_kernel, sw=sw),
        grid=(1,),
        in_specs=[_row_spec(t, d), mod, mod, _full_spec(g.shape), _full_spec(w_in.shape),
                  _full_spec(ln_g.shape), _full_spec(ln_b.shape), _full_spec(w0.shape),
                  _full_spec(b0.shape)],
        out_specs=[_row_spec(t, sw)] * 3,
        out_shape=[jax.ShapeDtypeStruct((t, sw), F32)] * 3,
        compiler_params=_params(1),
        name="premix_step",
    )(x, sh, sc, g, w_in, ln_g, ln_b, w0, b0)


def _s5_gate(y, u, d_ref, wglu_ref, bglu_ref):
    y = jax.nn.gelu(y + d_ref[...] * u)
    gate = jax.nn.sigmoid(_dot(y, wglu_ref[...]) + bglu_ref[...])
    return y * gate


def _s5_seq_kernel(u_ref, bj_ref, cj_ref, are_ref, aim_ref, d_ref, wglu_ref, bglu_ref,
                   y_ref, sre_ref, sim_ref, bu_ref, st_ref, ut_ref, yt_ref,
                   *, nb, tl, n_chunks, unroll):
    i = pl.program_id(0)
    n_tiles = bu_ref.shape[0]
    tpc = n_tiles // n_chunks
    hpc = tpc // 2
    hw = hpc * LANES
    assert u_ref.shape[2] == n_chunks * LANES

    @pl.when(i == 0)
    def _():
        st_ref[...] = jnp.zeros_like(st_ref)

    for b in range(nb):
        ub = u_ref[b]
        for j in range(n_chunks):
            ut_ref[j, pl.ds(b, tl, stride=nb), :] = ub[:, j * LANES:(j + 1) * LANES]
    for j in range(n_chunks):
        bu = jnp.dot(ut_ref[j].astype(BF16), bj_ref[j], preferred_element_type=F32)
        for c in range(tpc):
            bu_ref[j * tpc + c] = bu[:, c * LANES:(c + 1) * LANES]

    for j in range(n_chunks):
        re = slice(j * 2 * hw, j * 2 * hw + hw)
        im = slice(j * 2 * hw + hw, (j + 1) * 2 * hw)
        ar = jnp.broadcast_to(are_ref[:, j * hw:(j + 1) * hw], (nb, hw))
        ai = jnp.broadcast_to(aim_ref[:, j * hw:(j + 1) * hw], (nb, hw))
        re_tiles = [j * tpc + c for c in range(hpc)]
        im_tiles = [j * tpc + hpc + c for c in range(hpc)]

        def body(s, carry, ar=ar, ai=ai, re_tiles=re_tiles, im_tiles=im_tiles):
            xr, xi = carry
            for k in range(unroll):
                r0 = pl.multiple_of((s * unroll + k) * nb, nb)
                br = jnp.concatenate([bu_ref[c, pl.ds(r0, nb), :] for c in re_tiles], axis=-1)
                bi = jnp.concatenate([bu_ref[c, pl.ds(r0, nb), :] for c in im_tiles], axis=-1)
                nr = ar * xr - ai * xi + br
                ni = ar * xi + ai * xr + bi
                for q, c in enumerate(re_tiles):
                    bu_ref[c, pl.ds(r0, nb), :] = nr[:, q * LANES:(q + 1) * LANES]
                for q, c in enumerate(im_tiles):
                    bu_ref[c, pl.ds(r0, nb), :] = ni[:, q * LANES:(q + 1) * LANES]
                xr, xi = nr, ni
            return xr, xi

        xr, xi = lax.fori_loop(0, tl // unroll, body, (st_ref[:, re], st_ref[:, im]))
        st_ref[:, re] = xr
        st_ref[:, im] = xi
        sre_ref[:, j * hw:(j + 1) * hw] = xr
        sim_ref[:, j * hw:(j + 1) * hw] = xi

    for j in range(n_chunks):
        xs = jnp.concatenate([bu_ref[j * tpc + c] for c in range(tpc)], axis=-1)
        yt_ref[j] = jnp.dot(xs.astype(BF16), cj_ref[j], preferred_element_type=F32)
    y = jnp.concatenate(
        [jnp.concatenate([yt_ref[j, pl.ds(b, tl, stride=nb), :] for j in range(n_chunks)], axis=-1)
         for b in range(nb)], axis=0)
    u_all = u_ref[...].reshape(nb * tl, u_ref.shape[2])
    y_ref[...] = _s5_gate(y, u_all, d_ref, wglu_ref, bglu_ref).reshape(y_ref.shape)


def _s5_seq(u, bj, cj, a_re, a_im, d_vec, w_glu, b_glu, tl, unroll=4):
    nb, seq_len, sw = u.shape
    n_chunks, _, cw = bj.shape
    ns = a_re.shape[1]
    kern = functools.partial(_s5_seq_kernel, nb=nb, tl=tl, n_chunks=n_chunks, unroll=unroll)
    return pl.pallas_call(
        kern,
        grid=(seq_len // tl,),
        in_specs=[pl.BlockSpec((nb, tl, sw), lambda i: (0, i, 0)),
                  _full_spec(bj.shape), _full_spec(cj.shape), _full_spec(a_re.shape),
                  _full_spec(a_im.shape), _full_spec(d_vec.shape), _full_spec(w_glu.shape),
                  _full_spec(b_glu.shape)],
        out_specs=[pl.BlockSpec((nb, tl, sw), lambda i: (0, i, 0)),
                   _full_spec((nb, ns)), _full_spec((nb, ns))],
        out_shape=[jax.ShapeDtypeStruct((nb, seq_len, sw), F32),
                   jax.ShapeDtypeStruct((nb, ns), F32), jax.ShapeDtypeStruct((nb, ns), F32)],
        scratch_shapes=[pltpu.VMEM((n_chunks * cw // LANES, tl * nb, LANES), F32),
                        pltpu.VMEM((nb, n_chunks * cw), F32),
                        pltpu.VMEM((n_chunks, tl * nb, LANES), F32),
                        pltpu.VMEM((n_chunks, tl * nb, LANES), F32)],
        compiler_params=_params(1),
        name="s5_seq",
    )(u, bj, cj, a_re, a_im, d_vec, w_glu, b_glu)


def _s5_step_kernel(u_ref, hre_ref, him_ref, bj_ref, cj_ref, are_ref, aim_ref, d_ref,
                    wglu_ref, bglu_ref, y_ref, sre_ref, sim_ref, *, n_chunks):
    u = u_ref[...]
    ub = u.astype(BF16)
    kw = u.shape[1] // n_chunks
    hw = bj_ref.shape[2] // 2
    chunks = []
    for j in range(n_chunks):
        st = slice(j * hw, (j + 1) * hw)
        bu = jnp.dot(ub[:, j * kw:(j + 1) * kw], bj_ref[j], preferred_element_type=F32)
        ar, ai = are_ref[:, st], aim_ref[:, st]
        hr, hi = hre_ref[:, st], him_ref[:, st]
        xr = ar * hr - ai * hi + bu[:, :hw]
        xi = ar * hi + ai * hr + bu[:, hw:]
        sre_ref[:, st] = xr
        sim_ref[:, st] = xi
        chunks.append(jnp.concatenate([xr, xi], axis=-1))
    y = jnp.concatenate([jnp.dot(chunks[j].astype(BF16), cj_ref[j], preferred_element_type=F32)
                         for j in range(n_chunks)], axis=-1)
    y_ref[...] = _s5_gate(y, u, d_ref, wglu_ref, bglu_ref)


def _s5_step(u, h_re, h_im, bj, cj, a_re, a_im, d_vec, w_glu, b_glu):
    t, sw = u.shape
    ns = a_re.shape[1]
    args = (u, h_re, h_im, bj, cj, a_re, a_im, d_vec, w_glu, b_glu)
    return pl.pallas_call(
        functools.partial(_s5_step_kernel, n_chunks=bj.shape[0]),
        grid=(1,),
        in_specs=[_full_spec(a.shape) for a in args],
        out_specs=[_full_spec((t, sw)), _full_spec((t, ns)), _full_spec((t, ns))],
        out_shape=[jax.ShapeDtypeStruct((t, sw), F32), jax.ShapeDtypeStruct((t, ns), F32),
                   jax.ShapeDtypeStruct((t, ns), F32)],
        compiler_params=_params(1),
        name="s5_step",
    )(*args)


def _group_reduce(x, lane, width, op):
    n = x.shape[-1]
    s = 1
    while s < width:
        up = pltpu.roll(x, s, 1)
        dn = pltpu.roll(x, n - s, 1)
        x = op(x, jnp.where((lane & s) != 0, up, dn))
        s *= 2
    return x


def _first_max(x, lane, big):
    m = jnp.max(x, axis=-1, keepdims=True)
    return jnp.min(jnp.where(x == m, lane, big), axis=-1, keepdims=True)


def _route(logits, bias):
    rows, n = logits.shape
    lane = lax.broadcasted_iota(jnp.int32, (rows, n), 1)
    lane_f = lane.astype(F32)
    neg = jnp.float32(-jnp.inf)
    big = jnp.float32(n)
    gsz = N_EXPERTS // N_ROUTE_GROUPS
    group_start = (lane - (lane & (gsz - 1))).astype(F32)
    scores = jax.nn.sigmoid(logits)
    biased = scores + bias
    m1 = _group_reduce(biased, lane, gsz, jnp.maximum)
    first = _group_reduce(jnp.where(biased == m1, lane_f, big), lane, gsz, jnp.minimum)
    m2 = _group_reduce(jnp.where(lane_f == first, neg, biased), lane, gsz, jnp.maximum)
    gs = m1 + m2
    gsel = jnp.zeros((rows, n), F32)
    work = gs
    for _ in range(TOPK_ROUTE_GROUPS):
        hit = group_start == _first_max(work, lane_f, big)
        gsel = jnp.where(hit, 1.0, gsel)
        work = jnp.where(hit, neg, work)
    work = jnp.where(gsel > 0.0, biased, neg)
    w = jnp.zeros((rows, n), F32)
    for _ in range(TOP_K):
        hit = lane_f == _first_max(work, lane_f, big)
        w = jnp.where(hit, scores, w)
        work = jnp.where(hit, neg, work)
    return w / jnp.sum(w, axis=-1, keepdims=True) * ROUTED_SCALE


def _route_by_expert(logits, bias):
    ne, tm = logits.shape
    ng = N_ROUTE_GROUPS
    gsz = ne // ng
    neg = jnp.float32(-jnp.inf)
    shape = (ng, gsz, tm)
    scores = jax.nn.sigmoid(logits)
    biased = (scores + bias).reshape(shape)
    scores = scores.reshape(shape)
    sub = lax.broadcasted_iota(jnp.int32, shape, 1)
    grp = lax.broadcasted_iota(jnp.int32, (ng, 1, tm), 0)
    eid = lax.broadcasted_iota(jnp.int32, shape, 0) * gsz + sub
    m1 = jnp.max(biased, axis=1, keepdims=True)
    first = jnp.min(jnp.where(biased == m1, sub, gsz), axis=1, keepdims=True)
    m2 = jnp.max(jnp.where(sub == first, neg, biased), axis=1, keepdims=True)
    work = m1 + m2
    gsel = jnp.zeros((ng, 1, tm), jnp.bool_)
    for _ in range(TOPK_ROUTE_GROUPS):
        m = jnp.max(work, axis=0, keepdims=True)
        hit = grp == jnp.min(jnp.where(work == m, grp, ng), axis=0, keepdims=True)
        gsel = jnp.logical_or(gsel, hit)
        work = jnp.where(hit, neg, work)
    work = jnp.where(gsel, biased, neg)
    w = jnp.zeros(shape, F32)
    for _ in range(TOP_K):
        m = jnp.max(jnp.max(work, axis=0, keepdims=True), axis=1, keepdims=True)
        first = jnp.min(jnp.min(jnp.where(work == m, eid, ne), axis=0, keepdims=True),
                        axis=1, keepdims=True)
        hit = eid == first
        w = jnp.where(hit, scores, w)
        work = jnp.where(hit, neg, work)
    denom = jnp.sum(jnp.sum(w, axis=0, keepdims=True), axis=1, keepdims=True)
    return (w / denom * ROUTED_SCALE).reshape(ne, tm)


def _split_bf16(x):
    hi = x.astype(BF16)
    lo = (x - hi.astype(F32)).astype(BF16)
    return hi, lo


def _postmix_kernel(x_ref, ya_ref, yb_ref, gt1_ref, sh2_ref, sc2_ref, gt2_ref, g2_ref,
                    wout_ref, rwh_ref, rwl_ref, rb_ref, swg_ref, swu_ref, swd_ref,
                    xacc_ref, h2_ref, wd_ref, *, token_tiles):
    y = jnp.concatenate([ya_ref[...], yb_ref[...]], axis=-1)
    x1 = x_ref[...] + gt1_ref[...] * _dot(y, wout_ref[...])
    h2 = _rmsnorm(x1, g2_ref[...]) * (1.0 + sc2_ref[...]) + sh2_ref[...]
    hb = h2.astype(BF16)
    if token_tiles:
        tm, d = h2.shape
        nj = d // LANES
        for j in range(nj):
            h2_ref[pl.ds(j, tm, stride=nj), :] = h2[:, j * LANES:(j + 1) * LANES]
    else:
        h2_ref[...] = hb
    h_hi, h_lo = _split_bf16(h2)
    if token_tiles:
        nt = (((1,), (1,)), ((), ()))
        logits = (lax.dot_general(rwh_ref[...], h_hi, nt, preferred_element_type=F32)
                  + lax.dot_general(rwl_ref[...], h_hi, nt, preferred_element_type=F32)
                  + lax.dot_general(rwh_ref[...], h_lo, nt, preferred_element_type=F32))
        wd_ref[...] = _route_by_expert(logits, rb_ref[...])
    else:
        logits = (jnp.dot(h_hi, rwh_ref[...], preferred_element_type=F32)
                  + jnp.dot(h_hi, rwl_ref[...], preferred_element_type=F32)
                  + jnp.dot(h_lo, rwh_ref[...], preferred_element_type=F32))
        wd_ref[...] = _route(logits, rb_ref[...])
    hid = jax.nn.silu(jnp.dot(hb, swg_ref[...], preferred_element_type=F32)) * \
        jnp.dot(hb, swu_ref[...], preferred_element_type=F32)
    shared = _dot(hid, swd_ref[...])
    xacc_ref[...] = x1 + gt2_ref[...] * shared


def _postmix(x, ya, yb, gt1, sh2, sc2, gt2, g2, w_out, rw_hi, rw_lo, rb, swg, swu, swd,
             per_row, seq_len, tm, token_tiles):
    t, d = x.shape
    sw = ya.shape[1]
    nj = d // LANES
    mod = _mod_spec(per_row, tm, d, max(seq_len // tm, 1))
    weights = (g2, w_out, rw_hi, rw_lo, rb, swg, swu, swd)
    if token_tiles:
        ne = rw_hi.shape[0]
        h2_spec, h2_shape = _row_spec(tm * nj, LANES), jax.ShapeDtypeStruct((t * nj, LANES), F32)
        wd_spec, wd_shape = pl.BlockSpec((ne, tm), lambda i: (0, i)), (ne, t)
    else:
        ne = rw_hi.shape[1]
        h2_spec, h2_shape = _row_spec(tm, d), jax.ShapeDtypeStruct((t, d), BF16)
        wd_spec, wd_shape = _row_spec(tm, ne), (t, ne)
    return pl.pallas_call(
        functools.partial(_postmix_kernel, token_tiles=token_tiles),
        grid=(t // tm,),
        in_specs=[_row_spec(tm, d), _row_spec(tm, sw), _row_spec(tm, sw), mod, mod, mod, mod]
        + [_full_spec(w.shape) for w in weights],
        out_specs=[_row_spec(tm, d), h2_spec, wd_spec],
        out_shape=[jax.ShapeDtypeStruct((t, d), F32), h2_shape,
                   jax.ShapeDtypeStruct(wd_shape, F32)],
        compiler_params=_params(1),
        name="postmix",
    )(x, ya, yb, gt1, sh2, sc2, gt2, *weights)


def _moe_dense_kernel(h_ref, wd_ref, xacc_ref, gt2_ref, gf_ref, wg_ref, wu_ref, wdn_ref,
                      o_ref, acc_ref, *, final_norm):
    step = pl.program_id(1)
    per_step = wg_ref.shape[0]

    @pl.when(step == 0)
    def _():
        acc_ref[...] = jnp.zeros_like(acc_ref)

    hb = h_ref[...]
    wd = wd_ref[...]
    lane = lax.broadcasted_iota(jnp.int32, wd.shape, 1)
    total = None
    for q in range(per_step):
        col = jnp.sum(jnp.where(lane == step * per_step + q, wd, 0.0), axis=-1, keepdims=True)
        hid = jax.nn.silu(jnp.dot(hb, wg_ref[q], preferred_element_type=F32)) * \
            jnp.dot(hb, wu_ref[q], preferred_element_type=F32)
        part = col * _dot(hid, wdn_ref[q])
        total = part if total is None else total + part
    acc_ref[...] += total

    @pl.when(step == pl.num_programs(1) - 1)
    def _():
        xo = xacc_ref[...] + gt2_ref[...] * acc_ref[...]
        o_ref[...] = _rmsnorm(xo, gf_ref[...]) if final_norm else xo


def _moe_dense(h2, wd, xacc, gt2, g_final, wg, wu, wdn, per_row, seq_len, tm, final_norm):
    t, d = xacc.shape
    ne, _, eh = wg.shape
    nl = wd.shape[1]
    tpb = max(seq_len // tm, 1)
    if per_row:
        mod = pl.BlockSpec((tm, d), lambda i, e: (i, 0))
    else:
        mod = pl.BlockSpec((None, 1, d), lambda i, e: (i // tpb, 0, 0))
    row = lambda w: pl.BlockSpec((tm, w), lambda i, e: (i, 0))
    per_step = DENSE_EXPERTS_PER_STEP
    assert ne % per_step == 0
    return pl.pallas_call(
        functools.partial(_moe_dense_kernel, final_norm=final_norm),
        grid=(t // tm, ne // per_step),
        in_specs=[row(d), row(nl), row(d), mod,
                  pl.BlockSpec(g_final.shape, lambda i, e: (0, 0)),
                  pl.BlockSpec((per_step, d, eh), lambda i, e: (e, 0, 0)),
                  pl.BlockSpec((per_step, d, eh), lambda i, e: (e, 0, 0)),
                  pl.BlockSpec((per_step, eh, d), lambda i, e: (e, 0, 0))],
        out_specs=row(d),
        out_shape=jax.ShapeDtypeStruct((t, d), F32),
        scratch_shapes=[pltpu.VMEM((tm, d), F32)],
        compiler_params=_params(2),
        name="moe_dense",
    )(h2, wd, xacc, gt2, g_final, wg, wu, wdn)


TOKEN_BITS = 13


def _route_plan_kernel(wd_ref, rows_ref, wl_ref, cnt_ref, *, tg, nj):
    wt = wd_ref[...]
    sel = wt > 0.0
    lane = lax.broadcasted_iota(jnp.int32, wt.shape, 1)
    c = sel.astype(jnp.int32)
    s = 1
    while s < tg:
        c = c + jnp.where(lane >= s, pltpu.roll(c, s, 1), 0)
        s *= 2
    cnt_ref[...] = jnp.broadcast_to(c[:, tg - 1:tg], cnt_ref.shape)
    dist = lane - (c - 1)
    key = jnp.where(sel, (dist << TOKEN_BITS) | lane, -1)
    w = jnp.where(sel, wt, 0.0)
    s = 1
    while s < tg:
        leaving = (key >= 0) & (((key >> TOKEN_BITS) & s) != 0)
        key_in = pltpu.roll(key, tg - s, 1)
        w_in = pltpu.roll(w, tg - s, 1)
        arriving = (key_in >= 0) & (((key_in >> TOKEN_BITS) & s) != 0)
        key = jnp.where(arriving, key_in, jnp.where(leaving, -1, key))
        w = jnp.where(arriving, w_in, jnp.where(leaving, 0.0, w))
        s *= 2
    rows_ref[...] = jnp.where(key >= 0, key & ((1 << TOKEN_BITS) - 1), 0) * nj
    wl_ref[...] = w


def _route_plan(wd, tg, nj):
    ne, t = wd.shape
    assert ne == N_EXPERTS and tg <= (1 << TOKEN_BITS) and t % tg == 0
    g = t // tg
    grp = lambda w: pl.BlockSpec((None, N_EXPERTS, w), lambda i: (i, 0, 0))
    return pl.pallas_call(
        functools.partial(_route_plan_kernel, tg=tg, nj=nj),
        grid=(g,),
        in_specs=[pl.BlockSpec((ne, tg), lambda i: (0, i))],
        out_specs=[grp(tg), grp(tg), grp(LANES)],
        out_shape=[jax.ShapeDtypeStruct((g, N_EXPERTS, tg), jnp.int32),
                   jax.ShapeDtypeStruct((g, N_EXPERTS, tg), F32),
                   jax.ShapeDtypeStruct((g, N_EXPERTS, LANES), jnp.int32)],
        compiler_params=_params(1),
        name="route_plan",
    )(wd)


def _block_table(counts, tg, bm, nb_max):
    g, ne = counts.shape
    i32 = jnp.int32
    nblk = (counts + bm - 1) // bm
    bend = jnp.cumsum(nblk, axis=1)
    bstart = bend - nblk
    nb = bend[:, -1]
    k = jnp.arange(nb_max, dtype=i32)[None, :]
    kc = jnp.minimum(k, jnp.maximum(nb[:, None] - 1, 0))
    e_k = jnp.minimum(jnp.sum(bend[:, None, :] <= kc[:, :, None], axis=2), ne - 1).astype(i32)
    b_k = kc - jnp.take_along_axis(bstart, e_k, axis=1)
    first = (b_k == 0) & (k < nb[:, None])
    active = nblk > 0
    parity = (jnp.take_along_axis(jnp.cumsum(active.astype(i32), axis=1) - 1, e_k, axis=1)
              % MOE_WEIGHT_BUFFERS)
    ids = jnp.where(active, jnp.arange(ne, dtype=i32)[None, :], ne)
    first_active_from = jnp.flip(lax.cummin(jnp.flip(ids, 1), axis=1), 1)
    nxt = jnp.concatenate([first_active_from[:, 1:], jnp.full((g, 1), ne, i32)], axis=1)
    nxt = jnp.where(nxt >= ne, -1, nxt)
    nxt2 = jnp.where(nxt < 0, -1, jnp.take_along_axis(nxt, jnp.maximum(nxt, 0), axis=1))
    next_k = jnp.take_along_axis(nxt, e_k, axis=1)
    next2_k = jnp.take_along_axis(nxt2, e_k, axis=1)
    row = (jnp.arange(g, dtype=i32)[:, None] * ne + e_k) * (tg // bm) + b_k
    flat = lambda a: a.reshape(-1).astype(i32)
    table = (nb.astype(i32), flat(e_k), flat(first), flat(parity), flat(next_k), flat(next2_k))
    return table, flat(row)


def _moe_group_kernel(nb_ref, exp_ref, first_ref, par_ref, next_ref, next2_ref,
                      h_ref, rows_ref, wl_ref, wg_hbm, wu_hbm, wdn_hbm, o_ref,
                      wg_buf, wu_buf, wdn_buf, wsem, xt_a, xt_b, hid_a, hid_b, yt_a, yt_b,
                      *, bm, nj, batch, nb_max):
    g = pl.program_id(0)
    nb = nb_ref[g]
    tb = g * nb_max
    pitch = bm + 1
    o_ref[...] = jnp.zeros_like(o_ref)

    def weight_copies(e, slot):
        return (pltpu.make_async_copy(wg_hbm.at[e], wg_buf.at[slot], wsem.at[slot]),
                pltpu.make_async_copy(wu_hbm.at[e], wu_buf.at[slot], wsem.at[slot]),
                pltpu.make_async_copy(wdn_hbm.at[e], wdn_buf.at[slot], wsem.at[slot]))

    n_wbuf = wg_buf.shape[0]

    def handoff(k):
        @pl.when(first_ref[tb + k] == 1)
        def _():
            slot = par_ref[tb + k]
            for c in weight_copies(exp_ref[tb + k], slot):
                c.wait()
            ahead = next2_ref[tb + k]

            @pl.when(ahead >= 0)
            def _():
                for c in weight_copies(ahead, (slot + 2) % n_wbuf):
                    c.start()

    def gather_rows(k, xt):
        rows = rows_ref.at[pl.ds(pl.multiple_of(k * bm, bm), bm)]

        def emit(lo, hi):
            for m in range(lo, hi):
                r0 = pl.multiple_of(rows[m], nj)
                xt[pl.ds(m, nj, stride=pitch), :] = h_ref[pl.ds(r0, nj), :]
        return emit

    def scatter_rows(k, yt):
        rows = rows_ref.at[pl.ds(pl.multiple_of(k * bm, bm), bm)]

        def emit(lo, hi):
            for m0 in range(lo, hi, batch):
                updates = []
                for m in range(m0, min(m0 + batch, hi)):
                    r0 = pl.multiple_of(rows[m], nj)
                    updates.append((r0, o_ref[pl.ds(r0, nj), :]
                                    + yt[pl.ds(m, nj, stride=pitch), :]))
                for r0, v in reversed(updates):
                    o_ref[pl.ds(r0, nj), :] = v
        return emit

    n_chunks = nj // 2
    kc = 2 * LANES
    n_pieces = 2 * n_chunks + 1
    early = bm * MOE_EARLY_ROW_PERCENT // 100
    cuts = [early * i // (n_pieces - 1) for i in range(n_pieces)] + [bm]

    def step(up_blk, down_blk, side_work):
        piece = 0

        def side():
            nonlocal piece
            for emit in side_work:
                emit(cuts[piece], cuts[piece + 1])
            piece += 1

        gate = up = None
        for c in range(n_chunks):
            if up_blk is not None:
                k, xt, _ = up_blk
                slot = par_ref[tb + k]
                xc = jnp.concatenate([xt[j * pitch:j * pitch + bm, :] for j in (2 * c, 2 * c + 1)],
                                     axis=-1).astype(BF16)
                pg = jnp.dot(xc, wg_buf[slot, c * kc:(c + 1) * kc, :], preferred_element_type=F32)
                pu = jnp.dot(xc, wu_buf[slot, c * kc:(c + 1) * kc, :], preferred_element_type=F32)
                gate = pg if gate is None else gate + pg
                up = pu if up is None else up + pu
            side()
        for c in range(n_chunks):
            if down_blk is not None:
                k, hid, yt = down_blk
                slot = par_ref[tb + k]
                yc = jnp.dot(hid[...], wdn_buf[slot, :, c * kc:(c + 1) * kc],
                             preferred_element_type=F32)
                for q in range(2):
                    j = 2 * c + q
                    yt[j * pitch:j * pitch + bm, :] = yc[:, q * LANES:(q + 1) * LANES]
            side()
        side()
        if up_blk is not None:
            k, _, hid = up_blk
            w_col = jnp.broadcast_to(wl_ref[k], (SUBLANES, bm)).T[:, :1]
            hid[...] = (jax.nn.silu(gate) * up * w_col).astype(BF16)

    @pl.when(nb > 0)
    def _():
        for c in weight_copies(exp_ref[tb], 0):
            c.start()
        second = next_ref[tb]

        @pl.when(second >= 0)
        def _():
            for c in weight_copies(second, 1):
                c.start()
        gather_rows(0, xt_a)(0, bm)
        handoff(0)
        step((0, xt_a, hid_a), None, [gather_rows(1, xt_b)])
        handoff(1)
        step((1, xt_b, hid_b), (0, hid_a, yt_a), [gather_rows(2, xt_a)])

        def trip(i, carry):
            t = 2 * i + 1
            handoff(t + 1)
            step((t + 1, xt_a, hid_a), (t, hid_b, yt_b),
                 [gather_rows(t + 2, xt_b), scatter_rows(t - 1, yt_a)])
            handoff(t + 2)
            step((t + 2, xt_b, hid_b), (t + 1, hid_a, yt_a),
                 [gather_rows(t + 3, xt_a), scatter_rows(t, yt_b)])
            return carry

        lax.fori_loop(0, nb // 2, trip, 0)

        @pl.when(nb % 2 == 1)
        def _():
            scatter_rows(nb - 1, yt_a)(0, bm)


def _moe_group(table, h2t, rows, wl, wg, wu, wdn, tg, bm, nb_max, batch=4):
    ne, d, eh = wg.shape
    nj = d // LANES
    assert nj == SUBLANES
    g = h2t.shape[0] // (tg * nj)
    kern = functools.partial(_moe_group_kernel, bm=bm, nj=nj, batch=batch, nb_max=nb_max)
    slots = pl.BlockSpec((nb_max * bm,), lambda gi, *_: (gi,), memory_space=pltpu.SMEM)
    block_w = pl.BlockSpec((nb_max, 1, bm), lambda gi, *_: (gi, 0, 0))
    hbm = pl.BlockSpec(memory_space=pl.ANY)
    pitch_rows = nj * (bm + 1)
    grid_spec = pltpu.PrefetchScalarGridSpec(
        num_scalar_prefetch=len(table),
        grid=(g,),
        in_specs=[pl.BlockSpec((tg * nj, LANES), lambda gi, *_: (gi, 0),
                               pipeline_mode=pl.Buffered(1)),
                  slots, block_w, hbm, hbm, hbm],
        out_specs=pl.BlockSpec((tg * nj, LANES), lambda gi, *_: (gi, 0),
                               pipeline_mode=pl.Buffered(1)),
        scratch_shapes=[pltpu.VMEM((MOE_WEIGHT_BUFFERS, d, eh), BF16),
                        pltpu.VMEM((MOE_WEIGHT_BUFFERS, d, eh), BF16),
                        pltpu.VMEM((MOE_WEIGHT_BUFFERS, eh, d), BF16),
                        pltpu.SemaphoreType.DMA((MOE_WEIGHT_BUFFERS,)),
                        pltpu.VMEM((pitch_rows, LANES), F32), pltpu.VMEM((pitch_rows, LANES), F32),
                        pltpu.VMEM((bm, eh), BF16), pltpu.VMEM((bm, eh), BF16),
                        pltpu.VMEM((pitch_rows, LANES), F32), pltpu.VMEM((pitch_rows, LANES), F32)],
    )
    return pl.pallas_call(
        kern,
        grid_spec=grid_spec,
        out_shape=jax.ShapeDtypeStruct((g * tg * nj, LANES), F32),
        compiler_params=_params(1),
        name="moe_group",
    )(*table, h2t, rows, wl.reshape(g * nb_max, 1, bm), wg, wu, wdn)


def _finalize_kernel(r_ref, xacc_ref, gt2_ref, gf_ref, o_ref, *, final_norm):
    tm, d = xacc_ref.shape
    nj = d // LANES
    routed = jnp.concatenate([r_ref[pl.ds(j, tm, stride=nj), :] for j in range(nj)], axis=-1)
    xo = xacc_ref[...] + gt2_ref[...] * routed
    o_ref[...] = _rmsnorm(xo, gf_ref[...]) if final_norm else xo


def _finalize(routed, xacc, gt2, g_final, seq_len, tm, final_norm):
    t, d = xacc.shape
    nj = d // LANES
    return pl.pallas_call(
        functools.partial(_finalize_kernel, final_norm=final_norm),
        grid=(t // tm,),
        in_specs=[_row_spec(tm * nj, LANES),
                  _row_spec(tm, d), _mod_spec(False, tm, d, seq_len // tm),
                  _full_spec(g_final.shape)],
        out_specs=_row_spec(tm, d),
        out_shape=jax.ShapeDtypeStruct((t, d), F32),
        compiler_params=_params(1),
        name="finalize",
    )(routed, xacc, gt2, g_final)


def _s5_discretize(lam_re, lam_im, log_dt, b_re, b_im, c_re, c_im):
    g, n = lam_re.shape
    hg = b_re.shape[2]
    lr = jnp.minimum(lam_re.astype(F32), -1e-4)
    li = lam_im.astype(F32)
    dt = jnp.exp(log_dt.astype(F32))[:, None]
    mag = jnp.exp(lr * dt)
    abar_r, abar_i = mag * jnp.cos(li * dt), mag * jnp.sin(li * dt)
    den = lr * lr + li * li
    nr, ni = abar_r - 1.0, abar_i
    f_r = (nr * lr + ni * li) / den
    f_i = (ni * lr - nr * li) / den
    br_, bi_ = b_re.astype(F32), b_im.astype(F32)
    bbar_r = f_r[..., None] * br_ - f_i[..., None] * bi_
    bbar_i = f_r[..., None] * bi_ + f_i[..., None] * br_
    gpc = LANES // hg
    n_chunks = g // gpc
    eye = jnp.eye(gpc, dtype=F32)

    def b_chunks(bb):
        bb = bb.reshape(n_chunks, gpc, n, hg)
        m = jnp.einsum('jgnh,gk->jghkn', bb, eye)
        return m.reshape(n_chunks, gpc * hg, gpc * n)

    def c_chunks(cc):
        cc = cc.astype(F32).reshape(n_chunks, gpc, hg, n)
        m = jnp.einsum('jghn,gk->jgnkh', cc, eye)
        return m.reshape(n_chunks, gpc * n, gpc * hg)

    bj = jnp.concatenate([b_chunks(bbar_r), b_chunks(bbar_i)], axis=2).astype(BF16)
    cj = jnp.concatenate([c_chunks(c_re), -c_chunks(c_im)], axis=1).astype(BF16)
    return abar_r.reshape(1, g * n), abar_i.reshape(1, g * n), bj, cj


def kernel(x_prompt, x_sample, c_prompt, c_sample, state_ssm_re, state_ssm_im, g_norm1, g_norm2, w_ada, b_ada, w_in, w_out, ssm_lambda_re, ssm_lambda_im, ssm_log_dt, ssm_b_re, ssm_b_im, ssm_c_re, ssm_c_im, ssm_d, ssm_w_glu, ssm_b_glu, gm_ln_g, gm_ln_b, gm_w_s, gm_b_s, router_w, router_bias, moe_w_gate, moe_w_up, moe_w_down, shared_w_gate, shared_w_up, shared_w_down, g_final):
    bp, seq_len, d = x_prompt.shape
    bs, dec_len, _ = x_sample.shape
    depth = w_ada.shape[0]
    assert dec_len == 1 and seq_len % CHUNK == 0 and bp % SUBLANES == 0
    groups, n_state = ssm_lambda_re.shape[1:]
    sw = groups * SSM_GROUP
    gw = gm_ln_g.shape[1]
    hd = gw // GM_HEADS

    xp = x_prompt.reshape(bp * seq_len, d)
    xs = x_sample.reshape(bs, d)
    c_all = jnp.concatenate([c_prompt, c_sample], axis=0)
    gf = g_final.reshape(1, d)

    p_re, p_im, s_re, s_im, s_v = [], [], [], [], []
    for l in range(depth):
        mod = _ada(c_all, w_ada[l], b_ada[l].reshape(1, -1))
        mod_p = [m.reshape(bp, 1, d) for m in jnp.split(mod[:bp], 6, axis=-1)]
        mod_s = jnp.split(mod[bp:], 6, axis=-1)

        g1 = g_norm1[l].reshape(1, d)
        g2 = g_norm2[l].reshape(1, d)
        w_in_b = w_in[l].astype(BF16)
        w_out_b = w_out[l].astype(BF16)
        ln_g = gm_ln_g[l].reshape(1, gw)
        ln_b = gm_ln_b[l].reshape(1, gw)
        ws_tril = (gm_w_s[l] * jnp.tril(jnp.ones((CHUNK, CHUNK), F32))[None]).astype(BF16)
        bs_full = jnp.repeat(gm_b_s[l].T, hd, axis=1)
        w0 = jnp.repeat(gm_w_s[l][:, 0, 0], hd).reshape(1, gw)
        b0 = jnp.repeat(gm_b_s[l][:, 0], hd).reshape(1, gw)
        a_re, a_im, bj, cj = _s5_discretize(ssm_lambda_re[l], ssm_lambda_im[l], ssm_log_dt[l],
                                            ssm_b_re[l], ssm_b_im[l], ssm_c_re[l], ssm_c_im[l])
        d_vec = ssm_d[l].reshape(1, sw)
        w_glu = ssm_w_glu[l].astype(BF16)
        b_glu = ssm_b_glu[l].reshape(1, sw)
        pad = LANES - N_EXPERTS
        rw = jnp.pad(router_w[l], ((0, 0), (0, pad)))
        rw_hi = rw.astype(BF16)
        rw_lo = (rw - rw_hi.astype(F32)).astype(BF16)
        rb = jnp.pad(router_bias[l], (0, pad), constant_values=-jnp.inf).reshape(1, LANES)
        rwt_hi, rwt_lo = rw_hi[:, :N_EXPERTS].T, rw_lo[:, :N_EXPERTS].T
        rb_col = router_bias[l].reshape(N_EXPERTS, 1)
        swg, swu, swd = (shared_w_gate[l].astype(BF16), shared_w_up[l].astype(BF16),
                         shared_w_down[l].astype(BF16))
        ewg, ewu, ewd = (moe_w_gate[l].astype(BF16), moe_w_up[l].astype(BF16),
                         moe_w_down[l].astype(BF16))

        tm_row = min(ROW_TILE, seq_len)
        ua, yb = _premix_seq(xp, mod_p[0], mod_p[1], g1, w_in_b, ln_g, ln_b, ws_tril, bs_full,
                             seq_len, tm=min(PREMIX_TILE, seq_len))
        ya, hpr, hpi = _s5_seq(ua.reshape(bp, seq_len, sw), bj, cj, a_re, a_im, d_vec,
                               w_glu, b_glu, tl=min(SCAN_TILE, seq_len))
        xacc, h2, wd = _postmix(xp, ya.reshape(bp * seq_len, sw), yb, mod_p[2], mod_p[3],
                                mod_p[4], mod_p[5], g2, w_out_b, rwt_hi, rwt_lo, rb_col,
                                swg, swu, swd,
                                per_row=False, seq_len=seq_len, tm=tm_row, token_tiles=True)
        last = l == depth - 1
        tg = min(MOE_GROUP, bp * seq_len)
        assert (bp * seq_len) % tg == 0
        bm = min(MOE_BLOCK, tg)
        per_tile = max(SMEM_TILE // bm, 1)
        nb_max = -(-(tg * TOP_K // bm + N_EXPERTS + 3) // per_tile) * per_tile
        slot_rows, slot_w, counts = _route_plan(wd, tg, d // LANES)
        table, block_row = _block_table(counts[:, :, 0], tg, bm, nb_max)
        by_block = lambda a: jnp.take(a.reshape(-1, bm), block_row, axis=0).reshape(-1)
        routed = _moe_group(table, h2, by_block(slot_rows), by_block(slot_w),
                            ewg, ewu, ewd, tg, bm, nb_max)
        xp = _finalize(routed, xacc, mod_p[5], gf, seq_len, min(PREMIX_TILE, seq_len),
                       final_norm=last)
        p_re.append(hpr.reshape(bp, groups, n_state))
        p_im.append(hpi.reshape(bp, groups, n_state))

        ua, yb, vn = _premix_step(xs, mod_s[0], mod_s[1], g1, w_in_b, ln_g, ln_b, w0, b0)
        ya, hsr, hsi = _s5_step(ua, state_ssm_re[l].reshape(bs, groups * n_state),
                                state_ssm_im[l].reshape(bs, groups * n_state),
                                bj, cj, a_re, a_im, d_vec, w_glu, b_glu)
        xacc, h2, wd = _postmix(xs, ya, yb, mod_s[2], mod_s[3], mod_s[4], mod_s[5], g2, w_out_b,
                                rw_hi, rw_lo, rb, swg, swu, swd, per_row=True, seq_len=1, tm=bs,
                                token_tiles=False)
        xs = _moe_dense(h2, wd, xacc, mod_s[5], gf, ewg, ewu, ewd, per_row=True, seq_len=1,
                        tm=bs, final_norm=last)
        s_re.append(hsr.reshape(bs, groups, n_state))
        s_im.append(hsi.reshape(bs, groups, n_state))
        s_v.append(vn.reshape(bs, 1, gw))

    y_prompt = xp.reshape(bp, seq_len, d)
    y_sample = xs.reshape(bs, 1, d)
    return (y_prompt, y_sample, jnp.stack(p_re), jnp.stack(p_im), jnp.stack(s_re),
            jnp.stack(s_im), jnp.stack(s_v))
```

```python
import functools

import jax
import jax.numpy as jnp
from jax import lax
from jax.experimental import pallas as pl
from jax.experimental.pallas import tpu as pltpu

F32 = jnp.float32
BF16 = jnp.bfloat16

EPS = 1e-6
SSM_GROUP = 16
SSM_STATE = 64
GM_HEADS = 4
CHUNK = 128
N_EXPERTS = 64
TOP_K = 8
N_ROUTE_GROUPS = 8
TOPK_ROUTE_GROUPS = 4
ROUTED_SCALE = 2.5

LANES = 128
SUBLANES = 8
VMEM_LIMIT = 60 * 1024 * 1024
ROW_TILE = 1024
PREMIX_TILE = 1024
DENSE_EXPERTS_PER_STEP = 8
SCAN_TILE = 128
MOE_GROUP = 4096
SMEM_TILE = 1024
MOE_BLOCK = 128
MOE_EARLY_ROW_PERCENT = 56
MOE_WEIGHT_BUFFERS = 4


def _params(n_axes):
    return pltpu.CompilerParams(dimension_semantics=("arbitrary",) * n_axes,
                                vmem_limit_bytes=VMEM_LIMIT)


def _dot(a, b):
    return jnp.dot(a.astype(BF16), b.astype(BF16), preferred_element_type=F32)


def _rmsnorm(x, g):
    ms = jnp.mean(x * x, axis=-1, keepdims=True)
    return x * lax.rsqrt(ms + EPS) * g


def _ada_kernel(c_ref, w_ref, b_ref, o_ref):
    c = c_ref[...]
    o_ref[...] = _dot(jax.nn.silu(c), w_ref[...]) + b_ref[...]


def _ada(c, w, b, tn=512):
    m, d = c.shape
    n = w.shape[1]
    return pl.pallas_call(
        _ada_kernel,
        grid=(n // tn,),
        in_specs=[pl.BlockSpec((m, d), lambda j: (0, 0)),
                  pl.BlockSpec((d, tn), lambda j: (0, j)),
                  pl.BlockSpec((1, tn), lambda j: (0, j))],
        out_specs=pl.BlockSpec((m, tn), lambda j: (0, j)),
        out_shape=jax.ShapeDtypeStruct((m, n), F32),
        compiler_params=_params(1),
        name="ada_mod",
    )(c, w, b)


def _premix_common(x_ref, sh_ref, sc_ref, g_ref, win_ref, lng_ref, lnb_ref, sw):
    x = x_ref[...]
    h = _rmsnorm(x, g_ref[...]) * (1.0 + sc_ref[...]) + sh_ref[...]
    proj = _dot(h, win_ref[...])
    u_a = proj[:, :sw]
    z = jax.nn.gelu(proj[:, sw:])
    gw = z.shape[1] // 2
    u_b = z[:, :gw]
    v_b = z[:, gw:]
    hd = gw // GM_HEADS
    parts = []
    for k in range(GM_HEADS):
        vh = v_b[:, k * hd:(k + 1) * hd]
        mu = jnp.mean(vh, axis=-1, keepdims=True)
        dv = vh - mu
        var = jnp.mean(dv * dv, axis=-1, keepdims=True)
        parts.append(dv * lax.rsqrt(var + EPS))
    vn = jnp.concatenate(parts, axis=-1) * lng_ref[...] + lnb_ref[...]
    return u_a, u_b, vn


def _premix_seq_kernel(x_ref, sh_ref, sc_ref, g_ref, win_ref, lng_ref, lnb_ref,
                       ws_ref, bs_ref, ua_ref, yb_ref, *, sw):
    u_a, u_b, vn = _premix_common(x_ref, sh_ref, sc_ref, g_ref, win_ref, lng_ref, lnb_ref, sw)
    ua_ref[...] = u_a
    tm, gw = u_b.shape
    hd = gw // GM_HEADS
    vnb = vn.astype(BF16)
    for c in range(tm // CHUNK):
        rows = slice(c * CHUNK, (c + 1) * CHUNK)
        for k in range(GM_HEADS):
            cols = slice(k * hd, (k + 1) * hd)
            s = jnp.dot(ws_ref[k], vnb[rows, cols], preferred_element_type=F32)
            yb_ref[rows, cols] = u_b[rows, cols] * (s + bs_ref[:, cols])


def _premix_step_kernel(x_ref, sh_ref, sc_ref, g_ref, win_ref, lng_ref, lnb_ref,
                        w0_ref, b0_ref, ua_ref, yb_ref, vn_ref, *, sw):
    u_a, u_b, vn = _premix_common(x_ref, sh_ref, sc_ref, g_ref, win_ref, lng_ref, lnb_ref, sw)
    ua_ref[...] = u_a
    vn_ref[...] = vn
    yb_ref[...] = u_b * (vn * w0_ref[...] + b0_ref[...])


def _row_spec(tm, d):
    return pl.BlockSpec((tm, d), lambda i: (i, 0))


def _full_spec(shape):
    nd = len(shape)
    return pl.BlockSpec(shape, lambda i: (0,) * nd)


def _mod_spec(per_row, tm, d, tiles_per_batch):
    if per_row:
        return pl.BlockSpec((tm, d), lambda i: (i, 0))
    return pl.BlockSpec((None, 1, d), lambda i: (i // tiles_per_batch, 0, 0))


def _premix_seq(x, sh, sc, g, w_in, ln_g, ln_b, ws_tril, bs_full, seq_len, tm):
    t, d = x.shape
    sw = ln_g.shape[1]
    tpb = seq_len // tm
    mod = _mod_spec(False, tm, d, tpb)
    return pl.pallas_call(
        functools.partial(_premix_seq_kernel, sw=sw),
        grid=(t // tm,),
        in_specs=[_row_spec(tm, d), mod, mod, _full_spec(g.shape), _full_spec(w_in.shape),
                  _full_spec(ln_g.shape), _full_spec(ln_b.shape), _full_spec(ws_tril.shape),
                  _full_spec(bs_full.shape)],
        out_specs=[_row_spec(tm, sw), _row_spec(tm, sw)],
        out_shape=[jax.ShapeDtypeStruct((t, sw), F32), jax.ShapeDtypeStruct((t, sw), F32)],
        compiler_params=_params(1),
        name="premix_seq",
    )(x, sh, sc, g, w_in, ln_g, ln_b, ws_tril, bs_full)


def _premix_step(x, sh, sc, g, w_in, ln_g, ln_b, w0, b0):
    t, d = x.shape
    sw = ln_g.shape[1]
    mod = _mod_spec(True, t, d, 1)
    return pl.pallas_call(
        functools.partial(_premix_step_kernel, sw=sw),
        grid=(1,),
        in_specs=[_row_spec(t, d), mod, mod, _full_spec(g.shape), _full_spec(w_in.shape),
                  _full_spec(ln_g.shape), _full_spec(ln_b.shape), _full_spec(w0.shape),
                  _full_spec(b0.shape)],
        out_specs=[_row_spec(t, sw)] * 3,
        out_shape=[jax.ShapeDtypeStruct((t, sw), F32)] * 3,
        compiler_params=_params(1),
        name="premix_step",
    )(x, sh, sc, g, w_in, ln_g, ln_b, w0, b0)


def _s5_gate(y, u, d_ref, wglu_ref, bglu_ref):
    y = jax.nn.gelu(y + d_ref[...] * u)
    gate = jax.nn.sigmoid(_dot(y, wglu_ref[...]) + bglu_ref[...])
    return y * gate


def _s5_seq_kernel(u_ref, bj_ref, cj_ref, are_ref, aim_ref, d_ref, wglu_ref, bglu_ref,
                   y_ref, sre_ref, sim_ref, bu_ref, st_ref, ut_ref, yt_ref,
                   *, nb, tl, n_chunks, unroll):
    i = pl.program_id(0)
    n_tiles = bu_ref.shape[0]
    tpc = n_tiles // n_chunks
    hpc = tpc // 2
    hw = hpc * LANES
    assert u_ref.shape[2] == n_chunks * LANES

    @pl.when(i == 0)
    def _():
        st_ref[...] = jnp.zeros_like(st_ref)

    for b in range(nb):
        ub = u_ref[b]
        for j in range(n_chunks):
            ut_ref[j, pl.ds(b, tl, stride=nb), :] = ub[:, j * LANES:(j + 1) * LANES]
    for j in range(n_chunks):
        bu = jnp.dot(ut_ref[j].astype(BF16), bj_ref[j], preferred_element_type=F32)
        for c in range(tpc):
            bu_ref[j * tpc + c] = bu[:, c * LANES:(c + 1) * LANES]

    for j in range(n_chunks):
        re = slice(j * 2 * hw, j * 2 * hw + hw)
        im = slice(j * 2 * hw + hw, (j + 1) * 2 * hw)
        ar = jnp.broadcast_to(are_ref[:, j * hw:(j + 1) * hw], (nb, hw))
        ai = jnp.broadcast_to(aim_ref[:, j * hw:(j + 1) * hw], (nb, hw))
        re_tiles = [j * tpc + c for c in range(hpc)]
        im_tiles = [j * tpc + hpc + c for c in range(hpc)]

        def body(s, carry, ar=ar, ai=ai, re_tiles=re_tiles, im_tiles=im_tiles):
            xr, xi = carry
            for k in range(unroll):
                r0 = pl.multiple_of((s * unroll + k) * nb, nb)
                br = jnp.concatenate([bu_ref[c, pl.ds(r0, nb), :] for c in re_tiles], axis=-1)
                bi = jnp.concatenate([bu_ref[c, pl.ds(r0, nb), :] for c in im_tiles], axis=-1)
                nr = ar * xr - ai * xi + br
                ni = ar * xi + ai * xr + bi
                for q, c in enumerate(re_tiles):
                    bu_ref[c, pl.ds(r0, nb), :] = nr[:, q * LANES:(q + 1) * LANES]
                for q, c in enumerate(im_tiles):
                    bu_ref[c, pl.ds(r0, nb), :] = ni[:, q * LANES:(q + 1) * LANES]
                xr, xi = nr, ni
            return xr, xi

        xr, xi = lax.fori_loop(0, tl // unroll, body, (st_ref[:, re], st_ref[:, im]))
        st_ref[:, re] = xr
        st_ref[:, im] = xi
        sre_ref[:, j * hw:(j + 1) * hw] = xr
        sim_ref[:, j * hw:(j + 1) * hw] = xi

    for j in range(n_chunks):
        xs = jnp.concatenate([bu_ref[j * tpc + c] for c in range(tpc)], axis=-1)
        yt_ref[j] = jnp.dot(xs.astype(BF16), cj_ref[j], preferred_element_type=F32)
    y = jnp.concatenate(
        [jnp.concatenate([yt_ref[j, pl.ds(b, tl, stride=nb), :] for j in range(n_chunks)], axis=-1)
         for b in range(nb)], axis=0)
    u_all = u_ref[...].reshape(nb * tl, u_ref.shape[2])
    y_ref[...] = _s5_gate(y, u_all, d_ref, wglu_ref, bglu_ref).reshape(y_ref.shape)


def _s5_seq(u, bj, cj, a_re, a_im, d_vec, w_glu, b_glu, tl, unroll=4):
    nb, seq_len, sw = u.shape
    n_chunks, _, cw = bj.shape
    ns = a_re.shape[1]
    kern = functools.partial(_s5_seq_kernel, nb=nb, tl=tl, n_chunks=n_chunks, unroll=unroll)
    return pl.pallas_call(
        kern,
        grid=(seq_len // tl,),
        in_specs=[pl.BlockSpec((nb, tl, sw), lambda i: (0, i, 0)),
                  _full_spec(bj.shape), _full_spec(cj.shape), _full_spec(a_re.shape),
                  _full_spec(a_im.shape), _full_spec(d_vec.shape), _full_spec(w_glu.shape),
                  _full_spec(b_glu.shape)],
        out_specs=[pl.BlockSpec((nb, tl, sw), lambda i: (0, i, 0)),
                   _full_spec((nb, ns)), _full_spec((nb, ns))],
        out_shape=[jax.ShapeDtypeStruct((nb, seq_len, sw), F32),
                   jax.ShapeDtypeStruct((nb, ns), F32), jax.ShapeDtypeStruct((nb, ns), F32)],
        scratch_shapes=[pltpu.VMEM((n_chunks * cw // LANES, tl * nb, LANES), F32),
                        pltpu.VMEM((nb, n_chunks * cw), F32),
                        pltpu.VMEM((n_chunks, tl * nb, LANES), F32),
                        pltpu.VMEM((n_chunks, tl * nb, LANES), F32)],
        compiler_params=_params(1),
        name="s5_seq",
    )(u, bj, cj, a_re, a_im, d_vec, w_glu, b_glu)


def _s5_step_kernel(u_ref, hre_ref, him_ref, bj_ref, cj_ref, are_ref, aim_ref, d_ref,
                    wglu_ref, bglu_ref, y_ref, sre_ref, sim_ref, *, n_chunks):
    u = u_ref[...]
    ub = u.astype(BF16)
    kw = u.shape[1] // n_chunks
    hw = bj_ref.shape[2] // 2
    chunks = []
    for j in range(n_chunks):
        st = slice(j * hw, (j + 1) * hw)
        bu = jnp.dot(ub[:, j * kw:(j + 1) * kw], bj_ref[j], preferred_element_type=F32)
        ar, ai = are_ref[:, st], aim_ref[:, st]
        hr, hi = hre_ref[:, st], him_ref[:, st]
        xr = ar * hr - ai * hi + bu[:, :hw]
        xi = ar * hi + ai * hr + bu[:, hw:]
        sre_ref[:, st] = xr
        sim_ref[:, st] = xi
        chunks.append(jnp.concatenate([xr, xi], axis=-1))
    y = jnp.concatenate([jnp.dot(chunks[j].astype(BF16), cj_ref[j], preferred_element_type=F32)
                         for j in range(n_chunks)], axis=-1)
    y_ref[...] = _s5_gate(y, u, d_ref, wglu_ref, bglu_ref)


def _s5_step(u, h_re, h_im, bj, cj, a_re, a_im, d_vec, w_glu, b_glu):
    t, sw = u.shape
    ns = a_re.shape[1]
    args = (u, h_re, h_im, bj, cj, a_re, a_im, d_vec, w_glu, b_glu)
    return pl.pallas_call(
        functools.partial(_s5_step_kernel, n_chunks=bj.shape[0]),
        grid=(1,),
        in_specs=[_full_spec(a.shape) for a in args],
        out_specs=[_full_spec((t, sw)), _full_spec((t, ns)), _full_spec((t, ns))],
        out_shape=[jax.ShapeDtypeStruct((t, sw), F32), jax.ShapeDtypeStruct((t, ns), F32),
                   jax.ShapeDtypeStruct((t, ns), F32)],
        compiler_params=_params(1),
        name="s5_step",
    )(*args)


def _group_reduce(x, lane, width, op):
    n = x.shape[-1]
    s = 1
    while s < width:
        up = pltpu.roll(x, s, 1)
        dn = pltpu.roll(x, n - s, 1)
        x = op(x, jnp.where((lane & s) != 0, up, dn))
        s *= 2
    return x


def _first_max(x, lane, big):
    m = jnp.max(x, axis=-1, keepdims=True)
    return jnp.min(jnp.where(x == m, lane, big), axis=-1, keepdims=True)


def _route(logits, bias):
    rows, n = logits.shape
    lane = lax.broadcasted_iota(jnp.int32, (rows, n), 1)
    lane_f = lane.astype(F32)
    neg = jnp.float32(-jnp.inf)
    big = jnp.float32(n)
    gsz = N_EXPERTS // N_ROUTE_GROUPS
    group_start = (lane - (lane & (gsz - 1))).astype(F32)
    scores = jax.nn.sigmoid(logits)
    biased = scores + bias
    m1 = _group_reduce(biased, lane, gsz, jnp.maximum)
    first = _group_reduce(jnp.where(biased == m1, lane_f, big), lane, gsz, jnp.minimum)
    m2 = _group_reduce(jnp.where(lane_f == first, neg, biased), lane, gsz, jnp.maximum)
    gs = m1 + m2
    gsel = jnp.zeros((rows, n), F32)
    work = gs
    for _ in range(TOPK_ROUTE_GROUPS):
        hit = group_start == _first_max(work, lane_f, big)
        gsel = jnp.where(hit, 1.0, gsel)
        work = jnp.where(hit, neg, work)
    work = jnp.where(gsel > 0.0, biased, neg)
    w = jnp.zeros((rows, n), F32)
    for _ in range(TOP_K):
        hit = lane_f == _first_max(work, lane_f, big)
        w = jnp.where(hit, scores, w)
        work = jnp.where(hit, neg, work)
    return w / jnp.sum(w, axis=-1, keepdims=True) * ROUTED_SCALE


def _route_by_expert(logits, bias):
    ne, tm = logits.shape
    ng = N_ROUTE_GROUPS
    gsz = ne // ng
    neg = jnp.float32(-jnp.inf)
    shape = (ng, gsz, tm)
    scores = jax.nn.sigmoid(logits)
    biased = (scores + bias).reshape(shape)
    scores = scores.reshape(shape)
    sub = lax.broadcasted_iota(jnp.int32, shape, 1)
    grp = lax.broadcasted_iota(jnp.int32, (ng, 1, tm), 0)
    eid = lax.broadcasted_iota(jnp.int32, shape, 0) * gsz + sub
    m1 = jnp.max(biased, axis=1, keepdims=True)
    first = jnp.min(jnp.where(biased == m1, sub, gsz), axis=1, keepdims=True)
    m2 = jnp.max(jnp.where(sub == first, neg, biased), axis=1, keepdims=True)
    work = m1 + m2
    gsel = jnp.zeros((ng, 1, tm), jnp.bool_)
    for _ in range(TOPK_ROUTE_GROUPS):
        m = jnp.max(work, axis=0, keepdims=True)
        hit = grp == jnp.min(jnp.where(work == m, grp, ng), axis=0, keepdims=True)
        gsel = jnp.logical_or(gsel, hit)
        work = jnp.where(hit, neg, work)
    work = jnp.where(gsel, biased, neg)
    w = jnp.zeros(shape, F32)
    for _ in range(TOP_K):
        m = jnp.max(jnp.max(work, axis=0, keepdims=True), axis=1, keepdims=True)
        first = jnp.min(jnp.min(jnp.where(work == m, eid, ne), axis=0, keepdims=True),
                        axis=1, keepdims=True)
        hit = eid == first
        w = jnp.where(hit, scores, w)
        work = jnp.where(hit, neg, work)
    denom = jnp.sum(jnp.sum(w, axis=0, keepdims=True), axis=1, keepdims=True)
    return (w / denom * ROUTED_SCALE).reshape(ne, tm)


def _split_bf16(x):
    hi = x.astype(BF16)
    lo = (x - hi.astype(F32)).astype(BF16)
    return hi, lo


def _postmix_kernel(x_ref, ya_ref, yb_ref, gt1_ref, sh2_ref, sc2_ref, gt2_ref, g2_ref,
                    wout_ref, rwh_ref, rwl_ref, rb_ref, swg_ref, swu_ref, swd_ref,
                    xacc_ref, h2_ref, wd_ref, *, token_tiles):
    y = jnp.concatenate([ya_ref[...], yb_ref[...]], axis=-1)
    x1 = x_ref[...] + gt1_ref[...] * _dot(y, wout_ref[...])
    h2 = _rmsnorm(x1, g2_ref[...]) * (1.0 + sc2_ref[...]) + sh2_ref[...]
    hb = h2.astype(BF16)
    if token_tiles:
        tm, d = h2.shape
        nj = d // LANES
        for j in range(nj):
            h2_ref[pl.ds(j, tm, stride=nj), :] = h2[:, j * LANES:(j + 1) * LANES]
    else:
        h2_ref[...] = hb
    h_hi, h_lo = _split_bf16(h2)
    if token_tiles:
        nt = (((1,), (1,)), ((), ()))
        logits = (lax.dot_general(rwh_ref[...], h_hi, nt, preferred_element_type=F32)
                  + lax.dot_general(rwl_ref[...], h_hi, nt, preferred_element_type=F32)
                  + lax.dot_general(rwh_ref[...], h_lo, nt, preferred_element_type=F32))
        wd_ref[...] = _route_by_expert(logits, rb_ref[...])
    else:
        logits = (jnp.dot(h_hi, rwh_ref[...], preferred_element_type=F32)
                  + jnp.dot(h_hi, rwl_ref[...], preferred_element_type=F32)
                  + jnp.dot(h_lo, rwh_ref[...], preferred_element_type=F32))
        wd_ref[...] = _route(logits, rb_ref[...])
    hid = jax.nn.silu(jnp.dot(hb, swg_ref[...], preferred_element_type=F32)) * \
        jnp.dot(hb, swu_ref[...], preferred_element_type=F32)
    shared = _dot(hid, swd_ref[...])
    xacc_ref[...] = x1 + gt2_ref[...] * shared


def _postmix(x, ya, yb, gt1, sh2, sc2, gt2, g2, w_out, rw_hi, rw_lo, rb, swg, swu, swd,
             per_row, seq_len, tm, token_tiles):
    t, d = x.shape
    sw = ya.shape[1]
    nj = d // LANES
    mod = _mod_spec(per_row, tm, d, max(seq_len // tm, 1))
    weights = (g2, w_out, rw_hi, rw_lo, rb, swg, swu, swd)
    if token_tiles:
        ne = rw_hi.shape[0]
        h2_spec, h2_shape = _row_spec(tm * nj, LANES), jax.ShapeDtypeStruct((t * nj, LANES), F32)
        wd_spec, wd_shape = pl.BlockSpec((ne, tm), lambda i: (0, i)), (ne, t)
    else:
        ne = rw_hi.shape[1]
        h2_spec, h2_shape = _row_spec(tm, d), jax.ShapeDtypeStruct((t, d), BF16)
        wd_spec, wd_shape = _row_spec(tm, ne), (t, ne)
    return pl.pallas_call(
        functools.partial(_postmix_kernel, token_tiles=token_tiles),
        grid=(t // tm,),
        in_specs=[_row_spec(tm, d), _row_spec(tm, sw), _row_spec(tm, sw), mod, mod, mod, mod]
        + [_full_spec(w.shape) for w in weights],
        out_specs=[_row_spec(tm, d), h2_spec, wd_spec],
        out_shape=[jax.ShapeDtypeStruct((t, d), F32), h2_shape,
                   jax.ShapeDtypeStruct(wd_shape, F32)],
        compiler_params=_params(1),
        name="postmix",
    )(x, ya, yb, gt1, sh2, sc2, gt2, *weights)


def _moe_dense_kernel(h_ref, wd_ref, xacc_ref, gt2_ref, gf_ref, wg_ref, wu_ref, wdn_ref,
                      o_ref, acc_ref, *, final_norm):
    step = pl.program_id(1)
    per_step = wg_ref.shape[0]

    @pl.when(step == 0)
    def _():
        acc_ref[...] = jnp.zeros_like(acc_ref)

    hb = h_ref[...]
    wd = wd_ref[...]
    lane = lax.broadcasted_iota(jnp.int32, wd.shape, 1)
    total = None
    for q in range(per_step):
        col = jnp.sum(jnp.where(lane == step * per_step + q, wd, 0.0), axis=-1, keepdims=True)
        hid = jax.nn.silu(jnp.dot(hb, wg_ref[q], preferred_element_type=F32)) * \
            jnp.dot(hb, wu_ref[q], preferred_element_type=F32)
        part = col * _dot(hid, wdn_ref[q])
        total = part if total is None else total + part
    acc_ref[...] += total

    @pl.when(step == pl.num_programs(1) - 1)
    def _():
        xo = xacc_ref[...] + gt2_ref[...] * acc_ref[...]
        o_ref[...] = _rmsnorm(xo, gf_ref[...]) if final_norm else xo


def _moe_dense(h2, wd, xacc, gt2, g_final, wg, wu, wdn, per_row, seq_len, tm, final_norm):
    t, d = xacc.shape
    ne, _, eh = wg.shape
    nl = wd.shape[1]
    tpb = max(seq_len // tm, 1)
    if per_row:
        mod = pl.BlockSpec((tm, d), lambda i, e: (i, 0))
    else:
        mod = pl.BlockSpec((None, 1, d), lambda i, e: (i // tpb, 0, 0))
    row = lambda w: pl.BlockSpec((tm, w), lambda i, e: (i, 0))
    per_step = DENSE_EXPERTS_PER_STEP
    assert ne % per_step == 0
    return pl.pallas_call(
        functools.partial(_moe_dense_kernel, final_norm=final_norm),
        grid=(t // tm, ne // per_step),
        in_specs=[row(d), row(nl), row(d), mod,
                  pl.BlockSpec(g_final.shape, lambda i, e: (0, 0)),
                  pl.BlockSpec((per_step, d, eh), lambda i, e: (e, 0, 0)),
                  pl.BlockSpec((per_step, d, eh), lambda i, e: (e, 0, 0)),
                  pl.BlockSpec((per_step, eh, d), lambda i, e: (e, 0, 0))],
        out_specs=row(d),
        out_shape=jax.ShapeDtypeStruct((t, d), F32),
        scratch_shapes=[pltpu.VMEM((tm, d), F32)],
        compiler_params=_params(2),
        name="moe_dense",
    )(h2, wd, xacc, gt2, g_final, wg, wu, wdn)


TOKEN_BITS = 13


def _route_plan_kernel(wd_ref, rows_ref, wl_ref, cnt_ref, *, tg, nj):
    wt = wd_ref[...]
    sel = wt > 0.0
    lane = lax.broadcasted_iota(jnp.int32, wt.shape, 1)
    c = sel.astype(jnp.int32)
    s = 1
    while s < tg:
        c = c + jnp.where(lane >= s, pltpu.roll(c, s, 1), 0)
        s *= 2
    cnt_ref[...] = jnp.broadcast_to(c[:, tg - 1:tg], cnt_ref.shape)
    dist = lane - (c - 1)
    key = jnp.where(sel, (dist << TOKEN_BITS) | lane, -1)
    w = jnp.where(sel, wt, 0.0)
    s = 1
    while s < tg:
        leaving = (key >= 0) & (((key >> TOKEN_BITS) & s) != 0)
        key_in = pltpu.roll(key, tg - s, 1)
        w_in = pltpu.roll(w, tg - s, 1)
        arriving = (key_in >= 0) & (((key_in >> TOKEN_BITS) & s) != 0)
        key = jnp.where(arriving, key_in, jnp.where(leaving, -1, key))
        w = jnp.where(arriving, w_in, jnp.where(leaving, 0.0, w))
        s *= 2
    rows_ref[...] = jnp.where(key >= 0, key & ((1 << TOKEN_BITS) - 1), 0) * nj
    wl_ref[...] = w


def _route_plan(wd, tg, nj):
    ne, t = wd.shape
    assert ne == N_EXPERTS and tg <= (1 << TOKEN_BITS) and t % tg == 0
    g = t // tg
    grp = lambda w: pl.BlockSpec((None, N_EXPERTS, w), lambda i: (i, 0, 0))
    return pl.pallas_call(
        functools.partial(_route_plan_kernel, tg=tg, nj=nj),
        grid=(g,),
        in_specs=[pl.BlockSpec((ne, tg), lambda i: (0, i))],
        out_specs=[grp(tg), grp(tg), grp(LANES)],
        out_shape=[jax.ShapeDtypeStruct((g, N_EXPERTS, tg), jnp.int32),
                   jax.ShapeDtypeStruct((g, N_EXPERTS, tg), F32),
                   jax.ShapeDtypeStruct((g, N_EXPERTS, LANES), jnp.int32)],
        compiler_params=_params(1),
        name="route_plan",
    )(wd)


def _block_table(counts, tg, bm, nb_max):
    g, ne = counts.shape
    i32 = jnp.int32
    nblk = (counts + bm - 1) // bm
    bend = jnp.cumsum(nblk, axis=1)
    bstart = bend - nblk
    nb = bend[:, -1]
    k = jnp.arange(nb_max, dtype=i32)[None, :]
    kc = jnp.minimum(k, jnp.maximum(nb[:, None] - 1, 0))
    e_k = jnp.minimum(jnp.sum(bend[:, None, :] <= kc[:, :, None], axis=2), ne - 1).astype(i32)
    b_k = kc - jnp.take_along_axis(bstart, e_k, axis=1)
    first = (b_k == 0) & (k < nb[:, None])
    active = nblk > 0
    parity = (jnp.take_along_axis(jnp.cumsum(active.astype(i32), axis=1) - 1, e_k, axis=1)
              % MOE_WEIGHT_BUFFERS)
    ids = jnp.where(active, jnp.arange(ne, dtype=i32)[None, :], ne)
    first_active_from = jnp.flip(lax.cummin(jnp.flip(ids, 1), axis=1), 1)
    nxt = jnp.concatenate([first_active_from[:, 1:], jnp.full((g, 1), ne, i32)], axis=1)
    nxt = jnp.where(nxt >= ne, -1, nxt)
    nxt2 = jnp.where(nxt < 0, -1, jnp.take_along_axis(nxt, jnp.maximum(nxt, 0), axis=1))
    next_k = jnp.take_along_axis(nxt, e_k, axis=1)
    next2_k = jnp.take_along_axis(nxt2, e_k, axis=1)
    row = (jnp.arange(g, dtype=i32)[:, None] * ne + e_k) * (tg // bm) + b_k
    flat = lambda a: a.reshape(-1).astype(i32)
    table = (nb.astype(i32), flat(e_k), flat(first), flat(parity), flat(next_k), flat(next2_k))
    return table, flat(row)


def _moe_group_kernel(nb_ref, exp_ref, first_ref, par_ref, next_ref, next2_ref,
                      h_ref, rows_ref, wl_ref, wg_hbm, wu_hbm, wdn_hbm, o_ref,
                      wg_buf, wu_buf, wdn_buf, wsem, xt_a, xt_b, hid_a, hid_b, yt_a, yt_b,
                      *, bm, nj, batch, nb_max):
    g = pl.program_id(0)
    nb = nb_ref[g]
    tb = g * nb_max
    pitch = bm + 1
    o_ref[...] = jnp.zeros_like(o_ref)

    def weight_copies(e, slot):
        return (pltpu.make_async_copy(wg_hbm.at[e], wg_buf.at[slot], wsem.at[slot]),
                pltpu.make_async_copy(wu_hbm.at[e], wu_buf.at[slot], wsem.at[slot]),
                pltpu.make_async_copy(wdn_hbm.at[e], wdn_buf.at[slot], wsem.at[slot]))

    n_wbuf = wg_buf.shape[0]

    def handoff(k):
        @pl.when(first_ref[tb + k] == 1)
        def _():
            slot = par_ref[tb + k]
            for c in weight_copies(exp_ref[tb + k], slot):
                c.wait()
            ahead = next2_ref[tb + k]

            @pl.when(ahead >= 0)
            def _():
                for c in weight_copies(ahead, (slot + 2) % n_wbuf):
                    c.start()

    def gather_rows(k, xt):
        rows = rows_ref.at[pl.ds(pl.multiple_of(k * bm, bm), bm)]

        def emit(lo, hi):
            for m in range(lo, hi):
                r0 = pl.multiple_of(rows[m], nj)
                xt[pl.ds(m, nj, stride=pitch), :] = h_ref[pl.ds(r0, nj), :]
        return emit

    def scatter_rows(k, yt):
        rows = rows_ref.at[pl.ds(pl.multiple_of(k * bm, bm), bm)]

        def emit(lo, hi):
            for m0 in range(lo, hi, batch):
                updates = []
                for m in range(m0, min(m0 + batch, hi)):
                    r0 = pl.multiple_of(rows[m], nj)
                    updates.append((r0, o_ref[pl.ds(r0, nj), :]
                                    + yt[pl.ds(m, nj, stride=pitch), :]))
                for r0, v in reversed(updates):
                    o_ref[pl.ds(r0, nj), :] = v
        return emit

    n_chunks = nj // 2
    kc = 2 * LANES
    n_pieces = 2 * n_chunks + 1
    early = bm * MOE_EARLY_ROW_PERCENT // 100
    cuts = [early * i // (n_pieces - 1) for i in range(n_pieces)] + [bm]

    def step(up_blk, down_blk, side_work):
        piece = 0

        def side():
            nonlocal piece
            for emit in side_work:
                emit(cuts[piece], cuts[piece + 1])
            piece += 1

        gate = up = None
        for c in range(n_chunks):
            if up_blk is not None:
                k, xt, _ = up_blk
                slot = par_ref[tb + k]
                xc = jnp.concatenate([xt[j * pitch:j * pitch + bm, :] for j in (2 * c, 2 * c + 1)],
                                     axis=-1).astype(BF16)
                pg = jnp.dot(xc, wg_buf[slot, c * kc:(c + 1) * kc, :], preferred_element_type=F32)
                pu = jnp.dot(xc, wu_buf[slot, c * kc:(c + 1) * kc, :], preferred_element_type=F32)
                gate = pg if gate is None else gate + pg
                up = pu if up is None else up + pu
            side()
        for c in range(n_chunks):
            if down_blk is not None:
                k, hid, yt = down_blk
                slot = par_ref[tb + k]
                yc = jnp.dot(hid[...], wdn_buf[slot, :, c * kc:(c + 1) * kc],
                             preferred_element_type=F32)
                for q in range(2):
                    j = 2 * c + q
                    yt[j * pitch:j * pitch + bm, :] = yc[:, q * LANES:(q + 1) * LANES]
            side()
        side()
        if up_blk is not None:
            k, _, hid = up_blk
            w_col = jnp.broadcast_to(wl_ref[k], (SUBLANES, bm)).T[:, :1]
            hid[...] = (jax.nn.silu(gate) * up * w_col).astype(BF16)

    @pl.when(nb > 0)
    def _():
        for c in weight_copies(exp_ref[tb], 0):
            c.start()
        second = next_ref[tb]

        @pl.when(second >= 0)
        def _():
            for c in weight_copies(second, 1):
                c.start()
        gather_rows(0, xt_a)(0, bm)
        handoff(0)
        step((0, xt_a, hid_a), None, [gather_rows(1, xt_b)])
        handoff(1)
        step((1, xt_b, hid_b), (0, hid_a, yt_a), [gather_rows(2, xt_a)])

        def trip(i, carry):
            t = 2 * i + 1
            handoff(t + 1)
            step((t + 1, xt_a, hid_a), (t, hid_b, yt_b),
                 [gather_rows(t + 2, xt_b), scatter_rows(t - 1, yt_a)])
            handoff(t + 2)
            step((t + 2, xt_b, hid_b), (t + 1, hid_a, yt_a),
                 [gather_rows(t + 3, xt_a), scatter_rows(t, yt_b)])
            return carry

        lax.fori_loop(0, nb // 2, trip, 0)

        @pl.when(nb % 2 == 1)
        def _():
            scatter_rows(nb - 1, yt_a)(0, bm)


def _moe_group(table, h2t, rows, wl, wg, wu, wdn, tg, bm, nb_max, batch=4):
    ne, d, eh = wg.shape
    nj = d // LANES
    assert nj == SUBLANES
    g = h2t.shape[0] // (tg * nj)
    kern = functools.partial(_moe_group_kernel, bm=bm, nj=nj, batch=batch, nb_max=nb_max)
    slots = pl.BlockSpec((nb_max * bm,), lambda gi, *_: (gi,), memory_space=pltpu.SMEM)
    block_w = pl.BlockSpec((nb_max, 1, bm), lambda gi, *_: (gi, 0, 0))
    hbm = pl.BlockSpec(memory_space=pl.ANY)
    pitch_rows = nj * (bm + 1)
    grid_spec = pltpu.PrefetchScalarGridSpec(
        num_scalar_prefetch=len(table),
        grid=(g,),
        in_specs=[pl.BlockSpec((tg * nj, LANES), lambda gi, *_: (gi, 0)),
                  slots, block_w, hbm, hbm, hbm],
        out_specs=pl.BlockSpec((tg * nj, LANES), lambda gi, *_: (gi, 0),
                               pipeline_mode=pl.Buffered(1)),
        scratch_shapes=[pltpu.VMEM((MOE_WEIGHT_BUFFERS, d, eh), BF16),
                        pltpu.VMEM((MOE_WEIGHT_BUFFERS, d, eh), BF16),
                        pltpu.VMEM((MOE_WEIGHT_BUFFERS, eh, d), BF16),
                        pltpu.SemaphoreType.DMA((MOE_WEIGHT_BUFFERS,)),
                        pltpu.VMEM((pitch_rows, LANES), F32), pltpu.VMEM((pitch_rows, LANES), F32),
                        pltpu.VMEM((bm, eh), BF16), pltpu.VMEM((bm, eh), BF16),
                        pltpu.VMEM((pitch_rows, LANES), F32), pltpu.VMEM((pitch_rows, LANES), F32)],
    )
    return pl.pallas_call(
        kern,
        grid_spec=grid_spec,
        out_shape=jax.ShapeDtypeStruct((g * tg * nj, LANES), F32),
        compiler_params=_params(1),
        name="moe_group",
    )(*table, h2t, rows, wl.reshape(g * nb_max, 1, bm), wg, wu, wdn)


def _finalize_kernel(r_ref, xacc_ref, gt2_ref, gf_ref, o_ref, *, final_norm):
    tm, d = xacc_ref.shape
    nj = d // LANES
    routed = jnp.concatenate([r_ref[pl.ds(j, tm, stride=nj), :] for j in range(nj)], axis=-1)
    xo = xacc_ref[...] + gt2_ref[...] * routed
    o_ref[...] = _rmsnorm(xo, gf_ref[...]) if final_norm else xo


def _finalize(routed, xacc, gt2, g_final, seq_len, tm, final_norm):
    t, d = xacc.shape
    nj = d // LANES
    return pl.pallas_call(
        functools.partial(_finalize_kernel, final_norm=final_norm),
        grid=(t // tm,),
        in_specs=[_row_spec(tm * nj, LANES),
                  _row_spec(tm, d), _mod_spec(False, tm, d, seq_len // tm),
                  _full_spec(g_final.shape)],
        out_specs=_row_spec(tm, d),
        out_shape=jax.ShapeDtypeStruct((t, d), F32),
        compiler_params=_params(1),
        name="finalize",
    )(routed, xacc, gt2, g_final)


def _s5_discretize(lam_re, lam_im, log_dt, b_re, b_im, c_re, c_im):
    g, n = lam_re.shape
    hg = b_re.shape[2]
    lr = jnp.minimum(lam_re.astype(F32), -1e-4)
    li = lam_im.astype(F32)
    dt = jnp.exp(log_dt.astype(F32))[:, None]
    mag = jnp.exp(lr * dt)
    abar_r, abar_i = mag * jnp.cos(li * dt), mag * jnp.sin(li * dt)
    den = lr * lr + li * li
    nr, ni = abar_r - 1.0, abar_i
    f_r = (nr * lr + ni * li) / den
    f_i = (ni * lr - nr * li) / den
    br_, bi_ = b_re.astype(F32), b_im.astype(F32)
    bbar_r = f_r[..., None] * br_ - f_i[..., None] * bi_
    bbar_i = f_r[..., None] * bi_ + f_i[..., None] * br_
    gpc = LANES // hg
    n_chunks = g // gpc
    eye = jnp.eye(gpc, dtype=F32)

    def b_chunks(bb):
        bb = bb.reshape(n_chunks, gpc, n, hg)
        m = jnp.einsum('jgnh,gk->jghkn', bb, eye)
        return m.reshape(n_chunks, gpc * hg, gpc * n)

    def c_chunks(cc):
        cc = cc.astype(F32).reshape(n_chunks, gpc, hg, n)
        m = jnp.einsum('jghn,gk->jgnkh', cc, eye)
        return m.reshape(n_chunks, gpc * n, gpc * hg)

    bj = jnp.concatenate([b_chunks(bbar_r), b_chunks(bbar_i)], axis=2).astype(BF16)
    cj = jnp.concatenate([c_chunks(c_re), -c_chunks(c_im)], axis=1).astype(BF16)
    return abar_r.reshape(1, g * n), abar_i.reshape(1, g * n), bj, cj


def kernel(x_prompt, x_sample, c_prompt, c_sample, state_ssm_re, state_ssm_im, g_norm1, g_norm2, w_ada, b_ada, w_in, w_out, ssm_lambda_re, ssm_lambda_im, ssm_log_dt, ssm_b_re, ssm_b_im, ssm_c_re, ssm_c_im, ssm_d, ssm_w_glu, ssm_b_glu, gm_ln_g, gm_ln_b, gm_w_s, gm_b_s, router_w, router_bias, moe_w_gate, moe_w_up, moe_w_down, shared_w_gate, shared_w_up, shared_w_down, g_final):
    bp, seq_len, d = x_prompt.shape
    bs, dec_len, _ = x_sample.shape
    depth = w_ada.shape[0]
    assert dec_len == 1 and seq_len % CHUNK == 0 and bp % SUBLANES == 0
    groups, n_state = ssm_lambda_re.shape[1:]
    sw = groups * SSM_GROUP
    gw = gm_ln_g.shape[1]
    hd = gw // GM_HEADS

    xp = x_prompt.reshape(bp * seq_len, d)
    xs = x_sample.reshape(bs, d)
    c_all = jnp.concatenate([c_prompt, c_sample], axis=0)
    gf = g_final.reshape(1, d)

    p_re, p_im, s_re, s_im, s_v = [], [], [], [], []
    for l in range(depth):
        mod = _ada(c_all, w_ada[l], b_ada[l].reshape(1, -1))
        mod_p = [m.reshape(bp, 1, d) for m in jnp.split(mod[:bp], 6, axis=-1)]
        mod_s = jnp.split(mod[bp:], 6, axis=-1)

        g1 = g_norm1[l].reshape(1, d)
        g2 = g_norm2[l].reshape(1, d)
        w_in_b = w_in[l].astype(BF16)
        w_out_b = w_out[l].astype(BF16)
        ln_g = gm_ln_g[l].reshape(1, gw)
        ln_b = gm_ln_b[l].reshape(1, gw)
        ws_tril = (gm_w_s[l] * jnp.tril(jnp.ones((CHUNK, CHUNK), F32))[None]).astype(BF16)
        bs_full = jnp.repeat(gm_b_s[l].T, hd, axis=1)
        w0 = jnp.repeat(gm_w_s[l][:, 0, 0], hd).reshape(1, gw)
        b0 = jnp.repeat(gm_b_s[l][:, 0], hd).reshape(1, gw)
        a_re, a_im, bj, cj = _s5_discretize(ssm_lambda_re[l], ssm_lambda_im[l], ssm_log_dt[l],
                                            ssm_b_re[l], ssm_b_im[l], ssm_c_re[l], ssm_c_im[l])
        d_vec = ssm_d[l].reshape(1, sw)
        w_glu = ssm_w_glu[l].astype(BF16)
        b_glu = ssm_b_glu[l].reshape(1, sw)
        pad = LANES - N_EXPERTS
        rw = jnp.pad(router_w[l], ((0, 0), (0, pad)))
        rw_hi = rw.astype(BF16)
        rw_lo = (rw - rw_hi.astype(F32)).astype(BF16)
        rb = jnp.pad(router_bias[l], (0, pad), constant_values=-jnp.inf).reshape(1, LANES)
        rwt_hi, rwt_lo = rw_hi[:, :N_EXPERTS].T, rw_lo[:, :N_EXPERTS].T
        rb_col = router_bias[l].reshape(N_EXPERTS, 1)
        swg, swu, swd = (shared_w_gate[l].astype(BF16), shared_w_up[l].astype(BF16),
                         shared_w_down[l].astype(BF16))
        ewg, ewu, ewd = (moe_w_gate[l].astype(BF16), moe_w_up[l].astype(BF16),
                         moe_w_down[l].astype(BF16))

        tm_row = min(ROW_TILE, seq_len)
        ua, yb = _premix_seq(xp, mod_p[0], mod_p[1], g1, w_in_b, ln_g, ln_b, ws_tril, bs_full,
                             seq_len, tm=min(PREMIX_TILE, seq_len))
        ya, hpr, hpi = _s5_seq(ua.reshape(bp, seq_len, sw), bj, cj, a_re, a_im, d_vec,
                               w_glu, b_glu, tl=min(SCAN_TILE, seq_len))
        xacc, h2, wd = _postmix(xp, ya.reshape(bp * seq_len, sw), yb, mod_p[2], mod_p[3],
                                mod_p[4], mod_p[5], g2, w_out_b, rwt_hi, rwt_lo, rb_col,
                                swg, swu, swd,
                                per_row=False, seq_len=seq_len, tm=tm_row, token_tiles=True)
        last = l == depth - 1
        tg = min(MOE_GROUP, bp * seq_len)
        assert (bp * seq_len) % tg == 0
        bm = min(MOE_BLOCK, tg)
        per_tile = max(SMEM_TILE // bm, 1)
        nb_max = -(-(tg * TOP_K // bm + N_EXPERTS + 3) // per_tile) * per_tile
        slot_rows, slot_w, counts = _route_plan(wd, tg, d // LANES)
        table, block_row = _block_table(counts[:, :, 0], tg, bm, nb_max)
        by_block = lambda a: jnp.take(a.reshape(-1, bm), block_row, axis=0).reshape(-1)
        routed = _moe_group(table, h2, by_block(slot_rows), by_block(slot_w),
                            ewg, ewu, ewd, tg, bm, nb_max)
        xp = _finalize(routed, xacc, mod_p[5], gf, seq_len, min(PREMIX_TILE, seq_len),
                       final_norm=last)
        p_re.append(hpr.reshape(bp, groups, n_state))
        p_im.append(hpi.reshape(bp, groups, n_state))

        ua, yb, vn = _premix_step(xs, mod_s[0], mod_s[1], g1, w_in_b, ln_g, ln_b, w0, b0)
        ya, hsr, hsi = _s5_step(ua, state_ssm_re[l].reshape(bs, groups * n_state),
                                state_ssm_im[l].reshape(bs, groups * n_state),
                                bj, cj, a_re, a_im, d_vec, w_glu, b_glu)
        xacc, h2, wd = _postmix(xs, ya, yb, mod_s[2], mod_s[3], mod_s[4], mod_s[5], g2, w_out_b,
                                rw_hi, rw_lo, rb, swg, swu, swd, per_row=True, seq_len=1, tm=bs,
                                token_tiles=False)
        xs = _moe_dense(h2, wd, xacc, mod_s[5], gf, ewg, ewu, ewd, per_row=True, seq_len=1,
                        tm=bs, final_norm=last)
        s_re.append(hsr.reshape(bs, groups, n_state))
        s_im.append(hsi.reshape(bs, groups, n_state))
        s_v.append(vn.reshape(bs, 1, gw))

    y_prompt = xp.reshape(bp, seq_len, d)
    y_sample = xs.reshape(bs, 1, d)
    return (y_prompt, y_sample, jnp.stack(p_re), jnp.stack(p_im), jnp.stack(s_re),
            jnp.stack(s_im), jnp.stack(s_v))
```

```python
import functools

import jax
import jax.numpy as jnp
from jax import lax
from jax.experimental import pallas as pl
from jax.experimental.pallas import tpu as pltpu

F32 = jnp.float32
BF16 = jnp.bfloat16

EPS = 1e-6
SSM_GROUP = 16
SSM_STATE = 64
GM_HEADS = 4
CHUNK = 128
N_EXPERTS = 64
TOP_K = 8
N_ROUTE_GROUPS = 8
TOPK_ROUTE_GROUPS = 4
ROUTED_SCALE = 2.5

LANES = 128
SUBLANES = 8
VMEM_LIMIT = 48 * 1024 * 1024
ROW_TILE = 512
PREMIX_TILE = 1024
DENSE_EXPERTS_PER_STEP = 8
SCAN_TILE = 128
MOE_GROUP = 4096
SMEM_TILE = 1024
MOE_BLOCK = 128
MOE_EARLY_ROW_PERCENT = 56
MOE_WEIGHT_BUFFERS = 4


def _params(n_axes):
    return pltpu.CompilerParams(dimension_semantics=("arbitrary",) * n_axes,
                                vmem_limit_bytes=VMEM_LIMIT)


def _dot(a, b):
    return jnp.dot(a.astype(BF16), b.astype(BF16), preferred_element_type=F32)


def _rmsnorm(x, g):
    ms = jnp.mean(x * x, axis=-1, keepdims=True)
    return x * lax.rsqrt(ms + EPS) * g


def _ada_kernel(c_ref, w_ref, b_ref, o_ref):
    c = c_ref[...]
    o_ref[...] = _dot(jax.nn.silu(c), w_ref[...]) + b_ref[...]


def _ada(c, w, b, tn=512):
    m, d = c.shape
    n = w.shape[1]
    return pl.pallas_call(
        _ada_kernel,
        grid=(n // tn,),
        in_specs=[pl.BlockSpec((m, d), lambda j: (0, 0)),
                  pl.BlockSpec((d, tn), lambda j: (0, j)),
                  pl.BlockSpec((1, tn), lambda j: (0, j))],
        out_specs=pl.BlockSpec((m, tn), lambda j: (0, j)),
        out_shape=jax.ShapeDtypeStruct((m, n), F32),
        compiler_params=_params(1),
        name="ada_mod",
    )(c, w, b)


def _premix_common(x_ref, sh_ref, sc_ref, g_ref, win_ref, lng_ref, lnb_ref, sw):
    x = x_ref[...]
    h = _rmsnorm(x, g_ref[...]) * (1.0 + sc_ref[...]) + sh_ref[...]
    proj = _dot(h, win_ref[...])
    u_a = proj[:, :sw]
    z = jax.nn.gelu(proj[:, sw:])
    gw = z.shape[1] // 2
    u_b = z[:, :gw]
    v_b = z[:, gw:]
    hd = gw // GM_HEADS
    parts = []
    for k in range(GM_HEADS):
        vh = v_b[:, k * hd:(k + 1) * hd]
        mu = jnp.mean(vh, axis=-1, keepdims=True)
        dv = vh - mu
        var = jnp.mean(dv * dv, axis=-1, keepdims=True)
        parts.append(dv * lax.rsqrt(var + EPS))
    vn = jnp.concatenate(parts, axis=-1) * lng_ref[...] + lnb_ref[...]
    return u_a, u_b, vn


def _premix_seq_kernel(x_ref, sh_ref, sc_ref, g_ref, win_ref, lng_ref, lnb_ref,
                       ws_ref, bs_ref, ua_ref, yb_ref, *, sw):
    u_a, u_b, vn = _premix_common(x_ref, sh_ref, sc_ref, g_ref, win_ref, lng_ref, lnb_ref, sw)
    ua_ref[...] = u_a
    tm, gw = u_b.shape
    hd = gw // GM_HEADS
    vnb = vn.astype(BF16)
    for c in range(tm // CHUNK):
        rows = slice(c * CHUNK, (c + 1) * CHUNK)
        for k in range(GM_HEADS):
            cols = slice(k * hd, (k + 1) * hd)
            s = jnp.dot(ws_ref[k], vnb[rows, cols], preferred_element_type=F32)
            yb_ref[rows, cols] = u_b[rows, cols] * (s + bs_ref[:, cols])


def _premix_step_kernel(x_ref, sh_ref, sc_ref, g_ref, win_ref, lng_ref, lnb_ref,
                        w0_ref, b0_ref, ua_ref, yb_ref, vn_ref, *, sw):
    u_a, u_b, vn = _premix_common(x_ref, sh_ref, sc_ref, g_ref, win_ref, lng_ref, lnb_ref, sw)
    ua_ref[...] = u_a
    vn_ref[...] = vn
    yb_ref[...] = u_b * (vn * w0_ref[...] + b0_ref[...])


def _row_spec(tm, d):
    return pl.BlockSpec((tm, d), lambda i: (i, 0))


def _full_spec(shape):
    nd = len(shape)
    return pl.BlockSpec(shape, lambda i: (0,) * nd)


def _mod_spec(per_row, tm, d, tiles_per_batch):
    if per_row:
        return pl.BlockSpec((tm, d), lambda i: (i, 0))
    return pl.BlockSpec((None, 1, d), lambda i: (i // tiles_per_batch, 0, 0))


def _premix_seq(x, sh, sc, g, w_in, ln_g, ln_b, ws_tril, bs_full, seq_len, tm):
    t, d = x.shape
    sw = ln_g.shape[1]
    tpb = seq_len // tm
    mod = _mod_spec(False, tm, d, tpb)
    return pl.pallas_call(
        functools.partial(_premix_seq_kernel, sw=sw),
        grid=(t // tm,),
        in_specs=[_row_spec(tm, d), mod, mod, _full_spec(g.shape), _full_spec(w_in.shape),
                  _full_spec(ln_g.shape), _full_spec(ln_b.shape), _full_spec(ws_tril.shape),
                  _full_spec(bs_full.shape)],
        out_specs=[_row_spec(tm, sw), _row_spec(tm, sw)],
        out_shape=[jax.ShapeDtypeStruct((t, sw), F32), jax.ShapeDtypeStruct((t, sw), F32)],
        compiler_params=_params(1),
        name="premix_seq",
    )(x, sh, sc, g, w_in, ln_g, ln_b, ws_tril, bs_full)


def _premix_step(x, sh, sc, g, w_in, ln_g, ln_b, w0, b0):
    t, d = x.shape
    sw = ln_g.shape[1]
    mod = _mod_spec(True, t, d, 1)
    return pl.pallas_call(
        functools.partial(_premix_step_kernel, sw=sw),
        grid=(1,),
        in_specs=[_row_spec(t, d), mod, mod, _full_spec(g.shape), _full_spec(w_in.shape),
                  _full_spec(ln_g.shape), _full_spec(ln_b.shape), _full_spec(w0.shape),
                  _full_spec(b0.shape)],
        out_specs=[_row_spec(t, sw)] * 3,
        out_shape=[jax.ShapeDtypeStruct((t, sw), F32)] * 3,
        compiler_params=_params(1),
        name="premix_step",
    )(x, sh, sc, g, w_in, ln_g, ln_b, w0, b0)


def _s5_gate(y, u, d_ref, wglu_ref, bglu_ref):
    y = jax.nn.gelu(y + d_ref[...] * u)
    gate = jax.nn.sigmoid(_dot(y, wglu_ref[...]) + bglu_ref[...])
    return y * gate


def _s5_seq_kernel(u_ref, bj_ref, cj_ref, are_ref, aim_ref, d_ref, wglu_ref, bglu_ref,
                   y_ref, sre_ref, sim_ref, bu_ref, st_ref, ut_ref, yt_ref,
                   *, nb, tl, n_chunks, unroll):
    i = pl.program_id(0)
    n_tiles = bu_ref.shape[0]
    tpc = n_tiles // n_chunks
    hpc = tpc // 2
    hw = hpc * LANES
    assert u_ref.shape[2] == n_chunks * LANES

    @pl.when(i == 0)
    def _():
        st_ref[...] = jnp.zeros_like(st_ref)

    for b in range(nb):
        ub = u_ref[b]
        for j in range(n_chunks):
            ut_ref[j, pl.ds(b, tl, stride=nb), :] = ub[:, j * LANES:(j + 1) * LANES]
    for j in range(n_chunks):
        bu = jnp.dot(ut_ref[j].astype(BF16), bj_ref[j], preferred_element_type=F32)
        for c in range(tpc):
            bu_ref[j * tpc + c] = bu[:, c * LANES:(c + 1) * LANES]

    for j in range(n_chunks):
        re = slice(j * 2 * hw, j * 2 * hw + hw)
        im = slice(j * 2 * hw + hw, (j + 1) * 2 * hw)
        ar = jnp.broadcast_to(are_ref[:, j * hw:(j + 1) * hw], (nb, hw))
        ai = jnp.broadcast_to(aim_ref[:, j * hw:(j + 1) * hw], (nb, hw))
        re_tiles = [j * tpc + c for c in range(hpc)]
        im_tiles = [j * tpc + hpc + c for c in range(hpc)]

        def body(s, carry, ar=ar, ai=ai, re_tiles=re_tiles, im_tiles=im_tiles):
            xr, xi = carry
            for k in range(unroll):
                r0 = pl.multiple_of((s * unroll + k) * nb, nb)
                br = jnp.concatenate([bu_ref[c, pl.ds(r0, nb), :] for c in re_tiles], axis=-1)
                bi = jnp.concatenate([bu_ref[c, pl.ds(r0, nb), :] for c in im_tiles], axis=-1)
                nr = ar * xr - ai * xi + br
                ni = ar * xi + ai * xr + bi
                for q, c in enumerate(re_tiles):
                    bu_ref[c, pl.ds(r0, nb), :] = nr[:, q * LANES:(q + 1) * LANES]
                for q, c in enumerate(im_tiles):
                    bu_ref[c, pl.ds(r0, nb), :] = ni[:, q * LANES:(q + 1) * LANES]
                xr, xi = nr, ni
            return xr, xi

        xr, xi = lax.fori_loop(0, tl // unroll, body, (st_ref[:, re], st_ref[:, im]))
        st_ref[:, re] = xr
        st_ref[:, im] = xi
        sre_ref[:, j * hw:(j + 1) * hw] = xr
        sim_ref[:, j * hw:(j + 1) * hw] = xi

    for j in range(n_chunks):
        xs = jnp.concatenate([bu_ref[j * tpc + c] for c in range(tpc)], axis=-1)
        yt_ref[j] = jnp.dot(xs.astype(BF16), cj_ref[j], preferred_element_type=F32)
    y = jnp.concatenate(
        [jnp.concatenate([yt_ref[j, pl.ds(b, tl, stride=nb), :] for j in range(n_chunks)], axis=-1)
         for b in range(nb)], axis=0)
    u_all = u_ref[...].reshape(nb * tl, u_ref.shape[2])
    y_ref[...] = _s5_gate(y, u_all, d_ref, wglu_ref, bglu_ref).reshape(y_ref.shape)


def _s5_seq(u, bj, cj, a_re, a_im, d_vec, w_glu, b_glu, tl, unroll=4):
    nb, seq_len, sw = u.shape
    n_chunks, _, cw = bj.shape
    ns = a_re.shape[1]
    kern = functools.partial(_s5_seq_kernel, nb=nb, tl=tl, n_chunks=n_chunks, unroll=unroll)
    return pl.pallas_call(
        kern,
        grid=(seq_len // tl,),
        in_specs=[pl.BlockSpec((nb, tl, sw), lambda i: (0, i, 0)),
                  _full_spec(bj.shape), _full_spec(cj.shape), _full_spec(a_re.shape),
                  _full_spec(a_im.shape), _full_spec(d_vec.shape), _full_spec(w_glu.shape),
                  _full_spec(b_glu.shape)],
        out_specs=[pl.BlockSpec((nb, tl, sw), lambda i: (0, i, 0)),
                   _full_spec((nb, ns)), _full_spec((nb, ns))],
        out_shape=[jax.ShapeDtypeStruct((nb, seq_len, sw), F32),
                   jax.ShapeDtypeStruct((nb, ns), F32), jax.ShapeDtypeStruct((nb, ns), F32)],
        scratch_shapes=[pltpu.VMEM((n_chunks * cw // LANES, tl * nb, LANES), F32),
                        pltpu.VMEM((nb, n_chunks * cw), F32),
                        pltpu.VMEM((n_chunks, tl * nb, LANES), F32),
                        pltpu.VMEM((n_chunks, tl * nb, LANES), F32)],
        compiler_params=_params(1),
        name="s5_seq",
    )(u, bj, cj, a_re, a_im, d_vec, w_glu, b_glu)


def _s5_step_kernel(u_ref, hre_ref, him_ref, bj_ref, cj_ref, are_ref, aim_ref, d_ref,
                    wglu_ref, bglu_ref, y_ref, sre_ref, sim_ref, *, n_chunks):
    u = u_ref[...]
    ub = u.astype(BF16)
    kw = u.shape[1] // n_chunks
    hw = bj_ref.shape[2] // 2
    chunks = []
    for j in range(n_chunks):
        st = slice(j * hw, (j + 1) * hw)
        bu = jnp.dot(ub[:, j * kw:(j + 1) * kw], bj_ref[j], preferred_element_type=F32)
        ar, ai = are_ref[:, st], aim_ref[:, st]
        hr, hi = hre_ref[:, st], him_ref[:, st]
        xr = ar * hr - ai * hi + bu[:, :hw]
        xi = ar * hi + ai * hr + bu[:, hw:]
        sre_ref[:, st] = xr
        sim_ref[:, st] = xi
        chunks.append(jnp.concatenate([xr, xi], axis=-1))
    y = jnp.concatenate([jnp.dot(chunks[j].astype(BF16), cj_ref[j], preferred_element_type=F32)
                         for j in range(n_chunks)], axis=-1)
    y_ref[...] = _s5_gate(y, u, d_ref, wglu_ref, bglu_ref)


def _s5_step(u, h_re, h_im, bj, cj, a_re, a_im, d_vec, w_glu, b_glu):
    t, sw = u.shape
    ns = a_re.shape[1]
    args = (u, h_re, h_im, bj, cj, a_re, a_im, d_vec, w_glu, b_glu)
    return pl.pallas_call(
        functools.partial(_s5_step_kernel, n_chunks=bj.shape[0]),
        grid=(1,),
        in_specs=[_full_spec(a.shape) for a in args],
        out_specs=[_full_spec((t, sw)), _full_spec((t, ns)), _full_spec((t, ns))],
        out_shape=[jax.ShapeDtypeStruct((t, sw), F32), jax.ShapeDtypeStruct((t, ns), F32),
                   jax.ShapeDtypeStruct((t, ns), F32)],
        compiler_params=_params(1),
        name="s5_step",
    )(*args)


def _group_reduce(x, lane, width, op):
    n = x.shape[-1]
    s = 1
    while s < width:
        up = pltpu.roll(x, s, 1)
        dn = pltpu.roll(x, n - s, 1)
        x = op(x, jnp.where((lane & s) != 0, up, dn))
        s *= 2
    return x


def _first_max(x, lane, big):
    m = jnp.max(x, axis=-1, keepdims=True)
    return jnp.min(jnp.where(x == m, lane, big), axis=-1, keepdims=True)


def _route(logits, bias):
    rows, n = logits.shape
    lane = lax.broadcasted_iota(jnp.int32, (rows, n), 1)
    lane_f = lane.astype(F32)
    neg = jnp.float32(-jnp.inf)
    big = jnp.float32(n)
    gsz = N_EXPERTS // N_ROUTE_GROUPS
    group_start = (lane - (lane & (gsz - 1))).astype(F32)
    scores = jax.nn.sigmoid(logits)
    biased = scores + bias
    m1 = _group_reduce(biased, lane, gsz, jnp.maximum)
    first = _group_reduce(jnp.where(biased == m1, lane_f, big), lane, gsz, jnp.minimum)
    m2 = _group_reduce(jnp.where(lane_f == first, neg, biased), lane, gsz, jnp.maximum)
    gs = m1 + m2
    gsel = jnp.zeros((rows, n), F32)
    work = gs
    for _ in range(TOPK_ROUTE_GROUPS):
        hit = group_start == _first_max(work, lane_f, big)
        gsel = jnp.where(hit, 1.0, gsel)
        work = jnp.where(hit, neg, work)
    work = jnp.where(gsel > 0.0, biased, neg)
    w = jnp.zeros((rows, n), F32)
    for _ in range(TOP_K):
        hit = lane_f == _first_max(work, lane_f, big)
        w = jnp.where(hit, scores, w)
        work = jnp.where(hit, neg, work)
    return w / jnp.sum(w, axis=-1, keepdims=True) * ROUTED_SCALE


def _route_by_expert(logits, bias):
    ne, tm = logits.shape
    ng = N_ROUTE_GROUPS
    gsz = ne // ng
    neg = jnp.float32(-jnp.inf)
    shape = (ng, gsz, tm)
    scores = jax.nn.sigmoid(logits)
    biased = (scores + bias).reshape(shape)
    scores = scores.reshape(shape)
    sub = lax.broadcasted_iota(jnp.int32, shape, 1)
    grp = lax.broadcasted_iota(jnp.int32, (ng, 1, tm), 0)
    eid = lax.broadcasted_iota(jnp.int32, shape, 0) * gsz + sub
    m1 = jnp.max(biased, axis=1, keepdims=True)
    first = jnp.min(jnp.where(biased == m1, sub, gsz), axis=1, keepdims=True)
    m2 = jnp.max(jnp.where(sub == first, neg, biased), axis=1, keepdims=True)
    work = m1 + m2
    gsel = jnp.zeros((ng, 1, tm), jnp.bool_)
    for _ in range(TOPK_ROUTE_GROUPS):
        m = jnp.max(work, axis=0, keepdims=True)
        hit = grp == jnp.min(jnp.where(work == m, grp, ng), axis=0, keepdims=True)
        gsel = jnp.logical_or(gsel, hit)
        work = jnp.where(hit, neg, work)
    work = jnp.where(gsel, biased, neg)
    w = jnp.zeros(shape, F32)
    for _ in range(TOP_K):
        m = jnp.max(jnp.max(work, axis=0, keepdims=True), axis=1, keepdims=True)
        first = jnp.min(jnp.min(jnp.where(work == m, eid, ne), axis=0, keepdims=True),
                        axis=1, keepdims=True)
        hit = eid == first
        w = jnp.where(hit, scores, w)
        work = jnp.where(hit, neg, work)
    denom = jnp.sum(jnp.sum(w, axis=0, keepdims=True), axis=1, keepdims=True)
    return (w / denom * ROUTED_SCALE).reshape(ne, tm)


def _split_bf16(x):
    hi = x.astype(BF16)
    lo = (x - hi.astype(F32)).astype(BF16)
    return hi, lo


def _postmix_kernel(x_ref, ya_ref, yb_ref, gt1_ref, sh2_ref, sc2_ref, gt2_ref, g2_ref,
                    wout_ref, rwh_ref, rwl_ref, rb_ref, swg_ref, swu_ref, swd_ref,
                    xacc_ref, h2_ref, wd_ref, *, token_tiles):
    y = jnp.concatenate([ya_ref[...], yb_ref[...]], axis=-1)
    x1 = x_ref[...] + gt1_ref[...] * _dot(y, wout_ref[...])
    h2 = _rmsnorm(x1, g2_ref[...]) * (1.0 + sc2_ref[...]) + sh2_ref[...]
    hb = h2.astype(BF16)
    if token_tiles:
        tm, d = h2.shape
        nj = d // LANES
        for j in range(nj):
            h2_ref[pl.ds(j, tm, stride=nj), :] = h2[:, j * LANES:(j + 1) * LANES]
    else:
        h2_ref[...] = hb
    h_hi, h_lo = _split_bf16(h2)
    if token_tiles:
        nt = (((1,), (1,)), ((), ()))
        logits = (lax.dot_general(rwh_ref[...], h_hi, nt, preferred_element_type=F32)
                  + lax.dot_general(rwl_ref[...], h_hi, nt, preferred_element_type=F32)
                  + lax.dot_general(rwh_ref[...], h_lo, nt, preferred_element_type=F32))
        wd_ref[...] = _route_by_expert(logits, rb_ref[...])
    else:
        logits = (jnp.dot(h_hi, rwh_ref[...], preferred_element_type=F32)
                  + jnp.dot(h_hi, rwl_ref[...], preferred_element_type=F32)
                  + jnp.dot(h_lo, rwh_ref[...], preferred_element_type=F32))
        wd_ref[...] = _route(logits, rb_ref[...])
    hid = jax.nn.silu(jnp.dot(hb, swg_ref[...], preferred_element_type=F32)) * \
        jnp.dot(hb, swu_ref[...], preferred_element_type=F32)
    shared = _dot(hid, swd_ref[...])
    xacc_ref[...] = x1 + gt2_ref[...] * shared


def _postmix(x, ya, yb, gt1, sh2, sc2, gt2, g2, w_out, rw_hi, rw_lo, rb, swg, swu, swd,
             per_row, seq_len, tm, token_tiles):
    t, d = x.shape
    sw = ya.shape[1]
    nj = d // LANES
    mod = _mod_spec(per_row, tm, d, max(seq_len // tm, 1))
    weights = (g2, w_out, rw_hi, rw_lo, rb, swg, swu, swd)
    if token_tiles:
        ne = rw_hi.shape[0]
        h2_spec, h2_shape = _row_spec(tm * nj, LANES), jax.ShapeDtypeStruct((t * nj, LANES), F32)
        wd_spec, wd_shape = pl.BlockSpec((ne, tm), lambda i: (0, i)), (ne, t)
    else:
        ne = rw_hi.shape[1]
        h2_spec, h2_shape = _row_spec(tm, d), jax.ShapeDtypeStruct((t, d), BF16)
        wd_spec, wd_shape = _row_spec(tm, ne), (t, ne)
    return pl.pallas_call(
        functools.partial(_postmix_kernel, token_tiles=token_tiles),
        grid=(t // tm,),
        in_specs=[_row_spec(tm, d), _row_spec(tm, sw), _row_spec(tm, sw), mod, mod, mod, mod]
        + [_full_spec(w.shape) for w in weights],
        out_specs=[_row_spec(tm, d), h2_spec, wd_spec],
        out_shape=[jax.ShapeDtypeStruct((t, d), F32), h2_shape,
                   jax.ShapeDtypeStruct(wd_shape, F32)],
        compiler_params=_params(1),
        name="postmix",
    )(x, ya, yb, gt1, sh2, sc2, gt2, *weights)


def _moe_dense_kernel(h_ref, wd_ref, xacc_ref, gt2_ref, gf_ref, wg_ref, wu_ref, wdn_ref,
                      o_ref, acc_ref, *, final_norm):
    step = pl.program_id(1)
    per_step = wg_ref.shape[0]

    @pl.when(step == 0)
    def _():
        acc_ref[...] = jnp.zeros_like(acc_ref)

    hb = h_ref[...]
    wd = wd_ref[...]
    lane = lax.broadcasted_iota(jnp.int32, wd.shape, 1)
    total = None
    for q in range(per_step):
        col = jnp.sum(jnp.where(lane == step * per_step + q, wd, 0.0), axis=-1, keepdims=True)
        hid = jax.nn.silu(jnp.dot(hb, wg_ref[q], preferred_element_type=F32)) * \
            jnp.dot(hb, wu_ref[q], preferred_element_type=F32)
        part = col * _dot(hid, wdn_ref[q])
        total = part if total is None else total + part
    acc_ref[...] += total

    @pl.when(step == pl.num_programs(1) - 1)
    def _():
        xo = xacc_ref[...] + gt2_ref[...] * acc_ref[...]
        o_ref[...] = _rmsnorm(xo, gf_ref[...]) if final_norm else xo


def _moe_dense(h2, wd, xacc, gt2, g_final, wg, wu, wdn, per_row, seq_len, tm, final_norm):
    t, d = xacc.shape
    ne, _, eh = wg.shape
    nl = wd.shape[1]
    tpb = max(seq_len // tm, 1)
    if per_row:
        mod = pl.BlockSpec((tm, d), lambda i, e: (i, 0))
    else:
        mod = pl.BlockSpec((None, 1, d), lambda i, e: (i // tpb, 0, 0))
    row = lambda w: pl.BlockSpec((tm, w), lambda i, e: (i, 0))
    per_step = DENSE_EXPERTS_PER_STEP
    assert ne % per_step == 0
    return pl.pallas_call(
        functools.partial(_moe_dense_kernel, final_norm=final_norm),
        grid=(t // tm, ne // per_step),
        in_specs=[row(d), row(nl), row(d), mod,
                  pl.BlockSpec(g_final.shape, lambda i, e: (0, 0)),
                  pl.BlockSpec((per_step, d, eh), lambda i, e: (e, 0, 0)),
                  pl.BlockSpec((per_step, d, eh), lambda i, e: (e, 0, 0)),
                  pl.BlockSpec((per_step, eh, d), lambda i, e: (e, 0, 0))],
        out_specs=row(d),
        out_shape=jax.ShapeDtypeStruct((t, d), F32),
        scratch_shapes=[pltpu.VMEM((tm, d), F32)],
        compiler_params=_params(2),
        name="moe_dense",
    )(h2, wd, xacc, gt2, g_final, wg, wu, wdn)


TOKEN_BITS = 13


def _route_plan_kernel(wd_ref, rows_ref, wl_ref, cnt_ref, *, tg, nj):
    wt = wd_ref[...]
    sel = wt > 0.0
    lane = lax.broadcasted_iota(jnp.int32, wt.shape, 1)
    c = sel.astype(jnp.int32)
    s = 1
    while s < tg:
        c = c + jnp.where(lane >= s, pltpu.roll(c, s, 1), 0)
        s *= 2
    cnt_ref[...] = jnp.broadcast_to(c[:, tg - 1:tg], cnt_ref.shape)
    dist = lane - (c - 1)
    key = jnp.where(sel, (dist << TOKEN_BITS) | lane, -1)
    w = jnp.where(sel, wt, 0.0)
    s = 1
    while s < tg:
        leaving = (key >= 0) & (((key >> TOKEN_BITS) & s) != 0)
        key_in = pltpu.roll(key, tg - s, 1)
        w_in = pltpu.roll(w, tg - s, 1)
        arriving = (key_in >= 0) & (((key_in >> TOKEN_BITS) & s) != 0)
        key = jnp.where(arriving, key_in, jnp.where(leaving, -1, key))
        w = jnp.where(arriving, w_in, jnp.where(leaving, 0.0, w))
        s *= 2
    rows_ref[...] = jnp.where(key >= 0, key & ((1 << TOKEN_BITS) - 1), 0) * nj
    wl_ref[...] = w


def _route_plan(wd, tg, nj):
    ne, t = wd.shape
    assert ne == N_EXPERTS and tg <= (1 << TOKEN_BITS) and t % tg == 0
    g = t // tg
    grp = lambda w: pl.BlockSpec((None, N_EXPERTS, w), lambda i: (i, 0, 0))
    return pl.pallas_call(
        functools.partial(_route_plan_kernel, tg=tg, nj=nj),
        grid=(g,),
        in_specs=[pl.BlockSpec((ne, tg), lambda i: (0, i))],
        out_specs=[grp(tg), grp(tg), grp(LANES)],
        out_shape=[jax.ShapeDtypeStruct((g, N_EXPERTS, tg), jnp.int32),
                   jax.ShapeDtypeStruct((g, N_EXPERTS, tg), F32),
                   jax.ShapeDtypeStruct((g, N_EXPERTS, LANES), jnp.int32)],
        compiler_params=_params(1),
        name="route_plan",
    )(wd)


def _block_table(counts, tg, bm, nb_max):
    g, ne = counts.shape
    i32 = jnp.int32
    nblk = (counts + bm - 1) // bm
    bend = jnp.cumsum(nblk, axis=1)
    bstart = bend - nblk
    nb = bend[:, -1]
    k = jnp.arange(nb_max, dtype=i32)[None, :]
    kc = jnp.minimum(k, jnp.maximum(nb[:, None] - 1, 0))
    e_k = jnp.minimum(jnp.sum(bend[:, None, :] <= kc[:, :, None], axis=2), ne - 1).astype(i32)
    b_k = kc - jnp.take_along_axis(bstart, e_k, axis=1)
    first = (b_k == 0) & (k < nb[:, None])
    active = nblk > 0
    parity = (jnp.take_along_axis(jnp.cumsum(active.astype(i32), axis=1) - 1, e_k, axis=1)
              % MOE_WEIGHT_BUFFERS)
    ids = jnp.where(active, jnp.arange(ne, dtype=i32)[None, :], ne)
    first_active_from = jnp.flip(lax.cummin(jnp.flip(ids, 1), axis=1), 1)
    nxt = jnp.concatenate([first_active_from[:, 1:], jnp.full((g, 1), ne, i32)], axis=1)
    nxt = jnp.where(nxt >= ne, -1, nxt)
    nxt2 = jnp.where(nxt < 0, -1, jnp.take_along_axis(nxt, jnp.maximum(nxt, 0), axis=1))
    next_k = jnp.take_along_axis(nxt, e_k, axis=1)
    next2_k = jnp.take_along_axis(nxt2, e_k, axis=1)
    row = (jnp.arange(g, dtype=i32)[:, None] * ne + e_k) * (tg // bm) + b_k
    flat = lambda a: a.reshape(-1).astype(i32)
    table = (nb.astype(i32), flat(e_k), flat(first), flat(parity), flat(next_k), flat(next2_k))
    return table, flat(row)


def _moe_group_kernel(nb_ref, exp_ref, first_ref, par_ref, next_ref, next2_ref,
                      h_ref, rows_ref, wl_ref, wg_hbm, wu_hbm, wdn_hbm, o_ref,
                      wg_buf, wu_buf, wdn_buf, wsem, xt_a, xt_b, hid_a, hid_b, yt_a, yt_b,
                      *, bm, nj, batch, nb_max):
    g = pl.program_id(0)
    nb = nb_ref[g]
    tb = g * nb_max
    pitch = bm + 1
    o_ref[...] = jnp.zeros_like(o_ref)

    def weight_copies(e, slot):
        return (pltpu.make_async_copy(wg_hbm.at[e], wg_buf.at[slot], wsem.at[slot]),
                pltpu.make_async_copy(wu_hbm.at[e], wu_buf.at[slot], wsem.at[slot]),
                pltpu.make_async_copy(wdn_hbm.at[e], wdn_buf.at[slot], wsem.at[slot]))

    n_wbuf = wg_buf.shape[0]

    def handoff(k):
        @pl.when(first_ref[tb + k] == 1)
        def _():
            slot = par_ref[tb + k]
            for c in weight_copies(exp_ref[tb + k], slot):
                c.wait()
            ahead = next2_ref[tb + k]

            @pl.when(ahead >= 0)
            def _():
                for c in weight_copies(ahead, (slot + 2) % n_wbuf):
                    c.start()

    def gather_rows(k, xt):
        rows = rows_ref.at[pl.ds(pl.multiple_of(k * bm, bm), bm)]

        def emit(lo, hi):
            for m in range(lo, hi):
                r0 = pl.multiple_of(rows[m], nj)
                xt[pl.ds(m, nj, stride=pitch), :] = h_ref[pl.ds(r0, nj), :]
        return emit

    def scatter_rows(k, yt):
        rows = rows_ref.at[pl.ds(pl.multiple_of(k * bm, bm), bm)]

        def emit(lo, hi):
            for m0 in range(lo, hi, batch):
                updates = []
                for m in range(m0, min(m0 + batch, hi)):
                    r0 = pl.multiple_of(rows[m], nj)
                    updates.append((r0, o_ref[pl.ds(r0, nj), :]
                                    + yt[pl.ds(m, nj, stride=pitch), :]))
                for r0, v in reversed(updates):
                    o_ref[pl.ds(r0, nj), :] = v
        return emit

    n_chunks = nj // 2
    kc = 2 * LANES
    n_pieces = 2 * n_chunks + 1
    early = bm * MOE_EARLY_ROW_PERCENT // 100
    cuts = [early * i // (n_pieces - 1) for i in range(n_pieces)] + [bm]

    def step(up_blk, down_blk, side_work):
        piece = 0

        def side():
            nonlocal piece
            for emit in side_work:
                emit(cuts[piece], cuts[piece + 1])
            piece += 1

        gate = up = None
        for c in range(n_chunks):
            if up_blk is not None:
                k, xt, _ = up_blk
                slot = par_ref[tb + k]
                xc = jnp.concatenate([xt[j * pitch:j * pitch + bm, :] for j in (2 * c, 2 * c + 1)],
                                     axis=-1).astype(BF16)
                pg = jnp.dot(xc, wg_buf[slot, c * kc:(c + 1) * kc, :], preferred_element_type=F32)
                pu = jnp.dot(xc, wu_buf[slot, c * kc:(c + 1) * kc, :], preferred_element_type=F32)
                gate = pg if gate is None else gate + pg
                up = pu if up is None else up + pu
            side()
        for c in range(n_chunks):
            if down_blk is not None:
                k, hid, yt = down_blk
                slot = par_ref[tb + k]
                yc = jnp.dot(hid[...], wdn_buf[slot, :, c * kc:(c + 1) * kc],
                             preferred_element_type=F32)
                for q in range(2):
                    j = 2 * c + q
                    yt[j * pitch:j * pitch + bm, :] = yc[:, q * LANES:(q + 1) * LANES]
            side()
        side()
        if up_blk is not None:
            k, _, hid = up_blk
            w_col = jnp.broadcast_to(wl_ref[k], (SUBLANES, bm)).T[:, :1]
            hid[...] = (jax.nn.silu(gate) * up * w_col).astype(BF16)

    @pl.when(nb > 0)
    def _():
        for c in weight_copies(exp_ref[tb], 0):
            c.start()
        second = next_ref[tb]

        @pl.when(second >= 0)
        def _():
            for c in weight_copies(second, 1):
                c.start()
        gather_rows(0, xt_a)(0, bm)
        handoff(0)
        step((0, xt_a, hid_a), None, [gather_rows(1, xt_b)])
        handoff(1)
        step((1, xt_b, hid_b), (0, hid_a, yt_a), [gather_rows(2, xt_a)])

        def trip(i, carry):
            t = 2 * i + 1
            handoff(t + 1)
            step((t + 1, xt_a, hid_a), (t, hid_b, yt_b),
                 [gather_rows(t + 2, xt_b), scatter_rows(t - 1, yt_a)])
            handoff(t + 2)
            step((t + 2, xt_b, hid_b), (t + 1, hid_a, yt_a),
                 [gather_rows(t + 3, xt_a), scatter_rows(t, yt_b)])
            return carry

        lax.fori_loop(0, nb // 2, trip, 0)

        @pl.when(nb % 2 == 1)
        def _():
            scatter_rows(nb - 1, yt_a)(0, bm)


def _moe_group(table, h2t, rows, wl, wg, wu, wdn, tg, bm, nb_max, batch=2):
    ne, d, eh = wg.shape
    nj = d // LANES
    assert nj == SUBLANES
    g = h2t.shape[0] // (tg * nj)
    kern = functools.partial(_moe_group_kernel, bm=bm, nj=nj, batch=batch, nb_max=nb_max)
    slots = pl.BlockSpec((nb_max * bm,), lambda gi, *_: (gi,), memory_space=pltpu.SMEM)
    block_w = pl.BlockSpec((nb_max, 1, bm), lambda gi, *_: (gi, 0, 0))
    hbm = pl.BlockSpec(memory_space=pl.ANY)
    pitch_rows = nj * (bm + 1)
    grid_spec = pltpu.PrefetchScalarGridSpec(
        num_scalar_prefetch=len(table),
        grid=(g,),
        in_specs=[pl.BlockSpec((tg * nj, LANES), lambda gi, *_: (gi, 0),
                               pipeline_mode=pl.Buffered(1)),
                  slots, block_w, hbm, hbm, hbm],
        out_specs=pl.BlockSpec((tg * nj, LANES), lambda gi, *_: (gi, 0),
                               pipeline_mode=pl.Buffered(1)),
        scratch_shapes=[pltpu.VMEM((MOE_WEIGHT_BUFFERS, d, eh), BF16),
                        pltpu.VMEM((MOE_WEIGHT_BUFFERS, d, eh), BF16),
                        pltpu.VMEM((MOE_WEIGHT_BUFFERS, eh, d), BF16),
                        pltpu.SemaphoreType.DMA((MOE_WEIGHT_BUFFERS,)),
                        pltpu.VMEM((pitch_rows, LANES), F32), pltpu.VMEM((pitch_rows, LANES), F32),
                        pltpu.VMEM((bm, eh), BF16), pltpu.VMEM((bm, eh), BF16),
                        pltpu.VMEM((pitch_rows, LANES), F32), pltpu.VMEM((pitch_rows, LANES), F32)],
    )
    return pl.pallas_call(
        kern,
        grid_spec=grid_spec,
        out_shape=jax.ShapeDtypeStruct((g * tg * nj, LANES), F32),
        compiler_params=_params(1),
        name="moe_group",
    )(*table, h2t, rows, wl.reshape(g * nb_max, 1, bm), wg, wu, wdn)


def _finalize_kernel(r_ref, xacc_ref, gt2_ref, gf_ref, o_ref, *, final_norm):
    tm, d = xacc_ref.shape
    nj = d // LANES
    routed = jnp.concatenate([r_ref[pl.ds(j, tm, stride=nj), :] for j in range(nj)], axis=-1)
    xo = xacc_ref[...] + gt2_ref[...] * routed
    o_ref[...] = _rmsnorm(xo, gf_ref[...]) if final_norm else xo


def _finalize(routed, xacc, gt2, g_final, seq_len, tm, final_norm):
    t, d = xacc.shape
    nj = d // LANES
    return pl.pallas_call(
        functools.partial(_finalize_kernel, final_norm=final_norm),
        grid=(t // tm,),
        in_specs=[_row_spec(tm * nj, LANES),
                  _row_spec(tm, d), _mod_spec(False, tm, d, seq_len // tm),
                  _full_spec(g_final.shape)],
        out_specs=_row_spec(tm, d),
        out_shape=jax.ShapeDtypeStruct((t, d), F32),
        compiler_params=_params(1),
        name="finalize",
    )(routed, xacc, gt2, g_final)


def _s5_discretize(lam_re, lam_im, log_dt, b_re, b_im, c_re, c_im):
    g, n = lam_re.shape
    hg = b_re.shape[2]
    lr = jnp.minimum(lam_re.astype(F32), -1e-4)
    li = lam_im.astype(F32)
    dt = jnp.exp(log_dt.astype(F32))[:, None]
    mag = jnp.exp(lr * dt)
    abar_r, abar_i = mag * jnp.cos(li * dt), mag * jnp.sin(li * dt)
    den = lr * lr + li * li
    nr, ni = abar_r - 1.0, abar_i
    f_r = (nr * lr + ni * li) / den
    f_i = (ni * lr - nr * li) / den
    br_, bi_ = b_re.astype(F32), b_im.astype(F32)
    bbar_r = f_r[..., None] * br_ - f_i[..., None] * bi_
    bbar_i = f_r[..., None] * bi_ + f_i[..., None] * br_
    gpc = LANES // hg
    n_chunks = g // gpc
    eye = jnp.eye(gpc, dtype=F32)

    def b_chunks(bb):
        bb = bb.reshape(n_chunks, gpc, n, hg)
        m = jnp.einsum('jgnh,gk->jghkn', bb, eye)
        return m.reshape(n_chunks, gpc * hg, gpc * n)

    def c_chunks(cc):
        cc = cc.astype(F32).reshape(n_chunks, gpc, hg, n)
        m = jnp.einsum('jghn,gk->jgnkh', cc, eye)
        return m.reshape(n_chunks, gpc * n, gpc * hg)

    bj = jnp.concatenate([b_chunks(bbar_r), b_chunks(bbar_i)], axis=2).astype(BF16)
    cj = jnp.concatenate([c_chunks(c_re), -c_chunks(c_im)], axis=1).astype(BF16)
    return abar_r.reshape(1, g * n), abar_i.reshape(1, g * n), bj, cj


def kernel(x_prompt, x_sample, c_prompt, c_sample, state_ssm_re, state_ssm_im, g_norm1, g_norm2, w_ada, b_ada, w_in, w_out, ssm_lambda_re, ssm_lambda_im, ssm_log_dt, ssm_b_re, ssm_b_im, ssm_c_re, ssm_c_im, ssm_d, ssm_w_glu, ssm_b_glu, gm_ln_g, gm_ln_b, gm_w_s, gm_b_s, router_w, router_bias, moe_w_gate, moe_w_up, moe_w_down, shared_w_gate, shared_w_up, shared_w_down, g_final):
    bp, seq_len, d = x_prompt.shape
    bs, dec_len, _ = x_sample.shape
    depth = w_ada.shape[0]
    assert dec_len == 1 and seq_len % CHUNK == 0 and bp % SUBLANES == 0
    groups, n_state = ssm_lambda_re.shape[1:]
    sw = groups * SSM_GROUP
    gw = gm_ln_g.shape[1]
    hd = gw // GM_HEADS

    xp = x_prompt.reshape(bp * seq_len, d)
    xs = x_sample.reshape(bs, d)
    c_all = jnp.concatenate([c_prompt, c_sample], axis=0)
    gf = g_final.reshape(1, d)

    p_re, p_im, s_re, s_im, s_v = [], [], [], [], []
    for l in range(depth):
        mod = _ada(c_all, w_ada[l], b_ada[l].reshape(1, -1))
        mod_p = [m.reshape(bp, 1, d) for m in jnp.split(mod[:bp], 6, axis=-1)]
        mod_s = jnp.split(mod[bp:], 6, axis=-1)

        g1 = g_norm1[l].reshape(1, d)
        g2 = g_norm2[l].reshape(1, d)
        w_in_b = w_in[l].astype(BF16)
        w_out_b = w_out[l].astype(BF16)
        ln_g = gm_ln_g[l].reshape(1, gw)
        ln_b = gm_ln_b[l].reshape(1, gw)
        ws_tril = (gm_w_s[l] * jnp.tril(jnp.ones((CHUNK, CHUNK), F32))[None]).astype(BF16)
        bs_full = jnp.repeat(gm_b_s[l].T, hd, axis=1)
        w0 = jnp.repeat(gm_w_s[l][:, 0, 0], hd).reshape(1, gw)
        b0 = jnp.repeat(gm_b_s[l][:, 0], hd).reshape(1, gw)
        a_re, a_im, bj, cj = _s5_discretize(ssm_lambda_re[l], ssm_lambda_im[l], ssm_log_dt[l],
                                            ssm_b_re[l], ssm_b_im[l], ssm_c_re[l], ssm_c_im[l])
        d_vec = ssm_d[l].reshape(1, sw)
        w_glu = ssm_w_glu[l].astype(BF16)
        b_glu = ssm_b_glu[l].reshape(1, sw)
        pad = LANES - N_EXPERTS
        rw = jnp.pad(router_w[l], ((0, 0), (0, pad)))
        rw_hi = rw.astype(BF16)
        rw_lo = (rw - rw_hi.astype(F32)).astype(BF16)
        rb = jnp.pad(router_bias[l], (0, pad), constant_values=-jnp.inf).reshape(1, LANES)
        rwt_hi, rwt_lo = rw_hi[:, :N_EXPERTS].T, rw_lo[:, :N_EXPERTS].T
        rb_col = router_bias[l].reshape(N_EXPERTS, 1)
        swg, swu, swd = (shared_w_gate[l].astype(BF16), shared_w_up[l].astype(BF16),
                         shared_w_down[l].astype(BF16))
        ewg, ewu, ewd = (moe_w_gate[l].astype(BF16), moe_w_up[l].astype(BF16),
                         moe_w_down[l].astype(BF16))

        tm_row = min(ROW_TILE, seq_len)
        ua, yb = _premix_seq(xp, mod_p[0], mod_p[1], g1, w_in_b, ln_g, ln_b, ws_tril, bs_full,
                             seq_len, tm=min(PREMIX_TILE, seq_len))
        ya, hpr, hpi = _s5_seq(ua.reshape(bp, seq_len, sw), bj, cj, a_re, a_im, d_vec,
                               w_glu, b_glu, tl=min(SCAN_TILE, seq_len))
        xacc, h2, wd = _postmix(xp, ya.reshape(bp * seq_len, sw), yb, mod_p[2], mod_p[3],
                                mod_p[4], mod_p[5], g2, w_out_b, rwt_hi, rwt_lo, rb_col,
                                swg, swu, swd,
                                per_row=False, seq_len=seq_len, tm=tm_row, token_tiles=True)
        last = l == depth - 1
        tg = min(MOE_GROUP, bp * seq_len)
        assert (bp * seq_len) % tg == 0
        bm = min(MOE_BLOCK, tg)
        per_tile = max(SMEM_TILE // bm, 1)
        nb_max = -(-(tg * TOP_K // bm + N_EXPERTS + 3) // per_tile) * per_tile
        slot_rows, slot_w, counts = _route_plan(wd, tg, d // LANES)
        table, block_row = _block_table(counts[:, :, 0], tg, bm, nb_max)
        by_block = lambda a: jnp.take(a.reshape(-1, bm), block_row, axis=0).reshape(-1)
        routed = _moe_group(table, h2, by_block(slot_rows), by_block(slot_w),
                            ewg, ewu, ewd, tg, bm, nb_max)
        xp = _finalize(routed, xacc, mod_p[5], gf, seq_len, tm_row, final_norm=last)
        p_re.append(hpr.reshape(bp, groups, n_state))
        p_im.append(hpi.reshape(bp, groups, n_state))

        ua, yb, vn = _premix_step(xs, mod_s[0], mod_s[1], g1, w_in_b, ln_g, ln_b, w0, b0)
        ya, hsr, hsi = _s5_step(ua, state_ssm_re[l].reshape(bs, groups * n_state),
                                state_ssm_im[l].reshape(bs, groups * n_state),
                                bj, cj, a_re, a_im, d_vec, w_glu, b_glu)
        xacc, h2, wd = _postmix(xs, ya, yb, mod_s[2], mod_s[3], mod_s[4], mod_s[5], g2, w_out_b,
                                rw_hi, rw_lo, rb, swg, swu, swd, per_row=True, seq_len=1, tm=bs,
                                token_tiles=False)
        xs = _moe_dense(h2, wd, xacc, mod_s[5], gf, ewg, ewu, ewd, per_row=True, seq_len=1,
                        tm=bs, final_norm=last)
        s_re.append(hsr.reshape(bs, groups, n_state))
        s_im.append(hsi.reshape(bs, groups, n_state))
        s_v.append(vn.reshape(bs, 1, gw))

    y_prompt = xp.reshape(bp, seq_len, d)
    y_sample = xs.reshape(bs, 1, d)
    return (y_prompt, y_sample, jnp.stack(p_re), jnp.stack(p_im), jnp.stack(s_re),
            jnp.stack(s_im), jnp.stack(s_v))
```

```python
import functools

import jax
import jax.numpy as jnp
from jax import lax
from jax.experimental import pallas as pl
from jax.experimental.pallas import tpu as pltpu

F32 = jnp.float32
BF16 = jnp.bfloat16

EPS = 1e-6
SSM_GROUP = 16
SSM_STATE = 64
GM_HEADS = 4
CHUNK = 128
N_EXPERTS = 64
TOP_K = 8
N_ROUTE_GROUPS = 8
TOPK_ROUTE_GROUPS = 4
ROUTED_SCALE = 2.5

LANES = 128
SUBLANES = 8
VMEM_LIMIT = 48 * 1024 * 1024
ROW_TILE = 512
PREMIX_TILE = 1024
DENSE_EXPERTS_PER_STEP = 8
SCAN_TILE = 128
MOE_GROUP = 4096
SMEM_TILE = 1024
MOE_BLOCK = 128
MOE_EARLY_ROW_PERCENT = 56
MOE_WEIGHT_BUFFERS = 4


def _params(n_axes):
    return pltpu.CompilerParams(dimension_semantics=("arbitrary",) * n_axes,
                                vmem_limit_bytes=VMEM_LIMIT)


def _dot(a, b):
    return jnp.dot(a.astype(BF16), b.astype(BF16), preferred_element_type=F32)


def _rmsnorm(x, g):
    ms = jnp.mean(x * x, axis=-1, keepdims=True)
    return x * lax.rsqrt(ms + EPS) * g


def _ada_kernel(c_ref, w_ref, b_ref, o_ref):
    c = c_ref[...]
    o_ref[...] = _dot(jax.nn.silu(c), w_ref[...]) + b_ref[...]


def _ada(c, w, b, tn=512):
    m, d = c.shape
    n = w.shape[1]
    return pl.pallas_call(
        _ada_kernel,
        grid=(n // tn,),
        in_specs=[pl.BlockSpec((m, d), lambda j: (0, 0)),
                  pl.BlockSpec((d, tn), lambda j: (0, j)),
                  pl.BlockSpec((1, tn), lambda j: (0, j))],
        out_specs=pl.BlockSpec((m, tn), lambda j: (0, j)),
        out_shape=jax.ShapeDtypeStruct((m, n), F32),
        compiler_params=_params(1),
        name="ada_mod",
    )(c, w, b)


def _premix_common(x_ref, sh_ref, sc_ref, g_ref, win_ref, lng_ref, lnb_ref, sw):
    x = x_ref[...]
    h = _rmsnorm(x, g_ref[...]) * (1.0 + sc_ref[...]) + sh_ref[...]
    proj = _dot(h, win_ref[...])
    u_a = proj[:, :sw]
    z = jax.nn.gelu(proj[:, sw:])
    gw = z.shape[1] // 2
    u_b = z[:, :gw]
    v_b = z[:, gw:]
    hd = gw // GM_HEADS
    parts = []
    for k in range(GM_HEADS):
        vh = v_b[:, k * hd:(k + 1) * hd]
        mu = jnp.mean(vh, axis=-1, keepdims=True)
        dv = vh - mu
        var = jnp.mean(dv * dv, axis=-1, keepdims=True)
        parts.append(dv * lax.rsqrt(var + EPS))
    vn = jnp.concatenate(parts, axis=-1) * lng_ref[...] + lnb_ref[...]
    return u_a, u_b, vn


def _premix_seq_kernel(x_ref, sh_ref, sc_ref, g_ref, win_ref, lng_ref, lnb_ref,
                       ws_ref, bs_ref, ua_ref, yb_ref, *, sw):
    u_a, u_b, vn = _premix_common(x_ref, sh_ref, sc_ref, g_ref, win_ref, lng_ref, lnb_ref, sw)
    ua_ref[...] = u_a
    tm, gw = u_b.shape
    hd = gw // GM_HEADS
    vnb = vn.astype(BF16)
    for c in range(tm // CHUNK):
        rows = slice(c * CHUNK, (c + 1) * CHUNK)
        for k in range(GM_HEADS):
            cols = slice(k * hd, (k + 1) * hd)
            s = jnp.dot(ws_ref[k], vnb[rows, cols], preferred_element_type=F32)
            yb_ref[rows, cols] = u_b[rows, cols] * (s + bs_ref[:, cols])


def _premix_step_kernel(x_ref, sh_ref, sc_ref, g_ref, win_ref, lng_ref, lnb_ref,
                        w0_ref, b0_ref, ua_ref, yb_ref, vn_ref, *, sw):
    u_a, u_b, vn = _premix_common(x_ref, sh_ref, sc_ref, g_ref, win_ref, lng_ref, lnb_ref, sw)
    ua_ref[...] = u_a
    vn_ref[...] = vn
    yb_ref[...] = u_b * (vn * w0_ref[...] + b0_ref[...])


def _row_spec(tm, d):
    return pl.BlockSpec((tm, d), lambda i: (i, 0))


def _full_spec(shape):
    nd = len(shape)
    return pl.BlockSpec(shape, lambda i: (0,) * nd)


def _mod_spec(per_row, tm, d, tiles_per_batch):
    if per_row:
        return pl.BlockSpec((tm, d), lambda i: (i, 0))
    return pl.BlockSpec((None, 1, d), lambda i: (i // tiles_per_batch, 0, 0))


def _premix_seq(x, sh, sc, g, w_in, ln_g, ln_b, ws_tril, bs_full, seq_len, tm):
    t, d = x.shape
    sw = ln_g.shape[1]
    tpb = seq_len // tm
    mod = _mod_spec(False, tm, d, tpb)
    return pl.pallas_call(
        functools.partial(_premix_seq_kernel, sw=sw),
        grid=(t // tm,),
        in_specs=[_row_spec(tm, d), mod, mod, _full_spec(g.shape), _full_spec(w_in.shape),
                  _full_spec(ln_g.shape), _full_spec(ln_b.shape), _full_spec(ws_tril.shape),
                  _full_spec(bs_full.shape)],
        out_specs=[_row_spec(tm, sw), _row_spec(tm, sw)],
        out_shape=[jax.ShapeDtypeStruct((t, sw), F32), jax.ShapeDtypeStruct((t, sw), F32)],
        compiler_params=_params(1),
        name="premix_seq",
    )(x, sh, sc, g, w_in, ln_g, ln_b, ws_tril, bs_full)


def _premix_step(x, sh, sc, g, w_in, ln_g, ln_b, w0, b0):
    t, d = x.shape
    sw = ln_g.shape[1]
    mod = _mod_spec(True, t, d, 1)
    return pl.pallas_call(
        functools.partial(_premix_step_kernel, sw=sw),
        grid=(1,),
        in_specs=[_row_spec(t, d), mod, mod, _full_spec(g.shape), _full_spec(w_in.shape),
                  _full_spec(ln_g.shape), _full_spec(ln_b.shape), _full_spec(w0.shape),
                  _full_spec(b0.shape)],
        out_specs=[_row_spec(t, sw)] * 3,
        out_shape=[jax.ShapeDtypeStruct((t, sw), F32)] * 3,
        compiler_params=_params(1),
        name="premix_step",
    )(x, sh, sc, g, w_in, ln_g, ln_b, w0, b0)


def _s5_gate(y, u, d_ref, wglu_ref, bglu_ref):
    y = jax.nn.gelu(y + d_ref[...] * u)
    gate = jax.nn.sigmoid(_dot(y, wglu_ref[...]) + bglu_ref[...])
    return y * gate


def _s5_seq_kernel(u_ref, bj_ref, cj_ref, are_ref, aim_ref, d_ref, wglu_ref, bglu_ref,
                   y_ref, sre_ref, sim_ref, bu_ref, st_ref, ut_ref, yt_ref,
                   *, nb, tl, n_chunks, unroll):
    i = pl.program_id(0)
    n_tiles = bu_ref.shape[0]
    tpc = n_tiles // n_chunks
    hpc = tpc // 2
    hw = hpc * LANES
    assert u_ref.shape[2] == n_chunks * LANES

    @pl.when(i == 0)
    def _():
        st_ref[...] = jnp.zeros_like(st_ref)

    for b in range(nb):
        ub = u_ref[b]
        for j in range(n_chunks):
            ut_ref[j, pl.ds(b, tl, stride=nb), :] = ub[:, j * LANES:(j + 1) * LANES]
    for j in range(n_chunks):
        bu = jnp.dot(ut_ref[j].astype(BF16), bj_ref[j], preferred_element_type=F32)
        for c in range(tpc):
            bu_ref[j * tpc + c] = bu[:, c * LANES:(c + 1) * LANES]

    for j in range(n_chunks):
        re = slice(j * 2 * hw, j * 2 * hw + hw)
        im = slice(j * 2 * hw + hw, (j + 1) * 2 * hw)
        ar = jnp.broadcast_to(are_ref[:, j * hw:(j + 1) * hw], (nb, hw))
        ai = jnp.broadcast_to(aim_ref[:, j * hw:(j + 1) * hw], (nb, hw))
        re_tiles = [j * tpc + c for c in range(hpc)]
        im_tiles = [j * tpc + hpc + c for c in range(hpc)]

        def body(s, carry, ar=ar, ai=ai, re_tiles=re_tiles, im_tiles=im_tiles):
            xr, xi = carry
            for k in range(unroll):
                r0 = pl.multiple_of((s * unroll + k) * nb, nb)
                br = jnp.concatenate([bu_ref[c, pl.ds(r0, nb), :] for c in re_tiles], axis=-1)
                bi = jnp.concatenate([bu_ref[c, pl.ds(r0, nb), :] for c in im_tiles], axis=-1)
                nr = ar * xr - ai * xi + br
                ni = ar * xi + ai * xr + bi
                for q, c in enumerate(re_tiles):
                    bu_ref[c, pl.ds(r0, nb), :] = nr[:, q * LANES:(q + 1) * LANES]
                for q, c in enumerate(im_tiles):
                    bu_ref[c, pl.ds(r0, nb), :] = ni[:, q * LANES:(q + 1) * LANES]
                xr, xi = nr, ni
            return xr, xi

        xr, xi = lax.fori_loop(0, tl // unroll, body, (st_ref[:, re], st_ref[:, im]))
        st_ref[:, re] = xr
        st_ref[:, im] = xi
        sre_ref[:, j * hw:(j + 1) * hw] = xr
        sim_ref[:, j * hw:(j + 1) * hw] = xi

    for j in range(n_chunks):
        xs = jnp.concatenate([bu_ref[j * tpc + c] for c in range(tpc)], axis=-1)
        yt_ref[j] = jnp.dot(xs.astype(BF16), cj_ref[j], preferred_element_type=F32)
    y = jnp.concatenate(
        [jnp.concatenate([yt_ref[j, pl.ds(b, tl, stride=nb), :] for j in range(n_chunks)], axis=-1)
         for b in range(nb)], axis=0)
    u_all = u_ref[...].reshape(nb * tl, u_ref.shape[2])
    y_ref[...] = _s5_gate(y, u_all, d_ref, wglu_ref, bglu_ref).reshape(y_ref.shape)


def _s5_seq(u, bj, cj, a_re, a_im, d_vec, w_glu, b_glu, tl, unroll=4):
    nb, seq_len, sw = u.shape
    n_chunks, _, cw = bj.shape
    ns = a_re.shape[1]
    kern = functools.partial(_s5_seq_kernel, nb=nb, tl=tl, n_chunks=n_chunks, unroll=unroll)
    return pl.pallas_call(
        kern,
        grid=(seq_len // tl,),
        in_specs=[pl.BlockSpec((nb, tl, sw), lambda i: (0, i, 0)),
                  _full_spec(bj.shape), _full_spec(cj.shape), _full_spec(a_re.shape),
                  _full_spec(a_im.shape), _full_spec(d_vec.shape), _full_spec(w_glu.shape),
                  _full_spec(b_glu.shape)],
        out_specs=[pl.BlockSpec((nb, tl, sw), lambda i: (0, i, 0)),
                   _full_spec((nb, ns)), _full_spec((nb, ns))],
        out_shape=[jax.ShapeDtypeStruct((nb, seq_len, sw), F32),
                   jax.ShapeDtypeStruct((nb, ns), F32), jax.ShapeDtypeStruct((nb, ns), F32)],
        scratch_shapes=[pltpu.VMEM((n_chunks * cw // LANES, tl * nb, LANES), F32),
                        pltpu.VMEM((nb, n_chunks * cw), F32),
                        pltpu.VMEM((n_chunks, tl * nb, LANES), F32),
                        pltpu.VMEM((n_chunks, tl * nb, LANES), F32)],
        compiler_params=_params(1),
        name="s5_seq",
    )(u, bj, cj, a_re, a_im, d_vec, w_glu, b_glu)


def _s5_step_kernel(u_ref, hre_ref, him_ref, bj_ref, cj_ref, are_ref, aim_ref, d_ref,
                    wglu_ref, bglu_ref, y_ref, sre_ref, sim_ref, *, n_chunks):
    u = u_ref[...]
    ub = u.astype(BF16)
    kw = u.shape[1] // n_chunks
    hw = bj_ref.shape[2] // 2
    chunks = []
    for j in range(n_chunks):
        st = slice(j * hw, (j + 1) * hw)
        bu = jnp.dot(ub[:, j * kw:(j + 1) * kw], bj_ref[j], preferred_element_type=F32)
        ar, ai = are_ref[:, st], aim_ref[:, st]
        hr, hi = hre_ref[:, st], him_ref[:, st]
        xr = ar * hr - ai * hi + bu[:, :hw]
        xi = ar * hi + ai * hr + bu[:, hw:]
        sre_ref[:, st] = xr
        sim_ref[:, st] = xi
        chunks.append(jnp.concatenate([xr, xi], axis=-1))
    y = jnp.concatenate([jnp.dot(chunks[j].astype(BF16), cj_ref[j], preferred_element_type=F32)
                         for j in range(n_chunks)], axis=-1)
    y_ref[...] = _s5_gate(y, u, d_ref, wglu_ref, bglu_ref)


def _s5_step(u, h_re, h_im, bj, cj, a_re, a_im, d_vec, w_glu, b_glu):
    t, sw = u.shape
    ns = a_re.shape[1]
    args = (u, h_re, h_im, bj, cj, a_re, a_im, d_vec, w_glu, b_glu)
    return pl.pallas_call(
        functools.partial(_s5_step_kernel, n_chunks=bj.shape[0]),
        grid=(1,),
        in_specs=[_full_spec(a.shape) for a in args],
        out_specs=[_full_spec((t, sw)), _full_spec((t, ns)), _full_spec((t, ns))],
        out_shape=[jax.ShapeDtypeStruct((t, sw), F32), jax.ShapeDtypeStruct((t, ns), F32),
                   jax.ShapeDtypeStruct((t, ns), F32)],
        compiler_params=_params(1),
        name="s5_step",
    )(*args)


def _group_reduce(x, lane, width, op):
    n = x.shape[-1]
    s = 1
    while s < width:
        up = pltpu.roll(x, s, 1)
        dn = pltpu.roll(x, n - s, 1)
        x = op(x, jnp.where((lane & s) != 0, up, dn))
        s *= 2
    return x


def _first_max(x, lane, big):
    m = jnp.max(x, axis=-1, keepdims=True)
    return jnp.min(jnp.where(x == m, lane, big), axis=-1, keepdims=True)


def _route(logits, bias):
    rows, n = logits.shape
    lane = lax.broadcasted_iota(jnp.int32, (rows, n), 1)
    lane_f = lane.astype(F32)
    neg = jnp.float32(-jnp.inf)
    big = jnp.float32(n)
    gsz = N_EXPERTS // N_ROUTE_GROUPS
    group_start = (lane - (lane & (gsz - 1))).astype(F32)
    scores = jax.nn.sigmoid(logits)
    biased = scores + bias
    m1 = _group_reduce(biased, lane, gsz, jnp.maximum)
    first = _group_reduce(jnp.where(biased == m1, lane_f, big), lane, gsz, jnp.minimum)
    m2 = _group_reduce(jnp.where(lane_f == first, neg, biased), lane, gsz, jnp.maximum)
    gs = m1 + m2
    gsel = jnp.zeros((rows, n), F32)
    work = gs
    for _ in range(TOPK_ROUTE_GROUPS):
        hit = group_start == _first_max(work, lane_f, big)
        gsel = jnp.where(hit, 1.0, gsel)
        work = jnp.where(hit, neg, work)
    work = jnp.where(gsel > 0.0, biased, neg)
    w = jnp.zeros((rows, n), F32)
    for _ in range(TOP_K):
        hit = lane_f == _first_max(work, lane_f, big)
        w = jnp.where(hit, scores, w)
        work = jnp.where(hit, neg, work)
    return w / jnp.sum(w, axis=-1, keepdims=True) * ROUTED_SCALE


def _route_by_expert(logits, bias):
    ne, tm = logits.shape
    ng = N_ROUTE_GROUPS
    gsz = ne // ng
    neg = jnp.float32(-jnp.inf)
    shape = (ng, gsz, tm)
    scores = jax.nn.sigmoid(logits)
    biased = (scores + bias).reshape(shape)
    scores = scores.reshape(shape)
    sub = lax.broadcasted_iota(jnp.int32, shape, 1)
    grp = lax.broadcasted_iota(jnp.int32, (ng, 1, tm), 0)
    eid = lax.broadcasted_iota(jnp.int32, shape, 0) * gsz + sub
    m1 = jnp.max(biased, axis=1, keepdims=True)
    first = jnp.min(jnp.where(biased == m1, sub, gsz), axis=1, keepdims=True)
    m2 = jnp.max(jnp.where(sub == first, neg, biased), axis=1, keepdims=True)
    work = m1 + m2
    gsel = jnp.zeros((ng, 1, tm), jnp.bool_)
    for _ in range(TOPK_ROUTE_GROUPS):
        m = jnp.max(work, axis=0, keepdims=True)
        hit = grp == jnp.min(jnp.where(work == m, grp, ng), axis=0, keepdims=True)
        gsel = jnp.logical_or(gsel, hit)
        work = jnp.where(hit, neg, work)
    work = jnp.where(gsel, biased, neg)
    w = jnp.zeros(shape, F32)
    for _ in range(TOP_K):
        m = jnp.max(jnp.max(work, axis=0, keepdims=True), axis=1, keepdims=True)
        first = jnp.min(jnp.min(jnp.where(work == m, eid, ne), axis=0, keepdims=True),
                        axis=1, keepdims=True)
        hit = eid == first
        w = jnp.where(hit, scores, w)
        work = jnp.where(hit, neg, work)
    denom = jnp.sum(jnp.sum(w, axis=0, keepdims=True), axis=1, keepdims=True)
    return (w / denom * ROUTED_SCALE).reshape(ne, tm)


def _split_bf16(x):
    hi = x.astype(BF16)
    lo = (x - hi.astype(F32)).astype(BF16)
    return hi, lo


def _postmix_kernel(x_ref, ya_ref, yb_ref, gt1_ref, sh2_ref, sc2_ref, gt2_ref, g2_ref,
                    wout_ref, rwh_ref, rwl_ref, rb_ref, swg_ref, swu_ref, swd_ref,
                    xacc_ref, h2_ref, wd_ref, *, token_tiles):
    y = jnp.concatenate([ya_ref[...], yb_ref[...]], axis=-1)
    x1 = x_ref[...] + gt1_ref[...] * _dot(y, wout_ref[...])
    h2 = _rmsnorm(x1, g2_ref[...]) * (1.0 + sc2_ref[...]) + sh2_ref[...]
    hb = h2.astype(BF16)
    if token_tiles:
        tm, d = h2.shape
        nj = d // LANES
        for j in range(nj):
            h2_ref[pl.ds(j, tm, stride=nj), :] = h2[:, j * LANES:(j + 1) * LANES]
    else:
        h2_ref[...] = hb
    h_hi, h_lo = _split_bf16(h2)
    if token_tiles:
        nt = (((1,), (1,)), ((), ()))
        logits = (lax.dot_general(rwh_ref[...], h_hi, nt, preferred_element_type=F32)
                  + lax.dot_general(rwl_ref[...], h_hi, nt, preferred_element_type=F32)
                  + lax.dot_general(rwh_ref[...], h_lo, nt, preferred_element_type=F32))
        wd_ref[...] = _route_by_expert(logits, rb_ref[...])
    else:
        logits = (jnp.dot(h_hi, rwh_ref[...], preferred_element_type=F32)
                  + jnp.dot(h_hi, rwl_ref[...], preferred_element_type=F32)
                  + jnp.dot(h_lo, rwh_ref[...], preferred_element_type=F32))
        wd_ref[...] = _route(logits, rb_ref[...])
    hid = jax.nn.silu(jnp.dot(hb, swg_ref[...], preferred_element_type=F32)) * \
        jnp.dot(hb, swu_ref[...], preferred_element_type=F32)
    shared = _dot(hid, swd_ref[...])
    xacc_ref[...] = x1 + gt2_ref[...] * shared


def _postmix(x, ya, yb, gt1, sh2, sc2, gt2, g2, w_out, rw_hi, rw_lo, rb, swg, swu, swd,
             per_row, seq_len, tm, token_tiles):
    t, d = x.shape
    sw = ya.shape[1]
    nj = d // LANES
    mod = _mod_spec(per_row, tm, d, max(seq_len // tm, 1))
    weights = (g2, w_out, rw_hi, rw_lo, rb, swg, swu, swd)
    if token_tiles:
        ne = rw_hi.shape[0]
        h2_spec, h2_shape = _row_spec(tm * nj, LANES), jax.ShapeDtypeStruct((t * nj, LANES), F32)
        wd_spec, wd_shape = pl.BlockSpec((ne, tm), lambda i: (0, i)), (ne, t)
    else:
        ne = rw_hi.shape[1]
        h2_spec, h2_shape = _row_spec(tm, d), jax.ShapeDtypeStruct((t, d), BF16)
        wd_spec, wd_shape = _row_spec(tm, ne), (t, ne)
    return pl.pallas_call(
        functools.partial(_postmix_kernel, token_tiles=token_tiles),
        grid=(t // tm,),
        in_specs=[_row_spec(tm, d), _row_spec(tm, sw), _row_spec(tm, sw), mod, mod, mod, mod]
        + [_full_spec(w.shape) for w in weights],
        out_specs=[_row_spec(tm, d), h2_spec, wd_spec],
        out_shape=[jax.ShapeDtypeStruct((t, d), F32), h2_shape,
                   jax.ShapeDtypeStruct(wd_shape, F32)],
        compiler_params=_params(1),
        name="postmix",
    )(x, ya, yb, gt1, sh2, sc2, gt2, *weights)


def _moe_dense_kernel(h_ref, wd_ref, xacc_ref, gt2_ref, gf_ref, wg_ref, wu_ref, wdn_ref,
                      o_ref, acc_ref, *, final_norm):
    step = pl.program_id(1)
    per_step = wg_ref.shape[0]

    @pl.when(step == 0)
    def _():
        acc_ref[...] = jnp.zeros_like(acc_ref)

    hb = h_ref[...]
    wd = wd_ref[...]
    lane = lax.broadcasted_iota(jnp.int32, wd.shape, 1)
    total = None
    for q in range(per_step):
        col = jnp.sum(jnp.where(lane == step * per_step + q, wd, 0.0), axis=-1, keepdims=True)
        hid = jax.nn.silu(jnp.dot(hb, wg_ref[q], preferred_element_type=F32)) * \
            jnp.dot(hb, wu_ref[q], preferred_element_type=F32)
        part = col * _dot(hid, wdn_ref[q])
        total = part if total is None else total + part
    acc_ref[...] += total

    @pl.when(step == pl.num_programs(1) - 1)
    def _():
        xo = xacc_ref[...] + gt2_ref[...] * acc_ref[...]
        o_ref[...] = _rmsnorm(xo, gf_ref[...]) if final_norm else xo


def _moe_dense(h2, wd, xacc, gt2, g_final, wg, wu, wdn, per_row, seq_len, tm, final_norm):
    t, d = xacc.shape
    ne, _, eh = wg.shape
    nl = wd.shape[1]
    tpb = max(seq_len // tm, 1)
    if per_row:
        mod = pl.BlockSpec((tm, d), lambda i, e: (i, 0))
    else:
        mod = pl.BlockSpec((None, 1, d), lambda i, e: (i // tpb, 0, 0))
    row = lambda w: pl.BlockSpec((tm, w), lambda i, e: (i, 0))
    per_step = DENSE_EXPERTS_PER_STEP
    assert ne % per_step == 0
    return pl.pallas_call(
        functools.partial(_moe_dense_kernel, final_norm=final_norm),
        grid=(t // tm, ne // per_step),
        in_specs=[row(d), row(nl), row(d), mod,
                  pl.BlockSpec(g_final.shape, lambda i, e: (0, 0)),
                  pl.BlockSpec((per_step, d, eh), lambda i, e: (e, 0, 0)),
                  pl.BlockSpec((per_step, d, eh), lambda i, e: (e, 0, 0)),
                  pl.BlockSpec((per_step, eh, d), lambda i, e: (e, 0, 0))],
        out_specs=row(d),
        out_shape=jax.ShapeDtypeStruct((t, d), F32),
        scratch_shapes=[pltpu.VMEM((tm, d), F32)],
        compiler_params=_params(2),
        name="moe_dense",
    )(h2, wd, xacc, gt2, g_final, wg, wu, wdn)


TOKEN_BITS = 13


def _route_plan_kernel(wd_ref, rows_ref, wl_ref, cnt_ref, *, tg, nj):
    wt = wd_ref[...]
    sel = wt > 0.0
    lane = lax.broadcasted_iota(jnp.int32, wt.shape, 1)
    c = sel.astype(jnp.int32)
    s = 1
    while s < tg:
        c = c + jnp.where(lane >= s, pltpu.roll(c, s, 1), 0)
        s *= 2
    cnt_ref[...] = jnp.broadcast_to(c[:, tg - 1:tg], cnt_ref.shape)
    dist = lane - (c - 1)
    key = jnp.where(sel, (dist << TOKEN_BITS) | lane, -1)
    w = jnp.where(sel, wt, 0.0)
    s = 1
    while s < tg:
        leaving = (key >= 0) & (((key >> TOKEN_BITS) & s) != 0)
        key_in = pltpu.roll(key, tg - s, 1)
        w_in = pltpu.roll(w, tg - s, 1)
        arriving = (key_in >= 0) & (((key_in >> TOKEN_BITS) & s) != 0)
        key = jnp.where(arriving, key_in, jnp.where(leaving, -1, key))
        w = jnp.where(arriving, w_in, jnp.where(leaving, 0.0, w))
        s *= 2
    rows_ref[...] = jnp.where(key >= 0, key & ((1 << TOKEN_BITS) - 1), 0) * nj
    wl_ref[...] = w


def _route_plan(wd, tg, nj):
    ne, t = wd.shape
    assert ne == N_EXPERTS and tg <= (1 << TOKEN_BITS) and t % tg == 0
    g = t // tg
    grp = lambda w: pl.BlockSpec((None, N_EXPERTS, w), lambda i: (i, 0, 0))
    return pl.pallas_call(
        functools.partial(_route_plan_kernel, tg=tg, nj=nj),
        grid=(g,),
        in_specs=[pl.BlockSpec((ne, tg), lambda i: (0, i))],
        out_specs=[grp(tg), grp(tg), grp(LANES)],
        out_shape=[jax.ShapeDtypeStruct((g, N_EXPERTS, tg), jnp.int32),
                   jax.ShapeDtypeStruct((g, N_EXPERTS, tg), F32),
                   jax.ShapeDtypeStruct((g, N_EXPERTS, LANES), jnp.int32)],
        compiler_params=_params(1),
        name="route_plan",
    )(wd)


def _block_table(counts, tg, bm, nb_max):
    g, ne = counts.shape
    i32 = jnp.int32
    nblk = (counts + bm - 1) // bm
    bend = jnp.cumsum(nblk, axis=1)
    bstart = bend - nblk
    nb = bend[:, -1]
    k = jnp.arange(nb_max, dtype=i32)[None, :]
    kc = jnp.minimum(k, jnp.maximum(nb[:, None] - 1, 0))
    e_k = jnp.minimum(jnp.sum(bend[:, None, :] <= kc[:, :, None], axis=2), ne - 1).astype(i32)
    b_k = kc - jnp.take_along_axis(bstart, e_k, axis=1)
    first = (b_k == 0) & (k < nb[:, None])
    active = nblk > 0
    parity = (jnp.take_along_axis(jnp.cumsum(active.astype(i32), axis=1) - 1, e_k, axis=1)
              % MOE_WEIGHT_BUFFERS)
    ids = jnp.where(active, jnp.arange(ne, dtype=i32)[None, :], ne)
    first_active_from = jnp.flip(lax.cummin(jnp.flip(ids, 1), axis=1), 1)
    nxt = jnp.concatenate([first_active_from[:, 1:], jnp.full((g, 1), ne, i32)], axis=1)
    nxt = jnp.where(nxt >= ne, -1, nxt)
    nxt2 = jnp.where(nxt < 0, -1, jnp.take_along_axis(nxt, jnp.maximum(nxt, 0), axis=1))
    next_k = jnp.take_along_axis(nxt, e_k, axis=1)
    next2_k = jnp.take_along_axis(nxt2, e_k, axis=1)
    row = (jnp.arange(g, dtype=i32)[:, None] * ne + e_k) * (tg // bm) + b_k
    flat = lambda a: a.reshape(-1).astype(i32)
    table = (nb.astype(i32), flat(e_k), flat(first), flat(parity), flat(next_k), flat(next2_k))
    return table, flat(row)


def _moe_group_kernel(nb_ref, exp_ref, first_ref, par_ref, next_ref, next2_ref,
                      h_ref, rows_ref, wl_ref, wg_hbm, wu_hbm, wdn_hbm, o_ref,
                      wg_buf, wu_buf, wdn_buf, wsem, xt_a, xt_b, hid_a, hid_b, yt_a, yt_b,
                      *, bm, nj, batch, nb_max):
    g = pl.program_id(0)
    nb = nb_ref[g]
    tb = g * nb_max
    pitch = bm + 1
    o_ref[...] = jnp.zeros_like(o_ref)

    def weight_copies(e, slot):
        return (pltpu.make_async_copy(wg_hbm.at[e], wg_buf.at[slot], wsem.at[slot]),
                pltpu.make_async_copy(wu_hbm.at[e], wu_buf.at[slot], wsem.at[slot]),
                pltpu.make_async_copy(wdn_hbm.at[e], wdn_buf.at[slot], wsem.at[slot]))

    n_wbuf = wg_buf.shape[0]

    def handoff(k):
        @pl.when(first_ref[tb + k] == 1)
        def _():
            slot = par_ref[tb + k]
            for c in weight_copies(exp_ref[tb + k], slot):
                c.wait()
            ahead = next2_ref[tb + k]

            @pl.when(ahead >= 0)
            def _():
                for c in weight_copies(ahead, (slot + 2) % n_wbuf):
                    c.start()

    def gather_rows(k, xt):
        rows = rows_ref.at[pl.ds(pl.multiple_of(k * bm, bm), bm)]

        def emit(lo, hi):
            for m in range(lo, hi):
                r0 = pl.multiple_of(rows[m], nj)
                xt[pl.ds(m, nj, stride=pitch), :] = h_ref[pl.ds(r0, nj), :]
        return emit

    def scatter_rows(k, yt):
        rows = rows_ref.at[pl.ds(pl.multiple_of(k * bm, bm), bm)]

        def emit(lo, hi):
            for m0 in range(lo, hi, batch):
                updates = []
                for m in range(m0, min(m0 + batch, hi)):
                    r0 = pl.multiple_of(rows[m], nj)
                    updates.append((r0, o_ref[pl.ds(r0, nj), :]
                                    + yt[pl.ds(m, nj, stride=pitch), :]))
                for r0, v in reversed(updates):
                    o_ref[pl.ds(r0, nj), :] = v
        return emit

    n_chunks = nj // 2
    kc = 2 * LANES
    n_pieces = 2 * n_chunks + 1
    early = bm * MOE_EARLY_ROW_PERCENT // 100
    cuts = [early * i // (n_pieces - 1) for i in range(n_pieces)] + [bm]

    def step(up_blk, down_blk, side_work):
        piece = 0

        def side():
            nonlocal piece
            for emit in side_work:
                emit(cuts[piece], cuts[piece + 1])
            piece += 1

        gate = up = None
        for c in range(n_chunks):
            if up_blk is not None:
                k, xt, _ = up_blk
                slot = par_ref[tb + k]
                xc = jnp.concatenate([xt[j * pitch:j * pitch + bm, :] for j in (2 * c, 2 * c + 1)],
                                     axis=-1).astype(BF16)
                pg = jnp.dot(xc, wg_buf[slot, c * kc:(c + 1) * kc, :], preferred_element_type=F32)
                pu = jnp.dot(xc, wu_buf[slot, c * kc:(c + 1) * kc, :], preferred_element_type=F32)
                gate = pg if gate is None else gate + pg
                up = pu if up is None else up + pu
            side()
        for c in range(n_chunks):
            if down_blk is not None:
                k, hid, yt = down_blk
                slot = par_ref[tb + k]
                yc = jnp.dot(hid[...], wdn_buf[slot, :, c * kc:(c + 1) * kc],
                             preferred_element_type=F32)
                for q in range(2):
                    j = 2 * c + q
                    yt[j * pitch:j * pitch + bm, :] = yc[:, q * LANES:(q + 1) * LANES]
            side()
        side()
        if up_blk is not None:
            k, _, hid = up_blk
            w_col = jnp.broadcast_to(wl_ref[k], (SUBLANES, bm)).T[:, :1]
            hid[...] = (jax.nn.silu(gate) * up * w_col).astype(BF16)

    @pl.when(nb > 0)
    def _():
        for c in weight_copies(exp_ref[tb], 0):
            c.start()
        second = next_ref[tb]

        @pl.when(second >= 0)
        def _():
            for c in weight_copies(second, 1):
                c.start()
        gather_rows(0, xt_a)(0, bm)
        handoff(0)
        step((0, xt_a, hid_a), None, [gather_rows(1, xt_b)])
        handoff(1)
        step((1, xt_b, hid_b), (0, hid_a, yt_a), [gather_rows(2, xt_a)])

        def trip(i, carry):
            t = 2 * i + 1
            handoff(t + 1)
            step((t + 1, xt_a, hid_a), (t, hid_b, yt_b),
                 [gather_rows(t + 2, xt_b), scatter_rows(t - 1, yt_a)])
            handoff(t + 2)
            step((t + 2, xt_b, hid_b), (t + 1, hid_a, yt_a),
                 [gather_rows(t + 3, xt_a), scatter_rows(t, yt_b)])
            return carry

        lax.fori_loop(0, nb // 2, trip, 0)

        @pl.when(nb % 2 == 1)
        def _():
            scatter_rows(nb - 1, yt_a)(0, bm)


def _moe_group(table, h2t, rows, wl, wg, wu, wdn, tg, bm, nb_max, batch=1):
    ne, d, eh = wg.shape
    nj = d // LANES
    assert nj == SUBLANES
    g = h2t.shape[0] // (tg * nj)
    kern = functools.partial(_moe_group_kernel, bm=bm, nj=nj, batch=batch, nb_max=nb_max)
    slots = pl.BlockSpec((nb_max * bm,), lambda gi, *_: (gi,), memory_space=pltpu.SMEM)
    block_w = pl.BlockSpec((nb_max, 1, bm), lambda gi, *_: (gi, 0, 0))
    hbm = pl.BlockSpec(memory_space=pl.ANY)
    pitch_rows = nj * (bm + 1)
    grid_spec = pltpu.PrefetchScalarGridSpec(
        num_scalar_prefetch=len(table),
        grid=(g,),
        in_specs=[pl.BlockSpec((tg * nj, LANES), lambda gi, *_: (gi, 0),
                               pipeline_mode=pl.Buffered(1)),
                  slots, block_w, hbm, hbm, hbm],
        out_specs=pl.BlockSpec((tg * nj, LANES), lambda gi, *_: (gi, 0),
                               pipeline_mode=pl.Buffered(1)),
        scratch_shapes=[pltpu.VMEM((MOE_WEIGHT_BUFFERS, d, eh), BF16),
                        pltpu.VMEM((MOE_WEIGHT_BUFFERS, d, eh), BF16),
                        pltpu.VMEM((MOE_WEIGHT_BUFFERS, eh, d), BF16),
                        pltpu.SemaphoreType.DMA((MOE_WEIGHT_BUFFERS,)),
                        pltpu.VMEM((pitch_rows, LANES), F32), pltpu.VMEM((pitch_rows, LANES), F32),
                        pltpu.VMEM((bm, eh), BF16), pltpu.VMEM((bm, eh), BF16),
                        pltpu.VMEM((pitch_rows, LANES), F32), pltpu.VMEM((pitch_rows, LANES), F32)],
    )
    return pl.pallas_call(
        kern,
        grid_spec=grid_spec,
        out_shape=jax.ShapeDtypeStruct((g * tg * nj, LANES), F32),
        compiler_params=_params(1),
        name="moe_group",
    )(*table, h2t, rows, wl.reshape(g * nb_max, 1, bm), wg, wu, wdn)


def _finalize_kernel(r_ref, xacc_ref, gt2_ref, gf_ref, o_ref, *, final_norm):
    tm, d = xacc_ref.shape
    nj = d // LANES
    routed = jnp.concatenate([r_ref[pl.ds(j, tm, stride=nj), :] for j in range(nj)], axis=-1)
    xo = xacc_ref[...] + gt2_ref[...] * routed
    o_ref[...] = _rmsnorm(xo, gf_ref[...]) if final_norm else xo


def _finalize(routed, xacc, gt2, g_final, seq_len, tm, final_norm):
    t, d = xacc.shape
    nj = d // LANES
    return pl.pallas_call(
        functools.partial(_finalize_kernel, final_norm=final_norm),
        grid=(t // tm,),
        in_specs=[_row_spec(tm * nj, LANES),
                  _row_spec(tm, d), _mod_spec(False, tm, d, seq_len // tm),
                  _full_spec(g_final.shape)],
        out_specs=_row_spec(tm, d),
        out_shape=jax.ShapeDtypeStruct((t, d), F32),
        compiler_params=_params(1),
        name="finalize",
    )(routed, xacc, gt2, g_final)


def _s5_discretize(lam_re, lam_im, log_dt, b_re, b_im, c_re, c_im):
    g, n = lam_re.shape
    hg = b_re.shape[2]
    lr = jnp.minimum(lam_re.astype(F32), -1e-4)
    li = lam_im.astype(F32)
    dt = jnp.exp(log_dt.astype(F32))[:, None]
    mag = jnp.exp(lr * dt)
    abar_r, abar_i = mag * jnp.cos(li * dt), mag * jnp.sin(li * dt)
    den = lr * lr + li * li
    nr, ni = abar_r - 1.0, abar_i
    f_r = (nr * lr + ni * li) / den
    f_i = (ni * lr - nr * li) / den
    br_, bi_ = b_re.astype(F32), b_im.astype(F32)
    bbar_r = f_r[..., None] * br_ - f_i[..., None] * bi_
    bbar_i = f_r[..., None] * bi_ + f_i[..., None] * br_
    gpc = LANES // hg
    n_chunks = g // gpc
    eye = jnp.eye(gpc, dtype=F32)

    def b_chunks(bb):
        bb = bb.reshape(n_chunks, gpc, n, hg)
        m = jnp.einsum('jgnh,gk->jghkn', bb, eye)
        return m.reshape(n_chunks, gpc * hg, gpc * n)

    def c_chunks(cc):
        cc = cc.astype(F32).reshape(n_chunks, gpc, hg, n)
        m = jnp.einsum('jghn,gk->jgnkh', cc, eye)
        return m.reshape(n_chunks, gpc * n, gpc * hg)

    bj = jnp.concatenate([b_chunks(bbar_r), b_chunks(bbar_i)], axis=2).astype(BF16)
    cj = jnp.concatenate([c_chunks(c_re), -c_chunks(c_im)], axis=1).astype(BF16)
    return abar_r.reshape(1, g * n), abar_i.reshape(1, g * n), bj, cj


def kernel(x_prompt, x_sample, c_prompt, c_sample, state_ssm_re, state_ssm_im, g_norm1, g_norm2, w_ada, b_ada, w_in, w_out, ssm_lambda_re, ssm_lambda_im, ssm_log_dt, ssm_b_re, ssm_b_im, ssm_c_re, ssm_c_im, ssm_d, ssm_w_glu, ssm_b_glu, gm_ln_g, gm_ln_b, gm_w_s, gm_b_s, router_w, router_bias, moe_w_gate, moe_w_up, moe_w_down, shared_w_gate, shared_w_up, shared_w_down, g_final):
    bp, seq_len, d = x_prompt.shape
    bs, dec_len, _ = x_sample.shape
    depth = w_ada.shape[0]
    assert dec_len == 1 and seq_len % CHUNK == 0 and bp % SUBLANES == 0
    groups, n_state = ssm_lambda_re.shape[1:]
    sw = groups * SSM_GROUP
    gw = gm_ln_g.shape[1]
    hd = gw // GM_HEADS

    xp = x_prompt.reshape(bp * seq_len, d)
    xs = x_sample.reshape(bs, d)
    c_all = jnp.concatenate([c_prompt, c_sample], axis=0)
    gf = g_final.reshape(1, d)

    p_re, p_im, s_re, s_im, s_v = [], [], [], [], []
    for l in range(depth):
        mod = _ada(c_all, w_ada[l], b_ada[l].reshape(1, -1))
        mod_p = [m.reshape(bp, 1, d) for m in jnp.split(mod[:bp], 6, axis=-1)]
        mod_s = jnp.split(mod[bp:], 6, axis=-1)

        g1 = g_norm1[l].reshape(1, d)
        g2 = g_norm2[l].reshape(1, d)
        w_in_b = w_in[l].astype(BF16)
        w_out_b = w_out[l].astype(BF16)
        ln_g = gm_ln_g[l].reshape(1, gw)
        ln_b = gm_ln_b[l].reshape(1, gw)
        ws_tril = (gm_w_s[l] * jnp.tril(jnp.ones((CHUNK, CHUNK), F32))[None]).astype(BF16)
        bs_full = jnp.repeat(gm_b_s[l].T, hd, axis=1)
        w0 = jnp.repeat(gm_w_s[l][:, 0, 0], hd).reshape(1, gw)
        b0 = jnp.repeat(gm_b_s[l][:, 0], hd).reshape(1, gw)
        a_re, a_im, bj, cj = _s5_discretize(ssm_lambda_re[l], ssm_lambda_im[l], ssm_log_dt[l],
                                            ssm_b_re[l], ssm_b_im[l], ssm_c_re[l], ssm_c_im[l])
        d_vec = ssm_d[l].reshape(1, sw)
        w_glu = ssm_w_glu[l].astype(BF16)
        b_glu = ssm_b_glu[l].reshape(1, sw)
        pad = LANES - N_EXPERTS
        rw = jnp.pad(router_w[l], ((0, 0), (0, pad)))
        rw_hi = rw.astype(BF16)
        rw_lo = (rw - rw_hi.astype(F32)).astype(BF16)
        rb = jnp.pad(router_bias[l], (0, pad), constant_values=-jnp.inf).reshape(1, LANES)
        rwt_hi, rwt_lo = rw_hi[:, :N_EXPERTS].T, rw_lo[:, :N_EXPERTS].T
        rb_col = router_bias[l].reshape(N_EXPERTS, 1)
        swg, swu, swd = (shared_w_gate[l].astype(BF16), shared_w_up[l].astype(BF16),
                         shared_w_down[l].astype(BF16))
        ewg, ewu, ewd = (moe_w_gate[l].astype(BF16), moe_w_up[l].astype(BF16),
                         moe_w_down[l].astype(BF16))

        tm_row = min(ROW_TILE, seq_len)
        ua, yb = _premix_seq(xp, mod_p[0], mod_p[1], g1, w_in_b, ln_g, ln_b, ws_tril, bs_full,
                             seq_len, tm=min(PREMIX_TILE, seq_len))
        ya, hpr, hpi = _s5_seq(ua.reshape(bp, seq_len, sw), bj, cj, a_re, a_im, d_vec,
                               w_glu, b_glu, tl=min(SCAN_TILE, seq_len))
        xacc, h2, wd = _postmix(xp, ya.reshape(bp * seq_len, sw), yb, mod_p[2], mod_p[3],
                                mod_p[4], mod_p[5], g2, w_out_b, rwt_hi, rwt_lo, rb_col,
                                swg, swu, swd,
                                per_row=False, seq_len=seq_len, tm=tm_row, token_tiles=True)
        last = l == depth - 1
        tg = min(MOE_GROUP, bp * seq_len)
        assert (bp * seq_len) % tg == 0
        bm = min(MOE_BLOCK, tg)
        per_tile = max(SMEM_TILE // bm, 1)
        nb_max = -(-(tg * TOP_K // bm + N_EXPERTS + 3) // per_tile) * per_tile
        slot_rows, slot_w, counts = _route_plan(wd, tg, d // LANES)
        table, block_row = _block_table(counts[:, :, 0], tg, bm, nb_max)
        by_block = lambda a: jnp.take(a.reshape(-1, bm), block_row, axis=0).reshape(-1)
        routed = _moe_group(table, h2, by_block(slot_rows), by_block(slot_w),
                            ewg, ewu, ewd, tg, bm, nb_max)
        xp = _finalize(routed, xacc, mod_p[5], gf, seq_len, tm_row, final_norm=last)
        p_re.append(hpr.reshape(bp, groups, n_state))
        p_im.append(hpi.reshape(bp, groups, n_state))

        ua, yb, vn = _premix_step(xs, mod_s[0], mod_s[1], g1, w_in_b, ln_g, ln_b, w0, b0)
        ya, hsr, hsi = _s5_step(ua, state_ssm_re[l].reshape(bs, groups * n_state),
                                state_ssm_im[l].reshape(bs, groups * n_state),
                                bj, cj, a_re, a_im, d_vec, w_glu, b_glu)
        xacc, h2, wd = _postmix(xs, ya, yb, mod_s[2], mod_s[3], mod_s[4], mod_s[5], g2, w_out_b,
                                rw_hi, rw_lo, rb, swg, swu, swd, per_row=True, seq_len=1, tm=bs,
                                token_tiles=False)
        xs = _moe_dense(h2, wd, xacc, mod_s[5], gf, ewg, ewu, ewd, per_row=True, seq_len=1,
                        tm=bs, final_norm=last)
        s_re.append(hsr.reshape(bs, groups, n_state))
        s_im.append(hsi.reshape(bs, groups, n_state))
        s_v.append(vn.reshape(bs, 1, gw))

    y_prompt = xp.reshape(bp, seq_len, d)
    y_sample = xs.reshape(bs, 1, d)
    return (y_prompt, y_sample, jnp.stack(p_re), jnp.stack(p_im), jnp.stack(s_re),
            jnp.stack(s_im), jnp.stack(s_v))
```
